```python
import math
import jax
import jax.numpy as jnp
from jax import lax
import numpy as np

D_MODEL = 2048
BATCH = 8
SEQ = 8192
DEPTH = 2

GRID_W = 64
CTX_LEN = 256
N_BRANCH = 4
BRANCH_W = D_MODEL // 4
CHUNK = 64
CONV_W = 5
RMS_EPS = 1e-6
NEG_INF = -1e30
ROPE_BASE = 10000.0
GLA_H = 4
GLA_DV = BRANCH_W // GLA_H
GLA_DK = GLA_DV // 2
GLA_LR = 16
GLA_TAU = 16.0
NA_H = 4
NA_D = BRANCH_W // NA_H
NA_WIN_R = 8
NA_WIN_C = 16
GDN_H = 4
GDN_D = BRANCH_W // GDN_H
M2_P = 64
M2_H = BRANCH_W // M2_P
M2_N = 128
M2_G = 2
M2_CONV_CH = BRANCH_W + 2 * M2_G * M2_N
D_FF = ((8 * D_MODEL + 3 * 256 - 1) // (3 * 256)) * 256
GLA_IN = 2 * GLA_H * GLA_DK + 2 * BRANCH_W + 2 * GLA_LR
NA_IN = 3 * BRANCH_W
GDN_IN = 4 * BRANCH_W + 4 * GDN_H
M2_IN = BRANCH_W + M2_CONV_CH + 2 * M2_H
GATE_IN = N_BRANCH * D_MODEL
IN_COLS = GLA_IN + NA_IN + GDN_IN + M2_IN + GATE_IN

kernel_name = 'hybrid_gla_natten_gdn_ssd_dit'


def split_cols(t, sizes):
    return jnp.split(t, np.cumsum(sizes)[:-1].tolist(), axis=-1)


def rmsnorm(x, g):
    xf = x.astype(jnp.float32)
    y = xf * lax.rsqrt(jnp.mean(xf * xf, axis=-1, keepdims=True) + RMS_EPS)
    return (y * g.astype(jnp.float32)).astype(x.dtype)


def l2norm(x):
    xf = x.astype(jnp.float32)
    return xf * lax.rsqrt(jnp.sum(xf * xf, axis=-1, keepdims=True) + RMS_EPS)


def lower_tri(strict):
    return jnp.tril(jnp.ones((CHUNK, CHUNK), dtype=bool), -1 if strict else 0)


def seg_decay(b, mask):
    diff = b[..., :, None] - b[..., None, :]
    return jnp.where(mask, jnp.exp(jnp.where(mask, diff, 0.0)), 0.0)


def short_conv(x, w):
    return lax.conv_general_dilated(
        x, w.astype(x.dtype)[:, None, :], window_strides=(1,),
        padding=[(CONV_W // 2, CONV_W // 2)],
        dimension_numbers=('NWC', 'WIO', 'NWC'), feature_group_count=x.shape[-1])


def axial_rope(T, dim):
    n_freq = dim // 4
    freqs = ROPE_BASE ** (-jnp.arange(n_freq, dtype=jnp.float32) / n_freq)
    t = jnp.arange(T)
    row = (t // GRID_W).astype(jnp.float32)
    col = (t % GRID_W).astype(jnp.float32)
    ang = jnp.concatenate([row[:, None] * freqs, col[:, None] * freqs], axis=-1)
    return jnp.cos(ang), jnp.sin(ang)


def apply_rope(x, cos, sin):
    half = x.shape[-1] // 2
    x1, x2 = x[..., :half], x[..., half:]
    c = cos[None, :, None, :].astype(x.dtype)
    s = sin[None, :, None, :].astype(x.dtype)
    return jnp.concatenate([x1 * c - x2 * s, x1 * s + x2 * c], axis=-1)


def scan_states(decay, dS, S0):
    def step(S, xs):
        d, ds = xs
        return d * S + ds, S
    S_fin, S_starts = lax.scan(step, S0, (jnp.moveaxis(decay, 1, 0), jnp.moveaxis(dS, 1, 0)))
    return jnp.moveaxis(S_starts, 0, 1), S_fin


def gla_chunk(q, k, v, log_a, S0, with_out):
    B_, T, H, dk = q.shape
    dv = v.shape[-1]
    n = T // CHUNK
    q, k, v, log_a = (t.astype(jnp.float32).reshape(B_, n, CHUNK, H, t.shape[-1]) for t in (q, k, v, log_a))
    b = jnp.cumsum(log_a, axis=2)
    b_last = b[:, :, -1]
    dS = jnp.einsum('bnlhk,bnlhv->bnhkv', k * jnp.exp(b_last[:, :, None] - b), v)
    if S0 is None:
        S0 = jnp.zeros((B_, H, dk, dv), jnp.float32)
    S_starts, S_fin = scan_states(jnp.exp(b_last)[..., None], dS, S0)
    if not with_out:
        return None, S_fin
    q_in = q * jnp.exp(b)
    k_in = k * jnp.exp(-b)
    att = jnp.where(lower_tri(False), jnp.einsum('bnlhk,bnshk->bnhls', q_in, k_in), 0.0)
    o = jnp.einsum('bnlhk,bnhkv->bnlhv', q_in, S_starts) + jnp.einsum('bnhls,bnshv->bnlhv', att, v)
    return o.reshape(B_, T, H, dv), S_fin


def gdn_chunk(q, k, v, beta, log_a, S0, with_out):
    B_, T, H, dk = q.shape
    dv = v.shape[-1]
    n = T // CHUNK
    q, k, v = (t.astype(jnp.float32).reshape(B_, n, CHUNK, H, t.shape[-1]) for t in (q, k, v))
    beta = beta.astype(jnp.float32).reshape(B_, n, CHUNK, H)
    b = jnp.cumsum(log_a.astype(jnp.float32).reshape(B_, n, CHUNK, H), axis=2)
    bh = jnp.swapaxes(b, 2, 3)
    kk = jnp.einsum('bnlhk,bnshk->bnhls', k, k)
    a_mat = jnp.eye(CHUNK, dtype=jnp.float32) + jnp.swapaxes(beta, 2, 3)[..., None] * kk * seg_decay(bh, lower_tri(True))
    rhs = jnp.concatenate([k * (beta * jnp.exp(b))[..., None], v * beta[..., None]], axis=-1)
    sol = lax.linalg.triangular_solve(a_mat, jnp.transpose(rhs, (0, 1, 3, 2, 4)),
                                      left_side=True, lower=True, unit_diagonal=True)
    w, u0 = sol[..., :dk], sol[..., dk:]
    b_last = b[:, :, -1]
    k_end = k * jnp.exp(b_last[:, :, None] - b)[..., None]
    if S0 is None:
        S0 = jnp.zeros((B_, H, dk, dv), jnp.float32)

    def step(S, xs):
        w_n, u0_n, k_n, g_n = xs
        u_n = u0_n - jnp.einsum('bhlk,bhkv->bhlv', w_n, S)
        S_next = jnp.exp(g_n)[..., None, None] * S + jnp.einsum('blhk,bhlv->bhkv', k_n, u_n)
        return S_next, (S, u_n)

    S_fin, (S_starts, u) = lax.scan(step, S0, tuple(jnp.moveaxis(t, 1, 0) for t in (w, u0, k_end, b_last)))
    if not with_out:
        return None, S_fin
    S_starts = jnp.moveaxis(S_starts, 0, 1)
    u = jnp.moveaxis(u, 0, 1)
    qk = jnp.einsum('bnlhk,bnshk->bnhls', q, k) * seg_decay(bh, lower_tri(False))
    o = jnp.exp(b)[..., None] * jnp.einsum('bnlhk,bnhkv->bnlhv', q, S_starts) + jnp.einsum('bnhls,bnhsv->bnlhv', qk, u)
    return o.reshape(B_, T, H, dv), S_fin


def ssd_chunk(cq, bk, xv, log_a, S0, with_out):
    B_, T, H, ns = cq.shape
    p = xv.shape[-1]
    n = T // CHUNK
    cq, bk, xv = (t.astype(jnp.float32).reshape(B_, n, CHUNK, H, t.shape[-1]) for t in (cq, bk, xv))
    b = jnp.cumsum(log_a.astype(jnp.float32).reshape(B_, n, CHUNK, H), axis=2)
    b_last = b[:, :, -1]
    dS = jnp.einsum('bnlhd,bnlhp->bnhdp', bk * jnp.exp(b_last[:, :, None] - b)[..., None], xv)
    if S0 is None:
        S0 = jnp.zeros((B_, H, ns, p), jnp.float32)
    S_starts, S_fin = scan_states(jnp.exp(b_last)[..., None, None], dS, S0)
    if not with_out:
        return None, S_fin
    scores = jnp.einsum('bnlhd,bnshd->bnhls', cq, bk) * seg_decay(jnp.swapaxes(b, 2, 3), lower_tri(False))
    o = jnp.exp(b)[..., None] * jnp.einsum('bnlhd,bnhdp->bnlhp', cq, S_starts) + jnp.einsum('bnhls,bnshp->bnlhp', scores, xv)
    return o.reshape(B_, T, H, p), S_fin


def bidir_scan(chunk_fn, lat_dirs, ctx_dirs, with_ctx_out):
    y_lat, y_ctx = [], []
    for d in range(2):
        flip = (lambda t: jnp.flip(t, axis=1)) if d == 1 else (lambda t: t)
        o_c, s_c = chunk_fn(*map(flip, ctx_dirs[d]), None, with_ctx_out)
        o_l, _ = chunk_fn(*map(flip, lat_dirs[d]), s_c, True)
        y_lat.append(flip(o_l))
        if with_ctx_out:
            y_ctx.append(flip(o_c))
    return y_lat[0] + y_lat[1], (y_ctx[0] + y_ctx[1] if with_ctx_out else None)


def gla_mixer(pl, pc, a2, ab, norm_g, cos, sin, with_ctx_out):
    def prep(p, rope):
        B_, T = p.shape[:2]
        q, k, v, g, lr_f, lr_b = split_cols(p, [GLA_H * GLA_DK] * 2 + [BRANCH_W] * 2 + [GLA_LR] * 2)
        q = q.reshape(B_, T, GLA_H, GLA_DK)
        k = k.reshape(B_, T, GLA_H, GLA_DK)
        v = v.reshape(B_, T, GLA_H, GLA_DV)
        if rope:
            q = apply_rope(q, cos, sin)
            k = apply_rope(k, cos, sin)
        q = q * GLA_DK ** -0.5
        dirs = tuple(
            (q, k, v, (jax.nn.log_sigmoid(lr.astype(jnp.float32) @ a2[d] + ab[d]) / GLA_TAU).reshape(B_, T, GLA_H, GLA_DK))
            for d, lr in enumerate((lr_f, lr_b)))
        return dirs, g

    lat_dirs, g_l = prep(pl, True)
    ctx_dirs, g_c = prep(pc, False)
    o_l, o_c = bidir_scan(gla_chunk, lat_dirs, ctx_dirs, with_ctx_out)

    def finish(o, g):
        B_, T = g.shape[:2]
        gate = jax.nn.silu(g.astype(jnp.float32)).reshape(B_, T, GLA_H, GLA_DV)
        return (rmsnorm(o, norm_g) * gate).reshape(B_, T, BRANCH_W).astype(g.dtype)

    return finish(o_l, g_l), (finish(o_c, g_c) if with_ctx_out else None)


def na_mixer(pl, pc, rpb, rows, with_ctx_out):
    B_, T = pl.shape[:2]
    Lc = pc.shape[1]
    scale = NA_D ** -0.5
    q, k, v = (t.reshape(B_, T, NA_H, NA_D) for t in split_cols(pl, [BRANCH_W] * 3))
    qc, kc, vc = (t.reshape(B_, Lc, NA_H, NA_D) for t in split_cols(pc, [BRANCH_W] * 3))
    kr = min(NA_WIN_R, rows)
    r = jnp.arange(rows)
    row_idx = jnp.clip(r - NA_WIN_R // 2, 0, rows - kr)[:, None] + jnp.arange(kr)[None, :]
    ci = jnp.arange(GRID_W)
    c0 = jnp.clip(ci - NA_WIN_C // 2, 0, GRID_W - NA_WIN_C)
    col_ok = (ci[None, :] >= c0[:, None]) & (ci[None, :] < c0[:, None] + NA_WIN_C)
    dr = row_idx - r[:, None] + NA_WIN_R - 1
    dc = jnp.clip(ci[None, :] - ci[:, None], 1 - NA_WIN_C, NA_WIN_C - 1) + NA_WIN_C - 1
    bias = rpb[:, dr[:, None, :, None], dc[None, :, None, :]]
    qg = q.reshape(B_, rows, GRID_W, NA_H, NA_D) * scale
    kg = k.reshape(B_, rows, GRID_W, NA_H, NA_D)[:, row_idx]
    vg = v.reshape(B_, rows, GRID_W, NA_H, NA_D)[:, row_idx]
    s_nb = jnp.einsum('bnihd,bnrjhd->bhnirj', qg, kg).astype(jnp.float32) + bias.astype(jnp.float32)[None]
    s_nb = jnp.where(col_ok[:, None, :], s_nb, NEG_INF)
    s_cx = jnp.einsum('bnihd,bchd->bhnic', qg, kc).astype(jnp.float32)
    s = jnp.concatenate([s_nb.reshape(B_, NA_H, rows, GRID_W, kr * GRID_W), s_cx], axis=-1)
    p = jax.nn.softmax(s, axis=-1).astype(v.dtype)
    p_nb = p[..., :kr * GRID_W].reshape(B_, NA_H, rows, GRID_W, kr, GRID_W)
    o = jnp.einsum('bhnirj,bnrjhd->bnihd', p_nb, vg) + jnp.einsum('bhnic,bchd->bnihd', p[..., kr * GRID_W:], vc)
    y_lat = o.reshape(B_, T, BRANCH_W)
    if not with_ctx_out:
        return y_lat, None
    s_c = jnp.einsum('bqhd,bkhd->bhqk', qc * scale, kc).astype(jnp.float32)
    p_c = jax.nn.softmax(s_c, axis=-1).astype(vc.dtype)
    y_ctx = jnp.einsum('bhqk,bkhd->bqhd', p_c, vc).reshape(B_, Lc, BRANCH_W)
    return y_lat, y_ctx


def gdn_mixer(pl, pc, conv_w, a_log, dt_bias, norm_g, with_ctx_out):
    def prep(p):
        B_, T = p.shape[:2]
        qkv, z, b_f, b_b, a_f, a_b = split_cols(p, [3 * BRANCH_W, BRANCH_W] + [GDN_H] * 4)
        qkv = jax.nn.silu(short_conv(qkv, conv_w))
        q, k, v = (t.reshape(B_, T, GDN_H, GDN_D) for t in split_cols(qkv, [BRANCH_W] * 3))
        q = l2norm(q) * GDN_D ** -0.5
        k = l2norm(k)
        dirs = tuple(
            (q, k, v, jax.nn.sigmoid(bb.astype(jnp.float32)),
             -jnp.exp(a_log[d].astype(jnp.float32)) * jax.nn.softplus(aa.astype(jnp.float32) + dt_bias[d]))
            for d, (bb, aa) in enumerate(((b_f, a_f), (b_b, a_b))))
        return dirs, z

    lat_dirs, z_l = prep(pl)
    ctx_dirs, z_c = prep(pc)
    o_l, o_c = bidir_scan(gdn_chunk, lat_dirs, ctx_dirs, with_ctx_out)

    def finish(o, z):
        B_, T = z.shape[:2]
        gate = jax.nn.silu(z.astype(jnp.float32)).reshape(B_, T, GDN_H, GDN_D)
        return (rmsnorm(o, norm_g) * gate).reshape(B_, T, BRANCH_W).astype(z.dtype)

    return finish(o_l, z_l), (finish(o_c, z_c) if with_ctx_out else None)


def m2_mixer(pl, pc, conv_w, conv_b, a_log, dt_bias, d_skip, norm_g, with_ctx_out):
    def prep(p):
        B_, T = p.shape[:2]
        z, xbc, dt_f, dt_b = split_cols(p, [BRANCH_W, M2_CONV_CH, M2_H, M2_H])
        xbc = jax.nn.silu(short_conv(xbc, conv_w) + conv_b)
        xs, bm, cm = split_cols(xbc, [BRANCH_W, M2_G * M2_N, M2_G * M2_N])
        xs = xs.reshape(B_, T, M2_H, M2_P)
        bm = jnp.repeat(bm.reshape(B_, T, M2_G, M2_N), M2_H // M2_G, axis=2)
        cm = jnp.repeat(cm.reshape(B_, T, M2_G, M2_N), M2_H // M2_G, axis=2)
        dirs = []
        for d, dt_raw in enumerate((dt_f, dt_b)):
            dt = jax.nn.softplus(dt_raw.astype(jnp.float32) + dt_bias[d])
            dirs.append((cm, bm, xs.astype(jnp.float32) * dt[..., None], -jnp.exp(a_log[d].astype(jnp.float32)) * dt))
        return tuple(dirs), z, xs

    lat_dirs, z_l, xs_l = prep(pl)
    ctx_dirs, z_c, xs_c = prep(pc)
    o_l, o_c = bidir_scan(ssd_chunk, lat_dirs, ctx_dirs, with_ctx_out)

    def finish(o, z, xs):
        B_, T = z.shape[:2]
        y = (o + d_skip.astype(jnp.float32)[:, None] * xs.astype(jnp.float32)).reshape(B_, T, BRANCH_W)
        return rmsnorm(y * jax.nn.silu(z.astype(jnp.float32)), norm_g).astype(z.dtype)

    return finish(o_l, z_l, xs_l), (finish(o_c, z_c, xs_c) if with_ctx_out else None)


def merge_branches(branches, gate_cols, w_branch, b_merge, w_out):
    gates = jax.nn.sigmoid(gate_cols.reshape(gate_cols.shape[:-1] + (N_BRANCH, D_MODEL)) + b_merge)
    merged = gates[..., 0, :] * (branches[0] @ w_branch[0])
    for i in range(1, N_BRANCH):
        merged = merged + gates[..., i, :] * (branches[i] @ w_branch[i])
    return merged @ w_out


def swiglu(h, w1, w3, w2):
    return (jax.nn.silu(h @ w1) * (h @ w3)) @ w2


def token_mixers(h, hc, p, rows, cos, sin, with_ctx_out):
    sizes = [GLA_IN, NA_IN, GDN_IN, M2_IN, GATE_IN]
    gla_l, na_l, gdn_l, m2_l, gate_l = split_cols(h @ p['w_in'], sizes)
    gla_c, na_c, gdn_c, m2_c, gate_c = split_cols(hc @ p['w_in'], sizes)
    ya = gla_mixer(gla_l, gla_c, p['gla_a2'], p['gla_ab'], p['gla_norm_g'], cos, sin, with_ctx_out)
    yb = na_mixer(na_l, na_c, p['na_rpb'], rows, with_ctx_out)
    yc = gdn_mixer(gdn_l, gdn_c, p['gdn_conv'], p['gdn_a_log'], p['gdn_dt_bias'], p['gdn_norm_g'], with_ctx_out)
    yd = m2_mixer(m2_l, m2_c, p['m2_conv'], p['m2_conv_b'], p['m2_a_log'], p['m2_dt_bias'], p['m2_d'],
                  p['m2_norm_g'], with_ctx_out)
    y_lat = merge_branches((ya[0], yb[0], yc[0], yd[0]), gate_l, p['w_branch'], p['b_merge'], p['w_out'])
    if not with_ctx_out:
        return y_lat, None
    y_ctx = merge_branches((ya[1], yb[1], yc[1], yd[1]), gate_c, p['w_branch'], p['b_merge'], p['w_out'])
    return y_lat, y_ctx


def _fwd_setup_inputs(seed: int = 0) -> dict:
    key = jax.random.key(seed)
    keys = iter(jax.random.split(key, 48))

    def nrm(shape, scale):
        return scale * jax.random.normal(next(keys), shape, jnp.float32)

    def gain(shape):
        return 1.0 + nrm(shape, 0.1)

    def a_log(shape):
        return jnp.log(jax.random.uniform(next(keys), shape, jnp.float32, 1.0, 16.0))

    def dt_bias(shape):
        dt = jnp.exp(jax.random.uniform(next(keys), shape, jnp.float32, math.log(1e-3), math.log(1e-1)))
        return dt + jnp.log(-jnp.expm1(-dt))

    L, D = DEPTH, D_MODEL
    return {
        'x': nrm((BATCH, SEQ, D), 1.0),
        'c': nrm((BATCH, D), 1.0),
        'ctx': nrm((BATCH, CTX_LEN, D), 1.0),
        'c_ctx': nrm((D,), 1.0),
        'norm1_g': gain((L, D)),
        'norm2_g': gain((L, D)),
        'w_ada': nrm((L, D, 6 * D), 0.5 * D ** -0.5),
        'b_ada': nrm((L, 6 * D), 0.02),
        'w_in': nrm((L, D, IN_COLS), D ** -0.5),
        'b_merge': nrm((L, N_BRANCH, D), 0.1),
        'gla_a2': nrm((L, 2, GLA_LR, GLA_H * GLA_DK), GLA_LR ** -0.5),
        'gla_ab': nrm((L, 2, GLA_H * GLA_DK), 0.1),
        'gla_norm_g': gain((L, GLA_DV)),
        'na_rpb': nrm((L, NA_H, 2 * NA_WIN_R - 1, 2 * NA_WIN_C - 1), 0.1),
        'gdn_conv': nrm((L, CONV_W, 3 * BRANCH_W), CONV_W ** -0.5),
        'gdn_a_log': a_log((L, 2, GDN_H)),
        'gdn_dt_bias': dt_bias((L, 2, GDN_H)),
        'gdn_norm_g': gain((L, GDN_D)),
        'm2_conv': nrm((L, CONV_W, M2_CONV_CH), CONV_W ** -0.5),
        'm2_conv_b': nrm((L, M2_CONV_CH), 0.02),
        'm2_a_log': a_log((L, 2, M2_H)),
        'm2_dt_bias': dt_bias((L, 2, M2_H)),
        'm2_d': gain((L, M2_H)),
        'm2_norm_g': gain((L, BRANCH_W)),
        'w_branch': nrm((L, N_BRANCH, BRANCH_W, D), BRANCH_W ** -0.5),
        'w_out': nrm((L, D, D), D ** -0.5),
        'w_ffn1': nrm((L, D, D_FF), D ** -0.5),
        'w_ffn3': nrm((L, D, D_FF), D ** -0.5),
        'w_ffn2': nrm((L, D_FF, D), D_FF ** -0.5),
        'final_norm_g': gain((D,)),
    }


def _fwd_reference(x, c, ctx, c_ctx, norm1_g, norm2_g, w_ada, b_ada, w_in, b_merge, gla_a2, gla_ab, gla_norm_g,
              na_rpb, gdn_conv, gdn_a_log, gdn_dt_bias, gdn_norm_g, m2_conv, m2_conv_b, m2_a_log, m2_dt_bias,
              m2_d, m2_norm_g, w_branch, w_out, w_ffn1, w_ffn3, w_ffn2, final_norm_g):
    T = x.shape[1]
    rows = T // GRID_W
    cos, sin = axial_rope(T, GLA_DK)
    for l in range(DEPTH):
        last = l == DEPTH - 1
        p = {
            'w_in': w_in[l], 'b_merge': b_merge[l], 'gla_a2': gla_a2[l], 'gla_ab': gla_ab[l],
            'gla_norm_g': gla_norm_g[l], 'na_rpb': na_rpb[l], 'gdn_conv': gdn_conv[l],
            'gdn_a_log': gdn_a_log[l], 'gdn_dt_bias': gdn_dt_bias[l], 'gdn_norm_g': gdn_norm_g[l],
            'm2_conv': m2_conv[l], 'm2_conv_b': m2_conv_b[l], 'm2_a_log': m2_a_log[l],
            'm2_dt_bias': m2_dt_bias[l], 'm2_d': m2_d[l], 'm2_norm_g': m2_norm_g[l],
            'w_branch': w_branch[l], 'w_out': w_out[l],
        }
        sh1, sc1, g1, sh2, sc2, g2 = jnp.split((jax.nn.silu(c) @ w_ada[l] + b_ada[l])[:, None, :], 6, axis=-1)
        csh1, csc1, cg1, csh2, csc2, cg2 = jnp.split(jax.nn.silu(c_ctx) @ w_ada[l] + b_ada[l], 6, axis=-1)
        h = rmsnorm(x, norm1_g[l]) * (1 + sc1) + sh1
        hc = rmsnorm(ctx, norm1_g[l]) * (1 + csc1) + csh1
        y, yc = token_mixers(h, hc, p, rows, cos, sin, not last)
        x = x + g1 * y
        x = x + g2 * swiglu(rmsnorm(x, norm2_g[l]) * (1 + sc2) + sh2, w_ffn1[l], w_ffn3[l], w_ffn2[l])
        if not last:
            ctx = ctx + cg1 * yc
            ctx = ctx + cg2 * swiglu(rmsnorm(ctx, norm2_g[l]) * (1 + csc2) + csh2, w_ffn1[l], w_ffn3[l], w_ffn2[l])
    return rmsnorm(x, final_norm_g)


import jax as _jax
import jax.numpy as _jnp

TWIN_FORMAT = 'train_step'
FWD_PARAMS = ['x', 'c', 'ctx', 'c_ctx', 'norm1_g', 'norm2_g', 'w_ada', 'b_ada', 'w_in', 'b_merge', 'gla_a2', 'gla_ab', 'gla_norm_g', 'na_rpb', 'gdn_conv', 'gdn_a_log', 'gdn_dt_bias', 'gdn_norm_g', 'm2_conv', 'm2_conv_b', 'm2_a_log', 'm2_dt_bias', 'm2_d', 'm2_norm_g', 'w_branch', 'w_out', 'w_ffn1', 'w_ffn3', 'w_ffn2', 'final_norm_g']
TWIN_WEIGHTS = ['c_ctx', 'norm1_g', 'norm2_g', 'w_ada', 'b_ada', 'w_in', 'b_merge', 'gla_a2', 'gla_ab', 'gla_norm_g', 'na_rpb', 'gdn_conv', 'gdn_a_log', 'gdn_dt_bias', 'gdn_norm_g', 'm2_conv', 'm2_conv_b', 'm2_a_log', 'm2_dt_bias', 'm2_d', 'm2_norm_g', 'w_branch', 'w_out', 'w_ffn1', 'w_ffn3', 'w_ffn2', 'final_norm_g']
TWIN_DIFF_INPUT = 'x'
TWIN_INPUTS = ['x', 'c', 'ctx', 'c_ctx', 'norm1_g', 'norm2_g', 'w_ada', 'b_ada', 'w_in', 'b_merge', 'gla_a2', 'gla_ab', 'gla_norm_g', 'na_rpb', 'gdn_conv', 'gdn_a_log', 'gdn_dt_bias', 'gdn_norm_g', 'm2_conv', 'm2_conv_b', 'm2_a_log', 'm2_dt_bias', 'm2_d', 'm2_norm_g', 'w_branch', 'w_out', 'w_ffn1', 'w_ffn3', 'w_ffn2', 'final_norm_g', 'loss_target', 'm_c_ctx', 'm_norm1_g', 'm_norm2_g', 'm_w_ada', 'm_b_ada', 'm_w_in', 'm_b_merge', 'm_gla_a2', 'm_gla_ab', 'm_gla_norm_g', 'm_na_rpb', 'm_gdn_conv', 'm_gdn_a_log', 'm_gdn_dt_bias', 'm_gdn_norm_g', 'm_m2_conv', 'm_m2_conv_b', 'm_m2_a_log', 'm_m2_dt_bias', 'm_m2_d', 'm_m2_norm_g', 'm_w_branch', 'm_w_out', 'm_w_ffn1', 'm_w_ffn3', 'm_w_ffn2', 'm_final_norm_g', 'v_c_ctx', 'v_norm1_g', 'v_norm2_g', 'v_w_ada', 'v_b_ada', 'v_w_in', 'v_b_merge', 'v_gla_a2', 'v_gla_ab', 'v_gla_norm_g', 'v_na_rpb', 'v_gdn_conv', 'v_gdn_a_log', 'v_gdn_dt_bias', 'v_gdn_norm_g', 'v_m2_conv', 'v_m2_conv_b', 'v_m2_a_log', 'v_m2_dt_bias', 'v_m2_d', 'v_m2_norm_g', 'v_w_branch', 'v_w_out', 'v_w_ffn1', 'v_w_ffn3', 'v_w_ffn2', 'v_final_norm_g']
TWIN_OUTPUTS = ['loss', 'grad_x', 'grad_c_ctx', 'grad_norm1_g', 'grad_norm2_g', 'grad_w_ada', 'grad_b_ada', 'grad_w_in', 'grad_b_merge', 'grad_gla_a2', 'grad_gla_ab', 'grad_gla_norm_g', 'grad_na_rpb', 'grad_gdn_conv', 'grad_gdn_a_log', 'grad_gdn_dt_bias', 'grad_gdn_norm_g', 'grad_m2_conv', 'grad_m2_conv_b', 'grad_m2_a_log', 'grad_m2_dt_bias', 'grad_m2_d', 'grad_m2_norm_g', 'grad_w_branch', 'grad_w_out', 'grad_w_ffn1', 'grad_w_ffn3', 'grad_w_ffn2', 'grad_final_norm_g', 'delta_c_ctx', 'delta_norm1_g', 'delta_norm2_g', 'delta_w_ada', 'delta_b_ada', 'delta_w_in', 'delta_b_merge', 'delta_gla_a2', 'delta_gla_ab', 'delta_gla_norm_g', 'delta_na_rpb', 'delta_gdn_conv', 'delta_gdn_a_log', 'delta_gdn_dt_bias', 'delta_gdn_norm_g', 'delta_m2_conv', 'delta_m2_conv_b', 'delta_m2_a_log', 'delta_m2_dt_bias', 'delta_m2_d', 'delta_m2_norm_g', 'delta_w_branch', 'delta_w_out', 'delta_w_ffn1', 'delta_w_ffn3', 'delta_w_ffn2', 'delta_final_norm_g', 'new_m_c_ctx', 'new_m_norm1_g', 'new_m_norm2_g', 'new_m_w_ada', 'new_m_b_ada', 'new_m_w_in', 'new_m_b_merge', 'new_m_gla_a2', 'new_m_gla_ab', 'new_m_gla_norm_g', 'new_m_na_rpb', 'new_m_gdn_conv', 'new_m_gdn_a_log', 'new_m_gdn_dt_bias', 'new_m_gdn_norm_g', 'new_m_m2_conv', 'new_m_m2_conv_b', 'new_m_m2_a_log', 'new_m_m2_dt_bias', 'new_m_m2_d', 'new_m_m2_norm_g', 'new_m_w_branch', 'new_m_w_out', 'new_m_w_ffn1', 'new_m_w_ffn3', 'new_m_w_ffn2', 'new_m_final_norm_g', 'new_v_c_ctx', 'new_v_norm1_g', 'new_v_norm2_g', 'new_v_w_ada', 'new_v_b_ada', 'new_v_w_in', 'new_v_b_merge', 'new_v_gla_a2', 'new_v_gla_ab', 'new_v_gla_norm_g', 'new_v_na_rpb', 'new_v_gdn_conv', 'new_v_gdn_a_log', 'new_v_gdn_dt_bias', 'new_v_gdn_norm_g', 'new_v_m2_conv', 'new_v_m2_conv_b', 'new_v_m2_a_log', 'new_v_m2_dt_bias', 'new_v_m2_d', 'new_v_m2_norm_g', 'new_v_w_branch', 'new_v_w_out', 'new_v_w_ffn1', 'new_v_w_ffn3', 'new_v_w_ffn2', 'new_v_final_norm_g']
TWIN_LEAF_KINDS = {'loss': 'loss', 'grad_x': 'grad_x', 'grad_c_ctx': 'grad_w', 'grad_norm1_g': 'grad_w', 'grad_norm2_g': 'grad_w', 'grad_w_ada': 'grad_w', 'grad_b_ada': 'grad_w', 'grad_w_in': 'grad_w', 'grad_b_merge': 'grad_w', 'grad_gla_a2': 'grad_w', 'grad_gla_ab': 'grad_w', 'grad_gla_norm_g': 'grad_w', 'grad_na_rpb': 'grad_w', 'grad_gdn_conv': 'grad_w', 'grad_gdn_a_log': 'grad_w', 'grad_gdn_dt_bias': 'grad_w', 'grad_gdn_norm_g': 'grad_w', 'grad_m2_conv': 'grad_w', 'grad_m2_conv_b': 'grad_w', 'grad_m2_a_log': 'grad_w', 'grad_m2_dt_bias': 'grad_w', 'grad_m2_d': 'grad_w', 'grad_m2_norm_g': 'grad_w', 'grad_w_branch': 'grad_w', 'grad_w_out': 'grad_w', 'grad_w_ffn1': 'grad_w', 'grad_w_ffn3': 'grad_w', 'grad_w_ffn2': 'grad_w', 'grad_final_norm_g': 'grad_w', 'delta_c_ctx': 'delta_w', 'delta_norm1_g': 'delta_w', 'delta_norm2_g': 'delta_w', 'delta_w_ada': 'delta_w', 'delta_b_ada': 'delta_w', 'delta_w_in': 'delta_w', 'delta_b_merge': 'delta_w', 'delta_gla_a2': 'delta_w', 'delta_gla_ab': 'delta_w', 'delta_gla_norm_g': 'delta_w', 'delta_na_rpb': 'delta_w', 'delta_gdn_conv': 'delta_w', 'delta_gdn_a_log': 'delta_w', 'delta_gdn_dt_bias': 'delta_w', 'delta_gdn_norm_g': 'delta_w', 'delta_m2_conv': 'delta_w', 'delta_m2_conv_b': 'delta_w', 'delta_m2_a_log': 'delta_w', 'delta_m2_dt_bias': 'delta_w', 'delta_m2_d': 'delta_w', 'delta_m2_norm_g': 'delta_w', 'delta_w_branch': 'delta_w', 'delta_w_out': 'delta_w', 'delta_w_ffn1': 'delta_w', 'delta_w_ffn3': 'delta_w', 'delta_w_ffn2': 'delta_w', 'delta_final_norm_g': 'delta_w', 'new_m_c_ctx': 'new_m', 'new_m_norm1_g': 'new_m', 'new_m_norm2_g': 'new_m', 'new_m_w_ada': 'new_m', 'new_m_b_ada': 'new_m', 'new_m_w_in': 'new_m', 'new_m_b_merge': 'new_m', 'new_m_gla_a2': 'new_m', 'new_m_gla_ab': 'new_m', 'new_m_gla_norm_g': 'new_m', 'new_m_na_rpb': 'new_m', 'new_m_gdn_conv': 'new_m', 'new_m_gdn_a_log': 'new_m', 'new_m_gdn_dt_bias': 'new_m', 'new_m_gdn_norm_g': 'new_m', 'new_m_m2_conv': 'new_m', 'new_m_m2_conv_b': 'new_m', 'new_m_m2_a_log': 'new_m', 'new_m_m2_dt_bias': 'new_m', 'new_m_m2_d': 'new_m', 'new_m_m2_norm_g': 'new_m', 'new_m_w_branch': 'new_m', 'new_m_w_out': 'new_m', 'new_m_w_ffn1': 'new_m', 'new_m_w_ffn3': 'new_m', 'new_m_w_ffn2': 'new_m', 'new_m_final_norm_g': 'new_m', 'new_v_c_ctx': 'new_v', 'new_v_norm1_g': 'new_v', 'new_v_norm2_g': 'new_v', 'new_v_w_ada': 'new_v', 'new_v_b_ada': 'new_v', 'new_v_w_in': 'new_v', 'new_v_b_merge': 'new_v', 'new_v_gla_a2': 'new_v', 'new_v_gla_ab': 'new_v', 'new_v_gla_norm_g': 'new_v', 'new_v_na_rpb': 'new_v', 'new_v_gdn_conv': 'new_v', 'new_v_gdn_a_log': 'new_v', 'new_v_gdn_dt_bias': 'new_v', 'new_v_gdn_norm_g': 'new_v', 'new_v_m2_conv': 'new_v', 'new_v_m2_conv_b': 'new_v', 'new_v_m2_a_log': 'new_v', 'new_v_m2_dt_bias': 'new_v', 'new_v_m2_d': 'new_v', 'new_v_m2_norm_g': 'new_v', 'new_v_w_branch': 'new_v', 'new_v_w_out': 'new_v', 'new_v_w_ffn1': 'new_v', 'new_v_w_ffn3': 'new_v', 'new_v_w_ffn2': 'new_v', 'new_v_final_norm_g': 'new_v'}


def _forward(args):
    return _fwd_reference(*[args[k] for k in FWD_PARAMS])


def _output_shape():
    def fwd():
        inp = _fwd_setup_inputs(0)
        return _fwd_reference(*[inp[k] for k in FWD_PARAMS])
    out = _jax.eval_shape(fwd)
    return out.shape, out.dtype

N_MICROBATCH = 1
ADAM_LR = 0.001
ADAM_B1 = 0.9
ADAM_B2 = 0.999
ADAM_EPS = 1e-08
ADAM_WD = 0.01
ADAM_STEP = 10
PER_EXAMPLE_BATCH_AXIS = {'x': 0, 'c': 0, 'ctx': 0, 'loss_target': 0}
SHARED_INPUTS = []
_WEIGHT_DTYPES = {'c_ctx': _jnp.float32, 'norm1_g': _jnp.float32, 'norm2_g': _jnp.float32, 'w_ada': _jnp.float32, 'b_ada': _jnp.float32, 'w_in': _jnp.float32, 'b_merge': _jnp.float32, 'gla_a2': _jnp.float32, 'gla_ab': _jnp.float32, 'gla_norm_g': _jnp.float32, 'na_rpb': _jnp.float32, 'gdn_conv': _jnp.float32, 'gdn_a_log': _jnp.float32, 'gdn_dt_bias': _jnp.float32, 'gdn_norm_g': _jnp.float32, 'm2_conv': _jnp.float32, 'm2_conv_b': _jnp.float32, 'm2_a_log': _jnp.float32, 'm2_dt_bias': _jnp.float32, 'm2_d': _jnp.float32, 'm2_norm_g': _jnp.float32, 'w_branch': _jnp.float32, 'w_out': _jnp.float32, 'w_ffn1': _jnp.float32, 'w_ffn3': _jnp.float32, 'w_ffn2': _jnp.float32, 'final_norm_g': _jnp.float32}
MOMENT_SCALE = {'c_ctx': 9.426367e-03, 'norm1_g': 4.649808e-02, 'norm2_g': 3.655674e-02, 'w_ada': 8.316449e-02, 'b_ada': 1.852442e-01, 'w_in': 1.890638e-02, 'b_merge': 6.343009e-03, 'gla_a2': 5.139722e-03, 'gla_ab': 1.291290e-02, 'gla_norm_g': 5.614107e-02, 'na_rpb': 1.883136e-03, 'gdn_conv': 2.143282e-02, 'gdn_a_log': 8.153322e-02, 'gdn_dt_bias': 8.187961e-02, 'gdn_norm_g': 6.644289e-02, 'm2_conv': 3.106631e-02, 'm2_conv_b': 4.818680e-02, 'm2_a_log': 1.104240e-01, 'm2_dt_bias': 5.509141e-02, 'm2_d': 2.010142e-01, 'm2_norm_g': 4.639032e-02, 'w_branch': 1.653630e-02, 'w_out': 3.336478e-02, 'w_ffn1': 1.668415e-02, 'w_ffn3': 1.625261e-02, 'w_ffn2': 2.705626e-02, 'final_norm_g': 3.211105e+01}


def _to_microbatches(a, axis):
    t = _jnp.moveaxis(a, axis, 0)
    t = t.reshape((N_MICROBATCH, t.shape[0] // N_MICROBATCH) + t.shape[1:])
    return _jnp.moveaxis(t, 1, axis + 1)


def setup_inputs(seed: int = 0) -> dict:
    inp = _fwd_setup_inputs(seed)
    key = _jax.random.fold_in(_jax.random.key(seed), 7919)
    shape, _ = _output_shape()
    out = dict(inp)
    out["loss_target"] = _jax.random.normal(_jax.random.fold_in(key, 0), shape, _jnp.float32)
    for i, name in enumerate(TWIN_WEIGHTS):
        w = inp[name].astype(_jnp.float32)
        if MOMENT_SCALE is None:
            s = _jnp.sqrt(_jnp.mean(_jnp.square(w)) + 1e-30)
        else:
            s = MOMENT_SCALE[name]
        km, kv = _jax.random.split(_jax.random.fold_in(key, i + 1))
        out[name] = w
        out["m_" + name] = s * _jax.random.normal(km, w.shape, _jnp.float32)
        out["v_" + name] = (s * s) * _jax.random.uniform(kv, w.shape, _jnp.float32, 0.5, 1.5)
    if N_MICROBATCH > 1:
        for name, axis in PER_EXAMPLE_BATCH_AXIS.items():
            out[name] = _to_microbatches(out[name], axis)
    return {'x': out['x'], 'c': out['c'], 'ctx': out['ctx'], 'c_ctx': out['c_ctx'], 'norm1_g': out['norm1_g'], 'norm2_g': out['norm2_g'], 'w_ada': out['w_ada'], 'b_ada': out['b_ada'], 'w_in': out['w_in'], 'b_merge': out['b_merge'], 'gla_a2': out['gla_a2'], 'gla_ab': out['gla_ab'], 'gla_norm_g': out['gla_norm_g'], 'na_rpb': out['na_rpb'], 'gdn_conv': out['gdn_conv'], 'gdn_a_log': out['gdn_a_log'], 'gdn_dt_bias': out['gdn_dt_bias'], 'gdn_norm_g': out['gdn_norm_g'], 'm2_conv': out['m2_conv'], 'm2_conv_b': out['m2_conv_b'], 'm2_a_log': out['m2_a_log'], 'm2_dt_bias': out['m2_dt_bias'], 'm2_d': out['m2_d'], 'm2_norm_g': out['m2_norm_g'], 'w_branch': out['w_branch'], 'w_out': out['w_out'], 'w_ffn1': out['w_ffn1'], 'w_ffn3': out['w_ffn3'], 'w_ffn2': out['w_ffn2'], 'final_norm_g': out['final_norm_g'], 'loss_target': out['loss_target'], 'm_c_ctx': out['m_c_ctx'], 'm_norm1_g': out['m_norm1_g'], 'm_norm2_g': out['m_norm2_g'], 'm_w_ada': out['m_w_ada'], 'm_b_ada': out['m_b_ada'], 'm_w_in': out['m_w_in'], 'm_b_merge': out['m_b_merge'], 'm_gla_a2': out['m_gla_a2'], 'm_gla_ab': out['m_gla_ab'], 'm_gla_norm_g': out['m_gla_norm_g'], 'm_na_rpb': out['m_na_rpb'], 'm_gdn_conv': out['m_gdn_conv'], 'm_gdn_a_log': out['m_gdn_a_log'], 'm_gdn_dt_bias': out['m_gdn_dt_bias'], 'm_gdn_norm_g': out['m_gdn_norm_g'], 'm_m2_conv': out['m_m2_conv'], 'm_m2_conv_b': out['m_m2_conv_b'], 'm_m2_a_log': out['m_m2_a_log'], 'm_m2_dt_bias': out['m_m2_dt_bias'], 'm_m2_d': out['m_m2_d'], 'm_m2_norm_g': out['m_m2_norm_g'], 'm_w_branch': out['m_w_branch'], 'm_w_out': out['m_w_out'], 'm_w_ffn1': out['m_w_ffn1'], 'm_w_ffn3': out['m_w_ffn3'], 'm_w_ffn2': out['m_w_ffn2'], 'm_final_norm_g': out['m_final_norm_g'], 'v_c_ctx': out['v_c_ctx'], 'v_norm1_g': out['v_norm1_g'], 'v_norm2_g': out['v_norm2_g'], 'v_w_ada': out['v_w_ada'], 'v_b_ada': out['v_b_ada'], 'v_w_in': out['v_w_in'], 'v_b_merge': out['v_b_merge'], 'v_gla_a2': out['v_gla_a2'], 'v_gla_ab': out['v_gla_ab'], 'v_gla_norm_g': out['v_gla_norm_g'], 'v_na_rpb': out['v_na_rpb'], 'v_gdn_conv': out['v_gdn_conv'], 'v_gdn_a_log': out['v_gdn_a_log'], 'v_gdn_dt_bias': out['v_gdn_dt_bias'], 'v_gdn_norm_g': out['v_gdn_norm_g'], 'v_m2_conv': out['v_m2_conv'], 'v_m2_conv_b': out['v_m2_conv_b'], 'v_m2_a_log': out['v_m2_a_log'], 'v_m2_dt_bias': out['v_m2_dt_bias'], 'v_m2_d': out['v_m2_d'], 'v_m2_norm_g': out['v_m2_norm_g'], 'v_w_branch': out['v_w_branch'], 'v_w_out': out['v_w_out'], 'v_w_ffn1': out['v_w_ffn1'], 'v_w_ffn3': out['v_w_ffn3'], 'v_w_ffn2': out['v_w_ffn2'], 'v_final_norm_g': out['v_final_norm_g']}


def _loss(weights, diff, rest, loss_target):
    with _jax.named_scope("forward"):
        args = {**rest, TWIN_DIFF_INPUT: diff, **{k: w.astype(_WEIGHT_DTYPES[k]) for k, w in weights.items()}}
        y = _forward(args)
    with _jax.named_scope("loss_head"):
        err = _jnp.square(y.astype(_jnp.float32) - loss_target)
        return 0.5 * _jnp.sum(_jnp.mean(err, axis=-1)) if err.ndim else 0.5 * err


def _adamw(w, g, m, v):
    m = ADAM_B1 * m + (1.0 - ADAM_B1) * g
    v = ADAM_B2 * v + (1.0 - ADAM_B2) * _jnp.square(g)
    m_hat = m / (1.0 - ADAM_B1 ** ADAM_STEP)
    v_hat = v / (1.0 - ADAM_B2 ** ADAM_STEP)
    delta = -ADAM_LR * (m_hat / (_jnp.sqrt(v_hat) + ADAM_EPS) + ADAM_WD * w)
    return delta, m, v


def reference(x, c, ctx, c_ctx, norm1_g, norm2_g, w_ada, b_ada, w_in, b_merge, gla_a2, gla_ab, gla_norm_g, na_rpb, gdn_conv, gdn_a_log, gdn_dt_bias, gdn_norm_g, m2_conv, m2_conv_b, m2_a_log, m2_dt_bias, m2_d, m2_norm_g, w_branch, w_out, w_ffn1, w_ffn3, w_ffn2, final_norm_g, loss_target, m_c_ctx, m_norm1_g, m_norm2_g, m_w_ada, m_b_ada, m_w_in, m_b_merge, m_gla_a2, m_gla_ab, m_gla_norm_g, m_na_rpb, m_gdn_conv, m_gdn_a_log, m_gdn_dt_bias, m_gdn_norm_g, m_m2_conv, m_m2_conv_b, m_m2_a_log, m_m2_dt_bias, m_m2_d, m_m2_norm_g, m_w_branch, m_w_out, m_w_ffn1, m_w_ffn3, m_w_ffn2, m_final_norm_g, v_c_ctx, v_norm1_g, v_norm2_g, v_w_ada, v_b_ada, v_w_in, v_b_merge, v_gla_a2, v_gla_ab, v_gla_norm_g, v_na_rpb, v_gdn_conv, v_gdn_a_log, v_gdn_dt_bias, v_gdn_norm_g, v_m2_conv, v_m2_conv_b, v_m2_a_log, v_m2_dt_bias, v_m2_d, v_m2_norm_g, v_w_branch, v_w_out, v_w_ffn1, v_w_ffn3, v_w_ffn2, v_final_norm_g):
    given = dict(x=x, c=c, ctx=ctx, c_ctx=c_ctx, norm1_g=norm1_g, norm2_g=norm2_g, w_ada=w_ada, b_ada=b_ada, w_in=w_in, b_merge=b_merge, gla_a2=gla_a2, gla_ab=gla_ab, gla_norm_g=gla_norm_g, na_rpb=na_rpb, gdn_conv=gdn_conv, gdn_a_log=gdn_a_log, gdn_dt_bias=gdn_dt_bias, gdn_norm_g=gdn_norm_g, m2_conv=m2_conv, m2_conv_b=m2_conv_b, m2_a_log=m2_a_log, m2_dt_bias=m2_dt_bias, m2_d=m2_d, m2_norm_g=m2_norm_g, w_branch=w_branch, w_out=w_out, w_ffn1=w_ffn1, w_ffn3=w_ffn3, w_ffn2=w_ffn2, final_norm_g=final_norm_g, loss_target=loss_target, m_c_ctx=m_c_ctx, m_norm1_g=m_norm1_g, m_norm2_g=m_norm2_g, m_w_ada=m_w_ada, m_b_ada=m_b_ada, m_w_in=m_w_in, m_b_merge=m_b_merge, m_gla_a2=m_gla_a2, m_gla_ab=m_gla_ab, m_gla_norm_g=m_gla_norm_g, m_na_rpb=m_na_rpb, m_gdn_conv=m_gdn_conv, m_gdn_a_log=m_gdn_a_log, m_gdn_dt_bias=m_gdn_dt_bias, m_gdn_norm_g=m_gdn_norm_g, m_m2_conv=m_m2_conv, m_m2_conv_b=m_m2_conv_b, m_m2_a_log=m_m2_a_log, m_m2_dt_bias=m_m2_dt_bias, m_m2_d=m_m2_d, m_m2_norm_g=m_m2_norm_g, m_w_branch=m_w_branch, m_w_out=m_w_out, m_w_ffn1=m_w_ffn1, m_w_ffn3=m_w_ffn3, m_w_ffn2=m_w_ffn2, m_final_norm_g=m_final_norm_g, v_c_ctx=v_c_ctx, v_norm1_g=v_norm1_g, v_norm2_g=v_norm2_g, v_w_ada=v_w_ada, v_b_ada=v_b_ada, v_w_in=v_w_in, v_b_merge=v_b_merge, v_gla_a2=v_gla_a2, v_gla_ab=v_gla_ab, v_gla_norm_g=v_gla_norm_g, v_na_rpb=v_na_rpb, v_gdn_conv=v_gdn_conv, v_gdn_a_log=v_gdn_a_log, v_gdn_dt_bias=v_gdn_dt_bias, v_gdn_norm_g=v_gdn_norm_g, v_m2_conv=v_m2_conv, v_m2_conv_b=v_m2_conv_b, v_m2_a_log=v_m2_a_log, v_m2_dt_bias=v_m2_dt_bias, v_m2_d=v_m2_d, v_m2_norm_g=v_m2_norm_g, v_w_branch=v_w_branch, v_w_out=v_w_out, v_w_ffn1=v_w_ffn1, v_w_ffn3=v_w_ffn3, v_w_ffn2=v_w_ffn2, v_final_norm_g=v_final_norm_g)
    weights = {n: given[n] for n in TWIN_WEIGHTS}
    shared = {n: given[n] for n in SHARED_INPUTS}
    per_example = {n: given[n] for n in ['x', 'c', 'ctx']}
    grad_fn = _jax.value_and_grad(_loss, argnums=(0, 1))

    def one_microbatch(ex, loss_target):
        ex = dict(ex)
        diff = ex.pop(TWIN_DIFF_INPUT)
        return grad_fn(weights, diff, {**shared, **ex}, loss_target)

    if N_MICROBATCH == 1:
        loss, (grad_w, grad_x) = one_microbatch(per_example, given["loss_target"])
    else:
        def body(carry, xs):
            loss_sum, grad_sum = carry
            l_k, (gw_k, gx_k) = one_microbatch(xs[0], xs[1])
            with _jax.named_scope("update"):
                return (loss_sum + l_k, _jax.tree.map(_jnp.add, grad_sum, gw_k)), gx_k

        init = (_jnp.zeros((), _jnp.float32), _jax.tree.map(_jnp.zeros_like, weights))
        (loss, grad_w), grad_x = _jax.lax.scan(body, init, (per_example, given["loss_target"]))
    with _jax.named_scope("update"):
        delta_w, new_m, new_v = {}, {}, {}
        for n in TWIN_WEIGHTS:
            delta_w[n], new_m[n], new_v[n] = _adamw(weights[n], grad_w[n], given["m_" + n], given["v_" + n])
    return (loss, grad_x, *[grad_w[n] for n in TWIN_WEIGHTS], *[delta_w[n] for n in TWIN_WEIGHTS],
            *[new_m[n] for n in TWIN_WEIGHTS], *[new_v[n] for n in TWIN_WEIGHTS])
```

```python
import functools
import math

import numpy as np
import jax
import jax.numpy as jnp
from jax import lax
from jax.experimental import pallas as pl
from jax.experimental.pallas import tpu as pltpu

F32 = jnp.float32
BF16 = jnp.bfloat16
HI = lax.Precision.HIGHEST
MESH = pl.DeviceIdType.MESH
ANY = pl.BlockSpec(memory_space=pl.ANY)

VMEM_LIMIT = 56 * 1024 * 1024
LANES = 128
SUBLANES = 8

D_MODEL = 2048
DEPTH = 2
GRID_W = 64
CHUNK = 64
CONV_W = 5
RMS_EPS = 1e-6
NEG_INF = -1e30
ROPE_BASE = 10000.0
BRANCH_W = 512
GLA_H, GLA_DK, GLA_DV, GLA_LR, GLA_TAU = 4, 64, 128, 16, 16.0
NA_H, NA_D, NA_WIN_R, NA_WIN_C = 4, 128, 8, 16
GDN_H, GDN_D = 4, 128
M2_P, M2_H, M2_N, M2_G = 64, 8, 128, 2
D_FF = 5632
IN_COLS = 14912
N_SLOT = 4
ADAM_LR, ADAM_B1, ADAM_B2, ADAM_EPS, ADAM_WD, ADAM_STEP = 0.001, 0.9, 0.999, 1e-08, 0.01, 10


def _cparams(*sem):
    return pltpu.CompilerParams(dimension_semantics=sem if sem else None, vmem_limit_bytes=VMEM_LIMIT)


def _tile(n, target, mult):
    if n <= target:
        return n
    best = None
    for t in range(mult, target + 1, mult):
        if n % t == 0:
            best = t
    assert best is not None, (n, target, mult)
    return best


def _nt(a, b):
    return lax.dot_general(a.astype(BF16), b.astype(BF16), (((1,), (1,)), ((), ())), preferred_element_type=F32)


def _tn(a, b):
    return lax.dot_general(a.astype(BF16), b.astype(BF16), (((0,), (0,)), ((), ())), preferred_element_type=F32)


def _nn(a, b):
    return jnp.dot(a.astype(BF16), b.astype(BF16), preferred_element_type=F32)


def _nn_hi(a, b):
    return jnp.dot(a, b, precision=HI, preferred_element_type=F32)


def _w_spec(layout, prefix, r_idx, c_idx, br, bc, slot_dim):
    none = (None,) * len(prefix)
    if layout == "plain":
        return pl.BlockSpec(none + (br, bc), lambda i, j, k: prefix + (r_idx(i, j, k), c_idx(i, j, k)))
    if layout == "col":
        per = slot_dim // bc
        return pl.BlockSpec((None,) + none + (br, bc),
                            lambda i, j, k: (c_idx(i, j, k) // per,) + prefix + (r_idx(i, j, k), c_idx(i, j, k) % per))
    per = slot_dim // br
    return pl.BlockSpec((None,) + none + (br, bc),
                        lambda i, j, k: (r_idx(i, j, k) // per,) + prefix + (r_idx(i, j, k) % per, c_idx(i, j, k)))


def _mm(a, b, *, name, ta=False, tb=False, b_layout="plain", b_prefix=(), out_layout="plain", out_dtype=F32):
    m, kdim = (a.shape[1], a.shape[0]) if ta else a.shape
    rows, cols = b.shape[-2:]
    if b_layout == "col":
        cols *= N_SLOT
    elif b_layout == "row":
        rows *= N_SLOT
    n = rows if tb else cols
    assert (cols if tb else rows) == kdim, (a.shape, b.shape, ta, tb)
    n_unit = n // N_SLOT if (out_layout == "col" or (b_layout == ("row" if tb else "col"))) else n
    k_unit = kdim // N_SLOT if b_layout == ("col" if tb else "row") else kdim
    tm = _tile(m, 512, LANES if ta else SUBLANES)
    tn = _tile(n_unit, 1408 if n_unit % 1408 == 0 and n_unit % 512 != 0 else 512, LANES)
    tk = _tile(k_unit, 1408 if k_unit % 1408 == 0 and k_unit % 512 != 0 else 512, LANES if (not ta or tb) else SUBLANES)
    nk = kdim // tk
    a_spec = (pl.BlockSpec((tk, tm), lambda i, j, k: (k, i)) if ta else pl.BlockSpec((tm, tk), lambda i, j, k: (i, k)))
    slot_dim = b.shape[-1] if b_layout == "col" else b.shape[-2]
    if tb:
        b_spec = _w_spec(b_layout, tuple(b_prefix), lambda i, j, k: j, lambda i, j, k: k, tn, tk, slot_dim)
    else:
        b_spec = _w_spec(b_layout, tuple(b_prefix), lambda i, j, k: k, lambda i, j, k: j, tk, tn, slot_dim)
    if out_layout == "col":
        per = (n // N_SLOT) // tn
        out_shape = jax.ShapeDtypeStruct((N_SLOT, m, n // N_SLOT), out_dtype)
        out_spec = pl.BlockSpec((None, tm, tn), lambda i, j, k: (j // per, i, j % per))
    else:
        out_shape = jax.ShapeDtypeStruct((m, n), out_dtype)
        out_spec = pl.BlockSpec((tm, tn), lambda i, j, k: (i, j))
    dims = (((0 if ta else 1,), (1 if tb else 0,)), ((), ()))

    def body(a_ref, b_ref, o_ref, acc_ref):
        k = pl.program_id(2)

        @pl.when(k == 0)
        def _():
            acc_ref[...] = jnp.zeros_like(acc_ref)

        acc_ref[...] += lax.dot_general(a_ref[...].astype(BF16), b_ref[...].astype(BF16), dims,
                                        preferred_element_type=F32)

        @pl.when(k == nk - 1)
        def _():
            o_ref[...] = acc_ref[...].astype(o_ref.dtype)

    return pl.pallas_call(
        body, name=name, grid=(m // tm, n // tn, nk), in_specs=[a_spec, b_spec], out_specs=out_spec,
        out_shape=out_shape, scratch_shapes=[pltpu.VMEM((tm, tn), F32)],
        compiler_params=_cparams("parallel", "parallel", "arbitrary"),
    )(a, b)


def linear(a, w, grad_slot, *, name, layout="plain", prefix=()):
    @jax.custom_vjp
    def f(a, w, grad_slot):
        return _mm(a, w, name=name + "_fwd", b_layout=layout, b_prefix=prefix)

    def fwd(a, w, grad_slot):
        return f(a, w, grad_slot), (a, w)

    def bwd(res, g):
        a, w = res
        da = _mm(g, w, name=name + "_dgrad", tb=True, b_layout=layout, b_prefix=prefix)
        dw = _mm(a, g, name=name + "_wgrad", ta=True, out_layout="col" if layout == "col" else "plain")
        if layout == "row":
            dw = dw.reshape((N_SLOT, dw.shape[0] // N_SLOT, dw.shape[1]))
        return da, None, dw

    f.defvjp(fwd, bwd)
    return f(a, w, grad_slot)


def _rowwise_specs(rows, consts, params, tile, seg_tile):
    def row_spec(r):
        return pl.BlockSpec((tile, r.shape[1]), lambda i: (i, 0))

    def par_spec(p):
        if p.shape[0] == 2:
            return pl.BlockSpec((None,) + p.shape[1:], lambda i: (jnp.where(i >= seg_tile, 1, 0), 0, 0))
        return pl.BlockSpec((None,) + p.shape[1:], lambda i: (0, 0, 0))

    return [row_spec(r) for r in rows], [row_spec(r) for r in consts], [par_spec(p) for p in params]


def rowwise(f, rows, consts, params, *, out_widths, tile, n_lat, name):
    rows, consts, params = tuple(rows), tuple(consts), tuple(params)
    n_rows = rows[0].shape[0]
    tile = math.gcd(math.gcd(n_rows, n_lat), tile)
    assert tile % SUBLANES == 0
    seg_tile = n_lat // tile
    grid = (n_rows // tile,)
    nr, nc, npar = len(rows), len(consts), len(params)
    r_specs, c_specs, p_specs = _rowwise_specs(rows, consts, params, tile, seg_tile)
    out_shape = tuple(jax.ShapeDtypeStruct((n_rows, w), F32) for w in out_widths)
    out_specs = tuple(pl.BlockSpec((tile, w), lambda i: (i, 0)) for w in out_widths)
    n_out = len(out_widths)

    def fwd_call(rows, consts, params):
        def body(*refs):
            ins = [r[...] for r in refs[:nr + nc + npar]]
            outs = f(*ins)
            for o_ref, o in zip(refs[nr + nc + npar:], outs):
                o_ref[...] = o

        return pl.pallas_call(body, name=name + "_fwd", grid=grid, in_specs=r_specs + c_specs + p_specs,
                              out_specs=out_specs, out_shape=out_shape,
                              compiler_params=_cparams("parallel"))(*rows, *consts, *params)

    def bwd_call(rows, consts, params, gouts):
        def body(*refs):
            i = pl.program_id(0)
            ins = [r[...] for r in refs[:nr + nc + npar]]
            gs = tuple(r[...] for r in refs[nr + nc + npar:nr + nc + npar + n_out])
            d_refs = refs[nr + nc + npar + n_out:]
            cvals = ins[nr:nr + nc]

            def g(*diff):
                return tuple(f(*diff[:nr], *cvals, *diff[nr:]))

            _, vjp = jax.vjp(g, *ins[:nr], *ins[nr + nc:])
            grads = vjp(gs)
            for d_ref, gr in zip(d_refs[:nr], grads[:nr]):
                d_ref[...] = gr
            for p, d_ref, gr in zip(params, d_refs[nr:], grads[nr:]):
                first = (i == 0) | (i == seg_tile) if p.shape[0] == 2 else (i == 0)

                @pl.when(first)
                def _():
                    d_ref[...] = jnp.zeros_like(d_ref)

                d_ref[...] += gr

        d_shape = tuple(jax.ShapeDtypeStruct(r.shape, F32) for r in rows) + tuple(
            jax.ShapeDtypeStruct(p.shape, F32) for p in params)
        g_specs = [pl.BlockSpec((tile, w), lambda i: (i, 0)) for w in out_widths]
        return pl.pallas_call(body, name=name + "_bwd", grid=grid,
                              in_specs=r_specs + c_specs + p_specs + g_specs,
                              out_specs=tuple(r_specs + p_specs), out_shape=d_shape,
                              compiler_params=_cparams("arbitrary"))(*rows, *consts, *params, *gouts)

    @jax.custom_vjp
    def op(rows, consts, params):
        return fwd_call(rows, consts, params)

    def op_fwd(rows, consts, params):
        return op(rows, consts, params), (rows, consts, params)

    def op_bwd(res, gouts):
        rows, consts, params = res
        d = bwd_call(rows, consts, params, tuple(gouts))
        return tuple(d[:nr]), tuple(None for _ in consts), tuple(d[nr:])

    op.defvjp(op_fwd, op_bwd)
    return op(rows, consts, params)


def _rms(x, width=None):
    w = x.shape[-1] if width is None else width
    return x * lax.rsqrt(jnp.sum(x * x, axis=-1, keepdims=True) * (1.0 / w) + RMS_EPS)


def _silu(x):
    return x * jax.nn.sigmoid(x)


def _f_modnorm(x, g, sc, sh):
    return ((_rms(x) * g) * (1.0 + sc) + sh,)


def _f_silu(x):
    return (_silu(x),)


def _f_gla_prep(lr, a2, ab):
    z = _nn(lr, a2) + ab
    return ((jnp.minimum(z, 0.0) - jnp.log(1.0 + jnp.exp(-jnp.abs(z)))) * (1.0 / GLA_TAU),)


def _f_headnorm_gate(o, g, ng):
    outs = []
    for h in range(BRANCH_W // LANES):
        lo = h * LANES
        oh = o[:, lo:lo + LANES] + o[:, BRANCH_W + lo:BRANCH_W + lo + LANES]
        outs.append(_rms(oh) * ng * _silu(g[:, lo:lo + LANES]))
    return (jnp.concatenate(outs, axis=-1),)


def _f_gdn_prep(x, alog, dtb):
    half = x.shape[1] // 2
    beta = jax.nn.sigmoid(x[:, :half])
    la = -jnp.exp(alog) * jax.nn.softplus(x[:, half:] + dtb)
    return beta, la


def _f_m2_prep(x, alog, dtb):
    dt = jax.nn.softplus(x + dtb)
    return dt, -jnp.exp(alog) * dt


def _f_m2_fin(o, z, xs, dskip, ng):
    w = z.shape[1]
    y = (o[:, :w] + o[:, w:] + dskip * xs) * _silu(z)
    return (_rms(y, BRANCH_W) * ng,)


def _f_merge(gate, z0, z1, z2, z3, bm):
    acc = None
    for i, z in enumerate((z0, z1, z2, z3)):
        lo = i * D_MODEL
        t = jax.nn.sigmoid(gate[:, lo:lo + D_MODEL] + bm[:, lo:lo + D_MODEL]) * z
        acc = t if acc is None else acc + t
    return (acc,)


def _f_resid(x, y, g):
    return (x + g * y,)


def _f_swiglu(u1, u3):
    return (_silu(u1) * u3,)


def _f_loss(x, tgt, g):
    e = _rms(x) * g - tgt
    per_row = 0.5 * jnp.sum(e * e, axis=-1, keepdims=True) * (1.0 / D_MODEL)
    return (jnp.broadcast_to(per_row * (1.0 / LANES), (x.shape[0], LANES)),)


_HALO = 8


def _conv_segments(n_lat, n_ctx):
    segs = [(0, _HALO, n_lat), (n_lat, n_lat + 3 * _HALO, n_ctx)]
    return segs, n_lat + n_ctx + 4 * _HALO


def _conv_stage(buf, src, n_lat, n_ctx):
    zeros = jnp.zeros((_HALO, LANES), F32)
    buf[0:_HALO, :] = zeros
    buf[_HALO:_HALO + n_lat, :] = src[0:n_lat, :]
    buf[n_lat + _HALO:n_lat + 2 * _HALO, :] = zeros
    buf[n_lat + 2 * _HALO:n_lat + 3 * _HALO, :] = zeros
    buf[n_lat + 3 * _HALO:n_lat + 3 * _HALO + n_ctx, :] = src[n_lat:n_lat + n_ctx, :]
    buf[n_lat + n_ctx + 3 * _HALO:n_lat + n_ctx + 4 * _HALO, :] = zeros


def conv_silu(x, w, b, *, n_lat, name):
    n_rows, n_ch = x.shape
    n_ctx = n_rows - n_lat
    segs, n_buf = _conv_segments(n_lat, n_ctx)
    grid = (n_ch // LANES,)
    col = lambda r: pl.BlockSpec((r, LANES), lambda j: (0, j))
    half = CONV_W // 2

    def tiles():
        for row0, off, length in segs:
            tr = _tile(length, 256, SUBLANES)
            for t0 in range(0, length, tr):
                yield row0 + t0, off + t0, tr

    def pre_act(buf, w_ref, b_ref, off, tr):
        acc = jnp.broadcast_to(b_ref[...], (tr, LANES))
        for j in range(CONV_W):
            acc = acc + w_ref[j:j + 1, :] * buf[off + j - half:off + j - half + tr, :]
        return acc

    def fwd_call(x, w, b):
        def body(x_ref, w_ref, b_ref, o_ref, buf):
            _conv_stage(buf, x_ref, n_lat, n_ctx)
            for row, off, tr in tiles():
                o_ref[row:row + tr, :] = _silu(pre_act(buf, w_ref, b_ref, off, tr))

        return pl.pallas_call(body, name=name + "_fwd", grid=grid, in_specs=[col(n_rows), col(CONV_W), col(1)],
                              out_specs=col(n_rows), out_shape=jax.ShapeDtypeStruct(x.shape, F32),
                              scratch_shapes=[pltpu.VMEM((n_buf, LANES), F32)],
                              compiler_params=_cparams("parallel"))(x, w, b)

    def bwd_call(x, w, b, g):
        def body(x_ref, w_ref, b_ref, g_ref, dx_ref, dw_ref, db_ref, xbuf, dbuf):
            _conv_stage(xbuf, x_ref, n_lat, n_ctx)
            _conv_stage(dbuf, g_ref, n_lat, n_ctx)
            dw = [jnp.zeros((1, LANES), F32) for _ in range(CONV_W)]
            db = jnp.zeros((1, LANES), F32)
            for row, off, tr in tiles():
                pre = pre_act(xbuf, w_ref, b_ref, off, tr)
                s = jax.nn.sigmoid(pre)
                dpre = g_ref[row:row + tr, :] * (s * (1.0 + pre * (1.0 - s)))
                dbuf[off:off + tr, :] = dpre
                db = db + jnp.sum(dpre, axis=0, keepdims=True)
                for j in range(CONV_W):
                    dw[j] = dw[j] + jnp.sum(dpre * xbuf[off + j - half:off + j - half + tr, :], axis=0, keepdims=True)
            for row, off, tr in tiles():
                acc = jnp.zeros((tr, LANES), F32)
                for j in range(CONV_W):
                    acc = acc + w_ref[j:j + 1, :] * dbuf[off - j + half:off - j + half + tr, :]
                dx_ref[row:row + tr, :] = acc
            for j in range(CONV_W):
                dw_ref[j:j + 1, :] = dw[j]
            db_ref[...] = db

        return pl.pallas_call(
            body, name=name + "_bwd", grid=grid, in_specs=[col(n_rows), col(CONV_W), col(1), col(n_rows)],
            out_specs=(col(n_rows), col(CONV_W), col(1)),
            out_shape=(jax.ShapeDtypeStruct(x.shape, F32), jax.ShapeDtypeStruct(w.shape, F32),
                       jax.ShapeDtypeStruct(b.shape, F32)),
            scratch_shapes=[pltpu.VMEM((n_buf, LANES), F32), pltpu.VMEM((n_buf, LANES), F32)],
            compiler_params=_cparams("parallel"))(x, w, b, g)

    @jax.custom_vjp
    def op(x, w, b):
        return fwd_call(x, w, b)

    op.defvjp(lambda x, w, b: (op(x, w, b), (x, w, b)), lambda res, g: bwd_call(*res, g))
    return op(x, w, b)


def chunk_scan(step, shared, shared_lanes, perdir, perdir_lanes, consts, *, heads, state_shape, out_w, n_lat, name):
    shared, perdir, consts = tuple(shared), tuple(perdir), tuple(consts)
    n_rows = shared[0].shape[0]
    nl, ncx = n_lat // CHUNK, (n_rows - n_lat) // CHUNK
    n_chunks = nl + ncx
    ow_all = heads * out_w
    ns, npd, ncst = len(shared), len(perdir), len(consts)

    def cidx(d, n):
        m = n - ncx
        return jnp.where(n < ncx, nl + jnp.where(d == 0, n, ncx - 1 - n), jnp.where(d == 0, m, nl - 1 - m))

    def specs(order):
        sh = [pl.BlockSpec((CHUNK, a.shape[1]), lambda d, n: (cidx(d, order(n)), 0)) for a in shared]
        pd = [pl.BlockSpec((CHUNK, a.shape[1] // 2), lambda d, n: (cidx(d, order(n)), d)) for a in perdir]
        cs = [pl.BlockSpec((CHUNK, a.shape[1]), lambda d, n: (cidx(d, order(n)), 0)) for a in consts]
        o = pl.BlockSpec((CHUNK, ow_all), lambda d, n: (cidx(d, order(n)), d))
        st = pl.BlockSpec((None, None, heads) + state_shape, lambda d, n: (d, order(n), 0) + (0,) * len(state_shape))
        return sh, pd, cs, o, st

    def mask(d):
        r = lax.broadcasted_iota(jnp.int32, (CHUNK, CHUNK), 0)
        c = lax.broadcasted_iota(jnp.int32, (CHUNK, CHUNK), 1)
        lower = jnp.where(r >= c, 1.0, 0.0).astype(F32)
        upper = jnp.where(r <= c, 1.0, 0.0).astype(F32)
        return jnp.where(d == 0, lower, upper)

    def head_slices(h):
        out = []
        for lanes in tuple(shared_lanes) + tuple(perdir_lanes):
            out.append([slice(off + (h // hpg) * w, off + (h // hpg) * w + w) for off, w, hpg in lanes])
        return out

    def load(refs, h):
        return tuple(tuple(ref[:, s] for s in sl) for ref, sl in zip(refs, head_slices(h)))

    state_sds = jax.ShapeDtypeStruct((2, n_chunks, heads) + state_shape, F32)

    def fwd_call(shared, perdir, consts):
        sh, pd, cs, o_spec, st_spec = specs(lambda n: n)

        def body(*refs):
            in_refs = refs[:ns + npd]
            c_refs = refs[ns + npd:ns + npd + ncst]
            o_ref, ss_ref, s_scr = refs[ns + npd + ncst:]
            d, n = pl.program_id(0), pl.program_id(1)

            @pl.when(n == 0)
            def _():
                s_scr[...] = jnp.zeros_like(s_scr)

            m = mask(d)
            cv = tuple(c[...] for c in c_refs)
            for h in range(heads):
                s0 = s_scr[h]
                o_h, s_new = step(load(in_refs, h), cv, s0, m)
                o_ref[:, h * out_w:(h + 1) * out_w] = o_h
                ss_ref[h] = s0
                s_scr[h] = s_new

        return pl.pallas_call(
            body, name=name + "_fwd", grid=(2, n_chunks), in_specs=sh + pd + cs, out_specs=(o_spec, st_spec),
            out_shape=(jax.ShapeDtypeStruct((n_rows, 2 * ow_all), F32), state_sds),
            scratch_shapes=[pltpu.VMEM((heads,) + state_shape, F32)],
            compiler_params=_cparams("arbitrary", "arbitrary"))(*shared, *perdir, *consts)

    def bwd_call(shared, perdir, consts, starts, g):
        sh, pd, cs, o_spec, st_spec = specs(lambda n: n_chunks - 1 - n)
        dsh = [pl.BlockSpec((CHUNK, a.shape[1]), lambda d, n: (cidx(d, n_chunks - 1 - n), d)) for a in shared]

        def body(*refs):
            in_refs = refs[:ns + npd]
            c_refs = refs[ns + npd:ns + npd + ncst]
            ss_ref, g_ref = refs[ns + npd + ncst:ns + npd + ncst + 2]
            d_refs = refs[ns + npd + ncst + 2:ns + npd + ncst + 2 + ns + npd]
            ds_scr = refs[-1]
            d, n = pl.program_id(0), pl.program_id(1)

            @pl.when(n == 0)
            def _():
                ds_scr[...] = jnp.zeros_like(ds_scr)

            for d_ref in d_refs:
                d_ref[...] = jnp.zeros_like(d_ref)
            m = mask(d)
            cv = tuple(c[...] for c in c_refs)
            for h in range(heads):
                _, vjp = jax.vjp(lambda ins, s: step(ins, cv, s, m), load(in_refs, h), ss_ref[h])
                g_ins, g_s = vjp((g_ref[:, h * out_w:(h + 1) * out_w], ds_scr[h]))
                for d_ref, sl, gr in zip(d_refs, head_slices(h), g_ins):
                    for s, gv in zip(sl, gr):
                        d_ref[:, s] += gv
                ds_scr[h] = g_s

        d_shape = tuple(jax.ShapeDtypeStruct((n_rows, 2 * a.shape[1]), F32) for a in shared) + tuple(
            jax.ShapeDtypeStruct(a.shape, F32) for a in perdir)
        return pl.pallas_call(
            body, name=name + "_bwd", grid=(2, n_chunks), in_specs=sh + pd + cs + [st_spec, o_spec],
            out_specs=tuple(dsh + pd), out_shape=d_shape,
            scratch_shapes=[pltpu.VMEM((heads,) + state_shape, F32)],
            compiler_params=_cparams("arbitrary", "arbitrary"))(*shared, *perdir, *consts, starts, g)

    @jax.custom_vjp
    def op(shared, perdir, consts):
        return fwd_call(shared, perdir, consts)[0]

    def op_fwd(shared, perdir, consts):
        o, starts = fwd_call(shared, perdir, consts)
        return o, (shared, perdir, consts, starts)

    def op_bwd(res, g):
        shared, perdir, consts, starts = res
        d = bwd_call(shared, perdir, consts, starts, g)
        d_sh = tuple(a[:, :a.shape[1] // 2] + a[:, a.shape[1] // 2:] for a in d[:ns])
        return d_sh, tuple(d[ns:]), tuple(None for _ in consts)

    op.defvjp(op_fwd, op_bwd)
    return op(shared, perdir, consts)


@jax.custom_vjp
def _swap_halves(x):
    return pltpu.roll(x, LANES // 2, 1)


_swap_halves.defvjp(lambda x: (_swap_halves(x), None), lambda _, g: (_swap_halves(g),))


def _cum_decay(la, m, width):
    lab = jnp.broadcast_to(la, (CHUNK, width))
    return _nn_hi(m, lab), jnp.sum(lab, axis=0, keepdims=True)


def _pair_decay(la, m):
    r = lax.broadcasted_iota(jnp.int32, (CHUNK, CHUNK), 0)
    c = lax.broadcasted_iota(jnp.int32, (CHUNK, CHUNK), 1)
    eye = jnp.where(r == c, 1.0, 0.0).astype(F32)
    b = _nn_hi(m, jnp.broadcast_to(la, (CHUNK, CHUNK)))
    b_row = _nn_hi(jnp.ones((CHUNK, CHUNK), F32), eye * b)
    diff = b - b_row
    incl = jnp.where(m > 0, jnp.exp(jnp.where(m > 0, diff, 0.0)), 0.0)
    ms = m - eye
    strict = jnp.where(ms > 0, jnp.exp(jnp.where(ms > 0, diff, 0.0)), 0.0)
    return incl, strict, eye


def _gla_step(ins, consts, st, m):
    (q, k), (v,), (la,) = ins
    cos, sin = consts
    q = (q * cos + _swap_halves(q) * sin) * (GLA_DK ** -0.5)
    k = k * cos + _swap_halves(k) * sin
    b = _nn_hi(m, la)
    bl = jnp.sum(la, axis=0, keepdims=True)
    qi = q * jnp.exp(b)
    ki = k * jnp.exp(-b)
    att = _nt(qi, ki) * m
    o = _nt(qi, st) + _nn(att, v)
    st_new = st * jnp.exp(bl) + _tn(v, k * jnp.exp(bl - b))
    return o, st_new


def _l2n(x):
    return x * lax.rsqrt(jnp.sum(x * x, axis=-1, keepdims=True) + RMS_EPS)


def _gdn_step(ins, consts, s, m):
    (q, k, v), (beta,), (la,) = ins
    q = _l2n(q) * (GDN_D ** -0.5)
    k = _l2n(k)
    b, bl = _cum_decay(la, m, GDN_D)
    incl, strict, eye = _pair_decay(la, m)
    nmat = beta * _nt(k, k) * strict
    inv = eye - nmat
    p = nmat
    for _ in range(5):
        p = _nn_hi(p, p)
        inv = inv + _nn_hi(inv, p)
    w = _nn_hi(inv, k * (beta * jnp.exp(b)))
    u0 = _nn_hi(inv, v * beta)
    u = u0 - _nn(w, s)
    s_new = jnp.exp(bl) * s + _tn(k * jnp.exp(bl - b), u)
    o = jnp.exp(b) * _nn(q, s) + _nn(_nt(q, k) * incl, u)
    return o, s_new


def _ssd_step(ins, consts, s, m):
    (xs,), (bm, cm), (dt,), (la,) = ins
    xv = xs * dt
    b, bl = _cum_decay(la, m, M2_N)
    incl, _, _ = _pair_decay(la, m)
    o = jnp.exp(b) * _nn(cm, s) + _nn(_nt(cm, bm) * incl, xv)
    s_new = jnp.exp(bl) * s + _tn(bm * jnp.exp(bl - b), xv)
    return o, s_new


def _na_tile(q, kw, vw, kc, vc, bias):
    qs = q * (NA_D ** -0.5)
    s1 = _nt(qs, kw) + bias
    s2 = _nt(qs, kc)
    mx = lax.stop_gradient(jnp.maximum(jnp.max(s1, axis=-1, keepdims=True), jnp.max(s2, axis=-1, keepdims=True)))
    p1 = jnp.exp(s1 - mx)
    p2 = jnp.exp(s2 - mx)
    den = jnp.sum(p1, axis=-1, keepdims=True) + jnp.sum(p2, axis=-1, keepdims=True)
    return (_nn(p1, vw) + _nn(p2, vc)) / den


def _ctx_tile(q, k, v):
    s = _nt(q * (NA_D ** -0.5), k)
    p = jnp.exp(s - lax.stop_gradient(jnp.max(s, axis=-1, keepdims=True)))
    return _nn(p, v) / jnp.sum(p, axis=-1, keepdims=True)


def natten(q, k, v, bias, *, n_lat, name):
    n_rows = q.shape[0]
    n_ctx = n_rows - n_lat
    g_rows = n_lat // GRID_W
    win = NA_WIN_R * GRID_W
    ctx_blk = n_lat // n_ctx

    def start(n):
        return jnp.clip(n - NA_WIN_R // 2, 0, g_rows - NA_WIN_R)

    def case(n):
        return n - start(n)

    q_spec = pl.BlockSpec((GRID_W, LANES), lambda h, n: (n, h))
    lat_spec = pl.BlockSpec((n_lat, LANES), lambda h, n: (0, h))
    ctx_in = pl.BlockSpec((n_ctx, LANES), lambda h, n: (ctx_blk, h))
    ctx_out = pl.BlockSpec((n_ctx, LANES), lambda h, n: (0, h))
    bias_spec = pl.BlockSpec((None, None, GRID_W, win), lambda h, n: (h, case(n), 0, 0))
    lat_sds = jax.ShapeDtypeStruct((n_lat, BRANCH_W), F32)
    ctx_sds = jax.ShapeDtypeStruct((n_ctx, BRANCH_W), F32)

    def lat_fwd(q, k, v, bias):
        def body(q_ref, k_ref, v_ref, kc_ref, vc_ref, b_ref, o_ref):
            r0 = pl.multiple_of(start(pl.program_id(1)) * GRID_W, GRID_W)
            o_ref[...] = _na_tile(q_ref[...], k_ref[pl.ds(r0, win), :], v_ref[pl.ds(r0, win), :],
                                  kc_ref[...], vc_ref[...], b_ref[...])

        return pl.pallas_call(body, name=name + "_lat_fwd", grid=(NA_H, g_rows),
                              in_specs=[q_spec, lat_spec, lat_spec, ctx_in, ctx_in, bias_spec], out_specs=q_spec,
                              out_shape=lat_sds, compiler_params=_cparams("parallel", "arbitrary"))(q, k, v, k, v, bias)

    def lat_bwd(q, k, v, bias, g):
        def body(q_ref, k_ref, v_ref, kc_ref, vc_ref, b_ref, g_ref, dq_ref, dk_ref, dv_ref, dkc_ref, dvc_ref, db_ref):
            n = pl.program_id(1)
            r0 = pl.multiple_of(start(n) * GRID_W, GRID_W)

            @pl.when(n == 0)
            def _():
                for r in (dk_ref, dv_ref, dkc_ref, dvc_ref):
                    r[...] = jnp.zeros_like(r)

            @pl.when((n == 0) | (case(n) != case(jnp.maximum(n - 1, 0))))
            def _():
                db_ref[...] = jnp.zeros_like(db_ref)

            _, vjp = jax.vjp(_na_tile, q_ref[...], k_ref[pl.ds(r0, win), :], v_ref[pl.ds(r0, win), :],
                             kc_ref[...], vc_ref[...], b_ref[...])
            dq, dkw, dvw, dkc, dvc, db = vjp(g_ref[...])
            dq_ref[...] = dq
            dk_ref[pl.ds(r0, win), :] += dkw
            dv_ref[pl.ds(r0, win), :] += dvw
            dkc_ref[...] += dkc
            dvc_ref[...] += dvc
            db_ref[...] += db

        return pl.pallas_call(
            body, name=name + "_lat_bwd", grid=(NA_H, g_rows),
            in_specs=[q_spec, lat_spec, lat_spec, ctx_in, ctx_in, bias_spec, q_spec],
            out_specs=(q_spec, lat_spec, lat_spec, ctx_out, ctx_out, bias_spec),
            out_shape=(lat_sds, lat_sds, lat_sds, ctx_sds, ctx_sds, jax.ShapeDtypeStruct(bias.shape, F32)),
            compiler_params=_cparams("parallel", "arbitrary"))(q, k, v, k, v, bias, g)

    c_in = pl.BlockSpec((n_ctx, LANES), lambda h: (ctx_blk, h))
    c_out = pl.BlockSpec((n_ctx, LANES), lambda h: (0, h))

    def ctx_fwd(q, k, v):
        def body(q_ref, k_ref, v_ref, o_ref):
            o_ref[...] = _ctx_tile(q_ref[...], k_ref[...], v_ref[...])

        return pl.pallas_call(body, name=name + "_ctx_fwd", grid=(NA_H,), in_specs=[c_in, c_in, c_in], out_specs=c_out,
                              out_shape=ctx_sds, compiler_params=_cparams("parallel"))(q, k, v)

    def ctx_bwd(q, k, v, g):
        def body(q_ref, k_ref, v_ref, g_ref, dq_ref, dk_ref, dv_ref):
            _, vjp = jax.vjp(_ctx_tile, q_ref[...], k_ref[...], v_ref[...])
            dq_ref[...], dk_ref[...], dv_ref[...] = vjp(g_ref[...])

        return pl.pallas_call(body, name=name + "_ctx_bwd", grid=(NA_H,), in_specs=[c_in, c_in, c_in, c_out],
                              out_specs=(c_out, c_out, c_out), out_shape=(ctx_sds, ctx_sds, ctx_sds),
                              compiler_params=_cparams("parallel"))(q, k, v, g)

    @jax.custom_vjp
    def op(q, k, v, bias):
        return jnp.concatenate([lat_fwd(q, k, v, bias), ctx_fwd(q, k, v)], axis=0)

    def op_bwd(res, g):
        q, k, v, bias = res
        dq, dk, dv, dkc, dvc, db = lat_bwd(q, k, v, bias, g[:n_lat])
        dqc, dkc2, dvc2 = ctx_bwd(q, k, v, g[n_lat:])
        return (jnp.concatenate([dq, dqc], axis=0), jnp.concatenate([dk, dkc + dkc2], axis=0),
                jnp.concatenate([dv, dvc + dvc2], axis=0), db)

    op.defvjp(lambda q, k, v, bias: (op(q, k, v, bias), (q, k, v, bias)), op_bwd)
    return op(q, k, v, bias)


def _runs(src):
    src = np.asarray(src)
    out, i = [], 0
    while i < len(src):
        j = i + 1
        if src[i] < 0:
            while j < len(src) and src[j] < 0:
                j += 1
            out.append((-1, j - i))
        else:
            while j < len(src) and src[j] == src[j - 1] + 1:
                j += 1
            out.append((int(src[i]), j - i))
        i = j
    return out


def _take_cols(w, src):
    pieces = [jnp.zeros(w.shape[:-1] + (ln,), w.dtype) if s < 0 else w[..., s:s + ln] for s, ln in _runs(src)]
    return pieces[0] if len(pieces) == 1 else jnp.concatenate(pieces, axis=-1)


def _untake_cols(parts, n_cols):
    found = []
    for arr, src in parts:
        pos = 0
        for s, ln in _runs(src):
            if s >= 0:
                found.append((s, arr[..., pos:pos + ln]))
            pos += ln
    found.sort(key=lambda t: t[0])
    at = 0
    for s, piece in found:
        assert s == at, (s, at)
        at += piece.shape[-1]
    assert at == n_cols, (at, n_cols)
    return jnp.concatenate([p for _, p in found], axis=-1)


def _pad_heads(base, heads, real, width):
    return np.concatenate([np.concatenate([base + h * real + np.arange(real), -np.ones(width - real, np.int64)])
                           for h in range(heads)])


def _rope_heads(base, heads):
    z = -np.ones(32, np.int64)
    return np.concatenate([np.concatenate([base + h * 64 + np.arange(32), z, base + h * 64 + 32 + np.arange(32), z])
                           for h in range(heads)])


def _lane_block(base, n):
    return np.concatenate([base + np.arange(n), -np.ones(LANES - n, np.int64)])


def _in_groups():
    g0, n0, d0, m0, t0 = 0, 1568, 3104, 5168, 6720
    rng = lambda a, n: a + np.arange(n)
    return [
        ("gla_qk", np.concatenate([_rope_heads(g0, GLA_H), _rope_heads(g0 + 256, GLA_H)])),
        ("gla_v", rng(g0 + 512, 512)),
        ("gla_g", rng(g0 + 1024, 512)),
        ("gla_lr", _lane_block(g0 + 1536, 2 * GLA_LR)),
        ("na_q", rng(n0, 512)), ("na_k", rng(n0 + 512, 512)), ("na_v", rng(n0 + 1024, 512)),
        ("gdn_qkv", rng(d0, 1536)),
        ("gdn_z", rng(d0 + 1536, 512)),
        ("gdn_sm", np.concatenate([_lane_block(d0 + 2048 + 4 * i, GDN_H) for i in range(4)])),
        ("m2_z", _pad_heads(m0, M2_H, M2_P, LANES)),
        ("m2_xs", _pad_heads(m0 + 512, M2_H, M2_P, LANES)),
        ("m2_bc", rng(m0 + 1024, 512)),
        ("m2_dt", np.concatenate([_lane_block(m0 + 1536, M2_H), _lane_block(m0 + 1536 + M2_H, M2_H)])),
        ("gate", rng(t0, 4 * D_MODEL)),
    ]


_M2_PAD = _pad_heads(0, M2_H, M2_P, LANES)
_GLA_PAD = _rope_heads(0, GLA_H)


def _row3(v):
    return v.reshape((1, 1, -1))


def _dir_rows(p, n):
    return _row3(jnp.concatenate([_take_cols(p[d][None], _lane_block(0, n)) for d in range(2)], axis=-1))


def _rope_tables(n_lat, n_ctx):
    n_freq = GLA_DK // 4
    freqs = ROPE_BASE ** (-jnp.arange(n_freq, dtype=F32) / n_freq)
    t = jnp.arange(n_lat)
    row = (t // GRID_W).astype(F32)
    colv = (t % GRID_W).astype(F32)
    ang = jnp.concatenate([row[:, None] * freqs, colv[:, None] * freqs], axis=-1)
    c, s = jnp.cos(ang), jnp.sin(ang)
    one, zero = jnp.ones_like(c), jnp.zeros_like(c)
    cos_t = jnp.concatenate([c, one, c, one], axis=-1)
    sin_t = jnp.concatenate([-s, zero, s, zero], axis=-1)
    return (jnp.concatenate([cos_t, jnp.ones((n_ctx, LANES), F32)], axis=0),
            jnp.concatenate([sin_t, jnp.zeros((n_ctx, LANES), F32)], axis=0))


def _na_bias(rpb):
    case = np.arange(NA_WIN_R)
    r = np.arange(NA_WIN_R)
    dr = r[None, :] - case[:, None] + NA_WIN_R - 1
    ci = np.arange(GRID_W)
    dc = np.clip(ci[None, :] - ci[:, None], 1 - NA_WIN_C, NA_WIN_C - 1) + NA_WIN_C - 1
    c0 = np.clip(ci - NA_WIN_C // 2, 0, GRID_W - NA_WIN_C)
    ok = (ci[None, :] >= c0[:, None]) & (ci[None, :] < c0[:, None] + NA_WIN_C)
    tbl = rpb[:, dr[:, None, :, None], dc[None, :, None, :]]
    tbl = jnp.where(ok[None, None, :, None, :], tbl, NEG_INF)
    return tbl.reshape((NA_H, NA_WIN_R, GRID_W, NA_WIN_R * GRID_W))


def _layer(l, xs, mod_in, small, slots, gath, win, tables, *, n_lat):
    rw = functools.partial(rowwise, n_lat=n_lat)
    nm = lambda s: "l%d_%s" % (l, s)
    sl = slots[l]
    mod = linear(mod_in, gath["w_ada"], sl["w_ada"], name=nm("ada"), layout="col", prefix=(l,)) + small["b_ada"][l]
    sh1, sc1, g1, sh2, sc2, g2 = [mod[0:2, i * D_MODEL:(i + 1) * D_MODEL].reshape((2, 1, D_MODEL)) for i in range(6)]
    (h,) = rw(_f_modnorm, [xs], [], [_row3(small["norm1_g"][l]), sc1, sh1], out_widths=[D_MODEL], tile=256,
              name=nm("norm1"))
    p = {g: linear(h, win[l][g], sl["in_" + g], name=nm("in_" + g)) for g, _ in _in_groups()}

    a2 = small["gla_a2"][l]
    a2p = jnp.concatenate([
        jnp.concatenate([_take_cols(a2[0], _GLA_PAD), jnp.zeros((GLA_LR, 512), F32)], axis=1),
        jnp.concatenate([jnp.zeros((GLA_LR, 512), F32), _take_cols(a2[1], _GLA_PAD)], axis=1),
        jnp.zeros((LANES - 2 * GLA_LR, 1024), F32)], axis=0)[None]
    abp = _row3(jnp.concatenate([_take_cols(small["gla_ab"][l][d][None], _GLA_PAD) for d in range(2)], axis=-1))
    (la,) = rw(_f_gla_prep, [p["gla_lr"]], [], [a2p, abp], out_widths=[1024], tile=256, name=nm("gla_prep"))
    head = lambda off: (off, LANES, 1)
    o = chunk_scan(_gla_step, [p["gla_qk"], p["gla_v"]], [[head(0), head(512)], [head(0)]],
                   [la], [[head(0)]], tables, heads=GLA_H, state_shape=(GLA_DV, LANES), out_w=GLA_DV, n_lat=n_lat,
                   name=nm("gla_scan"))
    (ya,) = rw(_f_headnorm_gate, [o, p["gla_g"]], [], [_row3(small["gla_norm_g"][l])], out_widths=[BRANCH_W], tile=256,
               name=nm("gla_fin"))

    yb = natten(p["na_q"], p["na_k"], p["na_v"], _na_bias(small["na_rpb"][l]), n_lat=n_lat, name=nm("na"))

    cq = conv_silu(p["gdn_qkv"], small["gdn_conv"][l], jnp.zeros((1, 3 * BRANCH_W), F32), n_lat=n_lat, name=nm("gdn_conv"))
    beta, la = rw(_f_gdn_prep, [p["gdn_sm"]], [], [_dir_rows(small["gdn_a_log"][l], GDN_H), _dir_rows(small["gdn_dt_bias"][l], GDN_H)],
                  out_widths=[256, 256], tile=256, name=nm("gdn_prep"))
    o = chunk_scan(_gdn_step, [cq], [[head(0), head(512), head(1024)]], [beta, la], [[(0, 1, 1)], [(0, 1, 1)]], [],
                   heads=GDN_H, state_shape=(GDN_D, GDN_D), out_w=GDN_D, n_lat=n_lat, name=nm("gdn_scan"))
    (yc,) = rw(_f_headnorm_gate, [o, p["gdn_z"]], [], [_row3(small["gdn_norm_g"][l])], out_widths=[BRANCH_W], tile=256,
               name=nm("gdn_fin"))

    cw, cb = small["m2_conv"][l], small["m2_conv_b"][l][None]
    cxs = conv_silu(p["m2_xs"], _take_cols(cw[:, :512], _M2_PAD), _take_cols(cb[:, :512], _M2_PAD), n_lat=n_lat,
                    name=nm("m2_conv_x"))
    cbc = conv_silu(p["m2_bc"], cw[:, 512:], cb[:, 512:], n_lat=n_lat, name=nm("m2_conv_bc"))
    dt, la = rw(_f_m2_prep, [p["m2_dt"]], [], [_dir_rows(small["m2_a_log"][l], M2_H), _dir_rows(small["m2_dt_bias"][l], M2_H)],
                out_widths=[256, 256], tile=256, name=nm("m2_prep"))
    hpg = M2_H // M2_G
    o = chunk_scan(_ssd_step, [cxs, cbc], [[head(0)], [(0, LANES, hpg), (256, LANES, hpg)]], [dt, la],
                   [[(0, 1, 1)], [(0, 1, 1)]], [], heads=M2_H, state_shape=(M2_N, LANES), out_w=LANES, n_lat=n_lat,
                   name=nm("m2_scan"))
    dskip = _row3(jnp.repeat(small["m2_d"][l], LANES))
    (yd,) = rw(_f_m2_fin, [o, p["m2_z"], cxs], [], [dskip, _row3(_take_cols(small["m2_norm_g"][l][None], _M2_PAD))],
               out_widths=[2 * BRANCH_W], tile=128, name=nm("m2_fin"))

    wb = gath["w_branch"]
    zs = [linear(y, wb, sl["w_branch%d" % i], name=nm("branch%d" % i), layout="col", prefix=(l, i))
          for i, y in enumerate((ya, yb, yc))]
    wb3 = wb[:, l, 3].reshape((N_SLOT, M2_H, M2_P, BRANCH_W))
    wb3 = jnp.pad(wb3, ((0, 0), (0, 0), (0, LANES - M2_P), (0, 0))).reshape((N_SLOT, 2 * BRANCH_W, BRANCH_W))
    zs.append(linear(yd, wb3, sl["w_branch3"], name=nm("branch3"), layout="col"))
    (merged,) = rw(_f_merge, [p["gate"]] + zs, [], [_row3(small["b_merge"][l].reshape(-1))], out_widths=[D_MODEL], tile=64,
                   name=nm("merge"))
    y = linear(merged, gath["w_out"], sl["w_out"], name=nm("out"), layout="row", prefix=(l,))
    (x1,) = rw(_f_resid, [xs, y], [], [g1], out_widths=[D_MODEL], tile=256, name=nm("res1"))

    (h2,) = rw(_f_modnorm, [x1], [], [_row3(small["norm2_g"][l]), sc2, sh2], out_widths=[D_MODEL], tile=256,
               name=nm("norm2"))
    u1 = linear(h2, gath["w_ffn1"], sl["w_ffn1"], name=nm("ffn1"), layout="col", prefix=(l,))
    u3 = linear(h2, gath["w_ffn3"], sl["w_ffn3"], name=nm("ffn3"), layout="col", prefix=(l,))
    (act,) = rw(_f_swiglu, [u1, u3], [], [], out_widths=[D_FF], tile=128, name=nm("swiglu"))
    f = linear(act, gath["w_ffn2"], sl["w_ffn2"], name=nm("ffn2"), layout="row", prefix=(l,))
    (x2,) = rw(_f_resid, [x1, f], [], [g2], out_widths=[D_MODEL], tile=256, name=nm("res2"))
    return x2


def _slot_shapes():
    s = {"w_ada": (N_SLOT, D_MODEL, 6 * D_MODEL // N_SLOT), "w_out": (N_SLOT, D_MODEL // N_SLOT, D_MODEL),
         "w_ffn1": (N_SLOT, D_MODEL, D_FF // N_SLOT), "w_ffn3": (N_SLOT, D_MODEL, D_FF // N_SLOT),
         "w_ffn2": (N_SLOT, D_FF // N_SLOT, D_MODEL), "w_branch3": (N_SLOT, 2 * BRANCH_W, BRANCH_W)}
    for i in range(3):
        s["w_branch%d" % i] = (N_SLOT, BRANCH_W, BRANCH_W)
    for g, src in _in_groups():
        s["in_" + g] = (D_MODEL, len(src))
    return s


def _local_loss(diff, fixed, *, n_lat):
    small = diff["small"]
    n_ctx = fixed["ctx"].shape[0]
    xs = jnp.concatenate([diff["x"], fixed["ctx"]], axis=0)
    cc = jnp.concatenate([fixed["c"], small["c_ctx"][None], jnp.zeros((SUBLANES - 2, D_MODEL), F32)], axis=0)
    (mod_in,) = rowwise(_f_silu, [cc], [], [], out_widths=[D_MODEL], tile=SUBLANES, n_lat=SUBLANES, name="ada_silu")
    tables = _rope_tables(n_lat, n_ctx)
    for l in range(DEPTH):
        xs = _layer(l, xs, mod_in, small, diff["slots"], fixed["gath"], fixed["win"], tables, n_lat=n_lat)
    (lrow,) = rowwise(_f_loss, [xs[:n_lat]], [fixed["target"]], [_row3(small["final_norm_g"])], out_widths=[LANES],
                      tile=256, n_lat=n_lat, name="loss")
    return jnp.sum(lrow)


def _place():
    x, y, c = lax.axis_index("x"), lax.axis_index("y"), lax.axis_index("c")
    chips = [(1 - x, y), (x, 1 - y), (1 - x, 1 - y)]
    return x, y, c, (x, y, 1 - c), chips


def _remote(src, dst, send_sem, recv_sem, dev):
    return pltpu.make_async_remote_copy(src_ref=src, dst_ref=dst, send_sem=send_sem, recv_sem=recv_sem,
                                        device_id=dev, device_id_type=MESH)


def _dma_sems(*shape):
    return pltpu.SemaphoreType.DMA(shape)


def gather_weights(ws):
    n = len(ws)

    def body(*refs):
        w, o = refs[:n], refs[n:2 * n]
        send1, recv1, send2, recv2, lsem = refs[2 * n:]
        x, y, c, sibling, chips = _place()
        g = 2 * x + y
        local = [pltpu.make_async_copy(w[k], o[k].at[g], lsem.at[k]) for k in range(n)]
        for cp in local:
            cp.start()
        sent = []
        for k in range(n):
            for j, (cx, cy) in enumerate(chips):
                cp = _remote(w[k].at[c], o[k].at[g, c], send1.at[k, j], recv1.at[k, j], (cx, cy, c))
                cp.start()
                sent.append(cp)
        for k in range(n):
            for j, (cx, cy) in enumerate(chips):
                gj = 2 * cx + cy
                _remote(w[k].at[c], o[k].at[gj, c], send1.at[k, j], recv1.at[k, j], (cx, cy, c)).wait_recv()
                cp = _remote(o[k].at[gj, c], o[k].at[gj, c], send2.at[k, j], recv2.at[k, j], sibling)
                cp.start()
                sent.append(cp)
        for k in range(n):
            for j, (cx, cy) in enumerate(chips):
                gj = 2 * cx + cy
                _remote(o[k].at[gj, 1 - c], o[k].at[gj, 1 - c], send2.at[k, j], recv2.at[k, j], sibling).wait_recv()
        for cp in sent:
            cp.wait_send()
        for cp in local:
            cp.wait()

    return pl.pallas_call(
        body, name="gather_weights", in_specs=[ANY] * n, out_specs=[ANY] * n,
        out_shape=[jax.ShapeDtypeStruct((N_SLOT,) + w.shape, w.dtype) for w in ws],
        scratch_shapes=[_dma_sems(n, 3), _dma_sems(n, 3), _dma_sems(n, 3), _dma_sems(n, 3), _dma_sems(n)],
    )(*ws)


def allgather_small(buf, *, name):
    m_per = buf.shape[0]

    def body(x_ref, out_ref, send_sems, recv_sems, local_sem):
        x, y, c, sibling, chips = _place()
        me = (x, y, c)

        def rows(px, py, pc):
            return out_ref.at[pl.ds((4 * px + 2 * py + pc) * m_per, m_per), :]

        def copy(k, block, to, src=None):
            return _remote(rows(*block) if src is None else src, rows(*block), send_sems.at[k], recv_sems.at[k], to)

        mine = pltpu.make_async_copy(x_ref, rows(*me), local_sem)
        mine.start()
        first = [copy(0, me, sibling, src=x_ref)]
        first += [copy(1 + j, me, (*chip, c), src=x_ref) for j, chip in enumerate(chips)]
        for cp in first:
            cp.start()
        passed = [copy(4 + j, (*chip, c), sibling) for j, chip in enumerate(chips)]
        for j, chip in enumerate(chips):
            copy(1 + j, (*chip, c), me).wait_recv()
            passed[j].start()
        copy(0, sibling, me).wait_recv()
        for j, chip in enumerate(chips):
            copy(4 + j, (*chip, 1 - c), me).wait_recv()
        for cp in first + passed:
            cp.wait_send()
        mine.wait()

    return pl.pallas_call(
        body, name=name, out_shape=jax.ShapeDtypeStruct((8 * m_per, LANES), buf.dtype),
        in_specs=[pl.BlockSpec(memory_space=pltpu.VMEM)], out_specs=pl.BlockSpec(memory_space=pltpu.VMEM),
        scratch_shapes=[_dma_sems(7), _dma_sems(7), pltpu.SemaphoreType.DMA],
        compiler_params=pltpu.CompilerParams(vmem_limit_bytes=VMEM_LIMIT),
    )(buf)


def sum_blocks(stacked, n_blocks, *, name):
    m = stacked.shape[0] // n_blocks
    width = stacked.shape[1]
    x3 = stacked.reshape((n_blocks, m, width))
    tr = _tile(m, max(SUBLANES, (1 << 18) // width // SUBLANES * SUBLANES), SUBLANES)

    def body(x_ref, o_ref):
        acc = x_ref[0]
        for s in range(1, n_blocks):
            acc = acc + x_ref[s]
        o_ref[...] = acc

    return pl.pallas_call(body, name=name, grid=(m // tr,), in_specs=[pl.BlockSpec((n_blocks, tr, width), lambda i: (0, i, 0))],
                          out_specs=pl.BlockSpec((tr, width), lambda i: (i, 0)),
                          out_shape=jax.ShapeDtypeStruct((m, width), F32), compiler_params=_cparams("parallel"))(x3)


def reduce_pair(gs):
    n = len(gs)

    def body(*refs):
        g, r = refs[:n], refs[n:2 * n]
        send, recv = refs[2 * n:]
        x, y, c, sibling, _ = _place()
        cps = []
        for i in range(n):
            k2 = gs[i].shape[1] // 2
            cp = _remote(g[i].at[:, pl.ds((1 - c) * k2, k2), :], r[i], send.at[i], recv.at[i], sibling)
            cp.start()
            cps.append(cp)
        for cp in cps:
            cp.wait()

    return pl.pallas_call(
        body, name="reduce_pair", in_specs=[ANY] * n, out_specs=[ANY] * n,
        out_shape=[jax.ShapeDtypeStruct((g.shape[0], g.shape[1] // 2, g.shape[2]), g.dtype) for g in gs],
        scratch_shapes=[_dma_sems(n), _dma_sems(n)],
    )(*gs)


def add_own_half(g, recv, core, *, name):
    n_slot, k2, width = recv.shape
    tr = _tile(k2, max(SUBLANES, (1 << 19) // width // SUBLANES * SUBLANES), SUBLANES)
    nb = k2 // tr

    def body(c_ref, g_ref, r_ref, o_ref):
        o_ref[...] = g_ref[...] + r_ref[...]

    spec = pl.BlockSpec((None, tr, width), lambda s, i, c: (s, i, 0))
    return pl.pallas_call(
        body, name=name,
        grid_spec=pltpu.PrefetchScalarGridSpec(
            num_scalar_prefetch=1, grid=(n_slot, nb),
            in_specs=[pl.BlockSpec((None, tr, width), lambda s, i, c: (s, c[0] * nb + i, 0)), spec], out_specs=spec),
        out_shape=jax.ShapeDtypeStruct(recv.shape, F32), compiler_params=_cparams("parallel", "parallel"))(core, g, recv)


def reduce_chips(qs):
    n = len(qs)

    def body(*refs):
        q, r = refs[:n], refs[n:2 * n]
        send, recv, lsem = refs[2 * n:]
        x, y, c, _, chips = _place()
        g = 2 * x + y
        local = [pltpu.make_async_copy(q[i].at[g], r[i].at[g], lsem.at[i]) for i in range(n)]
        for cp in local:
            cp.start()
        for i in range(n):
            for j, (cx, cy) in enumerate(chips):
                _remote(q[i].at[2 * cx + cy], r[i].at[g], send.at[i, j], recv.at[i, j], (cx, cy, c)).start()
        for i in range(n):
            for j, (cx, cy) in enumerate(chips):
                gj = 2 * cx + cy
                _remote(q[i].at[gj], r[i].at[gj], send.at[i, j], recv.at[i, j], (cx, cy, c)).wait()
        for cp in local:
            cp.wait()

    return pl.pallas_call(
        body, name="reduce_chips", in_specs=[ANY] * n, out_specs=[ANY] * n,
        out_shape=[jax.ShapeDtypeStruct(q.shape, q.dtype) for q in qs],
        scratch_shapes=[_dma_sems(n, 3), _dma_sems(n, 3), _dma_sems(n)],
    )(*qs)


def gather_pair(rs, n_weights):
    n = len(rs)

    def body(*refs):
        r, o = refs[:n], refs[n:n + n_weights]
        send, recv, lsem = refs[n + n_weights:]
        x, y, c, sibling, _ = _place()
        cps = []
        for i in range(n):
            k, l = divmod(i, DEPTH)
            k2 = rs[i].shape[0]
            dst = o[k].at[l, pl.ds(c * k2, k2), :]
            lc = pltpu.make_async_copy(r[i], dst, lsem.at[i])
            lc.start()
            cp = _remote(r[i], dst, send.at[i], recv.at[i], sibling)
            cp.start()
            cps.append((lc, cp))
        for lc, cp in cps:
            cp.wait()
            lc.wait()

    return pl.pallas_call(
        body, name="gather_pair", in_specs=[ANY] * n, out_specs=[ANY] * n_weights,
        out_shape=[jax.ShapeDtypeStruct((DEPTH, 2 * rs[DEPTH * k].shape[0], rs[DEPTH * k].shape[1]), F32)
                   for k in range(n_weights)],
        scratch_shapes=[_dma_sems(n), _dma_sems(n), _dma_sems(n)],
    )(*rs)


def adamw(w, g, m, v, *, name):
    rows, width = w.shape
    tr = _tile(rows, max(SUBLANES, (1 << 19) // width // SUBLANES * SUBLANES), SUBLANES)

    def body(w_ref, g_ref, m_ref, v_ref, d_ref, mo_ref, vo_ref):
        gv = g_ref[...]
        mn = ADAM_B1 * m_ref[...] + (1.0 - ADAM_B1) * gv
        vn = ADAM_B2 * v_ref[...] + (1.0 - ADAM_B2) * (gv * gv)
        m_hat = mn / (1.0 - ADAM_B1 ** ADAM_STEP)
        v_hat = vn / (1.0 - ADAM_B2 ** ADAM_STEP)
        d_ref[...] = -ADAM_LR * (m_hat / (jnp.sqrt(v_hat) + ADAM_EPS) + ADAM_WD * w_ref[...])
        mo_ref[...] = mn
        vo_ref[...] = vn

    spec = pl.BlockSpec((tr, width), lambda i: (i, 0))
    sds = jax.ShapeDtypeStruct((rows, width), F32)
    return pl.pallas_call(body, name=name, grid=(rows // tr,), in_specs=[spec] * 4, out_specs=(spec,) * 3,
                          out_shape=(sds,) * 3, compiler_params=_cparams("parallel"))(w, g, m, v)


def _pack(arrs):
    flat = jnp.concatenate([a.reshape(-1) for a in arrs])
    pad = (-flat.shape[0]) % (SUBLANES * LANES)
    return jnp.pad(flat, (0, pad)).reshape((-1, LANES))


def _unpack(buf, shapes):
    flat, out, at = buf.reshape(-1), [], 0
    for s in shapes:
        size = int(np.prod(s))
        out.append(flat[at:at + size].reshape(s))
        at += size
    return out


BIG = ["w_ada", "w_in", "w_branch", "w_out", "w_ffn1", "w_ffn3", "w_ffn2"]
SMALL_SHARDED = ["b_merge", "gla_a2", "gla_ab", "gdn_conv", "m2_conv"]
SMALL_WHOLE = ["c_ctx", "norm1_g", "norm2_g", "b_ada", "gla_norm_g", "na_rpb", "gdn_a_log", "gdn_dt_bias", "gdn_norm_g",
               "m2_conv_b", "m2_a_log", "m2_dt_bias", "m2_d", "m2_norm_g", "final_norm_g"]
WEIGHTS = ["c_ctx", "norm1_g", "norm2_g", "w_ada", "b_ada", "w_in", "b_merge", "gla_a2", "gla_ab", "gla_norm_g", "na_rpb",
           "gdn_conv", "gdn_a_log", "gdn_dt_bias", "gdn_norm_g", "m2_conv", "m2_conv_b", "m2_a_log", "m2_dt_bias", "m2_d",
           "m2_norm_g", "w_branch", "w_out", "w_ffn1", "w_ffn3", "w_ffn2", "final_norm_g"]


def _step(a):
    n_lat = a["x"].shape[1]
    x_i, y_i, c_i = lax.axis_index("x"), lax.axis_index("y"), lax.axis_index("c")
    slot = 2 * x_i + y_i

    gath = dict(zip(BIG, gather_weights([a[n] for n in BIG])))
    shard_shapes = [a[n].shape for n in SMALL_SHARDED]
    own = _pack([a[n] for n in SMALL_SHARDED])
    everyone = allgather_small(own, name="gather_small").reshape((8,) + own.shape)
    per_slot = [_unpack(everyone[2 * s], shard_shapes) for s in range(N_SLOT)]
    small = {n: jnp.concatenate([per_slot[s][i] for s in range(N_SLOT)], axis=-1) for i, n in enumerate(SMALL_SHARDED)}
    small.update({n: a[n] for n in SMALL_WHOLE})

    groups = _in_groups()
    win = []
    for l in range(DEPTH):
        full = gath["w_in"][:, l].transpose((1, 0, 2)).reshape((D_MODEL, IN_COLS))
        win.append({g: _take_cols(full, src) for g, src in groups})
    slots = [{n: jnp.zeros(s, F32) for n, s in _slot_shapes().items()} for _ in range(DEPTH)]
    diff = {"x": a["x"][0], "small": small, "slots": slots}
    fixed = {"ctx": a["ctx"][0], "c": a["c"], "target": a["loss_target"][0], "gath": gath, "win": win}
    loss, grads = jax.value_and_grad(lambda d: _local_loss(d, fixed, n_lat=n_lat))(diff)

    parts = []
    for n in BIG:
        for l in range(DEPTH):
            sl = grads["slots"][l]
            if n == "w_in":
                gin = _untake_cols([(sl["in_" + g], src) for g, src in groups], IN_COLS)
                parts.append(gin.reshape((D_MODEL, N_SLOT, IN_COLS // N_SLOT)).transpose((1, 0, 2)))
            elif n == "w_branch":
                b3 = sl["w_branch3"].reshape((N_SLOT, M2_H, LANES, BRANCH_W))[:, :, :M2_P].reshape((N_SLOT, BRANCH_W, BRANCH_W))
                parts.append(jnp.concatenate([sl["w_branch0"], sl["w_branch1"], sl["w_branch2"], b3], axis=1))
            else:
                parts.append(sl[n])
    core = c_i.astype(jnp.int32).reshape((1,))
    from_sibling = reduce_pair(parts)
    pair_sums = [add_own_half(g, r, core, name="pair_sum%d" % i) for i, (g, r) in enumerate(zip(parts, from_sibling))]
    from_chips = reduce_chips(pair_sums)
    halves = [sum_blocks(r.reshape((-1, r.shape[2])), N_SLOT, name="chip_sum%d" % i) for i, r in enumerate(from_chips)]
    big_grads = dict(zip(BIG, [g.reshape(a[n].shape) for n, g in zip(BIG, gather_pair(halves, len(BIG)))]))

    small_names = SMALL_WHOLE + SMALL_SHARDED
    partial = _pack([grads["small"][n] for n in small_names] + [loss.reshape((1,))])
    total = sum_blocks(allgather_small(partial, name="gather_small_grads"), 8, name="sum_small_grads")
    pieces = _unpack(total, [grads["small"][n].shape for n in small_names] + [(1,)])
    small_grads = dict(zip(small_names, pieces[:-1]))
    for n in SMALL_SHARDED:
        width = a[n].shape[-1]
        small_grads[n] = lax.dynamic_slice_in_dim(small_grads[n], slot * width, width, axis=-1)
    loss_all = pieces[-1].reshape(())

    grad_w, delta, new_m, new_v = {}, {}, {}, {}
    two_d = lambda t: t.reshape((-1, t.shape[-1]))
    for n in BIG:
        d, mn, vn = adamw(two_d(a[n]), two_d(big_grads[n]), two_d(a["m_" + n]), two_d(a["v_" + n]), name="adamw_" + n)
        grad_w[n], delta[n], new_m[n], new_v[n] = big_grads[n], d.reshape(a[n].shape), mn.reshape(a[n].shape), vn.reshape(a[n].shape)
    shapes = [a[n].shape for n in small_names]
    d, mn, vn = adamw(_pack([a[n] for n in small_names]), _pack([small_grads[n] for n in small_names]),
                      _pack([a["m_" + n] for n in small_names]), _pack([a["v_" + n] for n in small_names]), name="adamw_small")
    for n, dd, mm, vv in zip(small_names, _unpack(d, shapes), _unpack(mn, shapes), _unpack(vn, shapes)):
        grad_w[n], delta[n], new_m[n], new_v[n] = small_grads[n], dd, mm, vv

    return (loss_all, grads["x"][None], *[grad_w[n] for n in WEIGHTS], *[delta[n] for n in WEIGHTS],
            *[new_m[n] for n in WEIGHTS], *[new_v[n] for n in WEIGHTS])


def kernel(x, c, ctx, c_ctx, norm1_g, norm2_g, w_ada, b_ada, w_in, b_merge, gla_a2, gla_ab, gla_norm_g, na_rpb, gdn_conv, gdn_a_log, gdn_dt_bias, gdn_norm_g, m2_conv, m2_conv_b, m2_a_log, m2_dt_bias, m2_d, m2_norm_g, w_branch, w_out, w_ffn1, w_ffn3, w_ffn2, final_norm_g, loss_target, m_c_ctx, m_norm1_g, m_norm2_g, m_w_ada, m_b_ada, m_w_in, m_b_merge, m_gla_a2, m_gla_ab, m_gla_norm_g, m_na_rpb, m_gdn_conv, m_gdn_a_log, m_gdn_dt_bias, m_gdn_norm_g, m_m2_conv, m_m2_conv_b, m_m2_a_log, m_m2_dt_bias, m_m2_d, m_m2_norm_g, m_w_branch, m_w_out, m_w_ffn1, m_w_ffn3, m_w_ffn2, m_final_norm_g, v_c_ctx, v_norm1_g, v_norm2_g, v_w_ada, v_b_ada, v_w_in, v_b_merge, v_gla_a2, v_gla_ab, v_gla_norm_g, v_na_rpb, v_gdn_conv, v_gdn_a_log, v_gdn_dt_bias, v_gdn_norm_g, v_m2_conv, v_m2_conv_b, v_m2_a_log, v_m2_dt_bias, v_m2_d, v_m2_norm_g, v_w_branch, v_w_out, v_w_ffn1, v_w_ffn3, v_w_ffn2, v_final_norm_g):
    return _step(dict(locals()))
```

```python
import functools
import math

import numpy as np
import jax
import jax.numpy as jnp
from jax import lax
from jax.experimental import pallas as pl
from jax.experimental.pallas import tpu as pltpu

F32 = jnp.float32
BF16 = jnp.bfloat16
HI = lax.Precision.HIGHEST
MESH = pl.DeviceIdType.MESH
ANY = pl.BlockSpec(memory_space=pl.ANY)

VMEM_LIMIT = 56 * 1024 * 1024
LANES = 128
SUBLANES = 8

D_MODEL = 2048
DEPTH = 2
GRID_W = 64
CHUNK = 64
CONV_W = 5
RMS_EPS = 1e-6
NEG_INF = -1e30
ROPE_BASE = 10000.0
BRANCH_W = 512
GLA_H, GLA_DK, GLA_DV, GLA_LR, GLA_TAU = 4, 64, 128, 16, 16.0
NA_H, NA_D, NA_WIN_R, NA_WIN_C = 4, 128, 8, 16
GDN_H, GDN_D = 4, 128
M2_P, M2_H, M2_N, M2_G = 64, 8, 128, 2
D_FF = 5632
IN_COLS = 14912
N_SLOT = 4
ADAM_LR, ADAM_B1, ADAM_B2, ADAM_EPS, ADAM_WD, ADAM_STEP = 0.001, 0.9, 0.999, 1e-08, 0.01, 10


def _cparams(*sem):
    return pltpu.CompilerParams(dimension_semantics=sem if sem else None, vmem_limit_bytes=VMEM_LIMIT)


def _tile(n, target, mult):
    if n <= target:
        return n
    best = None
    for t in range(mult, target + 1, mult):
        if n % t == 0:
            best = t
    assert best is not None, (n, target, mult)
    return best


def _nt(a, b):
    return lax.dot_general(a.astype(BF16), b.astype(BF16), (((1,), (1,)), ((), ())), preferred_element_type=F32)


def _tn(a, b):
    return lax.dot_general(a.astype(BF16), b.astype(BF16), (((0,), (0,)), ((), ())), preferred_element_type=F32)


def _nn(a, b):
    return jnp.dot(a.astype(BF16), b.astype(BF16), preferred_element_type=F32)


def _nn_hi(a, b):
    return jnp.dot(a, b, precision=HI, preferred_element_type=F32)


def _w_spec(layout, prefix, r_idx, c_idx, br, bc, slot_dim):
    none = (None,) * len(prefix)
    if layout == "plain":
        return pl.BlockSpec(none + (br, bc), lambda i, j, k: prefix + (r_idx(i, j, k), c_idx(i, j, k)))
    if layout == "col":
        per = slot_dim // bc
        return pl.BlockSpec((None,) + none + (br, bc),
                            lambda i, j, k: (c_idx(i, j, k) // per,) + prefix + (r_idx(i, j, k), c_idx(i, j, k) % per))
    per = slot_dim // br
    return pl.BlockSpec((None,) + none + (br, bc),
                        lambda i, j, k: (r_idx(i, j, k) // per,) + prefix + (r_idx(i, j, k) % per, c_idx(i, j, k)))


def _mm(a, b, *, name, ta=False, tb=False, b_layout="plain", b_prefix=(), out_layout="plain", out_dtype=F32):
    m, kdim = (a.shape[1], a.shape[0]) if ta else a.shape
    rows, cols = b.shape[-2:]
    if b_layout == "col":
        cols *= N_SLOT
    elif b_layout == "row":
        rows *= N_SLOT
    n = rows if tb else cols
    assert (cols if tb else rows) == kdim, (a.shape, b.shape, ta, tb)
    n_unit = n // N_SLOT if (out_layout == "col" or (b_layout == ("row" if tb else "col"))) else n
    k_unit = kdim // N_SLOT if b_layout == ("col" if tb else "row") else kdim
    tm = _tile(m, 512 if ta else 768, LANES if ta else 2 * SUBLANES)
    tn = _tile(n_unit, 1408 if n_unit % 1408 == 0 and n_unit % 512 != 0 else 512, LANES)
    k_target = 768 if ta else (1408 if k_unit % 1408 == 0 and k_unit % 512 != 0 else 2048)
    tk = _tile(k_unit, k_target, LANES if (not ta or tb) else 2 * SUBLANES)
    nk = kdim // tk
    a_spec = (pl.BlockSpec((tk, tm), lambda i, j, k: (k, i)) if ta else pl.BlockSpec((tm, tk), lambda i, j, k: (i, k)))
    slot_dim = b.shape[-1] if b_layout == "col" else b.shape[-2]
    if tb:
        b_spec = _w_spec(b_layout, tuple(b_prefix), lambda i, j, k: j, lambda i, j, k: k, tn, tk, slot_dim)
    else:
        b_spec = _w_spec(b_layout, tuple(b_prefix), lambda i, j, k: k, lambda i, j, k: j, tk, tn, slot_dim)
    if out_layout == "col":
        per = (n // N_SLOT) // tn
        out_shape = jax.ShapeDtypeStruct((N_SLOT, m, n // N_SLOT), out_dtype)
        out_spec = pl.BlockSpec((None, tm, tn), lambda i, j, k: (j // per, i, j % per))
    else:
        out_shape = jax.ShapeDtypeStruct((m, n), out_dtype)
        out_spec = pl.BlockSpec((tm, tn), lambda i, j, k: (i, j))
    dims = (((0 if ta else 1,), (1 if tb else 0,)), ((), ()))

    def product(a_ref, b_ref):
        return lax.dot_general(a_ref[...].astype(BF16), b_ref[...].astype(BF16), dims, preferred_element_type=F32)

    def body_once(a_ref, b_ref, o_ref):
        o_ref[...] = product(a_ref, b_ref).astype(o_ref.dtype)

    def body(a_ref, b_ref, o_ref, acc_ref):
        k = pl.program_id(2)

        @pl.when(k == 0)
        def _():
            acc_ref[...] = jnp.zeros_like(acc_ref)

        acc_ref[...] += product(a_ref, b_ref)

        @pl.when(k == nk - 1)
        def _():
            o_ref[...] = acc_ref[...].astype(o_ref.dtype)

    return pl.pallas_call(
        body_once if nk == 1 else body, name=name, grid=(m // tm, n // tn, nk), in_specs=[a_spec, b_spec],
        out_specs=out_spec, out_shape=out_shape, scratch_shapes=[] if nk == 1 else [pltpu.VMEM((tm, tn), F32)],
        compiler_params=_cparams("parallel", "parallel", "arbitrary"),
    )(a, b)


def linear(a, w, grad_slot, *, name, layout="plain", prefix=()):
    @jax.custom_vjp
    def f(a, w, grad_slot):
        return _mm(a, w, name=name + "_fwd", b_layout=layout, b_prefix=prefix)

    def fwd(a, w, grad_slot):
        return f(a, w, grad_slot), (a, w)

    def bwd(res, g):
        a, w = res
        da = _mm(g, w, name=name + "_dgrad", tb=True, b_layout=layout, b_prefix=prefix, out_dtype=a.dtype)
        dw = _mm(a, g, name=name + "_wgrad", ta=True, out_layout="col" if layout == "col" else "plain")
        if layout == "row":
            dw = dw.reshape((N_SLOT, dw.shape[0] // N_SLOT, dw.shape[1]))
        return da, None, dw

    f.defvjp(fwd, bwd)
    return f(a, w, grad_slot)


def _rowwise_specs(rows, consts, params, tile, seg_tile):
    def row_spec(r):
        return pl.BlockSpec((tile, r.shape[1]), lambda i: (i, 0))

    def par_spec(p):
        if p.shape[0] == 2:
            return pl.BlockSpec((None,) + p.shape[1:], lambda i: (jnp.where(i >= seg_tile, 1, 0), 0, 0))
        return pl.BlockSpec((None,) + p.shape[1:], lambda i: (0, 0, 0))

    return [row_spec(r) for r in rows], [row_spec(r) for r in consts], [par_spec(p) for p in params]


def rowwise(f, rows, consts, params, *, out_widths, tile, n_lat, name, out_dtype=F32):
    rows, consts, params = tuple(rows), tuple(consts), tuple(params)
    n_rows = rows[0].shape[0]
    tile = math.gcd(math.gcd(n_rows, n_lat), tile)
    assert tile % SUBLANES == 0
    seg_tile = n_lat // tile
    grid = (n_rows // tile,)
    nr, nc, npar = len(rows), len(consts), len(params)
    r_specs, c_specs, p_specs = _rowwise_specs(rows, consts, params, tile, seg_tile)
    out_shape = tuple(jax.ShapeDtypeStruct((n_rows, w), out_dtype) for w in out_widths)
    out_specs = tuple(pl.BlockSpec((tile, w), lambda i: (i, 0)) for w in out_widths)
    n_out = len(out_widths)

    def fwd_call(rows, consts, params):
        def body(*refs):
            ins = [r[...].astype(F32) for r in refs[:nr + nc + npar]]
            outs = f(*ins)
            for o_ref, o in zip(refs[nr + nc + npar:], outs):
                o_ref[...] = o.astype(o_ref.dtype)

        return pl.pallas_call(body, name=name + "_fwd", grid=grid, in_specs=r_specs + c_specs + p_specs,
                              out_specs=out_specs, out_shape=out_shape,
                              compiler_params=_cparams("parallel"))(*rows, *consts, *params)

    def bwd_call(rows, consts, params, gouts):
        def body(*refs):
            i = pl.program_id(0)
            ins = [r[...].astype(F32) for r in refs[:nr + nc + npar]]
            gs = tuple(r[...].astype(F32) for r in refs[nr + nc + npar:nr + nc + npar + n_out])
            d_refs = refs[nr + nc + npar + n_out:]
            cvals = ins[nr:nr + nc]

            def g(*diff):
                return tuple(f(*diff[:nr], *cvals, *diff[nr:]))

            _, vjp = jax.vjp(g, *ins[:nr], *ins[nr + nc:])
            grads = vjp(gs)
            for d_ref, gr in zip(d_refs[:nr], grads[:nr]):
                d_ref[...] = gr.astype(d_ref.dtype)
            for p, d_ref, gr in zip(params, d_refs[nr:], grads[nr:]):
                first = (i == 0) | (i == seg_tile) if p.shape[0] == 2 else (i == 0)

                @pl.when(first)
                def _():
                    d_ref[...] = jnp.zeros_like(d_ref)

                d_ref[...] += gr

        d_shape = tuple(jax.ShapeDtypeStruct(r.shape, r.dtype) for r in rows) + tuple(
            jax.ShapeDtypeStruct(p.shape, F32) for p in params)
        g_specs = [pl.BlockSpec((tile, w), lambda i: (i, 0)) for w in out_widths]
        return pl.pallas_call(body, name=name + "_bwd", grid=grid,
                              in_specs=r_specs + c_specs + p_specs + g_specs,
                              out_specs=tuple(r_specs + p_specs), out_shape=d_shape,
                              compiler_params=_cparams("arbitrary"))(*rows, *consts, *params, *gouts)

    @jax.custom_vjp
    def op(rows, consts, params):
        return fwd_call(rows, consts, params)

    def op_fwd(rows, consts, params):
        return op(rows, consts, params), (rows, consts, params)

    def op_bwd(res, gouts):
        rows, consts, params = res
        d = bwd_call(rows, consts, params, tuple(gouts))
        return tuple(d[:nr]), tuple(None for _ in consts), tuple(d[nr:])

    op.defvjp(op_fwd, op_bwd)
    return op(rows, consts, params)


def _rms(x, width=None):
    w = x.shape[-1] if width is None else width
    return x * lax.rsqrt(jnp.sum(x * x, axis=-1, keepdims=True) * (1.0 / w) + RMS_EPS)


def _silu(x):
    return x * jax.nn.sigmoid(x)


def _f_modnorm(x, g, sc, sh):
    return ((_rms(x) * g) * (1.0 + sc) + sh,)


def _f_silu(x):
    return (_silu(x),)


def _f_gla_prep(lr, a2, ab):
    z = _nn(lr, a2) + ab
    return ((jnp.minimum(z, 0.0) - jnp.log(1.0 + jnp.exp(-jnp.abs(z)))) * (1.0 / GLA_TAU),)


def _f_headnorm_gate(o, g, ng):
    outs = []
    for h in range(BRANCH_W // LANES):
        lo = h * LANES
        oh = o[:, lo:lo + LANES] + o[:, BRANCH_W + lo:BRANCH_W + lo + LANES]
        outs.append(_rms(oh) * ng * _silu(g[:, lo:lo + LANES]))
    return (jnp.concatenate(outs, axis=-1),)


def _f_gdn_prep(x, alog, dtb):
    half = x.shape[1] // 2
    beta = jax.nn.sigmoid(x[:, :half])
    la = -jnp.exp(alog) * jax.nn.softplus(x[:, half:] + dtb)
    return beta, la


def _f_m2_prep(x, alog, dtb):
    dt = jax.nn.softplus(x + dtb)
    return dt, -jnp.exp(alog) * dt


def _f_m2_fin(o, z, xs, dskip, ng):
    w = z.shape[1]
    y = (o[:, :w] + o[:, w:] + dskip * xs) * _silu(z)
    return (_rms(y, BRANCH_W) * ng,)


def _f_merge(gate, z0, z1, z2, z3, bm):
    acc = None
    for i, z in enumerate((z0, z1, z2, z3)):
        lo = i * D_MODEL
        t = jax.nn.sigmoid(gate[:, lo:lo + D_MODEL] + bm[:, lo:lo + D_MODEL]) * z
        acc = t if acc is None else acc + t
    return (acc,)


def _f_resid(x, y, g):
    return (x + g * y,)


def _f_swiglu(u1, u3):
    return (_silu(u1) * u3,)


def _f_loss(x, tgt, g):
    e = _rms(x) * g - tgt
    per_row = 0.5 * jnp.sum(e * e, axis=-1, keepdims=True) * (1.0 / D_MODEL)
    return (jnp.broadcast_to(per_row * (1.0 / LANES), (x.shape[0], LANES)),)


_HALO = 8


def _conv_segments(n_lat, n_ctx):
    segs = [(0, _HALO, n_lat), (n_lat, n_lat + 3 * _HALO, n_ctx)]
    return segs, n_lat + n_ctx + 4 * _HALO


def _conv_stage(buf, src, n_lat, n_ctx):
    zeros = jnp.zeros((_HALO, LANES), F32)
    buf[0:_HALO, :] = zeros
    buf[_HALO:_HALO + n_lat, :] = src[0:n_lat, :]
    buf[n_lat + _HALO:n_lat + 2 * _HALO, :] = zeros
    buf[n_lat + 2 * _HALO:n_lat + 3 * _HALO, :] = zeros
    buf[n_lat + 3 * _HALO:n_lat + 3 * _HALO + n_ctx, :] = src[n_lat:n_lat + n_ctx, :]
    buf[n_lat + n_ctx + 3 * _HALO:n_lat + n_ctx + 4 * _HALO, :] = zeros


def conv_silu(x, w, b, *, n_lat, name):
    n_rows, n_ch = x.shape
    n_ctx = n_rows - n_lat
    segs, n_buf = _conv_segments(n_lat, n_ctx)
    grid = (n_ch // LANES,)
    col = lambda r: pl.BlockSpec((r, LANES), lambda j: (0, j))
    half = CONV_W // 2

    def tiles():
        for row0, off, length in segs:
            tr = _tile(length, 256, SUBLANES)
            for t0 in range(0, length, tr):
                yield row0 + t0, off + t0, tr

    def pre_act(buf, w_ref, b_ref, off, tr):
        acc = jnp.broadcast_to(b_ref[...], (tr, LANES))
        for j in range(CONV_W):
            acc = acc + w_ref[j:j + 1, :] * buf[off + j - half:off + j - half + tr, :]
        return acc

    def fwd_call(x, w, b):
        def body(x_ref, w_ref, b_ref, o_ref, buf):
            _conv_stage(buf, x_ref, n_lat, n_ctx)
            for row, off, tr in tiles():
                o_ref[row:row + tr, :] = _silu(pre_act(buf, w_ref, b_ref, off, tr))

        return pl.pallas_call(body, name=name + "_fwd", grid=grid, in_specs=[col(n_rows), col(CONV_W), col(1)],
                              out_specs=col(n_rows), out_shape=jax.ShapeDtypeStruct(x.shape, F32),
                              scratch_shapes=[pltpu.VMEM((n_buf, LANES), F32)],
                              compiler_params=_cparams("parallel"))(x, w, b)

    def bwd_call(x, w, b, g):
        def body(x_ref, w_ref, b_ref, g_ref, dx_ref, dw_ref, db_ref, xbuf, dbuf):
            _conv_stage(xbuf, x_ref, n_lat, n_ctx)
            _conv_stage(dbuf, g_ref, n_lat, n_ctx)
            dw = [jnp.zeros((1, LANES), F32) for _ in range(CONV_W)]
            db = jnp.zeros((1, LANES), F32)
            for row, off, tr in tiles():
                pre = pre_act(xbuf, w_ref, b_ref, off, tr)
                s = jax.nn.sigmoid(pre)
                dpre = g_ref[row:row + tr, :] * (s * (1.0 + pre * (1.0 - s)))
                dbuf[off:off + tr, :] = dpre
                db = db + jnp.sum(dpre, axis=0, keepdims=True)
                for j in range(CONV_W):
                    dw[j] = dw[j] + jnp.sum(dpre * xbuf[off + j - half:off + j - half + tr, :], axis=0, keepdims=True)
            for row, off, tr in tiles():
                acc = jnp.zeros((tr, LANES), F32)
                for j in range(CONV_W):
                    acc = acc + w_ref[j:j + 1, :] * dbuf[off - j + half:off - j + half + tr, :]
                dx_ref[row:row + tr, :] = acc
            for j in range(CONV_W):
                dw_ref[j:j + 1, :] = dw[j]
            db_ref[...] = db

        return pl.pallas_call(
            body, name=name + "_bwd", grid=grid, in_specs=[col(n_rows), col(CONV_W), col(1), col(n_rows)],
            out_specs=(col(n_rows), col(CONV_W), col(1)),
            out_shape=(jax.ShapeDtypeStruct(x.shape, F32), jax.ShapeDtypeStruct(w.shape, F32),
                       jax.ShapeDtypeStruct(b.shape, F32)),
            scratch_shapes=[pltpu.VMEM((n_buf, LANES), F32), pltpu.VMEM((n_buf, LANES), F32)],
            compiler_params=_cparams("parallel"))(x, w, b, g)

    @jax.custom_vjp
    def op(x, w, b):
        return fwd_call(x, w, b)

    op.defvjp(lambda x, w, b: (op(x, w, b), (x, w, b)), lambda res, g: bwd_call(*res, g))
    return op(x, w, b)


def chunk_scan(step, shared, shared_lanes, perdir, perdir_lanes, consts, *, heads, state_shape, out_w, n_lat, name):
    shared, perdir, consts = tuple(shared), tuple(perdir), tuple(consts)
    n_rows = shared[0].shape[0]
    nl, ncx = n_lat // CHUNK, (n_rows - n_lat) // CHUNK
    n_chunks = nl + ncx
    ow_all = heads * out_w
    ns, npd, ncst = len(shared), len(perdir), len(consts)

    def cidx(d, n):
        m = n - ncx
        return jnp.where(n < ncx, nl + jnp.where(d == 0, n, ncx - 1 - n), jnp.where(d == 0, m, nl - 1 - m))

    def specs(order):
        sh = [pl.BlockSpec((CHUNK, a.shape[1]), lambda d, n: (cidx(d, order(n)), 0)) for a in shared]
        pd = [pl.BlockSpec((CHUNK, a.shape[1] // 2), lambda d, n: (cidx(d, order(n)), d)) for a in perdir]
        cs = [pl.BlockSpec((CHUNK, a.shape[1]), lambda d, n: (cidx(d, order(n)), 0)) for a in consts]
        o = pl.BlockSpec((CHUNK, ow_all), lambda d, n: (cidx(d, order(n)), d))
        st = pl.BlockSpec((None, None, heads) + state_shape, lambda d, n: (d, order(n), 0) + (0,) * len(state_shape))
        return sh, pd, cs, o, st

    def mask(d):
        r = lax.broadcasted_iota(jnp.int32, (CHUNK, CHUNK), 0)
        c = lax.broadcasted_iota(jnp.int32, (CHUNK, CHUNK), 1)
        lower = jnp.where(r >= c, 1.0, 0.0).astype(F32)
        upper = jnp.where(r <= c, 1.0, 0.0).astype(F32)
        return jnp.where(d == 0, lower, upper)

    def head_slices(h):
        out = []
        for lanes in tuple(shared_lanes) + tuple(perdir_lanes):
            out.append([slice(off + (h // hpg) * w, off + (h // hpg) * w + w) for off, w, hpg in lanes])
        return out

    def load(refs, h):
        return tuple(tuple(ref[:, s] for s in sl) for ref, sl in zip(refs, head_slices(h)))

    state_sds = jax.ShapeDtypeStruct((2, n_chunks, heads) + state_shape, F32)

    def fwd_call(shared, perdir, consts):
        sh, pd, cs, o_spec, st_spec = specs(lambda n: n)

        def body(*refs):
            in_refs = refs[:ns + npd]
            c_refs = refs[ns + npd:ns + npd + ncst]
            o_ref, ss_ref, s_scr = refs[ns + npd + ncst:]
            d, n = pl.program_id(0), pl.program_id(1)

            @pl.when(n == 0)
            def _():
                s_scr[...] = jnp.zeros_like(s_scr)

            m = mask(d)
            cv = tuple(c[...] for c in c_refs)
            for h in range(heads):
                s0 = s_scr[h]
                o_h, s_new = step(load(in_refs, h), cv, s0, m)
                o_ref[:, h * out_w:(h + 1) * out_w] = o_h
                ss_ref[h] = s0
                s_scr[h] = s_new

        return pl.pallas_call(
            body, name=name + "_fwd", grid=(2, n_chunks), in_specs=sh + pd + cs, out_specs=(o_spec, st_spec),
            out_shape=(jax.ShapeDtypeStruct((n_rows, 2 * ow_all), F32), state_sds),
            scratch_shapes=[pltpu.VMEM((heads,) + state_shape, F32)],
            compiler_params=_cparams("arbitrary", "arbitrary"))(*shared, *perdir, *consts)

    def bwd_call(shared, perdir, consts, starts, g):
        sh, pd, cs, o_spec, st_spec = specs(lambda n: n_chunks - 1 - n)
        dsh = [pl.BlockSpec((CHUNK, a.shape[1]), lambda d, n: (cidx(d, n_chunks - 1 - n), d)) for a in shared]

        def body(*refs):
            in_refs = refs[:ns + npd]
            c_refs = refs[ns + npd:ns + npd + ncst]
            ss_ref, g_ref = refs[ns + npd + ncst:ns + npd + ncst + 2]
            d_refs = refs[ns + npd + ncst + 2:ns + npd + ncst + 2 + ns + npd]
            ds_scr = refs[-1]
            d, n = pl.program_id(0), pl.program_id(1)

            @pl.when(n == 0)
            def _():
                ds_scr[...] = jnp.zeros_like(ds_scr)

            for d_ref in d_refs:
                d_ref[...] = jnp.zeros_like(d_ref)
            m = mask(d)
            cv = tuple(c[...] for c in c_refs)
            for h in range(heads):
                _, vjp = jax.vjp(lambda ins, s: step(ins, cv, s, m), load(in_refs, h), ss_ref[h])
                g_ins, g_s = vjp((g_ref[:, h * out_w:(h + 1) * out_w], ds_scr[h]))
                for d_ref, sl, gr in zip(d_refs, head_slices(h), g_ins):
                    for s, gv in zip(sl, gr):
                        d_ref[:, s] += gv
                ds_scr[h] = g_s

        d_shape = tuple(jax.ShapeDtypeStruct((n_rows, 2 * a.shape[1]), F32) for a in shared) + tuple(
            jax.ShapeDtypeStruct(a.shape, F32) for a in perdir)
        return pl.pallas_call(
            body, name=name + "_bwd", grid=(2, n_chunks), in_specs=sh + pd + cs + [st_spec, o_spec],
            out_specs=tuple(dsh + pd), out_shape=d_shape,
            scratch_shapes=[pltpu.VMEM((heads,) + state_shape, F32)],
            compiler_params=_cparams("arbitrary", "arbitrary"))(*shared, *perdir, *consts, starts, g)

    @jax.custom_vjp
    def op(shared, perdir, consts):
        return fwd_call(shared, perdir, consts)[0]

    def op_fwd(shared, perdir, consts):
        o, starts = fwd_call(shared, perdir, consts)
        return o, (shared, perdir, consts, starts)

    def op_bwd(res, g):
        shared, perdir, consts, starts = res
        d = bwd_call(shared, perdir, consts, starts, g)
        d_sh = tuple(a[:, :a.shape[1] // 2] + a[:, a.shape[1] // 2:] for a in d[:ns])
        return d_sh, tuple(d[ns:]), tuple(None for _ in consts)

    op.defvjp(op_fwd, op_bwd)
    return op(shared, perdir, consts)


@jax.custom_vjp
def _swap_halves(x):
    return pltpu.roll(x, LANES // 2, 1)


_swap_halves.defvjp(lambda x: (_swap_halves(x), None), lambda _, g: (_swap_halves(g),))


def _cum_decay(la, m, width):
    lab = jnp.broadcast_to(la, (CHUNK, width))
    return _nn_hi(m, lab), jnp.sum(lab, axis=0, keepdims=True)


def _pair_decay(la, m):
    r = lax.broadcasted_iota(jnp.int32, (CHUNK, CHUNK), 0)
    c = lax.broadcasted_iota(jnp.int32, (CHUNK, CHUNK), 1)
    eye = jnp.where(r == c, 1.0, 0.0).astype(F32)
    b = _nn_hi(m, jnp.broadcast_to(la, (CHUNK, CHUNK)))
    b_row = _nn_hi(jnp.ones((CHUNK, CHUNK), F32), eye * b)
    diff = b - b_row
    incl = jnp.where(m > 0, jnp.exp(jnp.where(m > 0, diff, 0.0)), 0.0)
    ms = m - eye
    strict = jnp.where(ms > 0, jnp.exp(jnp.where(ms > 0, diff, 0.0)), 0.0)
    return incl, strict, eye


def _gla_step(ins, consts, st, m):
    (q, k), (v,), (la,) = ins
    cos, sin = consts
    q = (q * cos + _swap_halves(q) * sin) * (GLA_DK ** -0.5)
    k = k * cos + _swap_halves(k) * sin
    b = _nn_hi(m, la)
    bl = jnp.sum(la, axis=0, keepdims=True)
    qi = q * jnp.exp(b)
    ki = k * jnp.exp(-b)
    att = _nt(qi, ki) * m
    o = _nt(qi, st) + _nn(att, v)
    st_new = st * jnp.exp(bl) + _tn(v, k * jnp.exp(bl - b))
    return o, st_new


def _l2n(x):
    return x * lax.rsqrt(jnp.sum(x * x, axis=-1, keepdims=True) + RMS_EPS)


def _tri_inv_fwd(nmat):
    r = lax.broadcasted_iota(jnp.int32, (CHUNK, CHUNK), 0)
    c = lax.broadcasted_iota(jnp.int32, (CHUNK, CHUNK), 1)
    inv = jnp.where(r == c, 1.0, 0.0).astype(F32) - nmat
    p = nmat
    for _ in range(5):
        p = _nn_hi(p, p)
        inv = inv + _nn_hi(inv, p)
    return inv


@jax.custom_vjp
def _unit_tri_inv(nmat):
    return _tri_inv_fwd(nmat)


def _unit_tri_inv_bwd(inv, g):
    t = lax.dot_general(inv, g, (((0,), (0,)), ((), ())), precision=HI, preferred_element_type=F32)
    return (-lax.dot_general(t, inv, (((1,), (1,)), ((), ())), precision=HI, preferred_element_type=F32),)


_unit_tri_inv.defvjp(lambda nmat: (lambda inv: (inv, inv))(_tri_inv_fwd(nmat)), _unit_tri_inv_bwd)


def _gdn_step(ins, consts, s, m):
    (q, k, v), (beta,), (la,) = ins
    q = _l2n(q) * (GDN_D ** -0.5)
    k = _l2n(k)
    b, bl = _cum_decay(la, m, GDN_D)
    incl, strict, eye = _pair_decay(la, m)
    inv = _unit_tri_inv(beta * _nt(k, k) * strict)
    w = _nn_hi(inv, k * (beta * jnp.exp(b)))
    u0 = _nn_hi(inv, v * beta)
    u = u0 - _nn(w, s)
    s_new = jnp.exp(bl) * s + _tn(k * jnp.exp(bl - b), u)
    o = jnp.exp(b) * _nn(q, s) + _nn(_nt(q, k) * incl, u)
    return o, s_new


def _ssd_step(ins, consts, s, m):
    (xs,), (bm, cm), (dt,), (la,) = ins
    xv = xs * dt
    b, bl = _cum_decay(la, m, M2_N)
    incl, _, _ = _pair_decay(la, m)
    o = jnp.exp(b) * _nn(cm, s) + _nn(_nt(cm, bm) * incl, xv)
    s_new = jnp.exp(bl) * s + _tn(bm * jnp.exp(bl - b), xv)
    return o, s_new


def _na_tile(q, kw, vw, kc, vc, bias):
    qs = q * (NA_D ** -0.5)
    s1 = _nt(qs, kw) + bias
    s2 = _nt(qs, kc)
    mx = lax.stop_gradient(jnp.maximum(jnp.max(s1, axis=-1, keepdims=True), jnp.max(s2, axis=-1, keepdims=True)))
    p1 = jnp.exp(s1 - mx)
    p2 = jnp.exp(s2 - mx)
    den = jnp.sum(p1, axis=-1, keepdims=True) + jnp.sum(p2, axis=-1, keepdims=True)
    return (_nn(p1, vw) + _nn(p2, vc)) / den


def _ctx_tile(q, k, v):
    s = _nt(q * (NA_D ** -0.5), k)
    p = jnp.exp(s - lax.stop_gradient(jnp.max(s, axis=-1, keepdims=True)))
    return _nn(p, v) / jnp.sum(p, axis=-1, keepdims=True)


def natten(q, k, v, bias, *, n_lat, name):
    n_rows = q.shape[0]
    n_ctx = n_rows - n_lat
    g_rows = n_lat // GRID_W
    win = NA_WIN_R * GRID_W
    ctx_blk = n_lat // n_ctx

    def start(n):
        return jnp.clip(n - NA_WIN_R // 2, 0, g_rows - NA_WIN_R)

    def case(n):
        return n - start(n)

    q_spec = pl.BlockSpec((GRID_W, LANES), lambda h, n: (n, h))
    lat_spec = pl.BlockSpec((n_lat, LANES), lambda h, n: (0, h))
    ctx_in = pl.BlockSpec((n_ctx, LANES), lambda h, n: (ctx_blk, h))
    ctx_out = pl.BlockSpec((n_ctx, LANES), lambda h, n: (0, h))
    bias_spec = pl.BlockSpec((None, None, GRID_W, win), lambda h, n: (h, case(n), 0, 0))
    lat_sds = jax.ShapeDtypeStruct((n_lat, BRANCH_W), F32)
    ctx_sds = jax.ShapeDtypeStruct((n_ctx, BRANCH_W), F32)

    def lat_fwd(q, k, v, bias):
        def body(q_ref, k_ref, v_ref, kc_ref, vc_ref, b_ref, o_ref):
            r0 = pl.multiple_of(start(pl.program_id(1)) * GRID_W, GRID_W)
            o_ref[...] = _na_tile(q_ref[...], k_ref[pl.ds(r0, win), :], v_ref[pl.ds(r0, win), :],
                                  kc_ref[...], vc_ref[...], b_ref[...])

        return pl.pallas_call(body, name=name + "_lat_fwd", grid=(NA_H, g_rows),
                              in_specs=[q_spec, lat_spec, lat_spec, ctx_in, ctx_in, bias_spec], out_specs=q_spec,
                              out_shape=lat_sds, compiler_params=_cparams("parallel", "arbitrary"))(q, k, v, k, v, bias)

    def lat_bwd(q, k, v, bias, g):
        def body(q_ref, k_ref, v_ref, kc_ref, vc_ref, b_ref, g_ref, dq_ref, dk_ref, dv_ref, dkc_ref, dvc_ref, db_ref):
            n = pl.program_id(1)
            r0 = pl.multiple_of(start(n) * GRID_W, GRID_W)

            @pl.when(n == 0)
            def _():
                for r in (dk_ref, dv_ref, dkc_ref, dvc_ref):
                    r[...] = jnp.zeros_like(r)

            @pl.when((n == 0) | (case(n) != case(jnp.maximum(n - 1, 0))))
            def _():
                db_ref[...] = jnp.zeros_like(db_ref)

            _, vjp = jax.vjp(_na_tile, q_ref[...], k_ref[pl.ds(r0, win), :], v_ref[pl.ds(r0, win), :],
                             kc_ref[...], vc_ref[...], b_ref[...])
            dq, dkw, dvw, dkc, dvc, db = vjp(g_ref[...])
            dq_ref[...] = dq
            dk_ref[pl.ds(r0, win), :] += dkw
            dv_ref[pl.ds(r0, win), :] += dvw
            dkc_ref[...] += dkc
            dvc_ref[...] += dvc
            db_ref[...] += db

        return pl.pallas_call(
            body, name=name + "_lat_bwd", grid=(NA_H, g_rows),
            in_specs=[q_spec, lat_spec, lat_spec, ctx_in, ctx_in, bias_spec, q_spec],
            out_specs=(q_spec, lat_spec, lat_spec, ctx_out, ctx_out, bias_spec),
            out_shape=(lat_sds, lat_sds, lat_sds, ctx_sds, ctx_sds, jax.ShapeDtypeStruct(bias.shape, F32)),
            compiler_params=_cparams("parallel", "arbitrary"))(q, k, v, k, v, bias, g)

    c_in = pl.BlockSpec((n_ctx, LANES), lambda h: (ctx_blk, h))
    c_out = pl.BlockSpec((n_ctx, LANES), lambda h: (0, h))

    def ctx_fwd(q, k, v):
        def body(q_ref, k_ref, v_ref, o_ref):
            o_ref[...] = _ctx_tile(q_ref[...], k_ref[...], v_ref[...])

        return pl.pallas_call(body, name=name + "_ctx_fwd", grid=(NA_H,), in_specs=[c_in, c_in, c_in], out_specs=c_out,
                              out_shape=ctx_sds, compiler_params=_cparams("parallel"))(q, k, v)

    def ctx_bwd(q, k, v, g):
        def body(q_ref, k_ref, v_ref, g_ref, dq_ref, dk_ref, dv_ref):
            _, vjp = jax.vjp(_ctx_tile, q_ref[...], k_ref[...], v_ref[...])
            dq_ref[...], dk_ref[...], dv_ref[...] = vjp(g_ref[...])

        return pl.pallas_call(body, name=name + "_ctx_bwd", grid=(NA_H,), in_specs=[c_in, c_in, c_in, c_out],
                              out_specs=(c_out, c_out, c_out), out_shape=(ctx_sds, ctx_sds, ctx_sds),
                              compiler_params=_cparams("parallel"))(q, k, v, g)

    @jax.custom_vjp
    def op(q, k, v, bias):
        return jnp.concatenate([lat_fwd(q, k, v, bias), ctx_fwd(q, k, v)], axis=0)

    def op_bwd(res, g):
        q, k, v, bias = res
        dq, dk, dv, dkc, dvc, db = lat_bwd(q, k, v, bias, g[:n_lat])
        dqc, dkc2, dvc2 = ctx_bwd(q, k, v, g[n_lat:])
        return (jnp.concatenate([dq, dqc], axis=0), jnp.concatenate([dk, dkc + dkc2], axis=0),
                jnp.concatenate([dv, dvc + dvc2], axis=0), db)

    op.defvjp(lambda q, k, v, bias: (op(q, k, v, bias), (q, k, v, bias)), op_bwd)
    return op(q, k, v, bias)


def _runs(src):
    src = np.asarray(src)
    out, i = [], 0
    while i < len(src):
        j = i + 1
        if src[i] < 0:
            while j < len(src) and src[j] < 0:
                j += 1
            out.append((-1, j - i))
        else:
            while j < len(src) and src[j] == src[j - 1] + 1:
                j += 1
            out.append((int(src[i]), j - i))
        i = j
    return out


def _take_cols(w, src):
    pieces = [jnp.zeros(w.shape[:-1] + (ln,), w.dtype) if s < 0 else w[..., s:s + ln] for s, ln in _runs(src)]
    return pieces[0] if len(pieces) == 1 else jnp.concatenate(pieces, axis=-1)


def _untake_cols(parts, n_cols):
    found = []
    for arr, src in parts:
        pos = 0
        for s, ln in _runs(src):
            if s >= 0:
                found.append((s, arr[..., pos:pos + ln]))
            pos += ln
    found.sort(key=lambda t: t[0])
    at = 0
    for s, piece in found:
        assert s == at, (s, at)
        at += piece.shape[-1]
    assert at == n_cols, (at, n_cols)
    return jnp.concatenate([p for _, p in found], axis=-1)


def _pad_heads(base, heads, real, width):
    return np.concatenate([np.concatenate([base + h * real + np.arange(real), -np.ones(width - real, np.int64)])
                           for h in range(heads)])


def _rope_heads(base, heads):
    z = -np.ones(32, np.int64)
    return np.concatenate([np.concatenate([base + h * 64 + np.arange(32), z, base + h * 64 + 32 + np.arange(32), z])
                           for h in range(heads)])


def _lane_block(base, n):
    return np.concatenate([base + np.arange(n), -np.ones(LANES - n, np.int64)])


def _in_groups():
    g0, n0, d0, m0, t0 = 0, 1568, 3104, 5168, 6720
    rng = lambda a, n: a + np.arange(n)
    return [
        ("gla_qk", np.concatenate([_rope_heads(g0, GLA_H), _rope_heads(g0 + 256, GLA_H)])),
        ("gla_v", rng(g0 + 512, 512)),
        ("gla_g", rng(g0 + 1024, 512)),
        ("gla_lr", _lane_block(g0 + 1536, 2 * GLA_LR)),
        ("na_q", rng(n0, 512)), ("na_k", rng(n0 + 512, 512)), ("na_v", rng(n0 + 1024, 512)),
        ("gdn_qkv", rng(d0, 1536)),
        ("gdn_z", rng(d0 + 1536, 512)),
        ("gdn_sm", np.concatenate([_lane_block(d0 + 2048 + 4 * i, GDN_H) for i in range(4)])),
        ("m2_z", _pad_heads(m0, M2_H, M2_P, LANES)),
        ("m2_xs", _pad_heads(m0 + 512, M2_H, M2_P, LANES)),
        ("m2_bc", rng(m0 + 1024, 512)),
        ("m2_dt", np.concatenate([_lane_block(m0 + 1536, M2_H), _lane_block(m0 + 1536 + M2_H, M2_H)])),
        ("gate", rng(t0, 4 * D_MODEL)),
    ]


_M2_PAD = _pad_heads(0, M2_H, M2_P, LANES)
_GLA_PAD = _rope_heads(0, GLA_H)


def _row3(v):
    return v.reshape((1, 1, -1))


def _dir_rows(p, n):
    return _row3(jnp.concatenate([_take_cols(p[d][None], _lane_block(0, n)) for d in range(2)], axis=-1))


def _rope_tables(n_lat, n_ctx):
    n_freq = GLA_DK // 4
    freqs = ROPE_BASE ** (-jnp.arange(n_freq, dtype=F32) / n_freq)
    t = jnp.arange(n_lat)
    row = (t // GRID_W).astype(F32)
    colv = (t % GRID_W).astype(F32)
    ang = jnp.concatenate([row[:, None] * freqs, colv[:, None] * freqs], axis=-1)
    c, s = jnp.cos(ang), jnp.sin(ang)
    one, zero = jnp.ones_like(c), jnp.zeros_like(c)
    cos_t = jnp.concatenate([c, one, c, one], axis=-1)
    sin_t = jnp.concatenate([-s, zero, s, zero], axis=-1)
    return (jnp.concatenate([cos_t, jnp.ones((n_ctx, LANES), F32)], axis=0),
            jnp.concatenate([sin_t, jnp.zeros((n_ctx, LANES), F32)], axis=0))


def _na_bias(rpb):
    case = np.arange(NA_WIN_R)
    r = np.arange(NA_WIN_R)
    dr = r[None, :] - case[:, None] + NA_WIN_R - 1
    ci = np.arange(GRID_W)
    dc = np.clip(ci[None, :] - ci[:, None], 1 - NA_WIN_C, NA_WIN_C - 1) + NA_WIN_C - 1
    c0 = np.clip(ci - NA_WIN_C // 2, 0, GRID_W - NA_WIN_C)
    ok = (ci[None, :] >= c0[:, None]) & (ci[None, :] < c0[:, None] + NA_WIN_C)
    pick_r = np.zeros((NA_WIN_R, NA_WIN_R, 2 * NA_WIN_R - 1), np.float32)
    pick_r[case[:, None], r[None, :], dr] = 1.0
    pick_c = np.zeros((2 * NA_WIN_C - 1, GRID_W, GRID_W), np.float32)
    pick_c[dc, ci[:, None], ci[None, :]] = 1.0
    rows = jnp.einsum("hdk,crd->hcrk", rpb, pick_r, precision=HI)
    tbl = jnp.einsum("hcrk,kij->hcirj", rows, pick_c, precision=HI)
    tbl = jnp.where(ok[None, None, :, None, :], tbl, NEG_INF)
    return tbl.reshape((NA_H, NA_WIN_R, GRID_W, NA_WIN_R * GRID_W))


def _layer(l, xs, mod_in, small, slots, gath, win, tables, *, n_lat):
    rw = functools.partial(rowwise, n_lat=n_lat)
    rwb = functools.partial(rowwise, n_lat=n_lat, out_dtype=BF16)
    nm = lambda s: "l%d_%s" % (l, s)
    sl = slots[l]
    mod = linear(mod_in, gath["w_ada"], sl["w_ada"], name=nm("ada"), layout="col", prefix=(l,)) + small["b_ada"][l]
    sh1, sc1, g1, sh2, sc2, g2 = [mod[0:2, i * D_MODEL:(i + 1) * D_MODEL].reshape((2, 1, D_MODEL)) for i in range(6)]
    (h,) = rwb(_f_modnorm, [xs], [], [_row3(small["norm1_g"][l]), sc1, sh1], out_widths=[D_MODEL], tile=256,
              name=nm("norm1"))
    p = {g: linear(h, win[l][g], sl["in_" + g], name=nm("in_" + g)) for g, _ in _in_groups()}

    a2 = small["gla_a2"][l]
    a2p = jnp.concatenate([
        jnp.concatenate([_take_cols(a2[0], _GLA_PAD), jnp.zeros((GLA_LR, 512), F32)], axis=1),
        jnp.concatenate([jnp.zeros((GLA_LR, 512), F32), _take_cols(a2[1], _GLA_PAD)], axis=1),
        jnp.zeros((LANES - 2 * GLA_LR, 1024), F32)], axis=0)[None]
    abp = _row3(jnp.concatenate([_take_cols(small["gla_ab"][l][d][None], _GLA_PAD) for d in range(2)], axis=-1))
    (la,) = rw(_f_gla_prep, [p["gla_lr"]], [], [a2p, abp], out_widths=[1024], tile=256, name=nm("gla_prep"))
    head = lambda off: (off, LANES, 1)
    o = chunk_scan(_gla_step, [p["gla_qk"], p["gla_v"]], [[head(0), head(512)], [head(0)]],
                   [la], [[head(0)]], tables, heads=GLA_H, state_shape=(GLA_DV, LANES), out_w=GLA_DV, n_lat=n_lat,
                   name=nm("gla_scan"))
    (ya,) = rwb(_f_headnorm_gate, [o, p["gla_g"]], [], [_row3(small["gla_norm_g"][l])], out_widths=[BRANCH_W], tile=256,
               name=nm("gla_fin"))

    yb = natten(p["na_q"], p["na_k"], p["na_v"], _na_bias(small["na_rpb"][l]), n_lat=n_lat, name=nm("na"))

    cq = conv_silu(p["gdn_qkv"], small["gdn_conv"][l], jnp.zeros((1, 3 * BRANCH_W), F32), n_lat=n_lat, name=nm("gdn_conv"))
    beta, la = rw(_f_gdn_prep, [p["gdn_sm"]], [], [_dir_rows(small["gdn_a_log"][l], GDN_H), _dir_rows(small["gdn_dt_bias"][l], GDN_H)],
                  out_widths=[256, 256], tile=256, name=nm("gdn_prep"))
    o = chunk_scan(_gdn_step, [cq], [[head(0), head(512), head(1024)]], [beta, la], [[(0, 1, 1)], [(0, 1, 1)]], [],
                   heads=GDN_H, state_shape=(GDN_D, GDN_D), out_w=GDN_D, n_lat=n_lat, name=nm("gdn_scan"))
    (yc,) = rwb(_f_headnorm_gate, [o, p["gdn_z"]], [], [_row3(small["gdn_norm_g"][l])], out_widths=[BRANCH_W], tile=256,
               name=nm("gdn_fin"))

    cw, cb = small["m2_conv"][l], small["m2_conv_b"][l][None]
    cxs = conv_silu(p["m2_xs"], _take_cols(cw[:, :512], _M2_PAD), _take_cols(cb[:, :512], _M2_PAD), n_lat=n_lat,
                    name=nm("m2_conv_x"))
    cbc = conv_silu(p["m2_bc"], cw[:, 512:], cb[:, 512:], n_lat=n_lat, name=nm("m2_conv_bc"))
    dt, la = rw(_f_m2_prep, [p["m2_dt"]], [], [_dir_rows(small["m2_a_log"][l], M2_H), _dir_rows(small["m2_dt_bias"][l], M2_H)],
                out_widths=[256, 256], tile=256, name=nm("m2_prep"))
    hpg = M2_H // M2_G
    o = chunk_scan(_ssd_step, [cxs, cbc], [[head(0)], [(0, LANES, hpg), (256, LANES, hpg)]], [dt, la],
                   [[(0, 1, 1)], [(0, 1, 1)]], [], heads=M2_H, state_shape=(M2_N, LANES), out_w=LANES, n_lat=n_lat,
                   name=nm("m2_scan"))
    dskip = _row3(jnp.repeat(small["m2_d"][l], LANES))
    (yd,) = rwb(_f_m2_fin, [o, p["m2_z"], cxs], [], [dskip, _row3(_take_cols(small["m2_norm_g"][l][None], _M2_PAD))],
               out_widths=[2 * BRANCH_W], tile=128, name=nm("m2_fin"))

    wb = gath["w_branch"]
    zs = [linear(y, wb, sl["w_branch%d" % i], name=nm("branch%d" % i), layout="col", prefix=(l, i))
          for i, y in enumerate((ya, yb, yc))]
    wb3 = wb[:, l, 3].reshape((N_SLOT, M2_H, M2_P, BRANCH_W))
    wb3 = jnp.pad(wb3, ((0, 0), (0, 0), (0, LANES - M2_P), (0, 0))).reshape((N_SLOT, 2 * BRANCH_W, BRANCH_W))
    zs.append(linear(yd, wb3, sl["w_branch3"], name=nm("branch3"), layout="col"))
    (merged,) = rwb(_f_merge, [p["gate"]] + zs, [], [_row3(small["b_merge"][l].reshape(-1))], out_widths=[D_MODEL], tile=64,
                   name=nm("merge"))
    y = linear(merged, gath["w_out"], sl["w_out"], name=nm("out"), layout="row", prefix=(l,))
    (x1,) = rw(_f_resid, [xs, y], [], [g1], out_widths=[D_MODEL], tile=256, name=nm("res1"))

    (h2,) = rwb(_f_modnorm, [x1], [], [_row3(small["norm2_g"][l]), sc2, sh2], out_widths=[D_MODEL], tile=256,
               name=nm("norm2"))
    u1 = linear(h2, gath["w_ffn1"], sl["w_ffn1"], name=nm("ffn1"), layout="col", prefix=(l,))
    u3 = linear(h2, gath["w_ffn3"], sl["w_ffn3"], name=nm("ffn3"), layout="col", prefix=(l,))
    (act,) = rwb(_f_swiglu, [u1, u3], [], [], out_widths=[D_FF], tile=128, name=nm("swiglu"))
    f = linear(act, gath["w_ffn2"], sl["w_ffn2"], name=nm("ffn2"), layout="row", prefix=(l,))
    (x2,) = rw(_f_resid, [x1, f], [], [g2], out_widths=[D_MODEL], tile=256, name=nm("res2"))
    return x2


def _slot_shapes():
    s = {"w_ada": (N_SLOT, D_MODEL, 6 * D_MODEL // N_SLOT), "w_out": (N_SLOT, D_MODEL // N_SLOT, D_MODEL),
         "w_ffn1": (N_SLOT, D_MODEL, D_FF // N_SLOT), "w_ffn3": (N_SLOT, D_MODEL, D_FF // N_SLOT),
         "w_ffn2": (N_SLOT, D_FF // N_SLOT, D_MODEL), "w_branch3": (N_SLOT, 2 * BRANCH_W, BRANCH_W)}
    for i in range(3):
        s["w_branch%d" % i] = (N_SLOT, BRANCH_W, BRANCH_W)
    for g, src in _in_groups():
        s["in_" + g] = (D_MODEL, len(src))
    return s


def _local_loss(diff, fixed, *, n_lat):
    small = diff["small"]
    n_ctx = fixed["ctx"].shape[0]
    xs = jnp.concatenate([diff["x"], fixed["ctx"]], axis=0)
    cc = jnp.concatenate([fixed["c"], small["c_ctx"][None], jnp.zeros((SUBLANES - 2, D_MODEL), F32)], axis=0)
    (mod_in,) = rowwise(_f_silu, [cc], [], [], out_widths=[D_MODEL], tile=SUBLANES, n_lat=SUBLANES, name="ada_silu",
                        out_dtype=BF16)
    tables = _rope_tables(n_lat, n_ctx)
    for l in range(DEPTH):
        xs = _layer(l, xs, mod_in, small, diff["slots"], fixed["gath"], fixed["win"], tables, n_lat=n_lat)
    (lrow,) = rowwise(_f_loss, [xs[:n_lat]], [fixed["target"]], [_row3(small["final_norm_g"])], out_widths=[LANES],
                      tile=256, n_lat=n_lat, name="loss")
    return jnp.sum(lrow)


def _place():
    x, y, c = lax.axis_index("x"), lax.axis_index("y"), lax.axis_index("c")
    chips = [(1 - x, y), (x, 1 - y), (1 - x, 1 - y)]
    return x, y, c, (x, y, 1 - c), chips


def _remote(src, dst, send_sem, recv_sem, dev):
    return pltpu.make_async_remote_copy(src_ref=src, dst_ref=dst, send_sem=send_sem, recv_sem=recv_sem,
                                        device_id=dev, device_id_type=MESH)


def _dma_sems(*shape):
    return pltpu.SemaphoreType.DMA(shape)


def place_shard(w, slot, *, name):
    depth, k, n = w.shape
    tr = _tile(k, max(2 * SUBLANES, (1 << 19) // n // (2 * SUBLANES) * (2 * SUBLANES)), 2 * SUBLANES)

    def body(s_ref, w_ref, o_ref):
        o_ref[...] = w_ref[...].astype(o_ref.dtype)

    return pl.pallas_call(
        body, name=name,
        grid_spec=pltpu.PrefetchScalarGridSpec(
            num_scalar_prefetch=1, grid=(depth, k // tr),
            in_specs=[pl.BlockSpec((None, tr, n), lambda l, i, s: (l, i, 0))],
            out_specs=pl.BlockSpec((None, None, tr, n), lambda l, i, s: (s[0], l, i, 0))),
        out_shape=jax.ShapeDtypeStruct((N_SLOT, depth, k, n), BF16),
        compiler_params=_cparams("parallel", "parallel"))(slot, w)


def gather_weights(bufs):
    n = len(bufs)

    def body(*refs):
        o = refs[n:2 * n]
        send1, recv1, send2, recv2 = refs[2 * n:]
        x, y, c, sibling, chips = _place()
        g = 2 * x + y
        sent = []
        for k in range(n):
            for j, (cx, cy) in enumerate(chips):
                cp = _remote(o[k].at[g, c], o[k].at[g, c], send1.at[k, j], recv1.at[k, j], (cx, cy, c))
                cp.start()
                sent.append(cp)
        for k in range(n):
            for j, (cx, cy) in enumerate(chips):
                gj = 2 * cx + cy
                _remote(o[k].at[g, c], o[k].at[gj, c], send1.at[k, j], recv1.at[k, j], (cx, cy, c)).wait_recv()
                cp = _remote(o[k].at[gj, c], o[k].at[gj, c], send2.at[k, j], recv2.at[k, j], sibling)
                cp.start()
                sent.append(cp)
        for k in range(n):
            for j, (cx, cy) in enumerate(chips):
                gj = 2 * cx + cy
                _remote(o[k].at[gj, 1 - c], o[k].at[gj, 1 - c], send2.at[k, j], recv2.at[k, j], sibling).wait_recv()
        for cp in sent:
            cp.wait_send()

    return pl.pallas_call(
        body, name="gather_weights", in_specs=[ANY] * n, out_specs=[ANY] * n,
        out_shape=[jax.ShapeDtypeStruct(b.shape, b.dtype) for b in bufs],
        input_output_aliases={k: k for k in range(n)},
        scratch_shapes=[_dma_sems(n, 3), _dma_sems(n, 3), _dma_sems(n, 3), _dma_sems(n, 3)],
    )(*bufs)


def allgather_small(buf, *, name):
    m_per = buf.shape[0]

    def body(x_ref, out_ref, send_sems, recv_sems, local_sem):
        x, y, c, sibling, chips = _place()
        me = (x, y, c)

        def rows(px, py, pc):
            return out_ref.at[pl.ds((4 * px + 2 * py + pc) * m_per, m_per), :]

        def copy(k, block, to, src=None):
            return _remote(rows(*block) if src is None else src, rows(*block), send_sems.at[k], recv_sems.at[k], to)

        mine = pltpu.make_async_copy(x_ref, rows(*me), local_sem)
        mine.start()
        first = [copy(0, me, sibling, src=x_ref)]
        first += [copy(1 + j, me, (*chip, c), src=x_ref) for j, chip in enumerate(chips)]
        for cp in first:
            cp.start()
        passed = [copy(4 + j, (*chip, c), sibling) for j, chip in enumerate(chips)]
        for j, chip in enumerate(chips):
            copy(1 + j, (*chip, c), me).wait_recv()
            passed[j].start()
        copy(0, sibling, me).wait_recv()
        for j, chip in enumerate(chips):
            copy(4 + j, (*chip, 1 - c), me).wait_recv()
        for cp in first + passed:
            cp.wait_send()
        mine.wait()

    return pl.pallas_call(
        body, name=name, out_shape=jax.ShapeDtypeStruct((8 * m_per, LANES), buf.dtype),
        in_specs=[pl.BlockSpec(memory_space=pltpu.VMEM)], out_specs=pl.BlockSpec(memory_space=pltpu.VMEM),
        scratch_shapes=[_dma_sems(7), _dma_sems(7), pltpu.SemaphoreType.DMA],
        compiler_params=pltpu.CompilerParams(vmem_limit_bytes=VMEM_LIMIT),
    )(buf)


def sum_blocks(stacked, n_blocks, *, name):
    m = stacked.shape[0] // n_blocks
    width = stacked.shape[1]
    x3 = stacked.reshape((n_blocks, m, width))
    tr = _tile(m, max(SUBLANES, (1 << 18) // width // SUBLANES * SUBLANES), SUBLANES)

    def body(x_ref, o_ref):
        acc = x_ref[0]
        for s in range(1, n_blocks):
            acc = acc + x_ref[s]
        o_ref[...] = acc

    return pl.pallas_call(body, name=name, grid=(m // tr,), in_specs=[pl.BlockSpec((n_blocks, tr, width), lambda i: (0, i, 0))],
                          out_specs=pl.BlockSpec((tr, width), lambda i: (i, 0)),
                          out_shape=jax.ShapeDtypeStruct((m, width), F32), compiler_params=_cparams("parallel"))(x3)


def reduce_pair(gs):
    n = len(gs)

    def body(*refs):
        g, r = refs[:n], refs[n:2 * n]
        send, recv = refs[2 * n:]
        x, y, c, sibling, _ = _place()
        cps = []
        for i in range(n):
            k2 = gs[i].shape[1] // 2
            cp = _remote(g[i].at[:, pl.ds((1 - c) * k2, k2), :], r[i], send.at[i], recv.at[i], sibling)
            cp.start()
            cps.append(cp)
        for cp in cps:
            cp.wait()

    return pl.pallas_call(
        body, name="reduce_pair", in_specs=[ANY] * n, out_specs=[ANY] * n,
        out_shape=[jax.ShapeDtypeStruct((g.shape[0], g.shape[1] // 2, g.shape[2]), g.dtype) for g in gs],
        scratch_shapes=[_dma_sems(n), _dma_sems(n)],
    )(*gs)


def _row_tile(rows, width, budget, mult):
    return _tile(rows, max(mult, budget // width // mult * mult), mult)


def add_own_half(g, recv, core, *, name):
    n_slot, k2, width = recv.shape
    tr = _row_tile(k2, width, 1 << 19, 2 * SUBLANES)
    nb = k2 // tr

    def body(c_ref, g_ref, r_ref, o_ref):
        o_ref[...] = (g_ref[...] + r_ref[...]).astype(o_ref.dtype)

    spec = pl.BlockSpec((None, tr, width), lambda s, i, c: (s, i, 0))
    return pl.pallas_call(
        body, name=name,
        grid_spec=pltpu.PrefetchScalarGridSpec(
            num_scalar_prefetch=1, grid=(n_slot, nb),
            in_specs=[pl.BlockSpec((None, tr, width), lambda s, i, c: (s, c[0] * nb + i, 0)), spec], out_specs=spec),
        out_shape=jax.ShapeDtypeStruct(recv.shape, BF16), compiler_params=_cparams("parallel", "parallel"))(core, g, recv)


def reduce_chips(qs):
    n = len(qs)

    def body(*refs):
        q, r = refs[:n], refs[n:2 * n]
        send, recv = refs[2 * n:]
        x, y, c, _, chips = _place()
        cps = []
        for i in range(n):
            for j, (cx, cy) in enumerate(chips):
                cp = _remote(q[i].at[2 * cx + cy], r[i].at[j], send.at[i, j], recv.at[i, j], (cx, cy, c))
                cp.start()
                cps.append(cp)
        for cp in cps:
            cp.wait()

    return pl.pallas_call(
        body, name="reduce_chips", in_specs=[ANY] * n, out_specs=[ANY] * n,
        out_shape=[jax.ShapeDtypeStruct((3,) + q.shape[1:], q.dtype) for q in qs],
        scratch_shapes=[_dma_sems(n, 3), _dma_sems(n, 3)],
    )(*qs)


def chip_sum(q, recv, slot, *, name):
    _, k2, width = recv.shape
    tr = _row_tile(k2, width, 1 << 18, 2 * SUBLANES)

    def body(s_ref, q_ref, r_ref, o_ref):
        acc = q_ref[...].astype(F32)
        for j in range(3):
            acc = acc + r_ref[j].astype(F32)
        o_ref[...] = acc

    return pl.pallas_call(
        body, name=name,
        grid_spec=pltpu.PrefetchScalarGridSpec(
            num_scalar_prefetch=1, grid=(k2 // tr,),
            in_specs=[pl.BlockSpec((None, tr, width), lambda i, s: (s[0], i, 0)),
                      pl.BlockSpec((3, tr, width), lambda i, s: (0, i, 0))],
            out_specs=pl.BlockSpec((tr, width), lambda i, s: (i, 0))),
        out_shape=jax.ShapeDtypeStruct((k2, width), F32), compiler_params=_cparams("parallel"))(slot, q, recv)


def swap_pair(rs):
    n = len(rs)

    def body(*refs):
        r, o = refs[:n], refs[n:2 * n]
        send, recv = refs[2 * n:]
        x, y, c, sibling, _ = _place()
        cps = []
        for i in range(n):
            cp = _remote(r[i], o[i], send.at[i], recv.at[i], sibling)
            cp.start()
            cps.append(cp)
        for cp in cps:
            cp.wait()

    return pl.pallas_call(
        body, name="swap_pair", in_specs=[ANY] * n, out_specs=[ANY] * n,
        out_shape=[jax.ShapeDtypeStruct(r.shape, r.dtype) for r in rs],
        scratch_shapes=[_dma_sems(n), _dma_sems(n)],
    )(*rs)


def adamw(w, g, m, v, *, name):
    rows, width = w.shape
    tr = _tile(rows, max(SUBLANES, (1 << 19) // width // SUBLANES * SUBLANES), SUBLANES)

    def body(w_ref, g_ref, m_ref, v_ref, d_ref, mo_ref, vo_ref):
        gv = g_ref[...]
        mn = ADAM_B1 * m_ref[...] + (1.0 - ADAM_B1) * gv
        vn = ADAM_B2 * v_ref[...] + (1.0 - ADAM_B2) * (gv * gv)
        m_hat = mn / (1.0 - ADAM_B1 ** ADAM_STEP)
        v_hat = vn / (1.0 - ADAM_B2 ** ADAM_STEP)
        d_ref[...] = -ADAM_LR * (m_hat / (jnp.sqrt(v_hat) + ADAM_EPS) + ADAM_WD * w_ref[...])
        mo_ref[...] = mn
        vo_ref[...] = vn

    spec = pl.BlockSpec((tr, width), lambda i: (i, 0))
    sds = jax.ShapeDtypeStruct((rows, width), F32)
    return pl.pallas_call(body, name=name, grid=(rows // tr,), in_specs=[spec] * 4, out_specs=(spec,) * 3,
                          out_shape=(sds,) * 3, compiler_params=_cparams("parallel"))(w, g, m, v)


def _pack(arrs):
    flat = jnp.concatenate([a.reshape(-1) for a in arrs])
    pad = (-flat.shape[0]) % (SUBLANES * LANES)
    return jnp.pad(flat, (0, pad)).reshape((-1, LANES))


def _unpack(buf, shapes):
    flat, out, at = buf.reshape(-1), [], 0
    for s in shapes:
        size = int(np.prod(s))
        out.append(flat[at:at + size].reshape(s))
        at += size
    return out


BIG = ["w_ada", "w_in", "w_branch", "w_out", "w_ffn1", "w_ffn3", "w_ffn2"]
SMALL_SHARDED = ["b_merge", "gla_a2", "gla_ab", "gdn_conv", "m2_conv"]
SMALL_WHOLE = ["c_ctx", "norm1_g", "norm2_g", "b_ada", "gla_norm_g", "na_rpb", "gdn_a_log", "gdn_dt_bias", "gdn_norm_g",
               "m2_conv_b", "m2_a_log", "m2_dt_bias", "m2_d", "m2_norm_g", "final_norm_g"]
WEIGHTS = ["c_ctx", "norm1_g", "norm2_g", "w_ada", "b_ada", "w_in", "b_merge", "gla_a2", "gla_ab", "gla_norm_g", "na_rpb",
           "gdn_conv", "gdn_a_log", "gdn_dt_bias", "gdn_norm_g", "m2_conv", "m2_conv_b", "m2_a_log", "m2_dt_bias", "m2_d",
           "m2_norm_g", "w_branch", "w_out", "w_ffn1", "w_ffn3", "w_ffn2", "final_norm_g"]


def _step(a):
    n_lat = a["x"].shape[1]
    x_i, y_i, c_i = lax.axis_index("x"), lax.axis_index("y"), lax.axis_index("c")
    slot = 2 * x_i + y_i

    slot_arr = slot.astype(jnp.int32).reshape((1,))
    core = c_i.astype(jnp.int32).reshape((1,))
    placed = [place_shard(a[n].reshape((DEPTH, -1, a[n].shape[-1])), slot_arr, name="place_" + n) for n in BIG]
    gath = dict(zip(BIG, gather_weights(placed)))
    gath["w_branch"] = gath["w_branch"].reshape((N_SLOT, DEPTH, 4, BRANCH_W, BRANCH_W))
    shard_shapes = [a[n].shape for n in SMALL_SHARDED]
    own = _pack([a[n] for n in SMALL_SHARDED])
    everyone = allgather_small(own, name="gather_small").reshape((8,) + own.shape)
    per_slot = [_unpack(everyone[2 * s], shard_shapes) for s in range(N_SLOT)]
    small = {n: jnp.concatenate([per_slot[s][i] for s in range(N_SLOT)], axis=-1) for i, n in enumerate(SMALL_SHARDED)}
    small.update({n: a[n] for n in SMALL_WHOLE})

    groups = _in_groups()
    win = []
    for l in range(DEPTH):
        full = gath["w_in"][:, l].transpose((1, 0, 2)).reshape((D_MODEL, IN_COLS))
        win.append({g: _take_cols(full, src) for g, src in groups})
    slots = [{n: jnp.zeros(s, F32) for n, s in _slot_shapes().items()} for _ in range(DEPTH)]
    diff = {"x": a["x"][0], "small": small, "slots": slots}
    fixed = {"ctx": a["ctx"][0], "c": a["c"], "target": a["loss_target"][0], "gath": gath, "win": win}
    loss, grads = jax.value_and_grad(lambda d: _local_loss(d, fixed, n_lat=n_lat))(diff)

    parts = []
    for n in BIG:
        for l in range(DEPTH):
            sl = grads["slots"][l]
            if n == "w_in":
                gin = _untake_cols([(sl["in_" + g], src) for g, src in groups], IN_COLS)
                parts.append(gin.reshape((D_MODEL, N_SLOT, IN_COLS // N_SLOT)).transpose((1, 0, 2)))
            elif n == "w_branch":
                b3 = sl["w_branch3"].reshape((N_SLOT, M2_H, LANES, BRANCH_W))[:, :, :M2_P].reshape((N_SLOT, BRANCH_W, BRANCH_W))
                parts.append(jnp.concatenate([sl["w_branch0"], sl["w_branch1"], sl["w_branch2"], b3], axis=1))
            else:
                parts.append(sl[n])
    from_sibling = reduce_pair(parts)
    pair_sums = [add_own_half(g, r, core, name="pair_sum%d" % i) for i, (g, r) in enumerate(zip(parts, from_sibling))]
    from_chips = reduce_chips(pair_sums)
    halves = [chip_sum(q, r, slot_arr, name="chip_sum%d" % i) for i, (q, r) in enumerate(zip(pair_sums, from_chips))]
    others = swap_pair(halves)
    big_grads = {}
    for k, n in enumerate(BIG):
        layers = []
        for l in range(DEPTH):
            mine, theirs = halves[DEPTH * k + l], others[DEPTH * k + l]
            layers.append(jnp.where(c_i == 0, jnp.concatenate([mine, theirs], axis=0),
                                    jnp.concatenate([theirs, mine], axis=0)))
        big_grads[n] = jnp.stack(layers).reshape(a[n].shape)

    small_names = SMALL_WHOLE + SMALL_SHARDED
    partial = _pack([grads["small"][n] for n in small_names] + [loss.reshape((1,))])
    total = sum_blocks(allgather_small(partial, name="gather_small_grads"), 8, name="sum_small_grads")
    pieces = _unpack(total, [grads["small"][n].shape for n in small_names] + [(1,)])
    small_grads = dict(zip(small_names, pieces[:-1]))
    for n in SMALL_SHARDED:
        width = a[n].shape[-1]
        small_grads[n] = lax.dynamic_slice_in_dim(small_grads[n], slot * width, width, axis=-1)
    loss_all = pieces[-1].reshape(())

    grad_w, delta, new_m, new_v = {}, {}, {}, {}
    two_d = lambda t: t.reshape((-1, t.shape[-1]))
    for n in BIG:
        d, mn, vn = adamw(two_d(a[n]), two_d(big_grads[n]), two_d(a["m_" + n]), two_d(a["v_" + n]), name="adamw_" + n)
        grad_w[n], delta[n], new_m[n], new_v[n] = big_grads[n], d.reshape(a[n].shape), mn.reshape(a[n].shape), vn.reshape(a[n].shape)
    shapes = [a[n].shape for n in small_names]
    d, mn, vn = adamw(_pack([a[n] for n in small_names]), _pack([small_grads[n] for n in small_names]),
                      _pack([a["m_" + n] for n in small_names]), _pack([a["v_" + n] for n in small_names]), name="adamw_small")
    for n, dd, mm, vv in zip(small_names, _unpack(d, shapes), _unpack(mn, shapes), _unpack(vn, shapes)):
        grad_w[n], delta[n], new_m[n], new_v[n] = small_grads[n], dd, mm, vv

    return (loss_all, grads["x"][None], *[grad_w[n] for n in WEIGHTS], *[delta[n] for n in WEIGHTS],
            *[new_m[n] for n in WEIGHTS], *[new_v[n] for n in WEIGHTS])


def kernel(x, c, ctx, c_ctx, norm1_g, norm2_g, w_ada, b_ada, w_in, b_merge, gla_a2, gla_ab, gla_norm_g, na_rpb, gdn_conv, gdn_a_log, gdn_dt_bias, gdn_norm_g, m2_conv, m2_conv_b, m2_a_log, m2_dt_bias, m2_d, m2_norm_g, w_branch, w_out, w_ffn1, w_ffn3, w_ffn2, final_norm_g, loss_target, m_c_ctx, m_norm1_g, m_norm2_g, m_w_ada, m_b_ada, m_w_in, m_b_merge, m_gla_a2, m_gla_ab, m_gla_norm_g, m_na_rpb, m_gdn_conv, m_gdn_a_log, m_gdn_dt_bias, m_gdn_norm_g, m_m2_conv, m_m2_conv_b, m_m2_a_log, m_m2_dt_bias, m_m2_d, m_m2_norm_g, m_w_branch, m_w_out, m_w_ffn1, m_w_ffn3, m_w_ffn2, m_final_norm_g, v_c_ctx, v_norm1_g, v_norm2_g, v_w_ada, v_b_ada, v_w_in, v_b_merge, v_gla_a2, v_gla_ab, v_gla_norm_g, v_na_rpb, v_gdn_conv, v_gdn_a_log, v_gdn_dt_bias, v_gdn_norm_g, v_m2_conv, v_m2_conv_b, v_m2_a_log, v_m2_dt_bias, v_m2_d, v_m2_norm_g, v_w_branch, v_w_out, v_w_ffn1, v_w_ffn3, v_w_ffn2, v_final_norm_g):
    return _step(dict(locals()))
```

```python
import functools
import math

import numpy as np
import jax
import jax.numpy as jnp
from jax import lax
from jax.experimental import pallas as pl
from jax.experimental.pallas import tpu as pltpu

F32 = jnp.float32
BF16 = jnp.bfloat16
HI = lax.Precision.HIGHEST
MESH = pl.DeviceIdType.MESH
ANY = pl.BlockSpec(memory_space=pl.ANY)

VMEM_LIMIT = 56 * 1024 * 1024
LANES = 128
SUBLANES = 8

D_MODEL = 2048
DEPTH = 2
GRID_W = 64
CHUNK = 64
CONV_W = 5
RMS_EPS = 1e-6
NEG_INF = -1e30
ROPE_BASE = 10000.0
BRANCH_W = 512
GLA_H, GLA_DK, GLA_DV, GLA_LR, GLA_TAU = 4, 64, 128, 16, 16.0
NA_H, NA_D, NA_WIN_R, NA_WIN_C = 4, 128, 8, 16
GDN_H, GDN_D = 4, 128
M2_P, M2_H, M2_N, M2_G = 64, 8, 128, 2
D_FF = 5632
IN_COLS = 14912
N_SLOT = 4
ADAM_LR, ADAM_B1, ADAM_B2, ADAM_EPS, ADAM_WD, ADAM_STEP = 0.001, 0.9, 0.999, 1e-08, 0.01, 10


def _cparams(*sem):
    return pltpu.CompilerParams(dimension_semantics=sem if sem else None, vmem_limit_bytes=VMEM_LIMIT)


def _tile(n, target, mult):
    if n <= target:
        return n
    best = None
    for t in range(mult, target + 1, mult):
        if n % t == 0:
            best = t
    assert best is not None, (n, target, mult)
    return best


def _nt(a, b):
    return lax.dot_general(a.astype(BF16), b.astype(BF16), (((1,), (1,)), ((), ())), preferred_element_type=F32)


def _tn(a, b):
    return lax.dot_general(a.astype(BF16), b.astype(BF16), (((0,), (0,)), ((), ())), preferred_element_type=F32)


def _nn(a, b):
    return jnp.dot(a.astype(BF16), b.astype(BF16), preferred_element_type=F32)


def _dot3(a, b, dims):
    a_hi, b_hi = a.astype(BF16), b.astype(BF16)
    a_lo = (a - a_hi.astype(F32)).astype(BF16)
    b_lo = (b - b_hi.astype(F32)).astype(BF16)
    dot = lambda u, v: lax.dot_general(u, v, dims, preferred_element_type=F32)
    return dot(a_hi, b_hi) + (dot(a_hi, b_lo) + dot(a_lo, b_hi))


_NN, _NT, _TN = ((((1,), (0,)), ((), ())), (((1,), (1,)), ((), ())), (((0,), (0,)), ((), ())))


@jax.custom_vjp
def _nn_hi(a, b):
    return _dot3(a, b, _NN)


_nn_hi.defvjp(lambda a, b: (_dot3(a, b, _NN), (a, b)),
              lambda res, g: (_dot3(g, res[1], _NT), _dot3(res[0], g, _TN)))


def _w_spec(layout, prefix, r_idx, c_idx, br, bc, slot_dim):
    none = (None,) * len(prefix)
    if layout == "plain":
        return pl.BlockSpec(none + (br, bc), lambda i, j, k: prefix + (r_idx(i, j, k), c_idx(i, j, k)))
    if layout == "col":
        per = slot_dim // bc
        return pl.BlockSpec((None,) + none + (br, bc),
                            lambda i, j, k: (c_idx(i, j, k) // per,) + prefix + (r_idx(i, j, k), c_idx(i, j, k) % per))
    per = slot_dim // br
    return pl.BlockSpec((None,) + none + (br, bc),
                        lambda i, j, k: (r_idx(i, j, k) // per,) + prefix + (r_idx(i, j, k) % per, c_idx(i, j, k)))


def _mm(a, b, *, name, ta=False, tb=False, b_layout="plain", b_prefix=(), out_layout="plain", out_dtype=F32):
    m, kdim = (a.shape[1], a.shape[0]) if ta else a.shape
    rows, cols = b.shape[-2:]
    if b_layout == "col":
        cols *= N_SLOT
    elif b_layout == "row":
        rows *= N_SLOT
    n = rows if tb else cols
    assert (cols if tb else rows) == kdim, (a.shape, b.shape, ta, tb)
    n_unit = n // N_SLOT if (out_layout == "col" or (b_layout == ("row" if tb else "col"))) else n
    k_unit = kdim // N_SLOT if b_layout == ("col" if tb else "row") else kdim
    odd_n = n_unit % 1408 == 0 and n_unit % 512 != 0
    odd_k = k_unit % 1408 == 0 and k_unit % 512 != 0
    if ta:
        tm = _tile(m, 2048, LANES)
        tn = _tile(n_unit, 1408 if odd_n else 1024, LANES)
        tk = _tile(k_unit, 512, 2 * SUBLANES)
    elif tb:
        tm = _tile(m, 768 if odd_k else 1408, 2 * SUBLANES)
        tn = _tile(n_unit, 1408 if odd_n else 2048, LANES)
        tk = _tile(k_unit, 1408 if odd_k else 512, LANES)
    else:
        tm = _tile(m, 768, 2 * SUBLANES)
        tn = _tile(n_unit, 1408 if odd_n else 512, LANES)
        tk = _tile(k_unit, 1408 if odd_k else 2048, LANES)
    nk = kdim // tk
    a_spec = (pl.BlockSpec((tk, tm), lambda i, j, k: (k, i)) if ta else pl.BlockSpec((tm, tk), lambda i, j, k: (i, k)))
    slot_dim = b.shape[-1] if b_layout == "col" else b.shape[-2]
    if tb:
        b_spec = _w_spec(b_layout, tuple(b_prefix), lambda i, j, k: j, lambda i, j, k: k, tn, tk, slot_dim)
    else:
        b_spec = _w_spec(b_layout, tuple(b_prefix), lambda i, j, k: k, lambda i, j, k: j, tk, tn, slot_dim)
    if out_layout == "col":
        per = (n // N_SLOT) // tn
        out_shape = jax.ShapeDtypeStruct((N_SLOT, m, n // N_SLOT), out_dtype)
        out_spec = pl.BlockSpec((None, tm, tn), lambda i, j, k: (j // per, i, j % per))
    else:
        out_shape = jax.ShapeDtypeStruct((m, n), out_dtype)
        out_spec = pl.BlockSpec((tm, tn), lambda i, j, k: (i, j))
    dims = (((0 if ta else 1,), (1 if tb else 0,)), ((), ()))

    def product(a_ref, b_ref):
        return lax.dot_general(a_ref[...].astype(BF16), b_ref[...].astype(BF16), dims, preferred_element_type=F32)

    def body_once(a_ref, b_ref, o_ref):
        o_ref[...] = product(a_ref, b_ref).astype(o_ref.dtype)

    def body(a_ref, b_ref, o_ref, acc_ref):
        k = pl.program_id(2)

        @pl.when(k == 0)
        def _():
            acc_ref[...] = jnp.zeros_like(acc_ref)

        acc_ref[...] += product(a_ref, b_ref)

        @pl.when(k == nk - 1)
        def _():
            o_ref[...] = acc_ref[...].astype(o_ref.dtype)

    return pl.pallas_call(
        body_once if nk == 1 else body, name=name, grid=(m // tm, n // tn, nk), in_specs=[a_spec, b_spec],
        out_specs=out_spec, out_shape=out_shape, scratch_shapes=[] if nk == 1 else [pltpu.VMEM((tm, tn), F32)],
        compiler_params=_cparams("parallel", "parallel", "arbitrary"),
    )(a, b)


def linear(a, w, grad_slot, *, name, layout="plain", prefix=()):
    @jax.custom_vjp
    def f(a, w, grad_slot):
        return _mm(a, w, name=name + "_fwd", b_layout=layout, b_prefix=prefix)

    def fwd(a, w, grad_slot):
        return f(a, w, grad_slot), (a, w)

    def bwd(res, g):
        a, w = res
        da = _mm(g, w, name=name + "_dgrad", tb=True, b_layout=layout, b_prefix=prefix, out_dtype=a.dtype)
        dw = _mm(a, g, name=name + "_wgrad", ta=True, out_layout="col" if layout == "col" else "plain")
        if layout == "row":
            dw = dw.reshape((N_SLOT, dw.shape[0] // N_SLOT, dw.shape[1]))
        return da, None, dw

    f.defvjp(fwd, bwd)
    return f(a, w, grad_slot)


def _rowwise_specs(rows, consts, params, tile, seg_tile):
    def row_spec(r):
        return pl.BlockSpec((tile, r.shape[1]), lambda i: (i, 0))

    def par_spec(p):
        if p.shape[0] == 2:
            return pl.BlockSpec((None,) + p.shape[1:], lambda i: (jnp.where(i >= seg_tile, 1, 0), 0, 0))
        return pl.BlockSpec((None,) + p.shape[1:], lambda i: (0, 0, 0))

    return [row_spec(r) for r in rows], [row_spec(r) for r in consts], [par_spec(p) for p in params]


def rowwise(f, rows, consts, params, *, out_widths, tile, n_lat, name, out_dtype=F32):
    rows, consts, params = tuple(rows), tuple(consts), tuple(params)
    n_rows = rows[0].shape[0]
    tile = math.gcd(math.gcd(n_rows, n_lat), tile)
    assert tile % SUBLANES == 0
    seg_tile = n_lat // tile
    grid = (n_rows // tile,)
    nr, nc, npar = len(rows), len(consts), len(params)
    r_specs, c_specs, p_specs = _rowwise_specs(rows, consts, params, tile, seg_tile)
    out_shape = tuple(jax.ShapeDtypeStruct((n_rows, w), out_dtype) for w in out_widths)
    out_specs = tuple(pl.BlockSpec((tile, w), lambda i: (i, 0)) for w in out_widths)
    n_out = len(out_widths)

    def fwd_call(rows, consts, params):
        def body(*refs):
            ins = [r[...].astype(F32) for r in refs[:nr + nc + npar]]
            outs = f(*ins)
            for o_ref, o in zip(refs[nr + nc + npar:], outs):
                o_ref[...] = o.astype(o_ref.dtype)

        return pl.pallas_call(body, name=name + "_fwd", grid=grid, in_specs=r_specs + c_specs + p_specs,
                              out_specs=out_specs, out_shape=out_shape,
                              compiler_params=_cparams("parallel"))(*rows, *consts, *params)

    def bwd_call(rows, consts, params, gouts):
        def body(*refs):
            i = pl.program_id(0)
            ins = [r[...].astype(F32) for r in refs[:nr + nc + npar]]
            gs = tuple(r[...].astype(F32) for r in refs[nr + nc + npar:nr + nc + npar + n_out])
            d_refs = refs[nr + nc + npar + n_out:]
            cvals = ins[nr:nr + nc]

            def g(*diff):
                return tuple(f(*diff[:nr], *cvals, *diff[nr:]))

            _, vjp = jax.vjp(g, *ins[:nr], *ins[nr + nc:])
            grads = vjp(gs)
            for d_ref, gr in zip(d_refs[:nr], grads[:nr]):
                d_ref[...] = gr.astype(d_ref.dtype)
            for p, d_ref, gr in zip(params, d_refs[nr:], grads[nr:]):
                first = (i == 0) | (i == seg_tile) if p.shape[0] == 2 else (i == 0)

                @pl.when(first)
                def _():
                    d_ref[...] = jnp.zeros_like(d_ref)

                d_ref[...] += gr

        d_shape = tuple(jax.ShapeDtypeStruct(r.shape, r.dtype) for r in rows) + tuple(
            jax.ShapeDtypeStruct(p.shape, F32) for p in params)
        g_specs = [pl.BlockSpec((tile, w), lambda i: (i, 0)) for w in out_widths]
        return pl.pallas_call(body, name=name + "_bwd", grid=grid,
                              in_specs=r_specs + c_specs + p_specs + g_specs,
                              out_specs=tuple(r_specs + p_specs), out_shape=d_shape,
                              compiler_params=_cparams("arbitrary"))(*rows, *consts, *params, *gouts)

    @jax.custom_vjp
    def op(rows, consts, params):
        return fwd_call(rows, consts, params)

    def op_fwd(rows, consts, params):
        return op(rows, consts, params), (rows, consts, params)

    def op_bwd(res, gouts):
        rows, consts, params = res
        d = bwd_call(rows, consts, params, tuple(gouts))
        return tuple(d[:nr]), tuple(None for _ in consts), tuple(d[nr:])

    op.defvjp(op_fwd, op_bwd)
    return op(rows, consts, params)


def _rms(x, width=None):
    w = x.shape[-1] if width is None else width
    return x * lax.rsqrt(jnp.sum(x * x, axis=-1, keepdims=True) * (1.0 / w) + RMS_EPS)


def _silu(x):
    return x * jax.nn.sigmoid(x)


def _f_modnorm(x, g, sc, sh):
    return ((_rms(x) * g) * (1.0 + sc) + sh,)


def _f_silu(x):
    return (_silu(x),)


def _f_gla_prep(lr, a2, ab):
    z = _nn(lr, a2) + ab
    return ((jnp.minimum(z, 0.0) - jnp.log(1.0 + jnp.exp(-jnp.abs(z)))) * (1.0 / GLA_TAU),)


def _f_headnorm_gate(o, g, ng):
    outs = []
    for h in range(BRANCH_W // LANES):
        lo = h * LANES
        oh = o[:, lo:lo + LANES] + o[:, BRANCH_W + lo:BRANCH_W + lo + LANES]
        outs.append(_rms(oh) * ng * _silu(g[:, lo:lo + LANES]))
    return (jnp.concatenate(outs, axis=-1),)


def _f_gdn_prep(x, alog, dtb):
    half = x.shape[1] // 2
    beta = jax.nn.sigmoid(x[:, :half])
    la = -jnp.exp(alog) * jax.nn.softplus(x[:, half:] + dtb)
    return beta, la


def _f_m2_prep(x, alog, dtb):
    dt = jax.nn.softplus(x + dtb)
    return dt, -jnp.exp(alog) * dt


def _f_m2_fin(o, z, xs, dskip, ng):
    w = z.shape[1]
    y = (o[:, :w] + o[:, w:] + dskip * xs) * _silu(z)
    return (_rms(y, BRANCH_W) * ng,)


def _f_merge(gate, z0, z1, z2, z3, bm):
    acc = None
    for i, z in enumerate((z0, z1, z2, z3)):
        lo = i * D_MODEL
        t = jax.nn.sigmoid(gate[:, lo:lo + D_MODEL] + bm[:, lo:lo + D_MODEL]) * z
        acc = t if acc is None else acc + t
    return (acc,)


def _f_resid(x, y, g):
    return (x + g * y,)


def _f_swiglu(u1, u3):
    return (_silu(u1) * u3,)


def _f_loss(x, tgt, g):
    e = _rms(x) * g - tgt
    per_row = 0.5 * jnp.sum(e * e, axis=-1, keepdims=True) * (1.0 / D_MODEL)
    return (jnp.broadcast_to(per_row * (1.0 / LANES), (x.shape[0], LANES)),)


_HALO = 8


def _conv_segments(n_lat, n_ctx):
    segs = [(0, _HALO, n_lat), (n_lat, n_lat + 3 * _HALO, n_ctx)]
    return segs, n_lat + n_ctx + 4 * _HALO


def _conv_stage(buf, src, n_lat, n_ctx):
    zeros = jnp.zeros((_HALO, LANES), F32)
    buf[0:_HALO, :] = zeros
    buf[_HALO:_HALO + n_lat, :] = src[0:n_lat, :]
    buf[n_lat + _HALO:n_lat + 2 * _HALO, :] = zeros
    buf[n_lat + 2 * _HALO:n_lat + 3 * _HALO, :] = zeros
    buf[n_lat + 3 * _HALO:n_lat + 3 * _HALO + n_ctx, :] = src[n_lat:n_lat + n_ctx, :]
    buf[n_lat + n_ctx + 3 * _HALO:n_lat + n_ctx + 4 * _HALO, :] = zeros


def conv_silu(x, w, b, *, n_lat, name):
    n_rows, n_ch = x.shape
    n_ctx = n_rows - n_lat
    segs, n_buf = _conv_segments(n_lat, n_ctx)
    grid = (n_ch // LANES,)
    col = lambda r: pl.BlockSpec((r, LANES), lambda j: (0, j))
    half = CONV_W // 2

    def tiles():
        for row0, off, length in segs:
            tr = _tile(length, 256, SUBLANES)
            for t0 in range(0, length, tr):
                yield row0 + t0, off + t0, tr

    def pre_act(buf, w_ref, b_ref, off, tr):
        acc = jnp.broadcast_to(b_ref[...], (tr, LANES))
        for j in range(CONV_W):
            acc = acc + w_ref[j:j + 1, :] * buf[off + j - half:off + j - half + tr, :]
        return acc

    def fwd_call(x, w, b):
        def body(x_ref, w_ref, b_ref, o_ref, buf):
            _conv_stage(buf, x_ref, n_lat, n_ctx)
            for row, off, tr in tiles():
                o_ref[row:row + tr, :] = _silu(pre_act(buf, w_ref, b_ref, off, tr))

        return pl.pallas_call(body, name=name + "_fwd", grid=grid, in_specs=[col(n_rows), col(CONV_W), col(1)],
                              out_specs=col(n_rows), out_shape=jax.ShapeDtypeStruct(x.shape, F32),
                              scratch_shapes=[pltpu.VMEM((n_buf, LANES), F32)],
                              compiler_params=_cparams("parallel"))(x, w, b)

    def bwd_call(x, w, b, g):
        def body(x_ref, w_ref, b_ref, g_ref, dx_ref, dw_ref, db_ref, xbuf, dbuf):
            _conv_stage(xbuf, x_ref, n_lat, n_ctx)
            _conv_stage(dbuf, g_ref, n_lat, n_ctx)
            dw = [jnp.zeros((1, LANES), F32) for _ in range(CONV_W)]
            db = jnp.zeros((1, LANES), F32)
            for row, off, tr in tiles():
                pre = pre_act(xbuf, w_ref, b_ref, off, tr)
                s = jax.nn.sigmoid(pre)
                dpre = g_ref[row:row + tr, :] * (s * (1.0 + pre * (1.0 - s)))
                dbuf[off:off + tr, :] = dpre
                db = db + jnp.sum(dpre, axis=0, keepdims=True)
                for j in range(CONV_W):
                    dw[j] = dw[j] + jnp.sum(dpre * xbuf[off + j - half:off + j - half + tr, :], axis=0, keepdims=True)
            for row, off, tr in tiles():
                acc = jnp.zeros((tr, LANES), F32)
                for j in range(CONV_W):
                    acc = acc + w_ref[j:j + 1, :] * dbuf[off - j + half:off - j + half + tr, :]
                dx_ref[row:row + tr, :] = acc
            for j in range(CONV_W):
                dw_ref[j:j + 1, :] = dw[j]
            db_ref[...] = db

        return pl.pallas_call(
            body, name=name + "_bwd", grid=grid, in_specs=[col(n_rows), col(CONV_W), col(1), col(n_rows)],
            out_specs=(col(n_rows), col(CONV_W), col(1)),
            out_shape=(jax.ShapeDtypeStruct(x.shape, F32), jax.ShapeDtypeStruct(w.shape, F32),
                       jax.ShapeDtypeStruct(b.shape, F32)),
            scratch_shapes=[pltpu.VMEM((n_buf, LANES), F32), pltpu.VMEM((n_buf, LANES), F32)],
            compiler_params=_cparams("parallel"))(x, w, b, g)

    @jax.custom_vjp
    def op(x, w, b):
        return fwd_call(x, w, b)

    op.defvjp(lambda x, w, b: (op(x, w, b), (x, w, b)), lambda res, g: bwd_call(*res, g))
    return op(x, w, b)


def chunk_scan(step, shared, shared_lanes, perdir, perdir_lanes, consts, *, heads, state_shape, out_w, n_lat, name):
    shared, perdir, consts = tuple(shared), tuple(perdir), tuple(consts)
    n_rows = shared[0].shape[0]
    nl, ncx = n_lat // CHUNK, (n_rows - n_lat) // CHUNK
    n_chunks = nl + ncx
    ow_all = heads * out_w
    ns, npd, ncst = len(shared), len(perdir), len(consts)

    def cidx(d, n):
        m = n - ncx
        return jnp.where(n < ncx, nl + jnp.where(d == 0, n, ncx - 1 - n), jnp.where(d == 0, m, nl - 1 - m))

    def specs(order):
        sh = [pl.BlockSpec((CHUNK, a.shape[1]), lambda d, n: (cidx(d, order(n)), 0)) for a in shared]
        pd = [pl.BlockSpec((CHUNK, a.shape[1] // 2), lambda d, n: (cidx(d, order(n)), d)) for a in perdir]
        cs = [pl.BlockSpec((CHUNK, a.shape[1]), lambda d, n: (cidx(d, order(n)), 0)) for a in consts]
        o = pl.BlockSpec((CHUNK, ow_all), lambda d, n: (cidx(d, order(n)), d))
        st = pl.BlockSpec((None, None, heads) + state_shape, lambda d, n: (d, order(n), 0) + (0,) * len(state_shape))
        return sh, pd, cs, o, st

    def mask(d):
        r = lax.broadcasted_iota(jnp.int32, (CHUNK, CHUNK), 0)
        c = lax.broadcasted_iota(jnp.int32, (CHUNK, CHUNK), 1)
        lower = jnp.where(r >= c, 1.0, 0.0).astype(F32)
        upper = jnp.where(r <= c, 1.0, 0.0).astype(F32)
        return jnp.where(d == 0, lower, upper)

    def head_slices(h):
        out = []
        for lanes in tuple(shared_lanes) + tuple(perdir_lanes):
            out.append([slice(off + (h // hpg) * w, off + (h // hpg) * w + w) for off, w, hpg in lanes])
        return out

    def load(refs, h):
        return tuple(tuple(ref[:, s] for s in sl) for ref, sl in zip(refs, head_slices(h)))

    state_sds = jax.ShapeDtypeStruct((2, n_chunks, heads) + state_shape, F32)

    def fwd_call(shared, perdir, consts):
        sh, pd, cs, o_spec, st_spec = specs(lambda n: n)

        def body(*refs):
            in_refs = refs[:ns + npd]
            c_refs = refs[ns + npd:ns + npd + ncst]
            o_ref, ss_ref, s_scr = refs[ns + npd + ncst:]
            d, n = pl.program_id(0), pl.program_id(1)

            @pl.when(n == 0)
            def _():
                s_scr[...] = jnp.zeros_like(s_scr)

            m = mask(d)
            cv = tuple(c[...] for c in c_refs)
            ins = [load(in_refs, h) for h in range(heads)]
            s0 = [s_scr[h] for h in range(heads)]
            res = [step(ins[h], cv, s0[h], m) for h in range(heads)]
            for h in range(heads):
                o_ref[:, h * out_w:(h + 1) * out_w] = res[h][0]
                ss_ref[h] = s0[h]
                s_scr[h] = res[h][1]

        return pl.pallas_call(
            body, name=name + "_fwd", grid=(2, n_chunks), in_specs=sh + pd + cs, out_specs=(o_spec, st_spec),
            out_shape=(jax.ShapeDtypeStruct((n_rows, 2 * ow_all), F32), state_sds),
            scratch_shapes=[pltpu.VMEM((heads,) + state_shape, F32)],
            compiler_params=_cparams("arbitrary", "arbitrary"))(*shared, *perdir, *consts)

    def bwd_call(shared, perdir, consts, starts, g):
        sh, pd, cs, o_spec, st_spec = specs(lambda n: n_chunks - 1 - n)
        dsh = [pl.BlockSpec((CHUNK, a.shape[1]), lambda d, n: (cidx(d, n_chunks - 1 - n), d)) for a in shared]

        def body(*refs):
            in_refs = refs[:ns + npd]
            c_refs = refs[ns + npd:ns + npd + ncst]
            ss_ref, g_ref = refs[ns + npd + ncst:ns + npd + ncst + 2]
            d_refs = refs[ns + npd + ncst + 2:ns + npd + ncst + 2 + ns + npd]
            ds_scr = refs[-1]
            d, n = pl.program_id(0), pl.program_id(1)

            @pl.when(n == 0)
            def _():
                ds_scr[...] = jnp.zeros_like(ds_scr)

            m = mask(d)
            cv = tuple(c[...] for c in c_refs)
            ins = [load(in_refs, h) for h in range(heads)]
            cots = [(g_ref[:, h * out_w:(h + 1) * out_w], ds_scr[h]) for h in range(heads)]
            starts = [ss_ref[h] for h in range(heads)]
            grads = []
            for h in range(heads):
                _, vjp = jax.vjp(lambda i_, s_: step(i_, cv, s_, m), ins[h], starts[h])
                grads.append(vjp(cots[h]))
            for d_ref in d_refs:
                d_ref[...] = jnp.zeros_like(d_ref)
            for h in range(heads):
                g_ins, g_s = grads[h]
                for d_ref, sl, gr in zip(d_refs, head_slices(h), g_ins):
                    for s, gv in zip(sl, gr):
                        d_ref[:, s] += gv
                ds_scr[h] = g_s

        d_shape = tuple(jax.ShapeDtypeStruct((n_rows, 2 * a.shape[1]), F32) for a in shared) + tuple(
            jax.ShapeDtypeStruct(a.shape, F32) for a in perdir)
        return pl.pallas_call(
            body, name=name + "_bwd", grid=(2, n_chunks), in_specs=sh + pd + cs + [st_spec, o_spec],
            out_specs=tuple(dsh + pd), out_shape=d_shape,
            scratch_shapes=[pltpu.VMEM((heads,) + state_shape, F32)],
            compiler_params=_cparams("arbitrary", "arbitrary"))(*shared, *perdir, *consts, starts, g)

    @jax.custom_vjp
    def op(shared, perdir, consts):
        return fwd_call(shared, perdir, consts)[0]

    def op_fwd(shared, perdir, consts):
        o, starts = fwd_call(shared, perdir, consts)
        return o, (shared, perdir, consts, starts)

    def op_bwd(res, g):
        shared, perdir, consts, starts = res
        d = bwd_call(shared, perdir, consts, starts, g)
        d_sh = tuple(a[:, :a.shape[1] // 2] + a[:, a.shape[1] // 2:] for a in d[:ns])
        return d_sh, tuple(d[ns:]), tuple(None for _ in consts)

    op.defvjp(op_fwd, op_bwd)
    return op(shared, perdir, consts)


@jax.custom_vjp
def _swap_halves(x):
    return pltpu.roll(x, LANES // 2, 1)


_swap_halves.defvjp(lambda x: (_swap_halves(x), None), lambda _, g: (_swap_halves(g),))


def _gla_step(ins, consts, st, m):
    (q, k), (v,), (la,) = ins
    cos, sin = consts
    q = (q * cos + _swap_halves(q) * sin) * (GLA_DK ** -0.5)
    k = k * cos + _swap_halves(k) * sin
    b = _nn_hi(m, la)
    bl = jnp.sum(la, axis=0, keepdims=True)
    qi = q * jnp.exp(b)
    ki = k * jnp.exp(-b)
    att = _nt(qi, ki) * m
    o = _nt(qi, st) + _nn(att, v)
    st_new = st * jnp.exp(bl) + _tn(v, k * jnp.exp(bl - b))
    return o, st_new


def _l2n(x):
    return x * lax.rsqrt(jnp.sum(x * x, axis=-1, keepdims=True) + RMS_EPS)


def _tri_inv_fwd(nmat):
    r = lax.broadcasted_iota(jnp.int32, nmat.shape, 0)
    c = lax.broadcasted_iota(jnp.int32, nmat.shape, 1)
    inv = jnp.where(r == c, 1.0, 0.0).astype(F32) - nmat
    p = nmat
    for _ in range(5):
        p = _nn_hi(p, p)
        inv = inv + _nn_hi(inv, p)
    return inv


@jax.custom_vjp
def _unit_tri_inv(nmat):
    return _tri_inv_fwd(nmat)


def _unit_tri_inv_bwd(inv, g):
    return (-_dot3(_dot3(inv, g, _TN), inv, _NT),)


_unit_tri_inv.defvjp(lambda nmat: (lambda inv: (inv, inv))(_tri_inv_fwd(nmat)), _unit_tri_inv_bwd)


def _lane_col(x, h):
    lane = lax.broadcasted_iota(jnp.int32, x.shape, 1)
    return jnp.sum(jnp.where(lane == h, x, 0.0), axis=1, keepdims=True)


def _masked_exp(diff, mask):
    return jnp.where(mask > 0, jnp.exp(jnp.where(mask > 0, diff, 0.0)), 0.0)


def _gdn_step(ins, consts, s, m):
    (q, k, v), (beta,), (la,) = ins
    n = GDN_H * CHUNK
    hs = range(GDN_H)
    blk = lambda x, h: x[:, h * GDN_D:(h + 1) * GDN_D]
    rows = lambda x, h: x[h * CHUNK:(h + 1) * CHUNK]
    qh = [_l2n(blk(q, h)) * (GDN_D ** -0.5) for h in hs]
    kh = [_l2n(blk(k, h)) for h in hs]
    k_st = jnp.concatenate(kh, axis=0)
    q_st = jnp.concatenate(qh, axis=0)
    v_st = jnp.concatenate([blk(v, h) for h in hs], axis=0)
    beta_st = jnp.concatenate([_lane_col(beta, h) for h in hs], axis=0)
    la_cols = [_lane_col(la, h) for h in hs]
    la_st = jnp.concatenate([jnp.broadcast_to(c, (CHUNK, GDN_D)) for c in la_cols], axis=0)
    r = lax.broadcasted_iota(jnp.int32, (n, n), 0)
    c = lax.broadcasted_iota(jnp.int32, (n, n), 1)
    e = jnp.where(lax.broadcasted_iota(jnp.int32, (n, CHUNK), 0) % CHUNK == lax.broadcasted_iota(jnp.int32, (n, CHUNK), 1),
                  1.0, 0.0).astype(F32)
    m_bd = jnp.where(r // CHUNK == c // CHUNK, _nt(_nn(e, m), e), 0.0)
    eye = jnp.where(r == c, 1.0, 0.0).astype(F32)
    b_st = _nn_hi(m_bd, la_st)
    b_t = b_st.T
    diff = jnp.concatenate([b_st, b_st], axis=1) - jnp.concatenate([b_t, b_t], axis=0)
    incl = _masked_exp(diff, m_bd)
    strict = _masked_exp(diff, m_bd - eye)
    inv = _unit_tri_inv(beta_st * _nt(k_st, k_st) * strict)
    wu = _nn_hi(inv, jnp.concatenate([k_st * (beta_st * jnp.exp(b_st)), v_st * beta_st], axis=-1))
    w, u0 = wu[:, :GDN_D], wu[:, GDN_D:]
    us, s_new, qs = [], [], []
    for h in hs:
        s_h = s[h * GDN_D:(h + 1) * GDN_D]
        bl = jnp.sum(jnp.broadcast_to(la_cols[h], (CHUNK, GDN_D)), axis=0, keepdims=True)
        u_h = rows(u0, h) - _nn(rows(w, h), s_h)
        s_new.append(jnp.exp(bl) * s_h + _tn(kh[h] * jnp.exp(bl - rows(b_st, h)), u_h))
        us.append(u_h)
        qs.append(_nn(qh[h], s_h))
    o_st = jnp.exp(b_st) * jnp.concatenate(qs, axis=0) + _nn(_nt(q_st, k_st) * incl, jnp.concatenate(us, axis=0))
    return jnp.concatenate([rows(o_st, h) for h in hs], axis=1), jnp.concatenate(s_new, axis=0)


def _ssd_step(ins, consts, s, m):
    (xs,), (bc,), (dt,), (la,) = ins
    hpg = M2_H // M2_G
    b_all = _nn_hi(m, la)
    bl_all = jnp.sum(la, axis=0, keepdims=True)
    b_t = b_all.T
    row_id = lax.broadcasted_iota(jnp.int32, b_t.shape, 0)
    bm = [bc[:, g * M2_N:(g + 1) * M2_N] for g in range(M2_G)]
    cm = [bc[:, (M2_G + g) * M2_N:(M2_G + g + 1) * M2_N] for g in range(M2_G)]
    scores = [_nt(cm[g], bm[g]) for g in range(M2_G)]
    outs, s_new = [], []
    for h in range(M2_H):
        g = h // hpg
        s_h = s[h * M2_N:(h + 1) * M2_N]
        b_col = _lane_col(b_all, h)
        bl = _lane_col(bl_all, h)
        b_row = jnp.sum(jnp.where(row_id == h, b_t, 0.0), axis=0, keepdims=True)
        xv = xs[:, h * LANES:(h + 1) * LANES] * _lane_col(dt, h)
        outs.append(jnp.exp(b_col) * _nn(cm[g], s_h) + _nn(scores[g] * _masked_exp(b_col - b_row, m), xv))
        s_new.append(jnp.exp(bl) * s_h + _tn(bm[g] * jnp.exp(bl - b_col), xv))
    return jnp.concatenate(outs, axis=1), jnp.concatenate(s_new, axis=0)


def _na_tile(q, kw, vw, kc, vc, bias):
    qs = q * (NA_D ** -0.5)
    s1 = _nt(qs, kw) + bias
    s2 = _nt(qs, kc)
    mx = lax.stop_gradient(jnp.maximum(jnp.max(s1, axis=-1, keepdims=True), jnp.max(s2, axis=-1, keepdims=True)))
    p1 = jnp.exp(s1 - mx)
    p2 = jnp.exp(s2 - mx)
    den = jnp.sum(p1, axis=-1, keepdims=True) + jnp.sum(p2, axis=-1, keepdims=True)
    return (_nn(p1, vw) + _nn(p2, vc)) / den


def _ctx_tile(q, k, v):
    s = _nt(q * (NA_D ** -0.5), k)
    p = jnp.exp(s - lax.stop_gradient(jnp.max(s, axis=-1, keepdims=True)))
    return _nn(p, v) / jnp.sum(p, axis=-1, keepdims=True)


def natten(q, k, v, bias, *, n_lat, name):
    n_rows = q.shape[0]
    n_ctx = n_rows - n_lat
    g_rows = n_lat // GRID_W
    win = NA_WIN_R * GRID_W
    ctx_blk = n_lat // n_ctx

    def start(n):
        return jnp.clip(n - NA_WIN_R // 2, 0, g_rows - NA_WIN_R)

    def case(n):
        return n - start(n)

    q_spec = pl.BlockSpec((GRID_W, LANES), lambda h, n: (n, h))
    lat_spec = pl.BlockSpec((n_lat, LANES), lambda h, n: (0, h))
    ctx_in = pl.BlockSpec((n_ctx, LANES), lambda h, n: (ctx_blk, h))
    ctx_out = pl.BlockSpec((n_ctx, LANES), lambda h, n: (0, h))
    bias_spec = pl.BlockSpec((None, None, GRID_W, win), lambda h, n: (h, case(n), 0, 0))
    lat_sds = jax.ShapeDtypeStruct((n_lat, BRANCH_W), F32)
    ctx_sds = jax.ShapeDtypeStruct((n_ctx, BRANCH_W), F32)

    def lat_fwd(q, k, v, bias):
        def body(q_ref, k_ref, v_ref, kc_ref, vc_ref, b_ref, o_ref):
            r0 = pl.multiple_of(start(pl.program_id(1)) * GRID_W, GRID_W)
            o_ref[...] = _na_tile(q_ref[...], k_ref[pl.ds(r0, win), :], v_ref[pl.ds(r0, win), :],
                                  kc_ref[...], vc_ref[...], b_ref[...])

        return pl.pallas_call(body, name=name + "_lat_fwd", grid=(NA_H, g_rows),
                              in_specs=[q_spec, lat_spec, lat_spec, ctx_in, ctx_in, bias_spec], out_specs=q_spec,
                              out_shape=lat_sds, compiler_params=_cparams("parallel", "arbitrary"))(q, k, v, k, v, bias)

    def lat_bwd(q, k, v, bias, g):
        def body(q_ref, k_ref, v_ref, kc_ref, vc_ref, b_ref, g_ref, dq_ref, dk_ref, dv_ref, dkc_ref, dvc_ref, db_ref):
            n = pl.program_id(1)
            r0 = pl.multiple_of(start(n) * GRID_W, GRID_W)

            @pl.when(n == 0)
            def _():
                for r in (dk_ref, dv_ref, dkc_ref, dvc_ref):
                    r[...] = jnp.zeros_like(r)

            @pl.when((n == 0) | (case(n) != case(jnp.maximum(n - 1, 0))))
            def _():
                db_ref[...] = jnp.zeros_like(db_ref)

            _, vjp = jax.vjp(_na_tile, q_ref[...], k_ref[pl.ds(r0, win), :], v_ref[pl.ds(r0, win), :],
                             kc_ref[...], vc_ref[...], b_ref[...])
            dq, dkw, dvw, dkc, dvc, db = vjp(g_ref[...])
            dq_ref[...] = dq
            dk_ref[pl.ds(r0, win), :] += dkw
            dv_ref[pl.ds(r0, win), :] += dvw
            dkc_ref[...] += dkc
            dvc_ref[...] += dvc
            db_ref[...] += db

        return pl.pallas_call(
            body, name=name + "_lat_bwd", grid=(NA_H, g_rows),
            in_specs=[q_spec, lat_spec, lat_spec, ctx_in, ctx_in, bias_spec, q_spec],
            out_specs=(q_spec, lat_spec, lat_spec, ctx_out, ctx_out, bias_spec),
            out_shape=(lat_sds, lat_sds, lat_sds, ctx_sds, ctx_sds, jax.ShapeDtypeStruct(bias.shape, F32)),
            compiler_params=_cparams("parallel", "arbitrary"))(q, k, v, k, v, bias, g)

    c_in = pl.BlockSpec((n_ctx, LANES), lambda h: (ctx_blk, h))
    c_out = pl.BlockSpec((n_ctx, LANES), lambda h: (0, h))

    def ctx_fwd(q, k, v):
        def body(q_ref, k_ref, v_ref, o_ref):
            o_ref[...] = _ctx_tile(q_ref[...], k_ref[...], v_ref[...])

        return pl.pallas_call(body, name=name + "_ctx_fwd", grid=(NA_H,), in_specs=[c_in, c_in, c_in], out_specs=c_out,
                              out_shape=ctx_sds, compiler_params=_cparams("parallel"))(q, k, v)

    def ctx_bwd(q, k, v, g):
        def body(q_ref, k_ref, v_ref, g_ref, dq_ref, dk_ref, dv_ref):
            _, vjp = jax.vjp(_ctx_tile, q_ref[...], k_ref[...], v_ref[...])
            dq_ref[...], dk_ref[...], dv_ref[...] = vjp(g_ref[...])

        return pl.pallas_call(body, name=name + "_ctx_bwd", grid=(NA_H,), in_specs=[c_in, c_in, c_in, c_out],
                              out_specs=(c_out, c_out, c_out), out_shape=(ctx_sds, ctx_sds, ctx_sds),
                              compiler_params=_cparams("parallel"))(q, k, v, g)

    @jax.custom_vjp
    def op(q, k, v, bias):
        return jnp.concatenate([lat_fwd(q, k, v, bias), ctx_fwd(q, k, v)], axis=0)

    def op_bwd(res, g):
        q, k, v, bias = res
        dq, dk, dv, dkc, dvc, db = lat_bwd(q, k, v, bias, g[:n_lat])
        dqc, dkc2, dvc2 = ctx_bwd(q, k, v, g[n_lat:])
        return (jnp.concatenate([dq, dqc], axis=0), jnp.concatenate([dk, dkc + dkc2], axis=0),
                jnp.concatenate([dv, dvc + dvc2], axis=0), db)

    op.defvjp(lambda q, k, v, bias: (op(q, k, v, bias), (q, k, v, bias)), op_bwd)
    return op(q, k, v, bias)


def _runs(src):
    src = np.asarray(src)
    out, i = [], 0
    while i < len(src):
        j = i + 1
        if src[i] < 0:
            while j < len(src) and src[j] < 0:
                j += 1
            out.append((-1, j - i))
        else:
            while j < len(src) and src[j] == src[j - 1] + 1:
                j += 1
            out.append((int(src[i]), j - i))
        i = j
    return out


def _take_cols(w, src):
    pieces = [jnp.zeros(w.shape[:-1] + (ln,), w.dtype) if s < 0 else w[..., s:s + ln] for s, ln in _runs(src)]
    return pieces[0] if len(pieces) == 1 else jnp.concatenate(pieces, axis=-1)


def _untake_cols(parts, n_cols):
    found = []
    for arr, src in parts:
        pos = 0
        for s, ln in _runs(src):
            if s >= 0:
                found.append((s, arr[..., pos:pos + ln]))
            pos += ln
    found.sort(key=lambda t: t[0])
    at = 0
    for s, piece in found:
        assert s == at, (s, at)
        at += piece.shape[-1]
    assert at == n_cols, (at, n_cols)
    return jnp.concatenate([p for _, p in found], axis=-1)


def _pad_heads(base, heads, real, width):
    return np.concatenate([np.concatenate([base + h * real + np.arange(real), -np.ones(width - real, np.int64)])
                           for h in range(heads)])


def _rope_heads(base, heads):
    z = -np.ones(32, np.int64)
    return np.concatenate([np.concatenate([base + h * 64 + np.arange(32), z, base + h * 64 + 32 + np.arange(32), z])
                           for h in range(heads)])


def _lane_block(base, n):
    return np.concatenate([base + np.arange(n), -np.ones(LANES - n, np.int64)])


def _in_groups():
    g0, n0, d0, m0, t0 = 0, 1568, 3104, 5168, 6720
    rng = lambda a, n: a + np.arange(n)
    return [
        ("gla_qk", np.concatenate([_rope_heads(g0, GLA_H), _rope_heads(g0 + 256, GLA_H)])),
        ("gla_v", rng(g0 + 512, 512)),
        ("gla_g", rng(g0 + 1024, 512)),
        ("gla_lr", _lane_block(g0 + 1536, 2 * GLA_LR)),
        ("na_q", rng(n0, 512)), ("na_k", rng(n0 + 512, 512)), ("na_v", rng(n0 + 1024, 512)),
        ("gdn_qkv", rng(d0, 1536)),
        ("gdn_z", rng(d0 + 1536, 512)),
        ("gdn_sm", np.concatenate([_lane_block(d0 + 2048 + 4 * i, GDN_H) for i in range(4)])),
        ("m2_z", _pad_heads(m0, M2_H, M2_P, LANES)),
        ("m2_xs", _pad_heads(m0 + 512, M2_H, M2_P, LANES)),
        ("m2_bc", rng(m0 + 1024, 512)),
        ("m2_dt", np.concatenate([_lane_block(m0 + 1536, M2_H), _lane_block(m0 + 1536 + M2_H, M2_H)])),
        ("gate", rng(t0, 4 * D_MODEL)),
    ]


_M2_PAD = _pad_heads(0, M2_H, M2_P, LANES)
_GLA_PAD = _rope_heads(0, GLA_H)


def _row3(v):
    return v.reshape((1, 1, -1))


def _dir_rows(p, n):
    return _row3(jnp.concatenate([_take_cols(p[d][None], _lane_block(0, n)) for d in range(2)], axis=-1))


def _rope_tables(n_lat, n_ctx):
    n_freq = GLA_DK // 4
    freqs = ROPE_BASE ** (-jnp.arange(n_freq, dtype=F32) / n_freq)
    t = jnp.arange(n_lat)
    row = (t // GRID_W).astype(F32)
    colv = (t % GRID_W).astype(F32)
    ang = jnp.concatenate([row[:, None] * freqs, colv[:, None] * freqs], axis=-1)
    c, s = jnp.cos(ang), jnp.sin(ang)
    one, zero = jnp.ones_like(c), jnp.zeros_like(c)
    cos_t = jnp.concatenate([c, one, c, one], axis=-1)
    sin_t = jnp.concatenate([-s, zero, s, zero], axis=-1)
    return (jnp.concatenate([cos_t, jnp.ones((n_ctx, LANES), F32)], axis=0),
            jnp.concatenate([sin_t, jnp.zeros((n_ctx, LANES), F32)], axis=0))


def _na_bias(rpb):
    case = np.arange(NA_WIN_R)
    r = np.arange(NA_WIN_R)
    dr = r[None, :] - case[:, None] + NA_WIN_R - 1
    ci = np.arange(GRID_W)
    dc = np.clip(ci[None, :] - ci[:, None], 1 - NA_WIN_C, NA_WIN_C - 1) + NA_WIN_C - 1
    c0 = np.clip(ci - NA_WIN_C // 2, 0, GRID_W - NA_WIN_C)
    ok = (ci[None, :] >= c0[:, None]) & (ci[None, :] < c0[:, None] + NA_WIN_C)
    pick_r = np.zeros((NA_WIN_R, NA_WIN_R, 2 * NA_WIN_R - 1), np.float32)
    pick_r[case[:, None], r[None, :], dr] = 1.0
    pick_c = np.zeros((2 * NA_WIN_C - 1, GRID_W, GRID_W), np.float32)
    pick_c[dc, ci[:, None], ci[None, :]] = 1.0
    rows = jnp.einsum("hdk,crd->hcrk", rpb, pick_r, precision=HI)
    tbl = jnp.einsum("hcrk,kij->hcirj", rows, pick_c, precision=HI)
    tbl = jnp.where(ok[None, None, :, None, :], tbl, NEG_INF)
    return tbl.reshape((NA_H, NA_WIN_R, GRID_W, NA_WIN_R * GRID_W))


def _layer(l, xs, mod_in, small, slots, gath, win, tables, *, n_lat):
    rw = functools.partial(rowwise, n_lat=n_lat)
    rwb = functools.partial(rowwise, n_lat=n_lat, out_dtype=BF16)
    nm = lambda s: "l%d_%s" % (l, s)
    sl = slots[l]
    mod = linear(mod_in, gath["w_ada"], sl["w_ada"], name=nm("ada"), layout="col", prefix=(l,)) + small["b_ada"][l]
    sh1, sc1, g1, sh2, sc2, g2 = [mod[0:2, i * D_MODEL:(i + 1) * D_MODEL].reshape((2, 1, D_MODEL)) for i in range(6)]
    (h,) = rwb(_f_modnorm, [xs], [], [_row3(small["norm1_g"][l]), sc1, sh1], out_widths=[D_MODEL], tile=256,
              name=nm("norm1"))
    p = {g: linear(h, win[l][g], sl["in_" + g], name=nm("in_" + g)) for g, _ in _in_groups()}

    a2 = small["gla_a2"][l]
    a2p = jnp.concatenate([
        jnp.concatenate([_take_cols(a2[0], _GLA_PAD), jnp.zeros((GLA_LR, 512), F32)], axis=1),
        jnp.concatenate([jnp.zeros((GLA_LR, 512), F32), _take_cols(a2[1], _GLA_PAD)], axis=1),
        jnp.zeros((LANES - 2 * GLA_LR, 1024), F32)], axis=0)[None]
    abp = _row3(jnp.concatenate([_take_cols(small["gla_ab"][l][d][None], _GLA_PAD) for d in range(2)], axis=-1))
    (la,) = rw(_f_gla_prep, [p["gla_lr"]], [], [a2p, abp], out_widths=[1024], tile=256, name=nm("gla_prep"))
    head = lambda off: (off, LANES, 1)
    o = chunk_scan(_gla_step, [p["gla_qk"], p["gla_v"]], [[head(0), head(512)], [head(0)]],
                   [la], [[head(0)]], tables, heads=GLA_H, state_shape=(GLA_DV, LANES), out_w=GLA_DV, n_lat=n_lat,
                   name=nm("gla_scan"))
    (ya,) = rwb(_f_headnorm_gate, [o, p["gla_g"]], [], [_row3(small["gla_norm_g"][l])], out_widths=[BRANCH_W], tile=256,
               name=nm("gla_fin"))

    yb = natten(p["na_q"], p["na_k"], p["na_v"], _na_bias(small["na_rpb"][l]), n_lat=n_lat, name=nm("na"))

    cq = conv_silu(p["gdn_qkv"], small["gdn_conv"][l], jnp.zeros((1, 3 * BRANCH_W), F32), n_lat=n_lat, name=nm("gdn_conv"))
    beta, la = rw(_f_gdn_prep, [p["gdn_sm"]], [], [_dir_rows(small["gdn_a_log"][l], GDN_H), _dir_rows(small["gdn_dt_bias"][l], GDN_H)],
                  out_widths=[256, 256], tile=256, name=nm("gdn_prep"))
    whole = lambda off, width: (off, width, 1)
    o = chunk_scan(_gdn_step, [cq], [[whole(0, 512), whole(512, 512), whole(1024, 512)]], [beta, la],
                   [[whole(0, LANES)], [whole(0, LANES)]], [], heads=1, state_shape=(GDN_H * GDN_D, GDN_D),
                   out_w=BRANCH_W, n_lat=n_lat, name=nm("gdn_scan"))
    (yc,) = rwb(_f_headnorm_gate, [o, p["gdn_z"]], [], [_row3(small["gdn_norm_g"][l])], out_widths=[BRANCH_W], tile=256,
               name=nm("gdn_fin"))

    cw, cb = small["m2_conv"][l], small["m2_conv_b"][l][None]
    cxs = conv_silu(p["m2_xs"], _take_cols(cw[:, :512], _M2_PAD), _take_cols(cb[:, :512], _M2_PAD), n_lat=n_lat,
                    name=nm("m2_conv_x"))
    cbc = conv_silu(p["m2_bc"], cw[:, 512:], cb[:, 512:], n_lat=n_lat, name=nm("m2_conv_bc"))
    dt, la = rw(_f_m2_prep, [p["m2_dt"]], [], [_dir_rows(small["m2_a_log"][l], M2_H), _dir_rows(small["m2_dt_bias"][l], M2_H)],
                out_widths=[256, 256], tile=256, name=nm("m2_prep"))
    o = chunk_scan(_ssd_step, [cxs, cbc], [[whole(0, 2 * BRANCH_W)], [whole(0, BRANCH_W)]], [dt, la],
                   [[whole(0, LANES)], [whole(0, LANES)]], [], heads=1, state_shape=(M2_H * M2_N, LANES),
                   out_w=2 * BRANCH_W, n_lat=n_lat, name=nm("m2_scan"))
    dskip = _row3(jnp.repeat(small["m2_d"][l], LANES))
    (yd,) = rwb(_f_m2_fin, [o, p["m2_z"], cxs], [], [dskip, _row3(_take_cols(small["m2_norm_g"][l][None], _M2_PAD))],
               out_widths=[2 * BRANCH_W], tile=128, name=nm("m2_fin"))

    wb = gath["w_branch"]
    zs = [linear(y, wb, sl["w_branch%d" % i], name=nm("branch%d" % i), layout="col", prefix=(l, i))
          for i, y in enumerate((ya, yb, yc))]
    wb3 = wb[:, l, 3].reshape((N_SLOT, M2_H, M2_P, BRANCH_W))
    wb3 = jnp.pad(wb3, ((0, 0), (0, 0), (0, LANES - M2_P), (0, 0))).reshape((N_SLOT, 2 * BRANCH_W, BRANCH_W))
    zs.append(linear(yd, wb3, sl["w_branch3"], name=nm("branch3"), layout="col"))
    (merged,) = rwb(_f_merge, [p["gate"]] + zs, [], [_row3(small["b_merge"][l].reshape(-1))], out_widths=[D_MODEL], tile=64,
                   name=nm("merge"))
    y = linear(merged, gath["w_out"], sl["w_out"], name=nm("out"), layout="row", prefix=(l,))
    (x1,) = rw(_f_resid, [xs, y], [], [g1], out_widths=[D_MODEL], tile=256, name=nm("res1"))

    (h2,) = rwb(_f_modnorm, [x1], [], [_row3(small["norm2_g"][l]), sc2, sh2], out_widths=[D_MODEL], tile=256,
               name=nm("norm2"))
    u1 = linear(h2, gath["w_ffn1"], sl["w_ffn1"], name=nm("ffn1"), layout="col", prefix=(l,))
    u3 = linear(h2, gath["w_ffn3"], sl["w_ffn3"], name=nm("ffn3"), layout="col", prefix=(l,))
    (act,) = rwb(_f_swiglu, [u1, u3], [], [], out_widths=[D_FF], tile=128, name=nm("swiglu"))
    f = linear(act, gath["w_ffn2"], sl["w_ffn2"], name=nm("ffn2"), layout="row", prefix=(l,))
    (x2,) = rw(_f_resid, [x1, f], [], [g2], out_widths=[D_MODEL], tile=256, name=nm("res2"))
    return x2


def _slot_shapes():
    s = {"w_ada": (N_SLOT, D_MODEL, 6 * D_MODEL // N_SLOT), "w_out": (N_SLOT, D_MODEL // N_SLOT, D_MODEL),
         "w_ffn1": (N_SLOT, D_MODEL, D_FF // N_SLOT), "w_ffn3": (N_SLOT, D_MODEL, D_FF // N_SLOT),
         "w_ffn2": (N_SLOT, D_FF // N_SLOT, D_MODEL), "w_branch3": (N_SLOT, 2 * BRANCH_W, BRANCH_W)}
    for i in range(3):
        s["w_branch%d" % i] = (N_SLOT, BRANCH_W, BRANCH_W)
    for g, src in _in_groups():
        s["in_" + g] = (D_MODEL, len(src))
    return s


def _local_loss(diff, fixed, *, n_lat):
    small = diff["small"]
    n_ctx = fixed["ctx"].shape[0]
    xs = jnp.concatenate([diff["x"], fixed["ctx"]], axis=0)
    cc = jnp.concatenate([fixed["c"], small["c_ctx"][None], jnp.zeros((SUBLANES - 2, D_MODEL), F32)], axis=0)
    (mod_in,) = rowwise(_f_silu, [cc], [], [], out_widths=[D_MODEL], tile=SUBLANES, n_lat=SUBLANES, name="ada_silu",
                        out_dtype=BF16)
    tables = _rope_tables(n_lat, n_ctx)
    for l in range(DEPTH):
        xs = _layer(l, xs, mod_in, small, diff["slots"], fixed["gath"], fixed["win"], tables, n_lat=n_lat)
    (lrow,) = rowwise(_f_loss, [xs[:n_lat]], [fixed["target"]], [_row3(small["final_norm_g"])], out_widths=[LANES],
                      tile=256, n_lat=n_lat, name="loss")
    return jnp.sum(lrow)


def _place():
    x, y, c = lax.axis_index("x"), lax.axis_index("y"), lax.axis_index("c")
    chips = [(1 - x, y), (x, 1 - y), (1 - x, 1 - y)]
    return x, y, c, (x, y, 1 - c), chips


def _remote(src, dst, send_sem, recv_sem, dev):
    return pltpu.make_async_remote_copy(src_ref=src, dst_ref=dst, send_sem=send_sem, recv_sem=recv_sem,
                                        device_id=dev, device_id_type=MESH)


def _dma_sems(*shape):
    return pltpu.SemaphoreType.DMA(shape)


def place_shard(w, slot, *, name):
    depth, k, n = w.shape
    tr = _tile(k, max(2 * SUBLANES, (1 << 19) // n // (2 * SUBLANES) * (2 * SUBLANES)), 2 * SUBLANES)

    def body(s_ref, w_ref, o_ref):
        o_ref[...] = w_ref[...].astype(o_ref.dtype)

    return pl.pallas_call(
        body, name=name,
        grid_spec=pltpu.PrefetchScalarGridSpec(
            num_scalar_prefetch=1, grid=(depth, k // tr),
            in_specs=[pl.BlockSpec((None, tr, n), lambda l, i, s: (l, i, 0))],
            out_specs=pl.BlockSpec((None, None, tr, n), lambda l, i, s: (s[0], l, i, 0))),
        out_shape=jax.ShapeDtypeStruct((N_SLOT, depth, k, n), BF16),
        compiler_params=_cparams("parallel", "parallel"))(slot, w)


def gather_weights(bufs):
    n = len(bufs)

    def body(*refs):
        o = refs[n:2 * n]
        send1, recv1, send2, recv2 = refs[2 * n:]
        x, y, c, sibling, chips = _place()
        g = 2 * x + y
        sent = []
        for k in range(n):
            for j, (cx, cy) in enumerate(chips):
                cp = _remote(o[k].at[g, c], o[k].at[g, c], send1.at[k, j], recv1.at[k, j], (cx, cy, c))
                cp.start()
                sent.append(cp)
        for k in range(n):
            for j, (cx, cy) in enumerate(chips):
                gj = 2 * cx + cy
                _remote(o[k].at[g, c], o[k].at[gj, c], send1.at[k, j], recv1.at[k, j], (cx, cy, c)).wait_recv()
                cp = _remote(o[k].at[gj, c], o[k].at[gj, c], send2.at[k, j], recv2.at[k, j], sibling)
                cp.start()
                sent.append(cp)
        for k in range(n):
            for j, (cx, cy) in enumerate(chips):
                gj = 2 * cx + cy
                _remote(o[k].at[gj, 1 - c], o[k].at[gj, 1 - c], send2.at[k, j], recv2.at[k, j], sibling).wait_recv()
        for cp in sent:
            cp.wait_send()

    return pl.pallas_call(
        body, name="gather_weights", in_specs=[ANY] * n, out_specs=[ANY] * n,
        out_shape=[jax.ShapeDtypeStruct(b.shape, b.dtype) for b in bufs],
        input_output_aliases={k: k for k in range(n)},
        scratch_shapes=[_dma_sems(n, 3), _dma_sems(n, 3), _dma_sems(n, 3), _dma_sems(n, 3)],
    )(*bufs)


def allgather_small(buf, *, name):
    m_per = buf.shape[0]

    def body(x_ref, out_ref, send_sems, recv_sems, local_sem):
        x, y, c, sibling, chips = _place()
        me = (x, y, c)

        def rows(px, py, pc):
            return out_ref.at[pl.ds((4 * px + 2 * py + pc) * m_per, m_per), :]

        def copy(k, block, to, src=None):
            return _remote(rows(*block) if src is None else src, rows(*block), send_sems.at[k], recv_sems.at[k], to)

        mine = pltpu.make_async_copy(x_ref, rows(*me), local_sem)
        mine.start()
        first = [copy(0, me, sibling, src=x_ref)]
        first += [copy(1 + j, me, (*chip, c), src=x_ref) for j, chip in enumerate(chips)]
        for cp in first:
            cp.start()
        passed = [copy(4 + j, (*chip, c), sibling) for j, chip in enumerate(chips)]
        for j, chip in enumerate(chips):
            copy(1 + j, (*chip, c), me).wait_recv()
            passed[j].start()
        copy(0, sibling, me).wait_recv()
        for j, chip in enumerate(chips):
            copy(4 + j, (*chip, 1 - c), me).wait_recv()
        for cp in first + passed:
            cp.wait_send()
        mine.wait()

    return pl.pallas_call(
        body, name=name, out_shape=jax.ShapeDtypeStruct((8 * m_per, LANES), buf.dtype),
        in_specs=[pl.BlockSpec(memory_space=pltpu.VMEM)], out_specs=pl.BlockSpec(memory_space=pltpu.VMEM),
        scratch_shapes=[_dma_sems(7), _dma_sems(7), pltpu.SemaphoreType.DMA],
        compiler_params=pltpu.CompilerParams(vmem_limit_bytes=VMEM_LIMIT),
    )(buf)


def sum_blocks(stacked, n_blocks, *, name):
    m = stacked.shape[0] // n_blocks
    width = stacked.shape[1]
    x3 = stacked.reshape((n_blocks, m, width))
    tr = _tile(m, max(SUBLANES, (1 << 18) // width // SUBLANES * SUBLANES), SUBLANES)

    def body(x_ref, o_ref):
        acc = x_ref[0]
        for s in range(1, n_blocks):
            acc = acc + x_ref[s]
        o_ref[...] = acc

    return pl.pallas_call(body, name=name, grid=(m // tr,), in_specs=[pl.BlockSpec((n_blocks, tr, width), lambda i: (0, i, 0))],
                          out_specs=pl.BlockSpec((tr, width), lambda i: (i, 0)),
                          out_shape=jax.ShapeDtypeStruct((m, width), F32), compiler_params=_cparams("parallel"))(x3)


def reduce_pair(gs):
    n = len(gs)

    def body(*refs):
        g, r = refs[:n], refs[n:2 * n]
        send, recv = refs[2 * n:]
        x, y, c, sibling, _ = _place()
        cps = []
        for i in range(n):
            k2 = gs[i].shape[1] // 2
            cp = _remote(g[i].at[:, pl.ds((1 - c) * k2, k2), :], r[i], send.at[i], recv.at[i], sibling)
            cp.start()
            cps.append(cp)
        for cp in cps:
            cp.wait()

    return pl.pallas_call(
        body, name="reduce_pair", in_specs=[ANY] * n, out_specs=[ANY] * n,
        out_shape=[jax.ShapeDtypeStruct((g.shape[0], g.shape[1] // 2, g.shape[2]), g.dtype) for g in gs],
        scratch_shapes=[_dma_sems(n), _dma_sems(n)],
    )(*gs)


def _row_tile(rows, width, budget, mult):
    return _tile(rows, max(mult, budget // width // mult * mult), mult)


def add_own_half(g, recv, core, *, name):
    n_slot, k2, width = recv.shape
    tr = _row_tile(k2, width, 1 << 19, 2 * SUBLANES)
    nb = k2 // tr

    def body(c_ref, g_ref, r_ref, o_ref):
        o_ref[...] = (g_ref[...] + r_ref[...]).astype(o_ref.dtype)

    spec = pl.BlockSpec((None, tr, width), lambda s, i, c: (s, i, 0))
    return pl.pallas_call(
        body, name=name,
        grid_spec=pltpu.PrefetchScalarGridSpec(
            num_scalar_prefetch=1, grid=(n_slot, nb),
            in_specs=[pl.BlockSpec((None, tr, width), lambda s, i, c: (s, c[0] * nb + i, 0)), spec], out_specs=spec),
        out_shape=jax.ShapeDtypeStruct(recv.shape, BF16), compiler_params=_cparams("parallel", "parallel"))(core, g, recv)


def reduce_chips(qs):
    n = len(qs)

    def body(*refs):
        q, r = refs[:n], refs[n:2 * n]
        send, recv = refs[2 * n:]
        x, y, c, _, chips = _place()
        cps = []
        for i in range(n):
            for j, (cx, cy) in enumerate(chips):
                cp = _remote(q[i].at[2 * cx + cy], r[i].at[j], send.at[i, j], recv.at[i, j], (cx, cy, c))
                cp.start()
                cps.append(cp)
        for cp in cps:
            cp.wait()

    return pl.pallas_call(
        body, name="reduce_chips", in_specs=[ANY] * n, out_specs=[ANY] * n,
        out_shape=[jax.ShapeDtypeStruct((3,) + q.shape[1:], q.dtype) for q in qs],
        scratch_shapes=[_dma_sems(n, 3), _dma_sems(n, 3)],
    )(*qs)


def chip_sum(q, recv, slot, *, name):
    _, k2, width = recv.shape
    tr = _row_tile(k2, width, 1 << 18, 2 * SUBLANES)

    def body(s_ref, q_ref, r_ref, o_ref):
        acc = q_ref[...].astype(F32)
        for j in range(3):
            acc = acc + r_ref[j].astype(F32)
        o_ref[...] = acc

    return pl.pallas_call(
        body, name=name,
        grid_spec=pltpu.PrefetchScalarGridSpec(
            num_scalar_prefetch=1, grid=(k2 // tr,),
            in_specs=[pl.BlockSpec((None, tr, width), lambda i, s: (s[0], i, 0)),
                      pl.BlockSpec((3, tr, width), lambda i, s: (0, i, 0))],
            out_specs=pl.BlockSpec((tr, width), lambda i, s: (i, 0))),
        out_shape=jax.ShapeDtypeStruct((k2, width), F32), compiler_params=_cparams("parallel"))(slot, q, recv)


def swap_pair(rs):
    n = len(rs)

    def body(*refs):
        r, o = refs[:n], refs[n:2 * n]
        send, recv = refs[2 * n:]
        x, y, c, sibling, _ = _place()
        cps = []
        for i in range(n):
            cp = _remote(r[i], o[i], send.at[i], recv.at[i], sibling)
            cp.start()
            cps.append(cp)
        for cp in cps:
            cp.wait()

    return pl.pallas_call(
        body, name="swap_pair", in_specs=[ANY] * n, out_specs=[ANY] * n,
        out_shape=[jax.ShapeDtypeStruct(r.shape, r.dtype) for r in rs],
        scratch_shapes=[_dma_sems(n), _dma_sems(n)],
    )(*rs)


def adamw(w, g, m, v, *, name):
    rows, width = w.shape
    tr = _tile(rows, max(SUBLANES, (1 << 19) // width // SUBLANES * SUBLANES), SUBLANES)

    def body(w_ref, g_ref, m_ref, v_ref, d_ref, mo_ref, vo_ref):
        gv = g_ref[...]
        mn = ADAM_B1 * m_ref[...] + (1.0 - ADAM_B1) * gv
        vn = ADAM_B2 * v_ref[...] + (1.0 - ADAM_B2) * (gv * gv)
        m_hat = mn / (1.0 - ADAM_B1 ** ADAM_STEP)
        v_hat = vn / (1.0 - ADAM_B2 ** ADAM_STEP)
        d_ref[...] = -ADAM_LR * (m_hat / (jnp.sqrt(v_hat) + ADAM_EPS) + ADAM_WD * w_ref[...])
        mo_ref[...] = mn
        vo_ref[...] = vn

    spec = pl.BlockSpec((tr, width), lambda i: (i, 0))
    sds = jax.ShapeDtypeStruct((rows, width), F32)
    return pl.pallas_call(body, name=name, grid=(rows // tr,), in_specs=[spec] * 4, out_specs=(spec,) * 3,
                          out_shape=(sds,) * 3, compiler_params=_cparams("parallel"))(w, g, m, v)


def _pack(arrs):
    flat = jnp.concatenate([a.reshape(-1) for a in arrs])
    pad = (-flat.shape[0]) % (SUBLANES * LANES)
    return jnp.pad(flat, (0, pad)).reshape((-1, LANES))


def _unpack(buf, shapes):
    flat, out, at = buf.reshape(-1), [], 0
    for s in shapes:
        size = int(np.prod(s))
        out.append(flat[at:at + size].reshape(s))
        at += size
    return out


BIG = ["w_ada", "w_in", "w_branch", "w_out", "w_ffn1", "w_ffn3", "w_ffn2"]
SMALL_SHARDED = ["b_merge", "gla_a2", "gla_ab", "gdn_conv", "m2_conv"]
SMALL_WHOLE = ["c_ctx", "norm1_g", "norm2_g", "b_ada", "gla_norm_g", "na_rpb", "gdn_a_log", "gdn_dt_bias", "gdn_norm_g",
               "m2_conv_b", "m2_a_log", "m2_dt_bias", "m2_d", "m2_norm_g", "final_norm_g"]
WEIGHTS = ["c_ctx", "norm1_g", "norm2_g", "w_ada", "b_ada", "w_in", "b_merge", "gla_a2", "gla_ab", "gla_norm_g", "na_rpb",
           "gdn_conv", "gdn_a_log", "gdn_dt_bias", "gdn_norm_g", "m2_conv", "m2_conv_b", "m2_a_log", "m2_dt_bias", "m2_d",
           "m2_norm_g", "w_branch", "w_out", "w_ffn1", "w_ffn3", "w_ffn2", "final_norm_g"]


def _step(a):
    n_lat = a["x"].shape[1]
    x_i, y_i, c_i = lax.axis_index("x"), lax.axis_index("y"), lax.axis_index("c")
    slot = 2 * x_i + y_i

    slot_arr = slot.astype(jnp.int32).reshape((1,))
    core = c_i.astype(jnp.int32).reshape((1,))
    placed = [place_shard(a[n].reshape((DEPTH, -1, a[n].shape[-1])), slot_arr, name="place_" + n) for n in BIG]
    gath = dict(zip(BIG, gather_weights(placed)))
    gath["w_branch"] = gath["w_branch"].reshape((N_SLOT, DEPTH, 4, BRANCH_W, BRANCH_W))
    shard_shapes = [a[n].shape for n in SMALL_SHARDED]
    own = _pack([a[n] for n in SMALL_SHARDED])
    everyone = allgather_small(own, name="gather_small").reshape((8,) + own.shape)
    per_slot = [_unpack(everyone[2 * s], shard_shapes) for s in range(N_SLOT)]
    small = {n: jnp.concatenate([per_slot[s][i] for s in range(N_SLOT)], axis=-1) for i, n in enumerate(SMALL_SHARDED)}
    small.update({n: a[n] for n in SMALL_WHOLE})

    groups = _in_groups()
    win = []
    for l in range(DEPTH):
        full = gath["w_in"][:, l].transpose((1, 0, 2)).reshape((D_MODEL, IN_COLS))
        win.append({g: _take_cols(full, src) for g, src in groups})
    slots = [{n: jnp.zeros(s, F32) for n, s in _slot_shapes().items()} for _ in range(DEPTH)]
    diff = {"x": a["x"][0], "small": small, "slots": slots}
    fixed = {"ctx": a["ctx"][0], "c": a["c"], "target": a["loss_target"][0], "gath": gath, "win": win}
    loss, grads = jax.value_and_grad(lambda d: _local_loss(d, fixed, n_lat=n_lat))(diff)

    parts = []
    for n in BIG:
        for l in range(DEPTH):
            sl = grads["slots"][l]
            if n == "w_in":
                gin = _untake_cols([(sl["in_" + g], src) for g, src in groups], IN_COLS)
                parts.append(gin.reshape((D_MODEL, N_SLOT, IN_COLS // N_SLOT)).transpose((1, 0, 2)))
            elif n == "w_branch":
                b3 = sl["w_branch3"].reshape((N_SLOT, M2_H, LANES, BRANCH_W))[:, :, :M2_P].reshape((N_SLOT, BRANCH_W, BRANCH_W))
                parts.append(jnp.concatenate([sl["w_branch0"], sl["w_branch1"], sl["w_branch2"], b3], axis=1))
            else:
                parts.append(sl[n])
    from_sibling = reduce_pair(parts)
    pair_sums = [add_own_half(g, r, core, name="pair_sum%d" % i) for i, (g, r) in enumerate(zip(parts, from_sibling))]
    from_chips = reduce_chips(pair_sums)
    halves = [chip_sum(q, r, slot_arr, name="chip_sum%d" % i) for i, (q, r) in enumerate(zip(pair_sums, from_chips))]
    others = swap_pair(halves)
    big_grads = {}
    for k, n in enumerate(BIG):
        layers = []
        for l in range(DEPTH):
            mine, theirs = halves[DEPTH * k + l], others[DEPTH * k + l]
            layers.append(jnp.where(c_i == 0, jnp.concatenate([mine, theirs], axis=0),
                                    jnp.concatenate([theirs, mine], axis=0)))
        big_grads[n] = jnp.stack(layers).reshape(a[n].shape)

    small_names = SMALL_WHOLE + SMALL_SHARDED
    partial = _pack([grads["small"][n] for n in small_names] + [loss.reshape((1,))])
    total = sum_blocks(allgather_small(partial, name="gather_small_grads"), 8, name="sum_small_grads")
    pieces = _unpack(total, [grads["small"][n].shape for n in small_names] + [(1,)])
    small_grads = dict(zip(small_names, pieces[:-1]))
    for n in SMALL_SHARDED:
        width = a[n].shape[-1]
        small_grads[n] = lax.dynamic_slice_in_dim(small_grads[n], slot * width, width, axis=-1)
    loss_all = pieces[-1].reshape(())

    grad_w, delta, new_m, new_v = {}, {}, {}, {}
    two_d = lambda t: t.reshape((-1, t.shape[-1]))
    for n in BIG:
        d, mn, vn = adamw(two_d(a[n]), two_d(big_grads[n]), two_d(a["m_" + n]), two_d(a["v_" + n]), name="adamw_" + n)
        grad_w[n], delta[n], new_m[n], new_v[n] = big_grads[n], d.reshape(a[n].shape), mn.reshape(a[n].shape), vn.reshape(a[n].shape)
    shapes = [a[n].shape for n in small_names]
    d, mn, vn = adamw(_pack([a[n] for n in small_names]), _pack([small_grads[n] for n in small_names]),
                      _pack([a["m_" + n] for n in small_names]), _pack([a["v_" + n] for n in small_names]), name="adamw_small")
    for n, dd, mm, vv in zip(small_names, _unpack(d, shapes), _unpack(mn, shapes), _unpack(vn, shapes)):
        grad_w[n], delta[n], new_m[n], new_v[n] = small_grads[n], dd, mm, vv

    return (loss_all, grads["x"][None], *[grad_w[n] for n in WEIGHTS], *[delta[n] for n in WEIGHTS],
            *[new_m[n] for n in WEIGHTS], *[new_v[n] for n in WEIGHTS])


def kernel(x, c, ctx, c_ctx, norm1_g, norm2_g, w_ada, b_ada, w_in, b_merge, gla_a2, gla_ab, gla_norm_g, na_rpb, gdn_conv, gdn_a_log, gdn_dt_bias, gdn_norm_g, m2_conv, m2_conv_b, m2_a_log, m2_dt_bias, m2_d, m2_norm_g, w_branch, w_out, w_ffn1, w_ffn3, w_ffn2, final_norm_g, loss_target, m_c_ctx, m_norm1_g, m_norm2_g, m_w_ada, m_b_ada, m_w_in, m_b_merge, m_gla_a2, m_gla_ab, m_gla_norm_g, m_na_rpb, m_gdn_conv, m_gdn_a_log, m_gdn_dt_bias, m_gdn_norm_g, m_m2_conv, m_m2_conv_b, m_m2_a_log, m_m2_dt_bias, m_m2_d, m_m2_norm_g, m_w_branch, m_w_out, m_w_ffn1, m_w_ffn3, m_w_ffn2, m_final_norm_g, v_c_ctx, v_norm1_g, v_norm2_g, v_w_ada, v_b_ada, v_w_in, v_b_merge, v_gla_a2, v_gla_ab, v_gla_norm_g, v_na_rpb, v_gdn_conv, v_gdn_a_log, v_gdn_dt_bias, v_gdn_norm_g, v_m2_conv, v_m2_conv_b, v_m2_a_log, v_m2_dt_bias, v_m2_d, v_m2_norm_g, v_w_branch, v_w_out, v_w_ffn1, v_w_ffn3, v_w_ffn2, v_final_norm_g):
    return _step(dict(locals()))
```

```python
import functools
import math

import numpy as np
import jax
import jax.numpy as jnp
from jax import lax
from jax.experimental import pallas as pl
from jax.experimental.pallas import tpu as pltpu

F32 = jnp.float32
BF16 = jnp.bfloat16
HI = lax.Precision.HIGHEST
MESH = pl.DeviceIdType.MESH
ANY = pl.BlockSpec(memory_space=pl.ANY)

VMEM_LIMIT = 56 * 1024 * 1024
LANES = 128
SUBLANES = 8

D_MODEL = 2048
DEPTH = 2
GRID_W = 64
CHUNK = 64
CONV_W = 5
RMS_EPS = 1e-6
NEG_INF = -1e30
ROPE_BASE = 10000.0
BRANCH_W = 512
GLA_H, GLA_DK, GLA_DV, GLA_LR, GLA_TAU = 4, 64, 128, 16, 16.0
NA_H, NA_D, NA_WIN_R, NA_WIN_C = 4, 128, 8, 16
GDN_H, GDN_D = 4, 128
M2_P, M2_H, M2_N, M2_G = 64, 8, 128, 2
D_FF = 5632
IN_COLS = 14912
N_SLOT = 4
ADAM_LR, ADAM_B1, ADAM_B2, ADAM_EPS, ADAM_WD, ADAM_STEP = 0.001, 0.9, 0.999, 1e-08, 0.01, 10


def _cparams(*sem):
    return pltpu.CompilerParams(dimension_semantics=sem if sem else None, vmem_limit_bytes=VMEM_LIMIT)


def _tile(n, target, mult):
    if n <= target:
        return n
    best = None
    for t in range(mult, target + 1, mult):
        if n % t == 0:
            best = t
    assert best is not None, (n, target, mult)
    return best


def _nt(a, b):
    return lax.dot_general(a.astype(BF16), b.astype(BF16), (((1,), (1,)), ((), ())), preferred_element_type=F32)


def _tn(a, b):
    return lax.dot_general(a.astype(BF16), b.astype(BF16), (((0,), (0,)), ((), ())), preferred_element_type=F32)


def _nn(a, b):
    return jnp.dot(a.astype(BF16), b.astype(BF16), preferred_element_type=F32)


def _dot3(a, b, dims):
    a_hi, b_hi = a.astype(BF16), b.astype(BF16)
    a_lo = (a - a_hi.astype(F32)).astype(BF16)
    b_lo = (b - b_hi.astype(F32)).astype(BF16)
    dot = lambda u, v: lax.dot_general(u, v, dims, preferred_element_type=F32)
    return dot(a_hi, b_hi) + (dot(a_hi, b_lo) + dot(a_lo, b_hi))


_NN, _NT, _TN = ((((1,), (0,)), ((), ())), (((1,), (1,)), ((), ())), (((0,), (0,)), ((), ())))


@jax.custom_vjp
def _nn_hi(a, b):
    return _dot3(a, b, _NN)


_nn_hi.defvjp(lambda a, b: (_dot3(a, b, _NN), (a, b)),
              lambda res, g: (_dot3(g, res[1], _NT), _dot3(res[0], g, _TN)))


def _w_spec(layout, prefix, r_idx, c_idx, br, bc, slot_dim):
    none = (None,) * len(prefix)
    if layout == "plain":
        return pl.BlockSpec(none + (br, bc), lambda i, j, k: prefix + (r_idx(i, j, k), c_idx(i, j, k)))
    if layout == "col":
        per = slot_dim // bc
        return pl.BlockSpec((None,) + none + (br, bc),
                            lambda i, j, k: (c_idx(i, j, k) // per,) + prefix + (r_idx(i, j, k), c_idx(i, j, k) % per))
    per = slot_dim // br
    return pl.BlockSpec((None,) + none + (br, bc),
                        lambda i, j, k: (r_idx(i, j, k) // per,) + prefix + (r_idx(i, j, k) % per, c_idx(i, j, k)))


def _mm(a, b, *, name, ta=False, tb=False, b_layout="plain", b_prefix=(), out_layout="plain", out_dtype=F32):
    m, kdim = (a.shape[1], a.shape[0]) if ta else a.shape
    rows, cols = b.shape[-2:]
    if b_layout == "col":
        cols *= N_SLOT
    elif b_layout == "row":
        rows *= N_SLOT
    n = rows if tb else cols
    assert (cols if tb else rows) == kdim, (a.shape, b.shape, ta, tb)
    n_unit = n // N_SLOT if (out_layout == "col" or (b_layout == ("row" if tb else "col"))) else n
    k_unit = kdim // N_SLOT if b_layout == ("col" if tb else "row") else kdim
    odd_n = n_unit % 1408 == 0 and n_unit % 512 != 0
    odd_k = k_unit % 1408 == 0 and k_unit % 512 != 0
    if ta:
        tm = _tile(m, 2048, LANES)
        tn = _tile(n_unit, 1408 if odd_n else 1024, LANES)
        tk = _tile(k_unit, 512, 2 * SUBLANES)
    elif tb:
        tm = _tile(m, 768 if odd_k else 1408, 2 * SUBLANES)
        tn = _tile(n_unit, 1408 if odd_n else 2048, LANES)
        tk = _tile(k_unit, 1408 if odd_k else 512, LANES)
    else:
        tm = _tile(m, 768, 2 * SUBLANES)
        tn = _tile(n_unit, 1408 if odd_n else 512, LANES)
        tk = _tile(k_unit, 1408 if odd_k else 2048, LANES)
    nk = kdim // tk
    a_spec = (pl.BlockSpec((tk, tm), lambda i, j, k: (k, i)) if ta else pl.BlockSpec((tm, tk), lambda i, j, k: (i, k)))
    slot_dim = b.shape[-1] if b_layout == "col" else b.shape[-2]
    if tb:
        b_spec = _w_spec(b_layout, tuple(b_prefix), lambda i, j, k: j, lambda i, j, k: k, tn, tk, slot_dim)
    else:
        b_spec = _w_spec(b_layout, tuple(b_prefix), lambda i, j, k: k, lambda i, j, k: j, tk, tn, slot_dim)
    if out_layout == "col":
        per = (n // N_SLOT) // tn
        out_shape = jax.ShapeDtypeStruct((N_SLOT, m, n // N_SLOT), out_dtype)
        out_spec = pl.BlockSpec((None, tm, tn), lambda i, j, k: (j // per, i, j % per))
    else:
        out_shape = jax.ShapeDtypeStruct((m, n), out_dtype)
        out_spec = pl.BlockSpec((tm, tn), lambda i, j, k: (i, j))
    dims = (((0 if ta else 1,), (1 if tb else 0,)), ((), ()))

    def product(a_ref, b_ref):
        return lax.dot_general(a_ref[...].astype(BF16), b_ref[...].astype(BF16), dims, preferred_element_type=F32)

    def body_once(a_ref, b_ref, o_ref):
        o_ref[...] = product(a_ref, b_ref).astype(o_ref.dtype)

    def body(a_ref, b_ref, o_ref, acc_ref):
        k = pl.program_id(2)

        @pl.when(k == 0)
        def _():
            acc_ref[...] = jnp.zeros_like(acc_ref)

        acc_ref[...] += product(a_ref, b_ref)

        @pl.when(k == nk - 1)
        def _():
            o_ref[...] = acc_ref[...].astype(o_ref.dtype)

    return pl.pallas_call(
        body_once if nk == 1 else body, name=name, grid=(m // tm, n // tn, nk), in_specs=[a_spec, b_spec],
        out_specs=out_spec, out_shape=out_shape, scratch_shapes=[] if nk == 1 else [pltpu.VMEM((tm, tn), F32)],
        compiler_params=_cparams("parallel", "parallel", "arbitrary"),
    )(a, b)


def linear(a, w, grad_slot, *, name, layout="plain", prefix=(), out_dtype=F32):
    @jax.custom_vjp
    def f(a, w, grad_slot):
        return _mm(a, w, name=name + "_fwd", b_layout=layout, b_prefix=prefix, out_dtype=out_dtype)

    def fwd(a, w, grad_slot):
        return f(a, w, grad_slot), (a, w)

    def bwd(res, g):
        a, w = res
        da = _mm(g, w, name=name + "_dgrad", tb=True, b_layout=layout, b_prefix=prefix, out_dtype=a.dtype)
        dw = _mm(a, g, name=name + "_wgrad", ta=True, out_layout="col" if layout == "col" else "plain")
        if layout == "row":
            dw = dw.reshape((N_SLOT, dw.shape[0] // N_SLOT, dw.shape[1]))
        return da, None, dw

    f.defvjp(fwd, bwd)
    return f(a, w, grad_slot)


def _rowwise_specs(rows, consts, params, tile, seg_tile):
    def row_spec(r):
        return pl.BlockSpec((tile, r.shape[1]), lambda i: (i, 0))

    def par_spec(p):
        if p.shape[0] == 2:
            return pl.BlockSpec((None,) + p.shape[1:], lambda i: (jnp.where(i >= seg_tile, 1, 0), 0, 0))
        return pl.BlockSpec((None,) + p.shape[1:], lambda i: (0, 0, 0))

    return [row_spec(r) for r in rows], [row_spec(r) for r in consts], [par_spec(p) for p in params]


def rowwise(f, rows, consts, params, *, out_widths, tile, n_lat, name, out_dtype=F32):
    rows, consts, params = tuple(rows), tuple(consts), tuple(params)
    n_rows = rows[0].shape[0]
    tile = math.gcd(math.gcd(n_rows, n_lat), tile)
    assert tile % SUBLANES == 0
    seg_tile = n_lat // tile
    grid = (n_rows // tile,)
    nr, nc, npar = len(rows), len(consts), len(params)
    r_specs, c_specs, p_specs = _rowwise_specs(rows, consts, params, tile, seg_tile)
    out_shape = tuple(jax.ShapeDtypeStruct((n_rows, w), out_dtype) for w in out_widths)
    out_specs = tuple(pl.BlockSpec((tile, w), lambda i: (i, 0)) for w in out_widths)
    n_out = len(out_widths)

    def fwd_call(rows, consts, params):
        def body(*refs):
            ins = [r[...].astype(F32) for r in refs[:nr + nc + npar]]
            outs = f(*ins)
            for o_ref, o in zip(refs[nr + nc + npar:], outs):
                o_ref[...] = o.astype(o_ref.dtype)

        return pl.pallas_call(body, name=name + "_fwd", grid=grid, in_specs=r_specs + c_specs + p_specs,
                              out_specs=out_specs, out_shape=out_shape,
                              compiler_params=_cparams("parallel"))(*rows, *consts, *params)

    def bwd_call(rows, consts, params, gouts):
        def body(*refs):
            i = pl.program_id(0)
            ins = [r[...].astype(F32) for r in refs[:nr + nc + npar]]
            gs = tuple(r[...].astype(F32) for r in refs[nr + nc + npar:nr + nc + npar + n_out])
            d_refs = refs[nr + nc + npar + n_out:]
            cvals = ins[nr:nr + nc]

            def g(*diff):
                return tuple(f(*diff[:nr], *cvals, *diff[nr:]))

            _, vjp = jax.vjp(g, *ins[:nr], *ins[nr + nc:])
            grads = vjp(gs)
            for d_ref, gr in zip(d_refs[:nr], grads[:nr]):
                d_ref[...] = gr.astype(d_ref.dtype)
            for p, d_ref, gr in zip(params, d_refs[nr:], grads[nr:]):
                first = (i == 0) | (i == seg_tile) if p.shape[0] == 2 else (i == 0)

                @pl.when(first)
                def _():
                    d_ref[...] = jnp.zeros_like(d_ref)

                d_ref[...] += gr

        d_shape = tuple(jax.ShapeDtypeStruct(r.shape, r.dtype) for r in rows) + tuple(
            jax.ShapeDtypeStruct(p.shape, F32) for p in params)
        g_specs = [pl.BlockSpec((tile, w), lambda i: (i, 0)) for w in out_widths]
        return pl.pallas_call(body, name=name + "_bwd", grid=grid,
                              in_specs=r_specs + c_specs + p_specs + g_specs,
                              out_specs=tuple(r_specs + p_specs), out_shape=d_shape,
                              compiler_params=_cparams("arbitrary"))(*rows, *consts, *params, *gouts)

    @jax.custom_vjp
    def op(rows, consts, params):
        return fwd_call(rows, consts, params)

    def op_fwd(rows, consts, params):
        return op(rows, consts, params), (rows, consts, params)

    def op_bwd(res, gouts):
        rows, consts, params = res
        d = bwd_call(rows, consts, params, tuple(gouts))
        return tuple(d[:nr]), tuple(None for _ in consts), tuple(d[nr:])

    op.defvjp(op_fwd, op_bwd)
    return op(rows, consts, params)


def _rms(x, width=None):
    w = x.shape[-1] if width is None else width
    return x * lax.rsqrt(jnp.sum(x * x, axis=-1, keepdims=True) * (1.0 / w) + RMS_EPS)


def _silu(x):
    return x * jax.nn.sigmoid(x)


def _f_modnorm(x, g, sc, sh):
    return ((_rms(x) * g) * (1.0 + sc) + sh,)


def _f_silu(x):
    return (_silu(x),)


def _f_gla_prep(lr, a2, ab):
    z = _nn(lr, a2) + ab
    return ((jnp.minimum(z, 0.0) - jnp.log(1.0 + jnp.exp(-jnp.abs(z)))) * (1.0 / GLA_TAU),)


def _f_headnorm_gate(o, g, ng):
    outs = []
    for h in range(BRANCH_W // LANES):
        lo = h * LANES
        oh = o[:, lo:lo + LANES] + o[:, BRANCH_W + lo:BRANCH_W + lo + LANES]
        outs.append(_rms(oh) * ng * _silu(g[:, lo:lo + LANES]))
    return (jnp.concatenate(outs, axis=-1),)


def _f_gdn_prep(x, alog, dtb):
    half = x.shape[1] // 2
    beta = jax.nn.sigmoid(x[:, :half])
    la = -jnp.exp(alog) * jax.nn.softplus(x[:, half:] + dtb)
    return beta, la


def _f_m2_prep(x, alog, dtb):
    dt = jax.nn.softplus(x + dtb)
    return dt, -jnp.exp(alog) * dt


def _f_m2_fin(o, z, xs, dskip, ng):
    w = z.shape[1]
    y = (o[:, :w] + o[:, w:] + dskip * xs) * _silu(z)
    return (_rms(y, BRANCH_W) * ng,)


def _f_merge(gate, z0, z1, z2, z3, bm):
    acc = None
    for i, z in enumerate((z0, z1, z2, z3)):
        lo = i * D_MODEL
        t = jax.nn.sigmoid(gate[:, lo:lo + D_MODEL] + bm[:, lo:lo + D_MODEL]) * z
        acc = t if acc is None else acc + t
    return (acc,)


def _f_resid(x, y, g):
    return (x + g * y,)


def _f_swiglu(u1, u3):
    return (_silu(u1) * u3,)


def _f_loss(x, tgt, g):
    e = _rms(x) * g - tgt
    per_row = 0.5 * jnp.sum(e * e, axis=-1, keepdims=True) * (1.0 / D_MODEL)
    return (jnp.broadcast_to(per_row * (1.0 / LANES), (x.shape[0], LANES)),)


_HALO = 8


def _conv_segments(n_lat, n_ctx):
    segs = [(0, _HALO, n_lat), (n_lat, n_lat + 3 * _HALO, n_ctx)]
    return segs, n_lat + n_ctx + 4 * _HALO


def _conv_stage(buf, src, n_lat, n_ctx):
    zeros = jnp.zeros((_HALO, LANES), F32)
    buf[0:_HALO, :] = zeros
    buf[_HALO:_HALO + n_lat, :] = src[0:n_lat, :]
    buf[n_lat + _HALO:n_lat + 2 * _HALO, :] = zeros
    buf[n_lat + 2 * _HALO:n_lat + 3 * _HALO, :] = zeros
    buf[n_lat + 3 * _HALO:n_lat + 3 * _HALO + n_ctx, :] = src[n_lat:n_lat + n_ctx, :]
    buf[n_lat + n_ctx + 3 * _HALO:n_lat + n_ctx + 4 * _HALO, :] = zeros


def conv_silu(x, w, b, *, n_lat, name):
    n_rows, n_ch = x.shape
    n_ctx = n_rows - n_lat
    segs, n_buf = _conv_segments(n_lat, n_ctx)
    grid = (n_ch // LANES,)
    col = lambda r: pl.BlockSpec((r, LANES), lambda j: (0, j))
    half = CONV_W // 2

    def tiles():
        for row0, off, length in segs:
            tr = _tile(length, 256, SUBLANES)
            for t0 in range(0, length, tr):
                yield row0 + t0, off + t0, tr

    def pre_act(buf, w_ref, b_ref, off, tr):
        acc = jnp.broadcast_to(b_ref[...], (tr, LANES))
        for j in range(CONV_W):
            acc = acc + w_ref[j:j + 1, :] * buf[off + j - half:off + j - half + tr, :]
        return acc

    def fwd_call(x, w, b):
        def body(x_ref, w_ref, b_ref, o_ref, buf):
            _conv_stage(buf, x_ref, n_lat, n_ctx)
            for row, off, tr in tiles():
                o_ref[row:row + tr, :] = _silu(pre_act(buf, w_ref, b_ref, off, tr))

        return pl.pallas_call(body, name=name + "_fwd", grid=grid, in_specs=[col(n_rows), col(CONV_W), col(1)],
                              out_specs=col(n_rows), out_shape=jax.ShapeDtypeStruct(x.shape, F32),
                              scratch_shapes=[pltpu.VMEM((n_buf, LANES), F32)],
                              compiler_params=_cparams("parallel"))(x, w, b)

    def bwd_call(x, w, b, g):
        def body(x_ref, w_ref, b_ref, g_ref, dx_ref, dw_ref, db_ref, xbuf, dbuf):
            _conv_stage(xbuf, x_ref, n_lat, n_ctx)
            _conv_stage(dbuf, g_ref, n_lat, n_ctx)
            dw = [jnp.zeros((1, LANES), F32) for _ in range(CONV_W)]
            db = jnp.zeros((1, LANES), F32)
            for row, off, tr in tiles():
                pre = pre_act(xbuf, w_ref, b_ref, off, tr)
                s = jax.nn.sigmoid(pre)
                dpre = g_ref[row:row + tr, :] * (s * (1.0 + pre * (1.0 - s)))
                dbuf[off:off + tr, :] = dpre
                db = db + jnp.sum(dpre, axis=0, keepdims=True)
                for j in range(CONV_W):
                    dw[j] = dw[j] + jnp.sum(dpre * xbuf[off + j - half:off + j - half + tr, :], axis=0, keepdims=True)
            for row, off, tr in tiles():
                acc = jnp.zeros((tr, LANES), F32)
                for j in range(CONV_W):
                    acc = acc + w_ref[j:j + 1, :] * dbuf[off - j + half:off - j + half + tr, :]
                dx_ref[row:row + tr, :] = acc
            for j in range(CONV_W):
                dw_ref[j:j + 1, :] = dw[j]
            db_ref[...] = db

        return pl.pallas_call(
            body, name=name + "_bwd", grid=grid, in_specs=[col(n_rows), col(CONV_W), col(1), col(n_rows)],
            out_specs=(col(n_rows), col(CONV_W), col(1)),
            out_shape=(jax.ShapeDtypeStruct(x.shape, F32), jax.ShapeDtypeStruct(w.shape, F32),
                       jax.ShapeDtypeStruct(b.shape, F32)),
            scratch_shapes=[pltpu.VMEM((n_buf, LANES), F32), pltpu.VMEM((n_buf, LANES), F32)],
            compiler_params=_cparams("parallel"))(x, w, b, g)

    @jax.custom_vjp
    def op(x, w, b):
        return fwd_call(x, w, b)

    op.defvjp(lambda x, w, b: (op(x, w, b), (x, w, b)), lambda res, g: bwd_call(*res, g))
    return op(x, w, b)


def chunk_scan(step, shared, shared_lanes, perdir, perdir_lanes, consts, *, heads, state_shape, out_w, n_lat, name):
    shared, perdir, consts = tuple(shared), tuple(perdir), tuple(consts)
    n_rows = shared[0].shape[0]
    nl, ncx = n_lat // CHUNK, (n_rows - n_lat) // CHUNK
    n_chunks = nl + ncx
    ow_all = heads * out_w
    ns, npd, ncst = len(shared), len(perdir), len(consts)

    def cidx(d, n):
        m = n - ncx
        return jnp.where(n < ncx, nl + jnp.where(d == 0, n, ncx - 1 - n), jnp.where(d == 0, m, nl - 1 - m))

    def specs(order):
        sh = [pl.BlockSpec((CHUNK, a.shape[1]), lambda d, n: (cidx(d, order(n)), 0)) for a in shared]
        pd = [pl.BlockSpec((CHUNK, a.shape[1] // 2), lambda d, n: (cidx(d, order(n)), d)) for a in perdir]
        cs = [pl.BlockSpec((CHUNK, a.shape[1]), lambda d, n: (cidx(d, order(n)), 0)) for a in consts]
        o = pl.BlockSpec((CHUNK, ow_all), lambda d, n: (cidx(d, order(n)), d))
        st = pl.BlockSpec((None, None, heads) + state_shape, lambda d, n: (d, order(n), 0) + (0,) * len(state_shape))
        return sh, pd, cs, o, st

    def mask(d):
        r = lax.broadcasted_iota(jnp.int32, (CHUNK, CHUNK), 0)
        c = lax.broadcasted_iota(jnp.int32, (CHUNK, CHUNK), 1)
        lower = jnp.where(r >= c, 1.0, 0.0).astype(F32)
        upper = jnp.where(r <= c, 1.0, 0.0).astype(F32)
        return jnp.where(d == 0, lower, upper)

    def head_slices(h):
        out = []
        for lanes in tuple(shared_lanes) + tuple(perdir_lanes):
            out.append([slice(off + (h // hpg) * w, off + (h // hpg) * w + w) for off, w, hpg in lanes])
        return out

    def load(refs, h):
        return tuple(tuple(ref[:, s] for s in sl) for ref, sl in zip(refs, head_slices(h)))

    state_sds = jax.ShapeDtypeStruct((2, n_chunks, heads) + state_shape, F32)

    def fwd_call(shared, perdir, consts):
        sh, pd, cs, o_spec, st_spec = specs(lambda n: n)

        def body(*refs):
            in_refs = refs[:ns + npd]
            c_refs = refs[ns + npd:ns + npd + ncst]
            o_ref, ss_ref, s_scr = refs[ns + npd + ncst:]
            d, n = pl.program_id(0), pl.program_id(1)

            @pl.when(n == 0)
            def _():
                s_scr[...] = jnp.zeros_like(s_scr)

            m = mask(d)
            cv = tuple(c[...] for c in c_refs)
            ins = [load(in_refs, h) for h in range(heads)]
            s0 = [s_scr[h] for h in range(heads)]
            res = [step(ins[h], cv, s0[h], m) for h in range(heads)]
            for h in range(heads):
                o_ref[:, h * out_w:(h + 1) * out_w] = res[h][0]
                ss_ref[h] = s0[h]
                s_scr[h] = res[h][1]

        return pl.pallas_call(
            body, name=name + "_fwd", grid=(2, n_chunks), in_specs=sh + pd + cs, out_specs=(o_spec, st_spec),
            out_shape=(jax.ShapeDtypeStruct((n_rows, 2 * ow_all), F32), state_sds),
            scratch_shapes=[pltpu.VMEM((heads,) + state_shape, F32)],
            compiler_params=_cparams("arbitrary", "arbitrary"))(*shared, *perdir, *consts)

    def bwd_call(shared, perdir, consts, starts, g):
        sh, pd, cs, o_spec, st_spec = specs(lambda n: n_chunks - 1 - n)
        dsh = [pl.BlockSpec((CHUNK, a.shape[1]), lambda d, n: (cidx(d, n_chunks - 1 - n), d)) for a in shared]

        def body(*refs):
            in_refs = refs[:ns + npd]
            c_refs = refs[ns + npd:ns + npd + ncst]
            ss_ref, g_ref = refs[ns + npd + ncst:ns + npd + ncst + 2]
            d_refs = refs[ns + npd + ncst + 2:ns + npd + ncst + 2 + ns + npd]
            ds_scr = refs[-1]
            d, n = pl.program_id(0), pl.program_id(1)

            @pl.when(n == 0)
            def _():
                ds_scr[...] = jnp.zeros_like(ds_scr)

            m = mask(d)
            cv = tuple(c[...] for c in c_refs)
            ins = [load(in_refs, h) for h in range(heads)]
            cots = [(g_ref[:, h * out_w:(h + 1) * out_w], ds_scr[h]) for h in range(heads)]
            starts = [ss_ref[h] for h in range(heads)]
            grads = []
            for h in range(heads):
                _, vjp = jax.vjp(lambda i_, s_: step(i_, cv, s_, m), ins[h], starts[h])
                grads.append(vjp(cots[h]))
            for d_ref in d_refs:
                d_ref[...] = jnp.zeros_like(d_ref)
            for h in range(heads):
                g_ins, g_s = grads[h]
                for d_ref, sl, gr in zip(d_refs, head_slices(h), g_ins):
                    for s, gv in zip(sl, gr):
                        d_ref[:, s] += gv
                ds_scr[h] = g_s

        d_shape = tuple(jax.ShapeDtypeStruct((n_rows, 2 * a.shape[1]), F32) for a in shared) + tuple(
            jax.ShapeDtypeStruct(a.shape, F32) for a in perdir)
        return pl.pallas_call(
            body, name=name + "_bwd", grid=(2, n_chunks), in_specs=sh + pd + cs + [st_spec, o_spec],
            out_specs=tuple(dsh + pd), out_shape=d_shape,
            scratch_shapes=[pltpu.VMEM((heads,) + state_shape, F32)],
            compiler_params=_cparams("arbitrary", "arbitrary"))(*shared, *perdir, *consts, starts, g)

    @jax.custom_vjp
    def op(shared, perdir, consts):
        return fwd_call(shared, perdir, consts)[0]

    def op_fwd(shared, perdir, consts):
        o, starts = fwd_call(shared, perdir, consts)
        return o, (shared, perdir, consts, starts)

    def op_bwd(res, g):
        shared, perdir, consts, starts = res
        d = bwd_call(shared, perdir, consts, starts, g)
        d_sh = tuple(a[:, :a.shape[1] // 2] + a[:, a.shape[1] // 2:] for a in d[:ns])
        return d_sh, tuple(d[ns:]), tuple(None for _ in consts)

    op.defvjp(op_fwd, op_bwd)
    return op(shared, perdir, consts)


@jax.custom_vjp
def _swap_halves(x):
    return pltpu.roll(x, LANES // 2, 1)


_swap_halves.defvjp(lambda x: (_swap_halves(x), None), lambda _, g: (_swap_halves(g),))


def _gla_step(ins, consts, st, m):
    (q, k), (v,), (la,) = ins
    cos, sin = consts
    q = (q * cos + _swap_halves(q) * sin) * (GLA_DK ** -0.5)
    k = k * cos + _swap_halves(k) * sin
    b = _nn_hi(m, la)
    bl = jnp.sum(la, axis=0, keepdims=True)
    qi = q * jnp.exp(b)
    ki = k * jnp.exp(-b)
    att = _nt(qi, ki) * m
    o = _nt(qi, st) + _nn(att, v)
    st_new = st * jnp.exp(bl) + _tn(v, k * jnp.exp(bl - b))
    return o, st_new


def _l2n(x):
    return x * lax.rsqrt(jnp.sum(x * x, axis=-1, keepdims=True) + RMS_EPS)


def _tri_inv_fwd(nmat):
    r = lax.broadcasted_iota(jnp.int32, nmat.shape, 0)
    c = lax.broadcasted_iota(jnp.int32, nmat.shape, 1)
    inv = jnp.where(r == c, 1.0, 0.0).astype(F32) - nmat
    p = nmat
    for _ in range(5):
        p = _nn_hi(p, p)
        inv = inv + _nn_hi(inv, p)
    return inv


@jax.custom_vjp
def _unit_tri_inv(nmat):
    return _tri_inv_fwd(nmat)


def _unit_tri_inv_bwd(inv, g):
    return (-_dot3(_dot3(inv, g, _TN), inv, _NT),)


_unit_tri_inv.defvjp(lambda nmat: (lambda inv: (inv, inv))(_tri_inv_fwd(nmat)), _unit_tri_inv_bwd)


def _lane_col(x, h):
    lane = lax.broadcasted_iota(jnp.int32, x.shape, 1)
    return jnp.sum(jnp.where(lane == h, x, 0.0), axis=1, keepdims=True)


def _masked_exp(diff, mask):
    return jnp.where(mask > 0, jnp.exp(jnp.where(mask > 0, diff, 0.0)), 0.0)


def _gdn_step(ins, consts, s, m):
    (q, k, v), (beta,), (la,) = ins
    n = GDN_H * CHUNK
    hs = range(GDN_H)
    blk = lambda x, h: x[:, h * GDN_D:(h + 1) * GDN_D]
    rows = lambda x, h: x[h * CHUNK:(h + 1) * CHUNK]
    qh = [_l2n(blk(q, h)) * (GDN_D ** -0.5) for h in hs]
    kh = [_l2n(blk(k, h)) for h in hs]
    k_st = jnp.concatenate(kh, axis=0)
    q_st = jnp.concatenate(qh, axis=0)
    v_st = jnp.concatenate([blk(v, h) for h in hs], axis=0)
    beta_st = jnp.concatenate([_lane_col(beta, h) for h in hs], axis=0)
    la_cols = [_lane_col(la, h) for h in hs]
    la_st = jnp.concatenate([jnp.broadcast_to(c, (CHUNK, GDN_D)) for c in la_cols], axis=0)
    r = lax.broadcasted_iota(jnp.int32, (n, n), 0)
    c = lax.broadcasted_iota(jnp.int32, (n, n), 1)
    e = jnp.where(lax.broadcasted_iota(jnp.int32, (n, CHUNK), 0) % CHUNK == lax.broadcasted_iota(jnp.int32, (n, CHUNK), 1),
                  1.0, 0.0).astype(F32)
    m_bd = jnp.where(r // CHUNK == c // CHUNK, _nt(_nn(e, m), e), 0.0)
    eye = jnp.where(r == c, 1.0, 0.0).astype(F32)
    b_st = _nn_hi(m_bd, la_st)
    b_t = b_st.T
    diff = jnp.concatenate([b_st, b_st], axis=1) - jnp.concatenate([b_t, b_t], axis=0)
    incl = _masked_exp(diff, m_bd)
    strict = _masked_exp(diff, m_bd - eye)
    inv = _unit_tri_inv(beta_st * _nt(k_st, k_st) * strict)
    wu = _nn_hi(inv, jnp.concatenate([k_st * (beta_st * jnp.exp(b_st)), v_st * beta_st], axis=-1))
    w, u0 = wu[:, :GDN_D], wu[:, GDN_D:]
    us, s_new, qs = [], [], []
    for h in hs:
        s_h = s[h * GDN_D:(h + 1) * GDN_D]
        bl = jnp.sum(jnp.broadcast_to(la_cols[h], (CHUNK, GDN_D)), axis=0, keepdims=True)
        u_h = rows(u0, h) - _nn(rows(w, h), s_h)
        s_new.append(jnp.exp(bl) * s_h + _tn(kh[h] * jnp.exp(bl - rows(b_st, h)), u_h))
        us.append(u_h)
        qs.append(_nn(qh[h], s_h))
    o_st = jnp.exp(b_st) * jnp.concatenate(qs, axis=0) + _nn(_nt(q_st, k_st) * incl, jnp.concatenate(us, axis=0))
    return jnp.concatenate([rows(o_st, h) for h in hs], axis=1), jnp.concatenate(s_new, axis=0)


def _ssd_step(ins, consts, s, m):
    (xs,), (bc,), (dt,), (la,) = ins
    hpg = M2_H // M2_G
    b_all = _nn_hi(m, la)
    bl_all = jnp.sum(la, axis=0, keepdims=True)
    b_t = b_all.T
    row_id = lax.broadcasted_iota(jnp.int32, b_t.shape, 0)
    bm = [bc[:, g * M2_N:(g + 1) * M2_N] for g in range(M2_G)]
    cm = [bc[:, (M2_G + g) * M2_N:(M2_G + g + 1) * M2_N] for g in range(M2_G)]
    scores = [_nt(cm[g], bm[g]) for g in range(M2_G)]
    outs, s_new = [], []
    for h in range(M2_H):
        g = h // hpg
        s_h = s[h * M2_N:(h + 1) * M2_N]
        b_col = _lane_col(b_all, h)
        bl = _lane_col(bl_all, h)
        b_row = jnp.sum(jnp.where(row_id == h, b_t, 0.0), axis=0, keepdims=True)
        xv = xs[:, h * LANES:(h + 1) * LANES] * _lane_col(dt, h)
        outs.append(jnp.exp(b_col) * _nn(cm[g], s_h) + _nn(scores[g] * _masked_exp(b_col - b_row, m), xv))
        s_new.append(jnp.exp(bl) * s_h + _tn(bm[g] * jnp.exp(bl - b_col), xv))
    return jnp.concatenate(outs, axis=1), jnp.concatenate(s_new, axis=0)


def _na_tile(q, kw, vw, kc, vc, bias):
    qs = q * (NA_D ** -0.5)
    s1 = _nt(qs, kw) + bias
    s2 = _nt(qs, kc)
    mx = lax.stop_gradient(jnp.maximum(jnp.max(s1, axis=-1, keepdims=True), jnp.max(s2, axis=-1, keepdims=True)))
    p1 = jnp.exp(s1 - mx)
    p2 = jnp.exp(s2 - mx)
    den = jnp.sum(p1, axis=-1, keepdims=True) + jnp.sum(p2, axis=-1, keepdims=True)
    return (_nn(p1, vw) + _nn(p2, vc)) / den


def _ctx_tile(q, k, v):
    s = _nt(q * (NA_D ** -0.5), k)
    p = jnp.exp(s - lax.stop_gradient(jnp.max(s, axis=-1, keepdims=True)))
    return _nn(p, v) / jnp.sum(p, axis=-1, keepdims=True)


def natten(q, k, v, bias, *, n_lat, name):
    n_rows = q.shape[0]
    n_ctx = n_rows - n_lat
    g_rows = n_lat // GRID_W
    win = NA_WIN_R * GRID_W
    ctx_blk = n_lat // n_ctx

    def start(n):
        return jnp.clip(n - NA_WIN_R // 2, 0, g_rows - NA_WIN_R)

    def case(n):
        return n - start(n)

    q_spec = pl.BlockSpec((GRID_W, LANES), lambda h, n: (n, h))
    lat_spec = pl.BlockSpec((n_lat, LANES), lambda h, n: (0, h))
    ctx_in = pl.BlockSpec((n_ctx, LANES), lambda h, n: (ctx_blk, h))
    ctx_out = pl.BlockSpec((n_ctx, LANES), lambda h, n: (0, h))
    bias_spec = pl.BlockSpec((None, None, GRID_W, win), lambda h, n: (h, case(n), 0, 0))
    lat_sds = jax.ShapeDtypeStruct((n_lat, BRANCH_W), F32)
    ctx_sds = jax.ShapeDtypeStruct((n_ctx, BRANCH_W), F32)

    def lat_fwd(q, k, v, bias):
        def body(q_ref, k_ref, v_ref, kc_ref, vc_ref, b_ref, o_ref):
            r0 = pl.multiple_of(start(pl.program_id(1)) * GRID_W, GRID_W)
            o_ref[...] = _na_tile(q_ref[...], k_ref[pl.ds(r0, win), :], v_ref[pl.ds(r0, win), :],
                                  kc_ref[...], vc_ref[...], b_ref[...])

        return pl.pallas_call(body, name=name + "_lat_fwd", grid=(NA_H, g_rows),
                              in_specs=[q_spec, lat_spec, lat_spec, ctx_in, ctx_in, bias_spec], out_specs=q_spec,
                              out_shape=lat_sds, compiler_params=_cparams("parallel", "arbitrary"))(q, k, v, k, v, bias)

    def lat_bwd(q, k, v, bias, g):
        def body(q_ref, k_ref, v_ref, kc_ref, vc_ref, b_ref, g_ref, dq_ref, dk_ref, dv_ref, dkc_ref, dvc_ref, db_ref):
            n = pl.program_id(1)
            r0 = pl.multiple_of(start(n) * GRID_W, GRID_W)

            @pl.when(n == 0)
            def _():
                for r in (dk_ref, dv_ref, dkc_ref, dvc_ref):
                    r[...] = jnp.zeros_like(r)

            @pl.when((n == 0) | (case(n) != case(jnp.maximum(n - 1, 0))))
            def _():
                db_ref[...] = jnp.zeros_like(db_ref)

            _, vjp = jax.vjp(_na_tile, q_ref[...], k_ref[pl.ds(r0, win), :], v_ref[pl.ds(r0, win), :],
                             kc_ref[...], vc_ref[...], b_ref[...])
            dq, dkw, dvw, dkc, dvc, db = vjp(g_ref[...])
            dq_ref[...] = dq
            dk_ref[pl.ds(r0, win), :] += dkw
            dv_ref[pl.ds(r0, win), :] += dvw
            dkc_ref[...] += dkc
            dvc_ref[...] += dvc
            db_ref[...] += db

        return pl.pallas_call(
            body, name=name + "_lat_bwd", grid=(NA_H, g_rows),
            in_specs=[q_spec, lat_spec, lat_spec, ctx_in, ctx_in, bias_spec, q_spec],
            out_specs=(q_spec, lat_spec, lat_spec, ctx_out, ctx_out, bias_spec),
            out_shape=(lat_sds, lat_sds, lat_sds, ctx_sds, ctx_sds, jax.ShapeDtypeStruct(bias.shape, F32)),
            compiler_params=_cparams("parallel", "arbitrary"))(q, k, v, k, v, bias, g)

    c_in = pl.BlockSpec((n_ctx, LANES), lambda h: (ctx_blk, h))
    c_out = pl.BlockSpec((n_ctx, LANES), lambda h: (0, h))

    def ctx_fwd(q, k, v):
        def body(q_ref, k_ref, v_ref, o_ref):
            o_ref[...] = _ctx_tile(q_ref[...], k_ref[...], v_ref[...])

        return pl.pallas_call(body, name=name + "_ctx_fwd", grid=(NA_H,), in_specs=[c_in, c_in, c_in], out_specs=c_out,
                              out_shape=ctx_sds, compiler_params=_cparams("parallel"))(q, k, v)

    def ctx_bwd(q, k, v, g):
        def body(q_ref, k_ref, v_ref, g_ref, dq_ref, dk_ref, dv_ref):
            _, vjp = jax.vjp(_ctx_tile, q_ref[...], k_ref[...], v_ref[...])
            dq_ref[...], dk_ref[...], dv_ref[...] = vjp(g_ref[...])

        return pl.pallas_call(body, name=name + "_ctx_bwd", grid=(NA_H,), in_specs=[c_in, c_in, c_in, c_out],
                              out_specs=(c_out, c_out, c_out), out_shape=(ctx_sds, ctx_sds, ctx_sds),
                              compiler_params=_cparams("parallel"))(q, k, v, g)

    @jax.custom_vjp
    def op(q, k, v, bias):
        return jnp.concatenate([lat_fwd(q, k, v, bias), ctx_fwd(q, k, v)], axis=0)

    def op_bwd(res, g):
        q, k, v, bias = res
        dq, dk, dv, dkc, dvc, db = lat_bwd(q, k, v, bias, g[:n_lat])
        dqc, dkc2, dvc2 = ctx_bwd(q, k, v, g[n_lat:])
        return (jnp.concatenate([dq, dqc], axis=0), jnp.concatenate([dk, dkc + dkc2], axis=0),
                jnp.concatenate([dv, dvc + dvc2], axis=0), db)

    op.defvjp(lambda q, k, v, bias: (op(q, k, v, bias), (q, k, v, bias)), op_bwd)
    return op(q, k, v, bias)


def _runs(src):
    src = np.asarray(src)
    out, i = [], 0
    while i < len(src):
        j = i + 1
        if src[i] < 0:
            while j < len(src) and src[j] < 0:
                j += 1
            out.append((-1, j - i))
        else:
            while j < len(src) and src[j] == src[j - 1] + 1:
                j += 1
            out.append((int(src[i]), j - i))
        i = j
    return out


def _take_cols(w, src):
    pieces = [jnp.zeros(w.shape[:-1] + (ln,), w.dtype) if s < 0 else w[..., s:s + ln] for s, ln in _runs(src)]
    return pieces[0] if len(pieces) == 1 else jnp.concatenate(pieces, axis=-1)


def _untake_cols(parts, n_cols):
    found = []
    for arr, src in parts:
        pos = 0
        for s, ln in _runs(src):
            if s >= 0:
                found.append((s, arr[..., pos:pos + ln]))
            pos += ln
    found.sort(key=lambda t: t[0])
    at = 0
    for s, piece in found:
        assert s == at, (s, at)
        at += piece.shape[-1]
    assert at == n_cols, (at, n_cols)
    return jnp.concatenate([p for _, p in found], axis=-1)


def _pad_heads(base, heads, real, width):
    return np.concatenate([np.concatenate([base + h * real + np.arange(real), -np.ones(width - real, np.int64)])
                           for h in range(heads)])


def _rope_heads(base, heads):
    z = -np.ones(32, np.int64)
    return np.concatenate([np.concatenate([base + h * 64 + np.arange(32), z, base + h * 64 + 32 + np.arange(32), z])
                           for h in range(heads)])


def _lane_block(base, n):
    return np.concatenate([base + np.arange(n), -np.ones(LANES - n, np.int64)])


def _in_groups():
    g0, n0, d0, m0, t0 = 0, 1568, 3104, 5168, 6720
    rng = lambda a, n: a + np.arange(n)
    return [
        ("gla_qk", np.concatenate([_rope_heads(g0, GLA_H), _rope_heads(g0 + 256, GLA_H)])),
        ("gla_v", rng(g0 + 512, 512)),
        ("gla_g", rng(g0 + 1024, 512)),
        ("gla_lr", _lane_block(g0 + 1536, 2 * GLA_LR)),
        ("na_q", rng(n0, 512)), ("na_k", rng(n0 + 512, 512)), ("na_v", rng(n0 + 1024, 512)),
        ("gdn_qkv", rng(d0, 1536)),
        ("gdn_z", rng(d0 + 1536, 512)),
        ("gdn_sm", np.concatenate([_lane_block(d0 + 2048 + 4 * i, GDN_H) for i in range(4)])),
        ("m2_z", _pad_heads(m0, M2_H, M2_P, LANES)),
        ("m2_xs", _pad_heads(m0 + 512, M2_H, M2_P, LANES)),
        ("m2_bc", rng(m0 + 1024, 512)),
        ("m2_dt", np.concatenate([_lane_block(m0 + 1536, M2_H), _lane_block(m0 + 1536 + M2_H, M2_H)])),
        ("gate", rng(t0, 4 * D_MODEL)),
    ]


_M2_PAD = _pad_heads(0, M2_H, M2_P, LANES)
_GLA_PAD = _rope_heads(0, GLA_H)


def _row3(v):
    return v.reshape((1, 1, -1))


def _dir_rows(p, n):
    return _row3(jnp.concatenate([_take_cols(p[d][None], _lane_block(0, n)) for d in range(2)], axis=-1))


def _rope_tables(n_lat, n_ctx):
    n_freq = GLA_DK // 4
    freqs = ROPE_BASE ** (-jnp.arange(n_freq, dtype=F32) / n_freq)
    t = jnp.arange(n_lat)
    row = (t // GRID_W).astype(F32)
    colv = (t % GRID_W).astype(F32)
    ang = jnp.concatenate([row[:, None] * freqs, colv[:, None] * freqs], axis=-1)
    c, s = jnp.cos(ang), jnp.sin(ang)
    one, zero = jnp.ones_like(c), jnp.zeros_like(c)
    cos_t = jnp.concatenate([c, one, c, one], axis=-1)
    sin_t = jnp.concatenate([-s, zero, s, zero], axis=-1)
    return (jnp.concatenate([cos_t, jnp.ones((n_ctx, LANES), F32)], axis=0),
            jnp.concatenate([sin_t, jnp.zeros((n_ctx, LANES), F32)], axis=0))


def _na_bias(rpb):
    case = np.arange(NA_WIN_R)
    r = np.arange(NA_WIN_R)
    dr = r[None, :] - case[:, None] + NA_WIN_R - 1
    ci = np.arange(GRID_W)
    dc = np.clip(ci[None, :] - ci[:, None], 1 - NA_WIN_C, NA_WIN_C - 1) + NA_WIN_C - 1
    c0 = np.clip(ci - NA_WIN_C // 2, 0, GRID_W - NA_WIN_C)
    ok = (ci[None, :] >= c0[:, None]) & (ci[None, :] < c0[:, None] + NA_WIN_C)
    pick_r = np.zeros((NA_WIN_R, NA_WIN_R, 2 * NA_WIN_R - 1), np.float32)
    pick_r[case[:, None], r[None, :], dr] = 1.0
    pick_c = np.zeros((2 * NA_WIN_C - 1, GRID_W, GRID_W), np.float32)
    pick_c[dc, ci[:, None], ci[None, :]] = 1.0
    rows = jnp.einsum("hdk,crd->hcrk", rpb, pick_r, precision=HI)
    tbl = jnp.einsum("hcrk,kij->hcirj", rows, pick_c, precision=HI)
    tbl = jnp.where(ok[None, None, :, None, :], tbl, NEG_INF)
    return tbl.reshape((NA_H, NA_WIN_R, GRID_W, NA_WIN_R * GRID_W))


def _layer(l, xs, mod, small, slots, gath, win, tables, *, n_lat):
    rw = functools.partial(rowwise, n_lat=n_lat)
    rwb = functools.partial(rowwise, n_lat=n_lat, out_dtype=BF16)
    nm = lambda s: "l%d_%s" % (l, s)
    sl = slots[l]
    sh1, sc1, g1, sh2, sc2, g2 = [mod[:, i * D_MODEL:(i + 1) * D_MODEL].reshape((2, 1, D_MODEL)) for i in range(6)]
    (h,) = rwb(_f_modnorm, [xs], [], [_row3(small["norm1_g"][l]), sc1, sh1], out_widths=[D_MODEL], tile=256,
              name=nm("norm1"))
    p = {g: linear(h, win[l][g], sl["in_" + g], name=nm("in_" + g), out_dtype=BF16 if g == "gate" else F32)
         for g, _ in _in_groups()}

    a2 = small["gla_a2"][l]
    a2p = jnp.concatenate([
        jnp.concatenate([_take_cols(a2[0], _GLA_PAD), jnp.zeros((GLA_LR, 512), F32)], axis=1),
        jnp.concatenate([jnp.zeros((GLA_LR, 512), F32), _take_cols(a2[1], _GLA_PAD)], axis=1),
        jnp.zeros((LANES - 2 * GLA_LR, 1024), F32)], axis=0)[None]
    abp = _row3(jnp.concatenate([_take_cols(small["gla_ab"][l][d][None], _GLA_PAD) for d in range(2)], axis=-1))
    (la,) = rw(_f_gla_prep, [p["gla_lr"]], [], [a2p, abp], out_widths=[1024], tile=256, name=nm("gla_prep"))
    head = lambda off: (off, LANES, 1)
    o = chunk_scan(_gla_step, [p["gla_qk"], p["gla_v"]], [[head(0), head(512)], [head(0)]],
                   [la], [[head(0)]], tables, heads=GLA_H, state_shape=(GLA_DV, LANES), out_w=GLA_DV, n_lat=n_lat,
                   name=nm("gla_scan"))
    (ya,) = rwb(_f_headnorm_gate, [o, p["gla_g"]], [], [_row3(small["gla_norm_g"][l])], out_widths=[BRANCH_W], tile=256,
               name=nm("gla_fin"))

    yb = natten(p["na_q"], p["na_k"], p["na_v"], _na_bias(small["na_rpb"][l]), n_lat=n_lat, name=nm("na"))

    cq = conv_silu(p["gdn_qkv"], small["gdn_conv"][l], jnp.zeros((1, 3 * BRANCH_W), F32), n_lat=n_lat, name=nm("gdn_conv"))
    beta, la = rw(_f_gdn_prep, [p["gdn_sm"]], [], [_dir_rows(small["gdn_a_log"][l], GDN_H), _dir_rows(small["gdn_dt_bias"][l], GDN_H)],
                  out_widths=[256, 256], tile=256, name=nm("gdn_prep"))
    whole = lambda off, width: (off, width, 1)
    o = chunk_scan(_gdn_step, [cq], [[whole(0, 512), whole(512, 512), whole(1024, 512)]], [beta, la],
                   [[whole(0, LANES)], [whole(0, LANES)]], [], heads=1, state_shape=(GDN_H * GDN_D, GDN_D),
                   out_w=BRANCH_W, n_lat=n_lat, name=nm("gdn_scan"))
    (yc,) = rwb(_f_headnorm_gate, [o, p["gdn_z"]], [], [_row3(small["gdn_norm_g"][l])], out_widths=[BRANCH_W], tile=256,
               name=nm("gdn_fin"))

    cw, cb = small["m2_conv"][l], small["m2_conv_b"][l][None]
    cxs = conv_silu(p["m2_xs"], _take_cols(cw[:, :512], _M2_PAD), _take_cols(cb[:, :512], _M2_PAD), n_lat=n_lat,
                    name=nm("m2_conv_x"))
    cbc = conv_silu(p["m2_bc"], cw[:, 512:], cb[:, 512:], n_lat=n_lat, name=nm("m2_conv_bc"))
    dt, la = rw(_f_m2_prep, [p["m2_dt"]], [], [_dir_rows(small["m2_a_log"][l], M2_H), _dir_rows(small["m2_dt_bias"][l], M2_H)],
                out_widths=[256, 256], tile=256, name=nm("m2_prep"))
    o = chunk_scan(_ssd_step, [cxs, cbc], [[whole(0, 2 * BRANCH_W)], [whole(0, BRANCH_W)]], [dt, la],
                   [[whole(0, LANES)], [whole(0, LANES)]], [], heads=1, state_shape=(M2_H * M2_N, LANES),
                   out_w=2 * BRANCH_W, n_lat=n_lat, name=nm("m2_scan"))
    dskip = _row3(jnp.repeat(small["m2_d"][l], LANES))
    (yd,) = rwb(_f_m2_fin, [o, p["m2_z"], cxs], [], [dskip, _row3(_take_cols(small["m2_norm_g"][l][None], _M2_PAD))],
               out_widths=[2 * BRANCH_W], tile=128, name=nm("m2_fin"))

    wb = gath["w_branch"]
    zs = [linear(y, wb, sl["w_branch%d" % i], name=nm("branch%d" % i), layout="col", prefix=(l, i))
          for i, y in enumerate((ya, yb, yc))]
    wb3 = wb[:, l, 3].reshape((N_SLOT, M2_H, M2_P, BRANCH_W))
    wb3 = jnp.pad(wb3, ((0, 0), (0, 0), (0, LANES - M2_P), (0, 0))).reshape((N_SLOT, 2 * BRANCH_W, BRANCH_W))
    zs.append(linear(yd, wb3, sl["w_branch3"], name=nm("branch3"), layout="col"))
    (merged,) = rwb(_f_merge, [p["gate"]] + zs, [], [_row3(small["b_merge"][l].reshape(-1))], out_widths=[D_MODEL], tile=64,
                   name=nm("merge"))
    y = linear(merged, gath["w_out"], sl["w_out"], name=nm("out"), layout="row", prefix=(l,))
    (x1,) = rw(_f_resid, [xs, y], [], [g1], out_widths=[D_MODEL], tile=256, name=nm("res1"))

    (h2,) = rwb(_f_modnorm, [x1], [], [_row3(small["norm2_g"][l]), sc2, sh2], out_widths=[D_MODEL], tile=256,
               name=nm("norm2"))
    u1 = linear(h2, gath["w_ffn1"], sl["w_ffn1"], name=nm("ffn1"), layout="col", prefix=(l,), out_dtype=BF16)
    u3 = linear(h2, gath["w_ffn3"], sl["w_ffn3"], name=nm("ffn3"), layout="col", prefix=(l,), out_dtype=BF16)
    (act,) = rwb(_f_swiglu, [u1, u3], [], [], out_widths=[D_FF], tile=128, name=nm("swiglu"))
    f = linear(act, gath["w_ffn2"], sl["w_ffn2"], name=nm("ffn2"), layout="row", prefix=(l,))
    (x2,) = rw(_f_resid, [x1, f], [], [g2], out_widths=[D_MODEL], tile=256, name=nm("res2"))
    return x2


def _slot_shapes():
    s = {"w_out": (N_SLOT, D_MODEL // N_SLOT, D_MODEL),
         "w_ffn1": (N_SLOT, D_MODEL, D_FF // N_SLOT), "w_ffn3": (N_SLOT, D_MODEL, D_FF // N_SLOT),
         "w_ffn2": (N_SLOT, D_FF // N_SLOT, D_MODEL), "w_branch3": (N_SLOT, 2 * BRANCH_W, BRANCH_W)}
    for i in range(3):
        s["w_branch%d" % i] = (N_SLOT, BRANCH_W, BRANCH_W)
    for g, src in _in_groups():
        s["in_" + g] = (D_MODEL, len(src))
    return s


ADA_ROWS = 2 * SUBLANES


def ada_shard(c_all, c_ctx, w_ada, slots):
    cc = jnp.concatenate([c_all, c_ctx[None], jnp.zeros((ADA_ROWS - c_all.shape[0] - 1, D_MODEL), F32)], axis=0)
    (act,) = rowwise(_f_silu, [cc], [], [], out_widths=[D_MODEL], tile=ADA_ROWS, n_lat=ADA_ROWS, name="ada_silu",
                     out_dtype=BF16)
    return [linear(act, w_ada, slots[l], name="l%d_ada" % l, prefix=(l,)) for l in range(DEPTH)]


def _local_loss(diff, fixed, *, n_lat):
    small = diff["small"]
    n_ctx = fixed["ctx"].shape[0]
    xs = jnp.concatenate([diff["x"], fixed["ctx"]], axis=0)
    tables = _rope_tables(n_lat, n_ctx)
    for l in range(DEPTH):
        xs = _layer(l, xs, diff["mod"][l], small, diff["slots"], fixed["gath"], fixed["win"], tables, n_lat=n_lat)
    (lrow,) = rowwise(_f_loss, [xs[:n_lat]], [fixed["target"]], [_row3(small["final_norm_g"])], out_widths=[LANES],
                      tile=256, n_lat=n_lat, name="loss")
    return jnp.sum(lrow)


def _place():
    x, y, c = lax.axis_index("x"), lax.axis_index("y"), lax.axis_index("c")
    chips = [(1 - x, y), (x, 1 - y), (1 - x, 1 - y)]
    return x, y, c, (x, y, 1 - c), chips


def _remote(src, dst, send_sem, recv_sem, dev):
    return pltpu.make_async_remote_copy(src_ref=src, dst_ref=dst, send_sem=send_sem, recv_sem=recv_sem,
                                        device_id=dev, device_id_type=MESH)


def _dma_sems(*shape):
    return pltpu.SemaphoreType.DMA(shape)


def place_shard(w, slot, *, name):
    depth, k, n = w.shape
    tr = _tile(k, max(2 * SUBLANES, (1 << 19) // n // (2 * SUBLANES) * (2 * SUBLANES)), 2 * SUBLANES)

    def body(s_ref, w_ref, o_ref):
        o_ref[...] = w_ref[...].astype(o_ref.dtype)

    return pl.pallas_call(
        body, name=name,
        grid_spec=pltpu.PrefetchScalarGridSpec(
            num_scalar_prefetch=1, grid=(depth, k // tr),
            in_specs=[pl.BlockSpec((None, tr, n), lambda l, i, s: (l, i, 0))],
            out_specs=pl.BlockSpec((None, None, tr, n), lambda l, i, s: (s[0], l, i, 0))),
        out_shape=jax.ShapeDtypeStruct((N_SLOT, depth, k, n), BF16),
        compiler_params=_cparams("parallel", "parallel"))(slot, w)


def gather_weights(bufs):
    n = len(bufs)

    def body(*refs):
        o = refs[n:2 * n]
        send1, recv1, send2, recv2 = refs[2 * n:]
        x, y, c, sibling, chips = _place()
        g = 2 * x + y
        sent = []
        for k in range(n):
            for j, (cx, cy) in enumerate(chips):
                cp = _remote(o[k].at[g, c], o[k].at[g, c], send1.at[k, j], recv1.at[k, j], (cx, cy, c))
                cp.start()
                sent.append(cp)
        for k in range(n):
            for j, (cx, cy) in enumerate(chips):
                gj = 2 * cx + cy
                _remote(o[k].at[g, c], o[k].at[gj, c], send1.at[k, j], recv1.at[k, j], (cx, cy, c)).wait_recv()
                cp = _remote(o[k].at[gj, c], o[k].at[gj, c], send2.at[k, j], recv2.at[k, j], sibling)
                cp.start()
                sent.append(cp)
        for k in range(n):
            for j, (cx, cy) in enumerate(chips):
                gj = 2 * cx + cy
                _remote(o[k].at[gj, 1 - c], o[k].at[gj, 1 - c], send2.at[k, j], recv2.at[k, j], sibling).wait_recv()
        for cp in sent:
            cp.wait_send()

    return pl.pallas_call(
        body, name="gather_weights", in_specs=[ANY] * n, out_specs=[ANY] * n,
        out_shape=[jax.ShapeDtypeStruct(b.shape, b.dtype) for b in bufs],
        input_output_aliases={k: k for k in range(n)},
        scratch_shapes=[_dma_sems(n, 3), _dma_sems(n, 3), _dma_sems(n, 3), _dma_sems(n, 3)],
    )(*bufs)


def allgather_small(buf, *, name):
    m_per = buf.shape[0]

    def body(x_ref, out_ref, send_sems, recv_sems, local_sem):
        x, y, c, sibling, chips = _place()
        me = (x, y, c)

        def rows(px, py, pc):
            return out_ref.at[pl.ds((4 * px + 2 * py + pc) * m_per, m_per), :]

        def copy(k, block, to, src=None):
            return _remote(rows(*block) if src is None else src, rows(*block), send_sems.at[k], recv_sems.at[k], to)

        mine = pltpu.make_async_copy(x_ref, rows(*me), local_sem)
        mine.start()
        first = [copy(0, me, sibling, src=x_ref)]
        first += [copy(1 + j, me, (*chip, c), src=x_ref) for j, chip in enumerate(chips)]
        for cp in first:
            cp.start()
        passed = [copy(4 + j, (*chip, c), sibling) for j, chip in enumerate(chips)]
        for j, chip in enumerate(chips):
            copy(1 + j, (*chip, c), me).wait_recv()
            passed[j].start()
        copy(0, sibling, me).wait_recv()
        for j, chip in enumerate(chips):
            copy(4 + j, (*chip, 1 - c), me).wait_recv()
        for cp in first + passed:
            cp.wait_send()
        mine.wait()

    return pl.pallas_call(
        body, name=name, out_shape=jax.ShapeDtypeStruct((8 * m_per, LANES), buf.dtype),
        in_specs=[pl.BlockSpec(memory_space=pltpu.VMEM)], out_specs=pl.BlockSpec(memory_space=pltpu.VMEM),
        scratch_shapes=[_dma_sems(7), _dma_sems(7), pltpu.SemaphoreType.DMA],
        compiler_params=pltpu.CompilerParams(vmem_limit_bytes=VMEM_LIMIT),
    )(buf)


def sum_blocks(stacked, n_blocks, *, name):
    m = stacked.shape[0] // n_blocks
    width = stacked.shape[1]
    x3 = stacked.reshape((n_blocks, m, width))
    tr = _tile(m, max(SUBLANES, (1 << 18) // width // SUBLANES * SUBLANES), SUBLANES)

    def body(x_ref, o_ref):
        acc = x_ref[0]
        for s in range(1, n_blocks):
            acc = acc + x_ref[s]
        o_ref[...] = acc

    return pl.pallas_call(body, name=name, grid=(m // tr,), in_specs=[pl.BlockSpec((n_blocks, tr, width), lambda i: (0, i, 0))],
                          out_specs=pl.BlockSpec((tr, width), lambda i: (i, 0)),
                          out_shape=jax.ShapeDtypeStruct((m, width), F32), compiler_params=_cparams("parallel"))(x3)


def reduce_pair(gs):
    n = len(gs)

    def body(*refs):
        g, r = refs[:n], refs[n:2 * n]
        send, recv = refs[2 * n:]
        x, y, c, sibling, _ = _place()
        cps = []
        for i in range(n):
            k2 = gs[i].shape[1] // 2
            cp = _remote(g[i].at[:, pl.ds((1 - c) * k2, k2), :], r[i], send.at[i], recv.at[i], sibling)
            cp.start()
            cps.append(cp)
        for cp in cps:
            cp.wait()

    return pl.pallas_call(
        body, name="reduce_pair", in_specs=[ANY] * n, out_specs=[ANY] * n,
        out_shape=[jax.ShapeDtypeStruct((g.shape[0], g.shape[1] // 2, g.shape[2]), g.dtype) for g in gs],
        scratch_shapes=[_dma_sems(n), _dma_sems(n)],
    )(*gs)


def _row_tile(rows, width, budget, mult):
    return _tile(rows, max(mult, budget // width // mult * mult), mult)


def add_own_half(g, recv, core, *, name):
    n_slot, k2, width = recv.shape
    tr = _row_tile(k2, width, 1 << 19, 2 * SUBLANES)
    nb = k2 // tr

    def body(c_ref, g_ref, r_ref, o_ref):
        o_ref[...] = (g_ref[...] + r_ref[...]).astype(o_ref.dtype)

    spec = pl.BlockSpec((None, tr, width), lambda s, i, c: (s, i, 0))
    return pl.pallas_call(
        body, name=name,
        grid_spec=pltpu.PrefetchScalarGridSpec(
            num_scalar_prefetch=1, grid=(n_slot, nb),
            in_specs=[pl.BlockSpec((None, tr, width), lambda s, i, c: (s, c[0] * nb + i, 0)), spec], out_specs=spec),
        out_shape=jax.ShapeDtypeStruct(recv.shape, BF16), compiler_params=_cparams("parallel", "parallel"))(core, g, recv)


def reduce_chips(qs):
    n = len(qs)

    def body(*refs):
        q, r = refs[:n], refs[n:2 * n]
        send, recv = refs[2 * n:]
        x, y, c, _, chips = _place()
        cps = []
        for i in range(n):
            for j, (cx, cy) in enumerate(chips):
                cp = _remote(q[i].at[2 * cx + cy], r[i].at[j], send.at[i, j], recv.at[i, j], (cx, cy, c))
                cp.start()
                cps.append(cp)
        for cp in cps:
            cp.wait()

    return pl.pallas_call(
        body, name="reduce_chips", in_specs=[ANY] * n, out_specs=[ANY] * n,
        out_shape=[jax.ShapeDtypeStruct((3,) + q.shape[1:], q.dtype) for q in qs],
        scratch_shapes=[_dma_sems(n, 3), _dma_sems(n, 3)],
    )(*qs)


def chip_sum(q, recv, slot, *, name):
    _, k2, width = recv.shape
    tr = _row_tile(k2, width, 1 << 18, 2 * SUBLANES)

    def body(s_ref, q_ref, r_ref, o_ref):
        acc = q_ref[...].astype(F32)
        for j in range(3):
            acc = acc + r_ref[j].astype(F32)
        o_ref[...] = acc

    return pl.pallas_call(
        body, name=name,
        grid_spec=pltpu.PrefetchScalarGridSpec(
            num_scalar_prefetch=1, grid=(k2 // tr,),
            in_specs=[pl.BlockSpec((None, tr, width), lambda i, s: (s[0], i, 0)),
                      pl.BlockSpec((3, tr, width), lambda i, s: (0, i, 0))],
            out_specs=pl.BlockSpec((tr, width), lambda i, s: (i, 0))),
        out_shape=jax.ShapeDtypeStruct((k2, width), F32), compiler_params=_cparams("parallel"))(slot, q, recv)


def swap_pair(rs):
    n = len(rs)

    def body(*refs):
        r, o = refs[:n], refs[n:2 * n]
        send, recv = refs[2 * n:]
        x, y, c, sibling, _ = _place()
        cps = []
        for i in range(n):
            cp = _remote(r[i], o[i], send.at[i], recv.at[i], sibling)
            cp.start()
            cps.append(cp)
        for cp in cps:
            cp.wait()

    return pl.pallas_call(
        body, name="swap_pair", in_specs=[ANY] * n, out_specs=[ANY] * n,
        out_shape=[jax.ShapeDtypeStruct(r.shape, r.dtype) for r in rs],
        scratch_shapes=[_dma_sems(n), _dma_sems(n)],
    )(*rs)


def adamw(w, g, m, v, *, name):
    rows, width = w.shape
    tr = _tile(rows, max(SUBLANES, (1 << 19) // width // SUBLANES * SUBLANES), SUBLANES)

    def body(w_ref, g_ref, m_ref, v_ref, d_ref, mo_ref, vo_ref):
        gv = g_ref[...]
        mn = ADAM_B1 * m_ref[...] + (1.0 - ADAM_B1) * gv
        vn = ADAM_B2 * v_ref[...] + (1.0 - ADAM_B2) * (gv * gv)
        m_hat = mn / (1.0 - ADAM_B1 ** ADAM_STEP)
        v_hat = vn / (1.0 - ADAM_B2 ** ADAM_STEP)
        d_ref[...] = -ADAM_LR * (m_hat / (jnp.sqrt(v_hat) + ADAM_EPS) + ADAM_WD * w_ref[...])
        mo_ref[...] = mn
        vo_ref[...] = vn

    spec = pl.BlockSpec((tr, width), lambda i: (i, 0))
    sds = jax.ShapeDtypeStruct((rows, width), F32)
    return pl.pallas_call(body, name=name, grid=(rows // tr,), in_specs=[spec] * 4, out_specs=(spec,) * 3,
                          out_shape=(sds,) * 3, compiler_params=_cparams("parallel"))(w, g, m, v)


def _pack(arrs):
    flat = jnp.concatenate([a.reshape(-1) for a in arrs])
    pad = (-flat.shape[0]) % (SUBLANES * LANES)
    return jnp.pad(flat, (0, pad)).reshape((-1, LANES))


def _unpack(buf, shapes):
    flat, out, at = buf.reshape(-1), [], 0
    for s in shapes:
        size = int(np.prod(s))
        out.append(flat[at:at + size].reshape(s))
        at += size
    return out


BIG = ["w_in", "w_branch", "w_out", "w_ffn1", "w_ffn3", "w_ffn2"]
SMALL_SHARDED = ["b_merge", "gla_a2", "gla_ab", "gdn_conv", "m2_conv"]
SMALL_WHOLE = ["norm1_g", "norm2_g", "gla_norm_g", "na_rpb", "gdn_a_log", "gdn_dt_bias", "gdn_norm_g",
               "m2_conv_b", "m2_a_log", "m2_dt_bias", "m2_d", "m2_norm_g", "final_norm_g"]
WEIGHTS = ["c_ctx", "norm1_g", "norm2_g", "w_ada", "b_ada", "w_in", "b_merge", "gla_a2", "gla_ab", "gla_norm_g", "na_rpb",
           "gdn_conv", "gdn_a_log", "gdn_dt_bias", "gdn_norm_g", "m2_conv", "m2_conv_b", "m2_a_log", "m2_dt_bias", "m2_d",
           "m2_norm_g", "w_branch", "w_out", "w_ffn1", "w_ffn3", "w_ffn2", "final_norm_g"]


def _step(a):
    n_lat = a["x"].shape[1]
    x_i, y_i, c_i = lax.axis_index("x"), lax.axis_index("y"), lax.axis_index("c")
    slot = 2 * x_i + y_i

    slot_arr = slot.astype(jnp.int32).reshape((1,))
    core = c_i.astype(jnp.int32).reshape((1,))
    placed = [place_shard(a[n].reshape((DEPTH, -1, a[n].shape[-1])), slot_arr, name="place_" + n) for n in BIG]
    gath = dict(zip(BIG, gather_weights(placed)))
    gath["w_branch"] = gath["w_branch"].reshape((N_SLOT, DEPTH, 4, BRANCH_W, BRANCH_W))
    shard_shapes = [a[n].shape for n in SMALL_SHARDED]
    own = _pack([a[n] for n in SMALL_SHARDED])
    everyone = allgather_small(own, name="gather_small").reshape((8,) + own.shape)
    per_slot = [_unpack(everyone[2 * s], shard_shapes) for s in range(N_SLOT)]
    small = {n: jnp.concatenate([per_slot[s][i] for s in range(N_SLOT)], axis=-1) for i, n in enumerate(SMALL_SHARDED)}
    small.update({n: a[n] for n in SMALL_WHOLE})

    me = 4 * x_i + 2 * y_i + c_i
    ada_cols = a["w_ada"].shape[-1]
    c_all = allgather_small(a["c"].reshape((-1, LANES)), name="gather_c").reshape((8, D_MODEL))
    ada_slots = [jnp.zeros(a["w_ada"].shape[1:], F32) for _ in range(DEPTH)]
    mod_shards, ada_vjp = jax.vjp(lambda c_ctx, sl: ada_shard(c_all, c_ctx, a["w_ada"], sl), a["c_ctx"], ada_slots)
    packed = _pack(mod_shards)
    every = allgather_small(packed, name="gather_ada").reshape((8,) + packed.shape)
    by_slot = [_unpack(every[2 * s], [(ADA_ROWS, ada_cols)] * DEPTH) for s in range(N_SLOT)]
    mod = []
    for l in range(DEPTH):
        rows = jnp.concatenate([by_slot[s][l] for s in range(N_SLOT)], axis=-1) + a["b_ada"][l]
        mod.append(jnp.concatenate([lax.dynamic_slice_in_dim(rows, me, 1, axis=0), rows[8:9]], axis=0))

    groups = _in_groups()
    win = []
    for l in range(DEPTH):
        full = gath["w_in"][:, l].transpose((1, 0, 2)).reshape((D_MODEL, IN_COLS))
        win.append({g: _take_cols(full, src) for g, src in groups})
    slots = [{n: jnp.zeros(s, F32) for n, s in _slot_shapes().items()} for _ in range(DEPTH)]
    diff = {"x": a["x"][0], "mod": mod, "small": small, "slots": slots}
    fixed = {"ctx": a["ctx"][0], "target": a["loss_target"][0], "gath": gath, "win": win}
    loss, grads = jax.value_and_grad(lambda d: _local_loss(d, fixed, n_lat=n_lat))(diff)

    dmod = _pack(grads["mod"])
    every = allgather_small(dmod, name="gather_dmod").reshape((8,) + dmod.shape)
    per_dev = [_unpack(every[i], [(2, 6 * D_MODEL)] * DEPTH) for i in range(8)]
    grad_b_ada, cots = [], []
    for l in range(DEPTH):
        lat = jnp.concatenate([per_dev[i][l][0:1] for i in range(8)], axis=0)
        ctx_rows = jnp.concatenate([per_dev[i][l][1].reshape((-1, LANES)) for i in range(8)], axis=0)
        ctx_sum = sum_blocks(ctx_rows, 8, name="l%d_dmod_ctx_sum" % l).reshape((1, 6 * D_MODEL))
        all_rows = jnp.concatenate([lat, ctx_sum, jnp.zeros((ADA_ROWS - 9, 6 * D_MODEL), F32)], axis=0)
        grad_b_ada.append(sum_blocks(all_rows.reshape((-1, LANES)), ADA_ROWS, name="l%d_b_ada_sum" % l).reshape(-1))
        cots.append(lax.dynamic_slice_in_dim(all_rows, slot * ada_cols, ada_cols, axis=1))
    c_ctx_part, ada_grads = ada_vjp(cots)

    parts = []
    for n in BIG:
        for l in range(DEPTH):
            sl = grads["slots"][l]
            if n == "w_in":
                gin = _untake_cols([(sl["in_" + g], src) for g, src in groups], IN_COLS)
                parts.append(gin.reshape((D_MODEL, N_SLOT, IN_COLS // N_SLOT)).transpose((1, 0, 2)))
            elif n == "w_branch":
                b3 = sl["w_branch3"].reshape((N_SLOT, M2_H, LANES, BRANCH_W))[:, :, :M2_P].reshape((N_SLOT, BRANCH_W, BRANCH_W))
                parts.append(jnp.concatenate([sl["w_branch0"], sl["w_branch1"], sl["w_branch2"], b3], axis=1))
            else:
                parts.append(sl[n])
    from_sibling = reduce_pair(parts)
    pair_sums = [add_own_half(g, r, core, name="pair_sum%d" % i) for i, (g, r) in enumerate(zip(parts, from_sibling))]
    from_chips = reduce_chips(pair_sums)
    halves = [chip_sum(q, r, slot_arr, name="chip_sum%d" % i) for i, (q, r) in enumerate(zip(pair_sums, from_chips))]
    others = swap_pair(halves)
    big_grads = {}
    for k, n in enumerate(BIG):
        layers = []
        for l in range(DEPTH):
            mine, theirs = halves[DEPTH * k + l], others[DEPTH * k + l]
            layers.append(jnp.where(c_i == 0, jnp.concatenate([mine, theirs], axis=0),
                                    jnp.concatenate([theirs, mine], axis=0)))
        big_grads[n] = jnp.stack(layers).reshape(a[n].shape)

    summed = SMALL_WHOLE + SMALL_SHARDED + ["c_ctx"]
    local = dict(grads["small"], c_ctx=0.5 * c_ctx_part)
    partial = _pack([local[n] for n in summed] + [loss.reshape((1,))])
    total = sum_blocks(allgather_small(partial, name="gather_small_grads"), 8, name="sum_small_grads")
    pieces = _unpack(total, [local[n].shape for n in summed] + [(1,)])
    small_grads = dict(zip(summed, pieces[:-1]))
    for n in SMALL_SHARDED:
        width = a[n].shape[-1]
        small_grads[n] = lax.dynamic_slice_in_dim(small_grads[n], slot * width, width, axis=-1)
    small_grads["b_ada"] = jnp.stack(grad_b_ada)
    big_grads["w_ada"] = jnp.stack(ada_grads)
    small_names = summed + ["b_ada"]
    loss_all = pieces[-1].reshape(())

    grad_w, delta, new_m, new_v = {}, {}, {}, {}
    two_d = lambda t: t.reshape((-1, t.shape[-1]))
    for n in BIG + ["w_ada"]:
        d, mn, vn = adamw(two_d(a[n]), two_d(big_grads[n]), two_d(a["m_" + n]), two_d(a["v_" + n]), name="adamw_" + n)
        grad_w[n], delta[n], new_m[n], new_v[n] = big_grads[n], d.reshape(a[n].shape), mn.reshape(a[n].shape), vn.reshape(a[n].shape)
    shapes = [a[n].shape for n in small_names]
    d, mn, vn = adamw(_pack([a[n] for n in small_names]), _pack([small_grads[n] for n in small_names]),
                      _pack([a["m_" + n] for n in small_names]), _pack([a["v_" + n] for n in small_names]), name="adamw_small")
    for n, dd, mm, vv in zip(small_names, _unpack(d, shapes), _unpack(mn, shapes), _unpack(vn, shapes)):
        grad_w[n], delta[n], new_m[n], new_v[n] = small_grads[n], dd, mm, vv

    return (loss_all, grads["x"][None], *[grad_w[n] for n in WEIGHTS], *[delta[n] for n in WEIGHTS],
            *[new_m[n] for n in WEIGHTS], *[new_v[n] for n in WEIGHTS])


def kernel(x, c, ctx, c_ctx, norm1_g, norm2_g, w_ada, b_ada, w_in, b_merge, gla_a2, gla_ab, gla_norm_g, na_rpb, gdn_conv, gdn_a_log, gdn_dt_bias, gdn_norm_g, m2_conv, m2_conv_b, m2_a_log, m2_dt_bias, m2_d, m2_norm_g, w_branch, w_out, w_ffn1, w_ffn3, w_ffn2, final_norm_g, loss_target, m_c_ctx, m_norm1_g, m_norm2_g, m_w_ada, m_b_ada, m_w_in, m_b_merge, m_gla_a2, m_gla_ab, m_gla_norm_g, m_na_rpb, m_gdn_conv, m_gdn_a_log, m_gdn_dt_bias, m_gdn_norm_g, m_m2_conv, m_m2_conv_b, m_m2_a_log, m_m2_dt_bias, m_m2_d, m_m2_norm_g, m_w_branch, m_w_out, m_w_ffn1, m_w_ffn3, m_w_ffn2, m_final_norm_g, v_c_ctx, v_norm1_g, v_norm2_g, v_w_ada, v_b_ada, v_w_in, v_b_merge, v_gla_a2, v_gla_ab, v_gla_norm_g, v_na_rpb, v_gdn_conv, v_gdn_a_log, v_gdn_dt_bias, v_gdn_norm_g, v_m2_conv, v_m2_conv_b, v_m2_a_log, v_m2_dt_bias, v_m2_d, v_m2_norm_g, v_w_branch, v_w_out, v_w_ffn1, v_w_ffn3, v_w_ffn2, v_final_norm_g):
    return _step(dict(locals()))
```

```python
import functools
import math

import numpy as np
import jax
import jax.numpy as jnp
from jax import lax
from jax.experimental import pallas as pl
from jax.experimental.pallas import tpu as pltpu

F32 = jnp.float32
BF16 = jnp.bfloat16
HI = lax.Precision.HIGHEST
MESH = pl.DeviceIdType.MESH
ANY = pl.BlockSpec(memory_space=pl.ANY)

VMEM_LIMIT = 56 * 1024 * 1024
LANES = 128
SUBLANES = 8

D_MODEL = 2048
DEPTH = 2
GRID_W = 64
CHUNK = 64
CONV_W = 5
RMS_EPS = 1e-6
NEG_INF = -1e30
ROPE_BASE = 10000.0
BRANCH_W = 512
GLA_H, GLA_DK, GLA_DV, GLA_LR, GLA_TAU = 4, 64, 128, 16, 16.0
NA_H, NA_D, NA_WIN_R, NA_WIN_C = 4, 128, 8, 16
GDN_H, GDN_D = 4, 128
M2_P, M2_H, M2_N, M2_G = 64, 8, 128, 2
D_FF = 5632
IN_COLS = 14912
N_SLOT = 4
ADAM_LR, ADAM_B1, ADAM_B2, ADAM_EPS, ADAM_WD, ADAM_STEP = 0.001, 0.9, 0.999, 1e-08, 0.01, 10


def _cparams(*sem):
    return pltpu.CompilerParams(dimension_semantics=sem if sem else None, vmem_limit_bytes=VMEM_LIMIT)


def _tile(n, target, mult):
    if n <= target:
        return n
    best = None
    for t in range(mult, target + 1, mult):
        if n % t == 0:
            best = t
    assert best is not None, (n, target, mult)
    return best


def _nt(a, b):
    return lax.dot_general(a.astype(BF16), b.astype(BF16), (((1,), (1,)), ((), ())), preferred_element_type=F32)


def _tn(a, b):
    return lax.dot_general(a.astype(BF16), b.astype(BF16), (((0,), (0,)), ((), ())), preferred_element_type=F32)


def _nn(a, b):
    return jnp.dot(a.astype(BF16), b.astype(BF16), preferred_element_type=F32)


def _dot3(a, b, dims):
    a_hi, b_hi = a.astype(BF16), b.astype(BF16)
    a_lo = (a - a_hi.astype(F32)).astype(BF16)
    b_lo = (b - b_hi.astype(F32)).astype(BF16)
    dot = lambda u, v: lax.dot_general(u, v, dims, preferred_element_type=F32)
    return dot(a_hi, b_hi) + (dot(a_hi, b_lo) + dot(a_lo, b_hi))


_NN, _NT, _TN = ((((1,), (0,)), ((), ())), (((1,), (1,)), ((), ())), (((0,), (0,)), ((), ())))


@jax.custom_vjp
def _nn_hi(a, b):
    return _dot3(a, b, _NN)


_nn_hi.defvjp(lambda a, b: (_dot3(a, b, _NN), (a, b)),
              lambda res, g: (_dot3(g, res[1], _NT), _dot3(res[0], g, _TN)))


def _w_spec(layout, prefix, r_idx, c_idx, br, bc, slot_dim):
    none = (None,) * len(prefix)
    if layout == "plain":
        return pl.BlockSpec(none + (br, bc), lambda i, j, k: prefix + (r_idx(i, j, k), c_idx(i, j, k)))
    if layout == "col":
        per = slot_dim // bc
        return pl.BlockSpec((None,) + none + (br, bc),
                            lambda i, j, k: (c_idx(i, j, k) // per,) + prefix + (r_idx(i, j, k), c_idx(i, j, k) % per))
    per = slot_dim // br
    return pl.BlockSpec((None,) + none + (br, bc),
                        lambda i, j, k: (r_idx(i, j, k) // per,) + prefix + (r_idx(i, j, k) % per, c_idx(i, j, k)))


def _mm(a, b, *, name, ta=False, tb=False, b_layout="plain", b_prefix=(), out_layout="plain", out_dtype=F32):
    m, kdim = (a.shape[1], a.shape[0]) if ta else a.shape
    rows, cols = b.shape[-2:]
    if b_layout == "col":
        cols *= N_SLOT
    elif b_layout == "row":
        rows *= N_SLOT
    n = rows if tb else cols
    assert (cols if tb else rows) == kdim, (a.shape, b.shape, ta, tb)
    n_unit = n // N_SLOT if (out_layout == "col" or (b_layout == ("row" if tb else "col"))) else n
    k_unit = kdim // N_SLOT if b_layout == ("col" if tb else "row") else kdim
    odd_n = n_unit % 1408 == 0 and n_unit % 512 != 0
    odd_k = k_unit % 1408 == 0 and k_unit % 512 != 0
    if ta:
        tm = _tile(m, 2048, LANES)
        tn = _tile(n_unit, 1408 if odd_n else 1024, LANES)
        tk = _tile(k_unit, 512, 2 * SUBLANES)
    elif tb:
        tm = _tile(m, 768 if odd_k else 1408, 2 * SUBLANES)
        tn = _tile(n_unit, 1408 if odd_n else 2048, LANES)
        tk = _tile(k_unit, 1408 if odd_k else 512, LANES)
    else:
        tm = _tile(m, 768, 2 * SUBLANES)
        tn = _tile(n_unit, 1408 if odd_n else 512, LANES)
        tk = _tile(k_unit, 1408 if odd_k else 2048, LANES)
    nk = kdim // tk
    a_spec = (pl.BlockSpec((tk, tm), lambda i, j, k: (k, i)) if ta else pl.BlockSpec((tm, tk), lambda i, j, k: (i, k)))
    slot_dim = b.shape[-1] if b_layout == "col" else b.shape[-2]
    if tb:
        b_spec = _w_spec(b_layout, tuple(b_prefix), lambda i, j, k: j, lambda i, j, k: k, tn, tk, slot_dim)
    else:
        b_spec = _w_spec(b_layout, tuple(b_prefix), lambda i, j, k: k, lambda i, j, k: j, tk, tn, slot_dim)
    if out_layout == "col":
        per = (n // N_SLOT) // tn
        out_shape = jax.ShapeDtypeStruct((N_SLOT, m, n // N_SLOT), out_dtype)
        out_spec = pl.BlockSpec((None, tm, tn), lambda i, j, k: (j // per, i, j % per))
    else:
        out_shape = jax.ShapeDtypeStruct((m, n), out_dtype)
        out_spec = pl.BlockSpec((tm, tn), lambda i, j, k: (i, j))
    dims = (((0 if ta else 1,), (1 if tb else 0,)), ((), ()))

    def product(a_ref, b_ref):
        return lax.dot_general(a_ref[...].astype(BF16), b_ref[...].astype(BF16), dims, preferred_element_type=F32)

    def body_once(a_ref, b_ref, o_ref):
        o_ref[...] = product(a_ref, b_ref).astype(o_ref.dtype)

    def body(a_ref, b_ref, o_ref, acc_ref):
        k = pl.program_id(2)

        @pl.when(k == 0)
        def _():
            acc_ref[...] = jnp.zeros_like(acc_ref)

        acc_ref[...] += product(a_ref, b_ref)

        @pl.when(k == nk - 1)
        def _():
            o_ref[...] = acc_ref[...].astype(o_ref.dtype)

    return pl.pallas_call(
        body_once if nk == 1 else body, name=name, grid=(m // tm, n // tn, nk), in_specs=[a_spec, b_spec],
        out_specs=out_spec, out_shape=out_shape, scratch_shapes=[] if nk == 1 else [pltpu.VMEM((tm, tn), F32)],
        compiler_params=_cparams("parallel", "parallel", "arbitrary"),
    )(a, b)


def linear(a, w, grad_slot, *, name, layout="plain", prefix=(), out_dtype=F32):
    @jax.custom_vjp
    def f(a, w, grad_slot):
        return _mm(a, w, name=name + "_fwd", b_layout=layout, b_prefix=prefix, out_dtype=out_dtype)

    def fwd(a, w, grad_slot):
        return f(a, w, grad_slot), (a, w)

    def bwd(res, g):
        a, w = res
        da = _mm(g, w, name=name + "_dgrad", tb=True, b_layout=layout, b_prefix=prefix, out_dtype=a.dtype)
        dw = _mm(a, g, name=name + "_wgrad", ta=True, out_layout="col" if layout == "col" else "plain")
        if layout == "row":
            dw = dw.reshape((N_SLOT, dw.shape[0] // N_SLOT, dw.shape[1]))
        return da, None, dw

    f.defvjp(fwd, bwd)
    return f(a, w, grad_slot)


def _rowwise_specs(rows, consts, params, tile, seg_tile):
    def row_spec(r):
        return pl.BlockSpec((tile, r.shape[1]), lambda i: (i, 0))

    def par_spec(p):
        if p.shape[0] == 2:
            return pl.BlockSpec((None,) + p.shape[1:], lambda i: (jnp.where(i >= seg_tile, 1, 0), 0, 0))
        return pl.BlockSpec((None,) + p.shape[1:], lambda i: (0, 0, 0))

    return [row_spec(r) for r in rows], [row_spec(r) for r in consts], [par_spec(p) for p in params]


def rowwise(f, rows, consts, params, *, out_widths, tile, n_lat, name, out_dtype=F32):
    rows, consts, params = tuple(rows), tuple(consts), tuple(params)
    n_rows = rows[0].shape[0]
    tile = math.gcd(math.gcd(n_rows, n_lat), tile)
    assert tile % SUBLANES == 0
    seg_tile = n_lat // tile
    grid = (n_rows // tile,)
    nr, nc, npar = len(rows), len(consts), len(params)
    r_specs, c_specs, p_specs = _rowwise_specs(rows, consts, params, tile, seg_tile)
    out_shape = tuple(jax.ShapeDtypeStruct((n_rows, w), out_dtype) for w in out_widths)
    out_specs = tuple(pl.BlockSpec((tile, w), lambda i: (i, 0)) for w in out_widths)
    n_out = len(out_widths)

    def fwd_call(rows, consts, params):
        def body(*refs):
            ins = [r[...].astype(F32) for r in refs[:nr + nc + npar]]
            outs = f(*ins)
            for o_ref, o in zip(refs[nr + nc + npar:], outs):
                o_ref[...] = o.astype(o_ref.dtype)

        return pl.pallas_call(body, name=name + "_fwd", grid=grid, in_specs=r_specs + c_specs + p_specs,
                              out_specs=out_specs, out_shape=out_shape,
                              compiler_params=_cparams("parallel"))(*rows, *consts, *params)

    def bwd_call(rows, consts, params, gouts):
        def body(*refs):
            i = pl.program_id(0)
            ins = [r[...].astype(F32) for r in refs[:nr + nc + npar]]
            gs = tuple(r[...].astype(F32) for r in refs[nr + nc + npar:nr + nc + npar + n_out])
            d_refs = refs[nr + nc + npar + n_out:]
            cvals = ins[nr:nr + nc]

            def g(*diff):
                return tuple(f(*diff[:nr], *cvals, *diff[nr:]))

            _, vjp = jax.vjp(g, *ins[:nr], *ins[nr + nc:])
            grads = vjp(gs)
            for d_ref, gr in zip(d_refs[:nr], grads[:nr]):
                d_ref[...] = gr.astype(d_ref.dtype)
            for p, d_ref, gr in zip(params, d_refs[nr:], grads[nr:]):
                first = (i == 0) | (i == seg_tile) if p.shape[0] == 2 else (i == 0)

                @pl.when(first)
                def _():
                    d_ref[...] = jnp.zeros_like(d_ref)

                d_ref[...] += gr

        d_shape = tuple(jax.ShapeDtypeStruct(r.shape, r.dtype) for r in rows) + tuple(
            jax.ShapeDtypeStruct(p.shape, F32) for p in params)
        g_specs = [pl.BlockSpec((tile, w), lambda i: (i, 0)) for w in out_widths]
        return pl.pallas_call(body, name=name + "_bwd", grid=grid,
                              in_specs=r_specs + c_specs + p_specs + g_specs,
                              out_specs=tuple(r_specs + p_specs), out_shape=d_shape,
                              compiler_params=_cparams("arbitrary"))(*rows, *consts, *params, *gouts)

    @jax.custom_vjp
    def op(rows, consts, params):
        return fwd_call(rows, consts, params)

    def op_fwd(rows, consts, params):
        return op(rows, consts, params), (rows, consts, params)

    def op_bwd(res, gouts):
        rows, consts, params = res
        d = bwd_call(rows, consts, params, tuple(gouts))
        return tuple(d[:nr]), tuple(None for _ in consts), tuple(d[nr:])

    op.defvjp(op_fwd, op_bwd)
    return op(rows, consts, params)


def _rms(x, width=None):
    w = x.shape[-1] if width is None else width
    return x * lax.rsqrt(jnp.sum(x * x, axis=-1, keepdims=True) * (1.0 / w) + RMS_EPS)


def _silu(x):
    return x * jax.nn.sigmoid(x)


def _f_modnorm(x, g, sc, sh):
    return ((_rms(x) * g) * (1.0 + sc) + sh,)


def _f_silu(x):
    return (_silu(x),)


def _f_gla_prep(lr, a2, ab):
    z = _nn(lr, a2) + ab
    return ((jnp.minimum(z, 0.0) - jnp.log(1.0 + jnp.exp(-jnp.abs(z)))) * (1.0 / GLA_TAU),)


def _f_headnorm_gate(o, g, ng):
    outs = []
    for h in range(BRANCH_W // LANES):
        lo = h * LANES
        oh = o[:, lo:lo + LANES] + o[:, BRANCH_W + lo:BRANCH_W + lo + LANES]
        outs.append(_rms(oh) * ng * _silu(g[:, lo:lo + LANES]))
    return (jnp.concatenate(outs, axis=-1),)


def _f_gdn_prep(x, alog, dtb):
    half = x.shape[1] // 2
    beta = jax.nn.sigmoid(x[:, :half])
    la = -jnp.exp(alog) * jax.nn.softplus(x[:, half:] + dtb)
    return beta, la


def _f_m2_prep(x, alog, dtb):
    dt = jax.nn.softplus(x + dtb)
    return dt, -jnp.exp(alog) * dt


def _f_m2_fin(o, z, xs, dskip, ng):
    w = z.shape[1]
    y = (o[:, :w] + o[:, w:] + dskip * xs) * _silu(z)
    return (_rms(y, BRANCH_W) * ng,)


def _f_merge(gate, z0, z1, z2, z3, bm):
    acc = None
    for i, z in enumerate((z0, z1, z2, z3)):
        lo = i * D_MODEL
        t = jax.nn.sigmoid(gate[:, lo:lo + D_MODEL] + bm[:, lo:lo + D_MODEL]) * z
        acc = t if acc is None else acc + t
    return (acc,)


def _f_resid(x, y, g):
    return (x + g * y,)


def _f_swiglu(u1, u3):
    return (_silu(u1) * u3,)


def _f_loss(x, tgt, g):
    e = _rms(x) * g - tgt
    per_row = 0.5 * jnp.sum(e * e, axis=-1, keepdims=True) * (1.0 / D_MODEL)
    return (jnp.broadcast_to(per_row * (1.0 / LANES), (x.shape[0], LANES)),)


_HALO = 8


def _conv_segments(n_lat, n_ctx):
    segs = [(0, _HALO, n_lat), (n_lat, n_lat + 3 * _HALO, n_ctx)]
    return segs, n_lat + n_ctx + 4 * _HALO


def _conv_stage(buf, src, n_lat, n_ctx):
    zeros = jnp.zeros((_HALO, LANES), F32)
    buf[0:_HALO, :] = zeros
    buf[_HALO:_HALO + n_lat, :] = src[0:n_lat, :]
    buf[n_lat + _HALO:n_lat + 2 * _HALO, :] = zeros
    buf[n_lat + 2 * _HALO:n_lat + 3 * _HALO, :] = zeros
    buf[n_lat + 3 * _HALO:n_lat + 3 * _HALO + n_ctx, :] = src[n_lat:n_lat + n_ctx, :]
    buf[n_lat + n_ctx + 3 * _HALO:n_lat + n_ctx + 4 * _HALO, :] = zeros


def conv_silu(x, w, b, *, n_lat, name):
    n_rows, n_ch = x.shape
    n_ctx = n_rows - n_lat
    segs, n_buf = _conv_segments(n_lat, n_ctx)
    grid = (n_ch // LANES,)
    col = lambda r: pl.BlockSpec((r, LANES), lambda j: (0, j))
    half = CONV_W // 2

    def tiles():
        for row0, off, length in segs:
            tr = _tile(length, 256, SUBLANES)
            for t0 in range(0, length, tr):
                yield row0 + t0, off + t0, tr

    def pre_act(buf, w_ref, b_ref, off, tr):
        acc = jnp.broadcast_to(b_ref[...], (tr, LANES))
        for j in range(CONV_W):
            acc = acc + w_ref[j:j + 1, :] * buf[off + j - half:off + j - half + tr, :]
        return acc

    def fwd_call(x, w, b):
        def body(x_ref, w_ref, b_ref, o_ref, buf):
            _conv_stage(buf, x_ref, n_lat, n_ctx)
            for row, off, tr in tiles():
                o_ref[row:row + tr, :] = _silu(pre_act(buf, w_ref, b_ref, off, tr))

        return pl.pallas_call(body, name=name + "_fwd", grid=grid, in_specs=[col(n_rows), col(CONV_W), col(1)],
                              out_specs=col(n_rows), out_shape=jax.ShapeDtypeStruct(x.shape, F32),
                              scratch_shapes=[pltpu.VMEM((n_buf, LANES), F32)],
                              compiler_params=_cparams("parallel"))(x, w, b)

    def bwd_call(x, w, b, g):
        def body(x_ref, w_ref, b_ref, g_ref, dx_ref, dw_ref, db_ref, xbuf, dbuf):
            _conv_stage(xbuf, x_ref, n_lat, n_ctx)
            _conv_stage(dbuf, g_ref, n_lat, n_ctx)
            dw = [jnp.zeros((1, LANES), F32) for _ in range(CONV_W)]
            db = jnp.zeros((1, LANES), F32)
            for row, off, tr in tiles():
                pre = pre_act(xbuf, w_ref, b_ref, off, tr)
                s = jax.nn.sigmoid(pre)
                dpre = g_ref[row:row + tr, :] * (s * (1.0 + pre * (1.0 - s)))
                dbuf[off:off + tr, :] = dpre
                db = db + jnp.sum(dpre, axis=0, keepdims=True)
                for j in range(CONV_W):
                    dw[j] = dw[j] + jnp.sum(dpre * xbuf[off + j - half:off + j - half + tr, :], axis=0, keepdims=True)
            for row, off, tr in tiles():
                acc = jnp.zeros((tr, LANES), F32)
                for j in range(CONV_W):
                    acc = acc + w_ref[j:j + 1, :] * dbuf[off - j + half:off - j + half + tr, :]
                dx_ref[row:row + tr, :] = acc
            for j in range(CONV_W):
                dw_ref[j:j + 1, :] = dw[j]
            db_ref[...] = db

        return pl.pallas_call(
            body, name=name + "_bwd", grid=grid, in_specs=[col(n_rows), col(CONV_W), col(1), col(n_rows)],
            out_specs=(col(n_rows), col(CONV_W), col(1)),
            out_shape=(jax.ShapeDtypeStruct(x.shape, F32), jax.ShapeDtypeStruct(w.shape, F32),
                       jax.ShapeDtypeStruct(b.shape, F32)),
            scratch_shapes=[pltpu.VMEM((n_buf, LANES), F32), pltpu.VMEM((n_buf, LANES), F32)],
            compiler_params=_cparams("parallel"))(x, w, b, g)

    @jax.custom_vjp
    def op(x, w, b):
        return fwd_call(x, w, b)

    op.defvjp(lambda x, w, b: (op(x, w, b), (x, w, b)), lambda res, g: bwd_call(*res, g))
    return op(x, w, b)


def chunk_scan(step, shared, shared_lanes, perdir, perdir_lanes, consts, *, heads, state_shape, out_w, n_lat, name,
               keep_shape=None):
    assert keep_shape is None or heads == 1
    shared, perdir, consts = tuple(shared), tuple(perdir), tuple(consts)
    n_rows = shared[0].shape[0]
    nl, ncx = n_lat // CHUNK, (n_rows - n_lat) // CHUNK
    n_chunks = nl + ncx
    ow_all = heads * out_w
    ns, npd, ncst = len(shared), len(perdir), len(consts)

    def cidx(d, n):
        m = n - ncx
        return jnp.where(n < ncx, nl + jnp.where(d == 0, n, ncx - 1 - n), jnp.where(d == 0, m, nl - 1 - m))

    def specs(order):
        sh = [pl.BlockSpec((CHUNK, a.shape[1]), lambda d, n: (cidx(d, order(n)), 0)) for a in shared]
        pd = [pl.BlockSpec((CHUNK, a.shape[1] // 2), lambda d, n: (cidx(d, order(n)), d)) for a in perdir]
        cs = [pl.BlockSpec((CHUNK, a.shape[1]), lambda d, n: (cidx(d, order(n)), 0)) for a in consts]
        o = pl.BlockSpec((CHUNK, ow_all), lambda d, n: (cidx(d, order(n)), d))
        st = pl.BlockSpec((None, None, heads) + state_shape, lambda d, n: (d, order(n), 0) + (0,) * len(state_shape))
        kp = [] if keep_shape is None else [pl.BlockSpec((None, None) + keep_shape,
                                                        lambda d, n: (d, order(n)) + (0,) * len(keep_shape))]
        return sh, pd, cs, o, st, kp

    def mask(d):
        r = lax.broadcasted_iota(jnp.int32, (CHUNK, CHUNK), 0)
        c = lax.broadcasted_iota(jnp.int32, (CHUNK, CHUNK), 1)
        lower = jnp.where(r >= c, 1.0, 0.0).astype(F32)
        upper = jnp.where(r <= c, 1.0, 0.0).astype(F32)
        return jnp.where(d == 0, lower, upper)

    def head_slices(h):
        out = []
        for lanes in tuple(shared_lanes) + tuple(perdir_lanes):
            out.append([slice(off + (h // hpg) * w, off + (h // hpg) * w + w) for off, w, hpg in lanes])
        return out

    def load(refs, h):
        return tuple(tuple(ref[:, s] for s in sl) for ref, sl in zip(refs, head_slices(h)))

    state_sds = jax.ShapeDtypeStruct((2, n_chunks, heads) + state_shape, F32)

    def fwd_call(shared, perdir, consts):
        sh, pd, cs, o_spec, st_spec, kp_spec = specs(lambda n: n)

        def body(*refs):
            in_refs = refs[:ns + npd]
            c_refs = refs[ns + npd:ns + npd + ncst]
            o_ref, ss_ref = refs[ns + npd + ncst:ns + npd + ncst + 2]
            s_scr = refs[-1]
            d, n = pl.program_id(0), pl.program_id(1)

            @pl.when(n == 0)
            def _():
                s_scr[...] = jnp.zeros_like(s_scr)

            m = mask(d)
            cv = tuple(c[...] for c in c_refs)
            ins = [load(in_refs, h) for h in range(heads)]
            s0 = [s_scr[h] for h in range(heads)]
            res = [step(ins[h], cv, s0[h], m, None) for h in range(heads)]
            for h in range(heads):
                o_ref[:, h * out_w:(h + 1) * out_w] = res[h][0]
                ss_ref[h] = s0[h]
                s_scr[h] = res[h][1]
            if keep_shape is not None:
                refs[-2][...] = res[0][2]

        keep_sds = [] if keep_shape is None else [jax.ShapeDtypeStruct((2, n_chunks) + keep_shape, F32)]
        return pl.pallas_call(
            body, name=name + "_fwd", grid=(2, n_chunks), in_specs=sh + pd + cs, out_specs=tuple([o_spec, st_spec] + kp_spec),
            out_shape=tuple([jax.ShapeDtypeStruct((n_rows, 2 * ow_all), F32), state_sds] + keep_sds),
            scratch_shapes=[pltpu.VMEM((heads,) + state_shape, F32)],
            compiler_params=_cparams("arbitrary", "arbitrary"))(*shared, *perdir, *consts)

    def bwd_call(shared, perdir, consts, starts, kept, g):
        sh, pd, cs, o_spec, st_spec, kp_spec = specs(lambda n: n_chunks - 1 - n)
        dsh = [pl.BlockSpec((CHUNK, a.shape[1]), lambda d, n: (cidx(d, n_chunks - 1 - n), d)) for a in shared]
        nk = len(kept)

        def body(*refs):
            in_refs = refs[:ns + npd]
            c_refs = refs[ns + npd:ns + npd + ncst]
            ss_ref, g_ref = refs[ns + npd + ncst:ns + npd + ncst + 2]
            kept_val = refs[ns + npd + ncst + 2][...] if nk else None
            d_refs = refs[ns + npd + ncst + 2 + nk:ns + npd + ncst + 2 + nk + ns + npd]
            ds_scr = refs[-1]
            d, n = pl.program_id(0), pl.program_id(1)

            @pl.when(n == 0)
            def _():
                ds_scr[...] = jnp.zeros_like(ds_scr)

            m = mask(d)
            cv = tuple(c[...] for c in c_refs)
            ins = [load(in_refs, h) for h in range(heads)]
            cots = [(g_ref[:, h * out_w:(h + 1) * out_w], ds_scr[h]) for h in range(heads)]
            starts = [ss_ref[h] for h in range(heads)]
            grads = []
            for h in range(heads):
                _, vjp = jax.vjp(lambda i_, s_: step(i_, cv, s_, m, kept_val)[:2], ins[h], starts[h])
                grads.append(vjp(cots[h]))
            for d_ref in d_refs:
                d_ref[...] = jnp.zeros_like(d_ref)
            for h in range(heads):
                g_ins, g_s = grads[h]
                for d_ref, sl, gr in zip(d_refs, head_slices(h), g_ins):
                    for s, gv in zip(sl, gr):
                        d_ref[:, s] += gv
                ds_scr[h] = g_s

        d_shape = tuple(jax.ShapeDtypeStruct((n_rows, 2 * a.shape[1]), F32) for a in shared) + tuple(
            jax.ShapeDtypeStruct(a.shape, F32) for a in perdir)
        return pl.pallas_call(
            body, name=name + "_bwd", grid=(2, n_chunks), in_specs=sh + pd + cs + [st_spec, o_spec] + kp_spec,
            out_specs=tuple(dsh + pd), out_shape=d_shape,
            scratch_shapes=[pltpu.VMEM((heads,) + state_shape, F32)],
            compiler_params=_cparams("arbitrary", "arbitrary"))(*shared, *perdir, *consts, starts, g, *kept)

    @jax.custom_vjp
    def op(shared, perdir, consts):
        return fwd_call(shared, perdir, consts)[0]

    def op_fwd(shared, perdir, consts):
        o, starts, *kept = fwd_call(shared, perdir, consts)
        return o, (shared, perdir, consts, starts, tuple(kept))

    def op_bwd(res, g):
        shared, perdir, consts, starts, kept = res
        d = bwd_call(shared, perdir, consts, starts, kept, g)
        d_sh = tuple(a[:, :a.shape[1] // 2] + a[:, a.shape[1] // 2:] for a in d[:ns])
        return d_sh, tuple(d[ns:]), tuple(None for _ in consts)

    op.defvjp(op_fwd, op_bwd)
    return op(shared, perdir, consts)


@jax.custom_vjp
def _swap_halves(x):
    return pltpu.roll(x, LANES // 2, 1)


_swap_halves.defvjp(lambda x: (_swap_halves(x), None), lambda _, g: (_swap_halves(g),))


def _gla_step(ins, consts, st, m, kept):
    (q, k), (v,), (la,) = ins
    cos, sin = consts
    q = (q * cos + _swap_halves(q) * sin) * (GLA_DK ** -0.5)
    k = k * cos + _swap_halves(k) * sin
    b = _nn_hi(m, la)
    bl = jnp.sum(la, axis=0, keepdims=True)
    qi = q * jnp.exp(b)
    ki = k * jnp.exp(-b)
    att = _nt(qi, ki) * m
    o = _nt(qi, st) + _nn(att, v)
    st_new = st * jnp.exp(bl) + _tn(v, k * jnp.exp(bl - b))
    return o, st_new, None


def _l2n(x):
    return x * lax.rsqrt(jnp.sum(x * x, axis=-1, keepdims=True) + RMS_EPS)


def _tri_inv_fwd(nmat):
    r = lax.broadcasted_iota(jnp.int32, nmat.shape, 0)
    c = lax.broadcasted_iota(jnp.int32, nmat.shape, 1)
    inv = jnp.where(r == c, 1.0, 0.0).astype(F32) - nmat
    p = nmat
    for _ in range(5):
        p = _nn_hi(p, p)
        inv = inv + _nn_hi(inv, p)
    return inv


@jax.custom_vjp
def _unit_tri_inv(nmat):
    return _tri_inv_fwd(nmat)


def _unit_tri_inv_bwd(inv, g):
    return (-_dot3(_dot3(inv, g, _TN), inv, _NT),)


_unit_tri_inv.defvjp(lambda nmat: (lambda inv: (inv, inv))(_tri_inv_fwd(nmat)), _unit_tri_inv_bwd)


@jax.custom_vjp
def _kept_tri_inv(nmat, inv):
    return inv


_kept_tri_inv.defvjp(lambda nmat, inv: (inv, inv), lambda inv, g: _unit_tri_inv_bwd(inv, g) + (jnp.zeros_like(inv),))


def _lane_col(x, h):
    lane = lax.broadcasted_iota(jnp.int32, x.shape, 1)
    return jnp.sum(jnp.where(lane == h, x, 0.0), axis=1, keepdims=True)


def _masked_exp(diff, mask):
    return jnp.where(mask > 0, jnp.exp(jnp.where(mask > 0, diff, 0.0)), 0.0)


def _gdn_step(ins, consts, s, m, kept):
    (q, k, v), (beta,), (la,) = ins
    n = GDN_H * CHUNK
    hs = range(GDN_H)
    blk = lambda x, h: x[:, h * GDN_D:(h + 1) * GDN_D]
    rows = lambda x, h: x[h * CHUNK:(h + 1) * CHUNK]
    qh = [_l2n(blk(q, h)) * (GDN_D ** -0.5) for h in hs]
    kh = [_l2n(blk(k, h)) for h in hs]
    k_st = jnp.concatenate(kh, axis=0)
    q_st = jnp.concatenate(qh, axis=0)
    v_st = jnp.concatenate([blk(v, h) for h in hs], axis=0)
    beta_st = jnp.concatenate([_lane_col(beta, h) for h in hs], axis=0)
    la_cols = [_lane_col(la, h) for h in hs]
    la_st = jnp.concatenate([jnp.broadcast_to(c, (CHUNK, GDN_D)) for c in la_cols], axis=0)
    r = lax.broadcasted_iota(jnp.int32, (n, n), 0)
    c = lax.broadcasted_iota(jnp.int32, (n, n), 1)
    e = jnp.where(lax.broadcasted_iota(jnp.int32, (n, CHUNK), 0) % CHUNK == lax.broadcasted_iota(jnp.int32, (n, CHUNK), 1),
                  1.0, 0.0).astype(F32)
    m_bd = jnp.where(r // CHUNK == c // CHUNK, _nt(_nn(e, m), e), 0.0)
    eye = jnp.where(r == c, 1.0, 0.0).astype(F32)
    b_st = _nn_hi(m_bd, la_st)
    b_t = b_st.T
    diff = jnp.concatenate([b_st, b_st], axis=1) - jnp.concatenate([b_t, b_t], axis=0)
    incl = _masked_exp(diff, m_bd)
    strict = _masked_exp(diff, m_bd - eye)
    nmat = beta_st * _nt(k_st, k_st) * strict
    inv = _unit_tri_inv(nmat) if kept is None else _kept_tri_inv(nmat, kept)
    wu = _nn_hi(inv, jnp.concatenate([k_st * (beta_st * jnp.exp(b_st)), v_st * beta_st], axis=-1))
    w, u0 = wu[:, :GDN_D], wu[:, GDN_D:]
    us, s_new, qs = [], [], []
    for h in hs:
        s_h = s[h * GDN_D:(h + 1) * GDN_D]
        bl = jnp.sum(jnp.broadcast_to(la_cols[h], (CHUNK, GDN_D)), axis=0, keepdims=True)
        u_h = rows(u0, h) - _nn(rows(w, h), s_h)
        s_new.append(jnp.exp(bl) * s_h + _tn(kh[h] * jnp.exp(bl - rows(b_st, h)), u_h))
        us.append(u_h)
        qs.append(_nn(qh[h], s_h))
    o_st = jnp.exp(b_st) * jnp.concatenate(qs, axis=0) + _nn(_nt(q_st, k_st) * incl, jnp.concatenate(us, axis=0))
    return jnp.concatenate([rows(o_st, h) for h in hs], axis=1), jnp.concatenate(s_new, axis=0), inv


def _ssd_step(ins, consts, s, m, kept):
    (xs,), (bc,), (dt,), (la,) = ins
    hpg = M2_H // M2_G
    b_all = _nn_hi(m, la)
    bl_all = jnp.sum(la, axis=0, keepdims=True)
    b_t = b_all.T
    row_id = lax.broadcasted_iota(jnp.int32, b_t.shape, 0)
    bm = [bc[:, g * M2_N:(g + 1) * M2_N] for g in range(M2_G)]
    cm = [bc[:, (M2_G + g) * M2_N:(M2_G + g + 1) * M2_N] for g in range(M2_G)]
    scores = [_nt(cm[g], bm[g]) for g in range(M2_G)]
    outs, s_new = [], []
    for h in range(M2_H):
        g = h // hpg
        s_h = s[h * M2_N:(h + 1) * M2_N]
        b_col = _lane_col(b_all, h)
        bl = _lane_col(bl_all, h)
        b_row = jnp.sum(jnp.where(row_id == h, b_t, 0.0), axis=0, keepdims=True)
        xv = xs[:, h * LANES:(h + 1) * LANES] * _lane_col(dt, h)
        outs.append(jnp.exp(b_col) * _nn(cm[g], s_h) + _nn(scores[g] * _masked_exp(b_col - b_row, m), xv))
        s_new.append(jnp.exp(bl) * s_h + _tn(bm[g] * jnp.exp(bl - b_col), xv))
    return jnp.concatenate(outs, axis=1), jnp.concatenate(s_new, axis=0), None


def _na_tile(q, kw, vw, kc, vc, bias):
    qs = q * (NA_D ** -0.5)
    s1 = _nt(qs, kw) + bias
    s2 = _nt(qs, kc)
    mx = lax.stop_gradient(jnp.maximum(jnp.max(s1, axis=-1, keepdims=True), jnp.max(s2, axis=-1, keepdims=True)))
    p1 = jnp.exp(s1 - mx)
    p2 = jnp.exp(s2 - mx)
    den = jnp.sum(p1, axis=-1, keepdims=True) + jnp.sum(p2, axis=-1, keepdims=True)
    return (_nn(p1, vw) + _nn(p2, vc)) / den


def _ctx_tile(q, k, v):
    s = _nt(q * (NA_D ** -0.5), k)
    p = jnp.exp(s - lax.stop_gradient(jnp.max(s, axis=-1, keepdims=True)))
    return _nn(p, v) / jnp.sum(p, axis=-1, keepdims=True)


def natten(q, k, v, bias, *, n_lat, name):
    n_rows = q.shape[0]
    n_ctx = n_rows - n_lat
    g_rows = n_lat // GRID_W
    win = NA_WIN_R * GRID_W
    ctx_blk = n_lat // n_ctx

    def start(n):
        return jnp.clip(n - NA_WIN_R // 2, 0, g_rows - NA_WIN_R)

    def case(n):
        return n - start(n)

    q_spec = pl.BlockSpec((GRID_W, LANES), lambda h, n: (n, h))
    lat_spec = pl.BlockSpec((n_lat, LANES), lambda h, n: (0, h))
    ctx_in = pl.BlockSpec((n_ctx, LANES), lambda h, n: (ctx_blk, h))
    ctx_out = pl.BlockSpec((n_ctx, LANES), lambda h, n: (0, h))
    bias_spec = pl.BlockSpec((None, None, GRID_W, win), lambda h, n: (h, case(n), 0, 0))
    lat_sds = jax.ShapeDtypeStruct((n_lat, BRANCH_W), F32)
    ctx_sds = jax.ShapeDtypeStruct((n_ctx, BRANCH_W), F32)

    def lat_fwd(q, k, v, bias):
        def body(q_ref, k_ref, v_ref, kc_ref, vc_ref, b_ref, o_ref):
            r0 = pl.multiple_of(start(pl.program_id(1)) * GRID_W, GRID_W)
            o_ref[...] = _na_tile(q_ref[...], k_ref[pl.ds(r0, win), :], v_ref[pl.ds(r0, win), :],
                                  kc_ref[...], vc_ref[...], b_ref[...])

        return pl.pallas_call(body, name=name + "_lat_fwd", grid=(NA_H, g_rows),
                              in_specs=[q_spec, lat_spec, lat_spec, ctx_in, ctx_in, bias_spec], out_specs=q_spec,
                              out_shape=lat_sds, compiler_params=_cparams("parallel", "arbitrary"))(q, k, v, k, v, bias)

    def lat_bwd(q, k, v, bias, g):
        def body(q_ref, k_ref, v_ref, kc_ref, vc_ref, b_ref, g_ref, dq_ref, dk_ref, dv_ref, dkc_ref, dvc_ref, db_ref):
            n = pl.program_id(1)
            r0 = pl.multiple_of(start(n) * GRID_W, GRID_W)

            @pl.when(n == 0)
            def _():
                for r in (dk_ref, dv_ref, dkc_ref, dvc_ref):
                    r[...] = jnp.zeros_like(r)

            @pl.when((n == 0) | (case(n) != case(jnp.maximum(n - 1, 0))))
            def _():
                db_ref[...] = jnp.zeros_like(db_ref)

            _, vjp = jax.vjp(_na_tile, q_ref[...], k_ref[pl.ds(r0, win), :], v_ref[pl.ds(r0, win), :],
                             kc_ref[...], vc_ref[...], b_ref[...])
            dq, dkw, dvw, dkc, dvc, db = vjp(g_ref[...])
            dq_ref[...] = dq
            dk_ref[pl.ds(r0, win), :] += dkw
            dv_ref[pl.ds(r0, win), :] += dvw
            dkc_ref[...] += dkc
            dvc_ref[...] += dvc
            db_ref[...] += db

        return pl.pallas_call(
            body, name=name + "_lat_bwd", grid=(NA_H, g_rows),
            in_specs=[q_spec, lat_spec, lat_spec, ctx_in, ctx_in, bias_spec, q_spec],
            out_specs=(q_spec, lat_spec, lat_spec, ctx_out, ctx_out, bias_spec),
            out_shape=(lat_sds, lat_sds, lat_sds, ctx_sds, ctx_sds, jax.ShapeDtypeStruct(bias.shape, F32)),
            compiler_params=_cparams("parallel", "arbitrary"))(q, k, v, k, v, bias, g)

    c_in = pl.BlockSpec((n_ctx, LANES), lambda h: (ctx_blk, h))
    c_out = pl.BlockSpec((n_ctx, LANES), lambda h: (0, h))

    def ctx_fwd(q, k, v):
        def body(q_ref, k_ref, v_ref, o_ref):
            o_ref[...] = _ctx_tile(q_ref[...], k_ref[...], v_ref[...])

        return pl.pallas_call(body, name=name + "_ctx_fwd", grid=(NA_H,), in_specs=[c_in, c_in, c_in], out_specs=c_out,
                              out_shape=ctx_sds, compiler_params=_cparams("parallel"))(q, k, v)

    def ctx_bwd(q, k, v, g):
        def body(q_ref, k_ref, v_ref, g_ref, dq_ref, dk_ref, dv_ref):
            _, vjp = jax.vjp(_ctx_tile, q_ref[...], k_ref[...], v_ref[...])
            dq_ref[...], dk_ref[...], dv_ref[...] = vjp(g_ref[...])

        return pl.pallas_call(body, name=name + "_ctx_bwd", grid=(NA_H,), in_specs=[c_in, c_in, c_in, c_out],
                              out_specs=(c_out, c_out, c_out), out_shape=(ctx_sds, ctx_sds, ctx_sds),
                              compiler_params=_cparams("parallel"))(q, k, v, g)

    @jax.custom_vjp
    def op(q, k, v, bias):
        return jnp.concatenate([lat_fwd(q, k, v, bias), ctx_fwd(q, k, v)], axis=0)

    def op_bwd(res, g):
        q, k, v, bias = res
        dq, dk, dv, dkc, dvc, db = lat_bwd(q, k, v, bias, g[:n_lat])
        dqc, dkc2, dvc2 = ctx_bwd(q, k, v, g[n_lat:])
        return (jnp.concatenate([dq, dqc], axis=0), jnp.concatenate([dk, dkc + dkc2], axis=0),
                jnp.concatenate([dv, dvc + dvc2], axis=0), db)

    op.defvjp(lambda q, k, v, bias: (op(q, k, v, bias), (q, k, v, bias)), op_bwd)
    return op(q, k, v, bias)


def _runs(src):
    src = np.asarray(src)
    out, i = [], 0
    while i < len(src):
        j = i + 1
        if src[i] < 0:
            while j < len(src) and src[j] < 0:
                j += 1
            out.append((-1, j - i))
        else:
            while j < len(src) and src[j] == src[j - 1] + 1:
                j += 1
            out.append((int(src[i]), j - i))
        i = j
    return out


def _take_cols(w, src):
    pieces = [jnp.zeros(w.shape[:-1] + (ln,), w.dtype) if s < 0 else w[..., s:s + ln] for s, ln in _runs(src)]
    return pieces[0] if len(pieces) == 1 else jnp.concatenate(pieces, axis=-1)


def _take_cols_slabs(slabs, src):
    width = slabs[0].shape[-1]
    pieces = []
    for s, ln in _runs(src):
        if s < 0:
            pieces.append(jnp.zeros(slabs[0].shape[:-1] + (ln,), slabs[0].dtype))
        while s >= 0 and ln > 0:
            off = s % width
            take = min(ln, width - off)
            pieces.append(slabs[s // width][..., off:off + take])
            s, ln = s + take, ln - take
    return pieces[0] if len(pieces) == 1 else jnp.concatenate(pieces, axis=-1)


def _untake_cols(parts, n_cols, n_slabs=1):
    found = []
    for arr, src in parts:
        pos = 0
        for s, ln in _runs(src):
            if s >= 0:
                found.append((s, ln, arr, pos))
            pos += ln
    found.sort(key=lambda t: t[0])
    width = n_cols // n_slabs
    slabs, at = [[] for _ in range(n_slabs)], 0
    for s, ln, arr, pos in found:
        assert s == at, (s, at)
        at += ln
        while ln > 0:
            take = min(ln, width - s % width)
            slabs[s // width].append(arr[..., pos:pos + take])
            s, pos, ln = s + take, pos + take, ln - take
    assert at == n_cols, (at, n_cols)
    return [jnp.concatenate(p, axis=-1) for p in slabs]


def _pad_heads(base, heads, real, width):
    return np.concatenate([np.concatenate([base + h * real + np.arange(real), -np.ones(width - real, np.int64)])
                           for h in range(heads)])


def _rope_heads(base, heads):
    z = -np.ones(32, np.int64)
    return np.concatenate([np.concatenate([base + h * 64 + np.arange(32), z, base + h * 64 + 32 + np.arange(32), z])
                           for h in range(heads)])


def _lane_block(base, n):
    return np.concatenate([base + np.arange(n), -np.ones(LANES - n, np.int64)])


def _in_groups():
    g0, n0, d0, m0, t0 = 0, 1568, 3104, 5168, 6720
    rng = lambda a, n: a + np.arange(n)
    return [
        ("gla_qk", np.concatenate([_rope_heads(g0, GLA_H), _rope_heads(g0 + 256, GLA_H)])),
        ("gla_v", rng(g0 + 512, 512)),
        ("gla_g", rng(g0 + 1024, 512)),
        ("gla_lr", _lane_block(g0 + 1536, 2 * GLA_LR)),
        ("na_q", rng(n0, 512)), ("na_k", rng(n0 + 512, 512)), ("na_v", rng(n0 + 1024, 512)),
        ("gdn_qkv", rng(d0, 1536)),
        ("gdn_z", rng(d0 + 1536, 512)),
        ("gdn_sm", np.concatenate([_lane_block(d0 + 2048 + 4 * i, GDN_H) for i in range(4)])),
        ("m2_z", _pad_heads(m0, M2_H, M2_P, LANES)),
        ("m2_xs", _pad_heads(m0 + 512, M2_H, M2_P, LANES)),
        ("m2_bc", rng(m0 + 1024, 512)),
        ("m2_dt", np.concatenate([_lane_block(m0 + 1536, M2_H), _lane_block(m0 + 1536 + M2_H, M2_H)])),
        ("gate", rng(t0, 4 * D_MODEL)),
    ]


_M2_PAD = _pad_heads(0, M2_H, M2_P, LANES)
_GLA_PAD = _rope_heads(0, GLA_H)


def _row3(v):
    return v.reshape((1, 1, -1))


def _dir_rows(p, n):
    return _row3(jnp.concatenate([_take_cols(p[d][None], _lane_block(0, n)) for d in range(2)], axis=-1))


def _rope_tables(n_lat, n_ctx):
    n_freq = GLA_DK // 4
    freqs = ROPE_BASE ** (-jnp.arange(n_freq, dtype=F32) / n_freq)
    t = jnp.arange(n_lat)
    row = (t // GRID_W).astype(F32)
    colv = (t % GRID_W).astype(F32)
    ang = jnp.concatenate([row[:, None] * freqs, colv[:, None] * freqs], axis=-1)
    c, s = jnp.cos(ang), jnp.sin(ang)
    one, zero = jnp.ones_like(c), jnp.zeros_like(c)
    cos_t = jnp.concatenate([c, one, c, one], axis=-1)
    sin_t = jnp.concatenate([-s, zero, s, zero], axis=-1)
    return (jnp.concatenate([cos_t, jnp.ones((n_ctx, LANES), F32)], axis=0),
            jnp.concatenate([sin_t, jnp.zeros((n_ctx, LANES), F32)], axis=0))


def _na_bias(rpb):
    case = np.arange(NA_WIN_R)
    r = np.arange(NA_WIN_R)
    dr = r[None, :] - case[:, None] + NA_WIN_R - 1
    ci = np.arange(GRID_W)
    dc = np.clip(ci[None, :] - ci[:, None], 1 - NA_WIN_C, NA_WIN_C - 1) + NA_WIN_C - 1
    c0 = np.clip(ci - NA_WIN_C // 2, 0, GRID_W - NA_WIN_C)
    ok = (ci[None, :] >= c0[:, None]) & (ci[None, :] < c0[:, None] + NA_WIN_C)
    pick_r = np.zeros((NA_WIN_R, NA_WIN_R, 2 * NA_WIN_R - 1), np.float32)
    pick_r[case[:, None], r[None, :], dr] = 1.0
    pick_c = np.zeros((2 * NA_WIN_C - 1, GRID_W, GRID_W), np.float32)
    pick_c[dc, ci[:, None], ci[None, :]] = 1.0
    rows = jnp.einsum("hdk,crd->hcrk", rpb, pick_r, precision=HI)
    tbl = jnp.einsum("hcrk,kij->hcirj", rows, pick_c, precision=HI)
    tbl = jnp.where(ok[None, None, :, None, :], tbl, NEG_INF)
    return tbl.reshape((NA_H, NA_WIN_R, GRID_W, NA_WIN_R * GRID_W))


def _layer(l, xs, mod, small, slots, gath, win, tables, *, n_lat):
    rw = functools.partial(rowwise, n_lat=n_lat)
    rwb = functools.partial(rowwise, n_lat=n_lat, out_dtype=BF16)
    nm = lambda s: "l%d_%s" % (l, s)
    sl = slots[l]
    sh1, sc1, g1, sh2, sc2, g2 = [mod[:, i * D_MODEL:(i + 1) * D_MODEL].reshape((2, 1, D_MODEL)) for i in range(6)]
    (h,) = rwb(_f_modnorm, [xs], [], [_row3(small["norm1_g"][l]), sc1, sh1], out_widths=[D_MODEL], tile=256,
              name=nm("norm1"))
    p = {g: linear(h, win[l][g], sl["in_" + g], name=nm("in_" + g), out_dtype=BF16 if g == "gate" else F32)
         for g, _ in _in_groups()}

    a2 = small["gla_a2"][l]
    a2p = jnp.concatenate([
        jnp.concatenate([_take_cols(a2[0], _GLA_PAD), jnp.zeros((GLA_LR, 512), F32)], axis=1),
        jnp.concatenate([jnp.zeros((GLA_LR, 512), F32), _take_cols(a2[1], _GLA_PAD)], axis=1),
        jnp.zeros((LANES - 2 * GLA_LR, 1024), F32)], axis=0)[None]
    abp = _row3(jnp.concatenate([_take_cols(small["gla_ab"][l][d][None], _GLA_PAD) for d in range(2)], axis=-1))
    (la,) = rw(_f_gla_prep, [p["gla_lr"]], [], [a2p, abp], out_widths=[1024], tile=256, name=nm("gla_prep"))
    head = lambda off: (off, LANES, 1)
    o = chunk_scan(_gla_step, [p["gla_qk"], p["gla_v"]], [[head(0), head(512)], [head(0)]],
                   [la], [[head(0)]], tables, heads=GLA_H, state_shape=(GLA_DV, LANES), out_w=GLA_DV, n_lat=n_lat,
                   name=nm("gla_scan"))
    (ya,) = rwb(_f_headnorm_gate, [o, p["gla_g"]], [], [_row3(small["gla_norm_g"][l])], out_widths=[BRANCH_W], tile=256,
               name=nm("gla_fin"))

    yb = natten(p["na_q"], p["na_k"], p["na_v"], _na_bias(small["na_rpb"][l]), n_lat=n_lat, name=nm("na"))

    cq = conv_silu(p["gdn_qkv"], small["gdn_conv"][l], jnp.zeros((1, 3 * BRANCH_W), F32), n_lat=n_lat, name=nm("gdn_conv"))
    beta, la = rw(_f_gdn_prep, [p["gdn_sm"]], [], [_dir_rows(small["gdn_a_log"][l], GDN_H), _dir_rows(small["gdn_dt_bias"][l], GDN_H)],
                  out_widths=[256, 256], tile=256, name=nm("gdn_prep"))
    whole = lambda off, width: (off, width, 1)
    o = chunk_scan(_gdn_step, [cq], [[whole(0, 512), whole(512, 512), whole(1024, 512)]], [beta, la],
                   [[whole(0, LANES)], [whole(0, LANES)]], [], heads=1, state_shape=(GDN_H * GDN_D, GDN_D),
                   out_w=BRANCH_W, n_lat=n_lat, name=nm("gdn_scan"), keep_shape=(GDN_H * CHUNK, GDN_H * CHUNK))
    (yc,) = rwb(_f_headnorm_gate, [o, p["gdn_z"]], [], [_row3(small["gdn_norm_g"][l])], out_widths=[BRANCH_W], tile=256,
               name=nm("gdn_fin"))

    cw, cb = small["m2_conv"][l], small["m2_conv_b"][l][None]
    cxs = conv_silu(p["m2_xs"], _take_cols(cw[:, :512], _M2_PAD), _take_cols(cb[:, :512], _M2_PAD), n_lat=n_lat,
                    name=nm("m2_conv_x"))
    cbc = conv_silu(p["m2_bc"], cw[:, 512:], cb[:, 512:], n_lat=n_lat, name=nm("m2_conv_bc"))
    dt, la = rw(_f_m2_prep, [p["m2_dt"]], [], [_dir_rows(small["m2_a_log"][l], M2_H), _dir_rows(small["m2_dt_bias"][l], M2_H)],
                out_widths=[256, 256], tile=256, name=nm("m2_prep"))
    o = chunk_scan(_ssd_step, [cxs, cbc], [[whole(0, 2 * BRANCH_W)], [whole(0, BRANCH_W)]], [dt, la],
                   [[whole(0, LANES)], [whole(0, LANES)]], [], heads=1, state_shape=(M2_H * M2_N, LANES),
                   out_w=2 * BRANCH_W, n_lat=n_lat, name=nm("m2_scan"))
    dskip = _row3(jnp.repeat(small["m2_d"][l], LANES))
    (yd,) = rwb(_f_m2_fin, [o, p["m2_z"], cxs], [], [dskip, _row3(_take_cols(small["m2_norm_g"][l][None], _M2_PAD))],
               out_widths=[2 * BRANCH_W], tile=128, name=nm("m2_fin"))

    wb = gath["w_branch"]
    zs = [linear(y, wb, sl["w_branch%d" % i], name=nm("branch%d" % i), layout="col", prefix=(l, i), out_dtype=BF16)
          for i, y in enumerate((ya, yb, yc))]
    wb3 = wb[:, l, 3].reshape((N_SLOT, M2_H, M2_P, BRANCH_W))
    wb3 = jnp.pad(wb3, ((0, 0), (0, 0), (0, LANES - M2_P), (0, 0))).reshape((N_SLOT, 2 * BRANCH_W, BRANCH_W))
    zs.append(linear(yd, wb3, sl["w_branch3"], name=nm("branch3"), layout="col", out_dtype=BF16))
    (merged,) = rwb(_f_merge, [p["gate"]] + zs, [], [_row3(small["b_merge"][l].reshape(-1))], out_widths=[D_MODEL], tile=64,
                   name=nm("merge"))
    y = linear(merged, gath["w_out"], sl["w_out"], name=nm("out"), layout="row", prefix=(l,))
    (x1,) = rw(_f_resid, [xs, y], [], [g1], out_widths=[D_MODEL], tile=256, name=nm("res1"))

    (h2,) = rwb(_f_modnorm, [x1], [], [_row3(small["norm2_g"][l]), sc2, sh2], out_widths=[D_MODEL], tile=256,
               name=nm("norm2"))
    u1 = linear(h2, gath["w_ffn1"], sl["w_ffn1"], name=nm("ffn1"), layout="col", prefix=(l,), out_dtype=BF16)
    u3 = linear(h2, gath["w_ffn3"], sl["w_ffn3"], name=nm("ffn3"), layout="col", prefix=(l,), out_dtype=BF16)
    (act,) = rwb(_f_swiglu, [u1, u3], [], [], out_widths=[D_FF], tile=128, name=nm("swiglu"))
    f = linear(act, gath["w_ffn2"], sl["w_ffn2"], name=nm("ffn2"), layout="row", prefix=(l,))
    (x2,) = rw(_f_resid, [x1, f], [], [g2], out_widths=[D_MODEL], tile=256, name=nm("res2"))
    return x2


def _slot_shapes():
    s = {"w_out": (N_SLOT, D_MODEL // N_SLOT, D_MODEL),
         "w_ffn1": (N_SLOT, D_MODEL, D_FF // N_SLOT), "w_ffn3": (N_SLOT, D_MODEL, D_FF // N_SLOT),
         "w_ffn2": (N_SLOT, D_FF // N_SLOT, D_MODEL), "w_branch3": (N_SLOT, 2 * BRANCH_W, BRANCH_W)}
    for i in range(3):
        s["w_branch%d" % i] = (N_SLOT, BRANCH_W, BRANCH_W)
    for g, src in _in_groups():
        s["in_" + g] = (D_MODEL, len(src))
    return s


ADA_ROWS = 2 * SUBLANES


def ada_shard(c_all, c_ctx, w_ada, slots):
    cc = jnp.concatenate([c_all, c_ctx[None], jnp.zeros((ADA_ROWS - c_all.shape[0] - 1, D_MODEL), F32)], axis=0)
    (act,) = rowwise(_f_silu, [cc], [], [], out_widths=[D_MODEL], tile=ADA_ROWS, n_lat=ADA_ROWS, name="ada_silu",
                     out_dtype=BF16)
    return [linear(act, w_ada, slots[l], name="l%d_ada" % l, prefix=(l,)) for l in range(DEPTH)]


def _local_loss(diff, fixed, *, n_lat):
    small = diff["small"]
    n_ctx = fixed["ctx"].shape[0]
    xs = jnp.concatenate([diff["x"], fixed["ctx"]], axis=0)
    tables = _rope_tables(n_lat, n_ctx)
    for l in range(DEPTH):
        xs = _layer(l, xs, diff["mod"][l], small, diff["slots"], fixed["gath"], fixed["win"], tables, n_lat=n_lat)
    (lrow,) = rowwise(_f_loss, [xs[:n_lat]], [fixed["target"]], [_row3(small["final_norm_g"])], out_widths=[LANES],
                      tile=256, n_lat=n_lat, name="loss")
    return jnp.sum(lrow)


def _place():
    x, y, c = lax.axis_index("x"), lax.axis_index("y"), lax.axis_index("c")
    chips = [(1 - x, y), (x, 1 - y), (1 - x, 1 - y)]
    return x, y, c, (x, y, 1 - c), chips


def _remote(src, dst, send_sem, recv_sem, dev):
    return pltpu.make_async_remote_copy(src_ref=src, dst_ref=dst, send_sem=send_sem, recv_sem=recv_sem,
                                        device_id=dev, device_id_type=MESH)


def _dma_sems(*shape):
    return pltpu.SemaphoreType.DMA(shape)


def place_shard(w, slot, *, name):
    depth, k, n = w.shape
    tr = _tile(k, max(2 * SUBLANES, (1 << 19) // n // (2 * SUBLANES) * (2 * SUBLANES)), 2 * SUBLANES)

    def body(s_ref, w_ref, o_ref):
        o_ref[...] = w_ref[...].astype(o_ref.dtype)

    return pl.pallas_call(
        body, name=name,
        grid_spec=pltpu.PrefetchScalarGridSpec(
            num_scalar_prefetch=1, grid=(depth, k // tr),
            in_specs=[pl.BlockSpec((None, tr, n), lambda l, i, s: (l, i, 0))],
            out_specs=pl.BlockSpec((None, None, tr, n), lambda l, i, s: (s[0], l, i, 0))),
        out_shape=jax.ShapeDtypeStruct((N_SLOT, depth, k, n), BF16),
        compiler_params=_cparams("parallel", "parallel"))(slot, w)


def gather_weights(bufs):
    n = len(bufs)

    def body(*refs):
        o = refs[n:2 * n]
        send1, recv1, send2, recv2 = refs[2 * n:]
        x, y, c, sibling, chips = _place()
        g = 2 * x + y
        sent = []
        for k in range(n):
            for j, (cx, cy) in enumerate(chips):
                cp = _remote(o[k].at[g, c], o[k].at[g, c], send1.at[k, j], recv1.at[k, j], (cx, cy, c))
                cp.start()
                sent.append(cp)
        for k in range(n):
            for j, (cx, cy) in enumerate(chips):
                gj = 2 * cx + cy
                _remote(o[k].at[g, c], o[k].at[gj, c], send1.at[k, j], recv1.at[k, j], (cx, cy, c)).wait_recv()
                cp = _remote(o[k].at[gj, c], o[k].at[gj, c], send2.at[k, j], recv2.at[k, j], sibling)
                cp.start()
                sent.append(cp)
        for k in range(n):
            for j, (cx, cy) in enumerate(chips):
                gj = 2 * cx + cy
                _remote(o[k].at[gj, 1 - c], o[k].at[gj, 1 - c], send2.at[k, j], recv2.at[k, j], sibling).wait_recv()
        for cp in sent:
            cp.wait_send()

    return pl.pallas_call(
        body, name="gather_weights", in_specs=[ANY] * n, out_specs=[ANY] * n,
        out_shape=[jax.ShapeDtypeStruct(b.shape, b.dtype) for b in bufs],
        input_output_aliases={k: k for k in range(n)},
        scratch_shapes=[_dma_sems(n, 3), _dma_sems(n, 3), _dma_sems(n, 3), _dma_sems(n, 3)],
    )(*bufs)


def allgather_small(buf, *, name):
    m_per = buf.shape[0]

    def body(x_ref, out_ref, send_sems, recv_sems, local_sem):
        x, y, c, sibling, chips = _place()
        me = (x, y, c)

        def rows(px, py, pc):
            return out_ref.at[pl.ds((4 * px + 2 * py + pc) * m_per, m_per), :]

        def copy(k, block, to, src=None):
            return _remote(rows(*block) if src is None else src, rows(*block), send_sems.at[k], recv_sems.at[k], to)

        mine = pltpu.make_async_copy(x_ref, rows(*me), local_sem)
        mine.start()
        first = [copy(0, me, sibling, src=x_ref)]
        first += [copy(1 + j, me, (*chip, c), src=x_ref) for j, chip in enumerate(chips)]
        for cp in first:
            cp.start()
        passed = [copy(4 + j, (*chip, c), sibling) for j, chip in enumerate(chips)]
        for j, chip in enumerate(chips):
            copy(1 + j, (*chip, c), me).wait_recv()
            passed[j].start()
        copy(0, sibling, me).wait_recv()
        for j, chip in enumerate(chips):
            copy(4 + j, (*chip, 1 - c), me).wait_recv()
        for cp in first + passed:
            cp.wait_send()
        mine.wait()

    return pl.pallas_call(
        body, name=name, out_shape=jax.ShapeDtypeStruct((8 * m_per, LANES), buf.dtype),
        in_specs=[pl.BlockSpec(memory_space=pltpu.VMEM)], out_specs=pl.BlockSpec(memory_space=pltpu.VMEM),
        scratch_shapes=[_dma_sems(7), _dma_sems(7), pltpu.SemaphoreType.DMA],
        compiler_params=pltpu.CompilerParams(vmem_limit_bytes=VMEM_LIMIT),
    )(buf)


def sum_blocks(stacked, n_blocks, *, name):
    m = stacked.shape[0] // n_blocks
    width = stacked.shape[1]
    x3 = stacked.reshape((n_blocks, m, width))
    tr = _tile(m, max(SUBLANES, (1 << 18) // width // SUBLANES * SUBLANES), SUBLANES)

    def body(x_ref, o_ref):
        acc = x_ref[0]
        for s in range(1, n_blocks):
            acc = acc + x_ref[s]
        o_ref[...] = acc

    return pl.pallas_call(body, name=name, grid=(m // tr,), in_specs=[pl.BlockSpec((n_blocks, tr, width), lambda i: (0, i, 0))],
                          out_specs=pl.BlockSpec((tr, width), lambda i: (i, 0)),
                          out_shape=jax.ShapeDtypeStruct((m, width), F32), compiler_params=_cparams("parallel"))(x3)


def reduce_pair(gs):
    n = len(gs)

    def body(*refs):
        g, r = refs[:n], refs[n:2 * n]
        send, recv = refs[2 * n:]
        x, y, c, sibling, _ = _place()
        cps = []
        for i in range(n):
            k2 = gs[i].shape[1] // 2
            cp = _remote(g[i].at[:, pl.ds((1 - c) * k2, k2), :], r[i], send.at[i], recv.at[i], sibling)
            cp.start()
            cps.append(cp)
        for cp in cps:
            cp.wait()

    return pl.pallas_call(
        body, name="reduce_pair", in_specs=[ANY] * n, out_specs=[ANY] * n,
        out_shape=[jax.ShapeDtypeStruct((g.shape[0], g.shape[1] // 2, g.shape[2]), g.dtype) for g in gs],
        scratch_shapes=[_dma_sems(n), _dma_sems(n)],
    )(*gs)


def _row_tile(rows, width, budget, mult):
    return _tile(rows, max(mult, budget // width // mult * mult), mult)


def add_own_half(g, recv, core, *, name):
    n_slot, k2, width = recv.shape
    tr = _row_tile(k2, width, 1 << 19, 2 * SUBLANES)
    nb = k2 // tr

    def body(c_ref, g_ref, r_ref, o_ref):
        o_ref[...] = (g_ref[...] + r_ref[...]).astype(o_ref.dtype)

    spec = pl.BlockSpec((None, tr, width), lambda s, i, c: (s, i, 0))
    return pl.pallas_call(
        body, name=name,
        grid_spec=pltpu.PrefetchScalarGridSpec(
            num_scalar_prefetch=1, grid=(n_slot, nb),
            in_specs=[pl.BlockSpec((None, tr, width), lambda s, i, c: (s, c[0] * nb + i, 0)), spec], out_specs=spec),
        out_shape=jax.ShapeDtypeStruct(recv.shape, BF16), compiler_params=_cparams("parallel", "parallel"))(core, g, recv)


def reduce_chips(qs):
    n = len(qs)

    def body(*refs):
        q, r = refs[:n], refs[n:2 * n]
        send, recv = refs[2 * n:]
        x, y, c, _, chips = _place()
        cps = []
        for i in range(n):
            for j, (cx, cy) in enumerate(chips):
                cp = _remote(q[i].at[2 * cx + cy], r[i].at[j], send.at[i, j], recv.at[i, j], (cx, cy, c))
                cp.start()
                cps.append(cp)
        for cp in cps:
            cp.wait()

    return pl.pallas_call(
        body, name="reduce_chips", in_specs=[ANY] * n, out_specs=[ANY] * n,
        out_shape=[jax.ShapeDtypeStruct((3,) + q.shape[1:], q.dtype) for q in qs],
        scratch_shapes=[_dma_sems(n, 3), _dma_sems(n, 3)],
    )(*qs)


def chip_sum(q, recv, slot, *, name):
    _, k2, width = recv.shape
    tr = _row_tile(k2, width, 1 << 18, 2 * SUBLANES)

    def body(s_ref, q_ref, r_ref, o_ref):
        acc = q_ref[...].astype(F32)
        for j in range(3):
            acc = acc + r_ref[j].astype(F32)
        o_ref[...] = acc

    return pl.pallas_call(
        body, name=name,
        grid_spec=pltpu.PrefetchScalarGridSpec(
            num_scalar_prefetch=1, grid=(k2 // tr,),
            in_specs=[pl.BlockSpec((None, tr, width), lambda i, s: (s[0], i, 0)),
                      pl.BlockSpec((3, tr, width), lambda i, s: (0, i, 0))],
            out_specs=pl.BlockSpec((tr, width), lambda i, s: (i, 0))),
        out_shape=jax.ShapeDtypeStruct((k2, width), F32), compiler_params=_cparams("parallel"))(slot, q, recv)


def swap_pair(rs):
    n = len(rs)

    def body(*refs):
        r, o = refs[:n], refs[n:2 * n]
        send, recv = refs[2 * n:]
        x, y, c, sibling, _ = _place()
        cps = []
        for i in range(n):
            cp = _remote(r[i], o[i], send.at[i], recv.at[i], sibling)
            cp.start()
            cps.append(cp)
        for cp in cps:
            cp.wait()

    return pl.pallas_call(
        body, name="swap_pair", in_specs=[ANY] * n, out_specs=[ANY] * n,
        out_shape=[jax.ShapeDtypeStruct(r.shape, r.dtype) for r in rs],
        scratch_shapes=[_dma_sems(n), _dma_sems(n)],
    )(*rs)


def adamw(w, g, m, v, *, name):
    rows, width = w.shape
    tr = _tile(rows, max(SUBLANES, (1 << 19) // width // SUBLANES * SUBLANES), SUBLANES)

    def body(w_ref, g_ref, m_ref, v_ref, d_ref, mo_ref, vo_ref):
        gv = g_ref[...]
        mn = ADAM_B1 * m_ref[...] + (1.0 - ADAM_B1) * gv
        vn = ADAM_B2 * v_ref[...] + (1.0 - ADAM_B2) * (gv * gv)
        m_hat = mn / (1.0 - ADAM_B1 ** ADAM_STEP)
        v_hat = vn / (1.0 - ADAM_B2 ** ADAM_STEP)
        d_ref[...] = -ADAM_LR * (m_hat / (jnp.sqrt(v_hat) + ADAM_EPS) + ADAM_WD * w_ref[...])
        mo_ref[...] = mn
        vo_ref[...] = vn

    spec = pl.BlockSpec((tr, width), lambda i: (i, 0))
    sds = jax.ShapeDtypeStruct((rows, width), F32)
    return pl.pallas_call(body, name=name, grid=(rows // tr,), in_specs=[spec] * 4, out_specs=(spec,) * 3,
                          out_shape=(sds,) * 3, compiler_params=_cparams("parallel"))(w, g, m, v)


def _pack(arrs):
    flat = jnp.concatenate([a.reshape(-1) for a in arrs])
    pad = (-flat.shape[0]) % (SUBLANES * LANES)
    return jnp.pad(flat, (0, pad)).reshape((-1, LANES))


def _unpack(buf, shapes):
    flat, out, at = buf.reshape(-1), [], 0
    for s in shapes:
        size = int(np.prod(s))
        out.append(flat[at:at + size].reshape(s))
        at += size
    return out


BIG = ["w_in", "w_branch", "w_out", "w_ffn1", "w_ffn3", "w_ffn2"]
SMALL_SHARDED = ["b_merge", "gla_a2", "gla_ab", "gdn_conv", "m2_conv"]
SMALL_WHOLE = ["norm1_g", "norm2_g", "gla_norm_g", "na_rpb", "gdn_a_log", "gdn_dt_bias", "gdn_norm_g",
               "m2_conv_b", "m2_a_log", "m2_dt_bias", "m2_d", "m2_norm_g", "final_norm_g"]
WEIGHTS = ["c_ctx", "norm1_g", "norm2_g", "w_ada", "b_ada", "w_in", "b_merge", "gla_a2", "gla_ab", "gla_norm_g", "na_rpb",
           "gdn_conv", "gdn_a_log", "gdn_dt_bias", "gdn_norm_g", "m2_conv", "m2_conv_b", "m2_a_log", "m2_dt_bias", "m2_d",
           "m2_norm_g", "w_branch", "w_out", "w_ffn1", "w_ffn3", "w_ffn2", "final_norm_g"]


def _step(a):
    n_lat = a["x"].shape[1]
    x_i, y_i, c_i = lax.axis_index("x"), lax.axis_index("y"), lax.axis_index("c")
    slot = 2 * x_i + y_i

    slot_arr = slot.astype(jnp.int32).reshape((1,))
    core = c_i.astype(jnp.int32).reshape((1,))
    placed = [place_shard(a[n].reshape((DEPTH, -1, a[n].shape[-1])), slot_arr, name="place_" + n) for n in BIG]
    gath = dict(zip(BIG, gather_weights(placed)))
    gath["w_branch"] = gath["w_branch"].reshape((N_SLOT, DEPTH, 4, BRANCH_W, BRANCH_W))
    shard_shapes = [a[n].shape for n in SMALL_SHARDED]
    own = _pack([a[n] for n in SMALL_SHARDED])
    everyone = allgather_small(own, name="gather_small").reshape((8,) + own.shape)
    per_slot = [_unpack(everyone[2 * s], shard_shapes) for s in range(N_SLOT)]
    small = {n: jnp.concatenate([per_slot[s][i] for s in range(N_SLOT)], axis=-1) for i, n in enumerate(SMALL_SHARDED)}
    small.update({n: a[n] for n in SMALL_WHOLE})

    me = 4 * x_i + 2 * y_i + c_i
    ada_cols = a["w_ada"].shape[-1]
    c_all = allgather_small(a["c"].reshape((-1, LANES)), name="gather_c").reshape((8, D_MODEL))
    ada_slots = [jnp.zeros(a["w_ada"].shape[1:], F32) for _ in range(DEPTH)]
    mod_shards, ada_vjp = jax.vjp(lambda c_ctx, sl: ada_shard(c_all, c_ctx, a["w_ada"], sl), a["c_ctx"], ada_slots)
    packed = _pack(mod_shards)
    every = allgather_small(packed, name="gather_ada").reshape((8,) + packed.shape)
    by_slot = [_unpack(every[2 * s], [(ADA_ROWS, ada_cols)] * DEPTH) for s in range(N_SLOT)]
    mod = []
    for l in range(DEPTH):
        rows = jnp.concatenate([by_slot[s][l] for s in range(N_SLOT)], axis=-1) + a["b_ada"][l]
        mod.append(jnp.concatenate([lax.dynamic_slice_in_dim(rows, me, 1, axis=0), rows[8:9]], axis=0))

    groups = _in_groups()
    win = []
    for l in range(DEPTH):
        slabs = [gath["w_in"][s, l] for s in range(N_SLOT)]
        win.append({g: _take_cols_slabs(slabs, src) for g, src in groups})
    slots = [{n: jnp.zeros(s, F32) for n, s in _slot_shapes().items()} for _ in range(DEPTH)]
    diff = {"x": a["x"][0], "mod": mod, "small": small, "slots": slots}
    fixed = {"ctx": a["ctx"][0], "target": a["loss_target"][0], "gath": gath, "win": win}
    loss, grads = jax.value_and_grad(lambda d: _local_loss(d, fixed, n_lat=n_lat))(diff)

    dmod = _pack(grads["mod"])
    every = allgather_small(dmod, name="gather_dmod").reshape((8,) + dmod.shape)
    per_dev = [_unpack(every[i], [(2, 6 * D_MODEL)] * DEPTH) for i in range(8)]
    grad_b_ada, cots = [], []
    for l in range(DEPTH):
        lat = jnp.concatenate([per_dev[i][l][0:1] for i in range(8)], axis=0)
        ctx_rows = jnp.concatenate([per_dev[i][l][1].reshape((-1, LANES)) for i in range(8)], axis=0)
        ctx_sum = sum_blocks(ctx_rows, 8, name="l%d_dmod_ctx_sum" % l).reshape((1, 6 * D_MODEL))
        all_rows = jnp.concatenate([lat, ctx_sum, jnp.zeros((ADA_ROWS - 9, 6 * D_MODEL), F32)], axis=0)
        grad_b_ada.append(sum_blocks(all_rows.reshape((-1, LANES)), ADA_ROWS, name="l%d_b_ada_sum" % l).reshape(-1))
        cots.append(lax.dynamic_slice_in_dim(all_rows, slot * ada_cols, ada_cols, axis=1))
    c_ctx_part, ada_grads = ada_vjp(cots)

    parts = []
    for n in BIG:
        for l in range(DEPTH):
            sl = grads["slots"][l]
            if n == "w_in":
                parts.append(jnp.stack(_untake_cols([(sl["in_" + g], src) for g, src in groups], IN_COLS, N_SLOT)))
            elif n == "w_branch":
                b3 = sl["w_branch3"].reshape((N_SLOT, M2_H, LANES, BRANCH_W))[:, :, :M2_P].reshape((N_SLOT, BRANCH_W, BRANCH_W))
                parts.append(jnp.concatenate([sl["w_branch0"], sl["w_branch1"], sl["w_branch2"], b3], axis=1))
            else:
                parts.append(sl[n])
    from_sibling = reduce_pair(parts)
    pair_sums = [add_own_half(g, r, core, name="pair_sum%d" % i) for i, (g, r) in enumerate(zip(parts, from_sibling))]
    from_chips = reduce_chips(pair_sums)
    halves = [chip_sum(q, r, slot_arr, name="chip_sum%d" % i) for i, (q, r) in enumerate(zip(pair_sums, from_chips))]
    others = swap_pair(halves)
    big_grads = {}
    for k, n in enumerate(BIG):
        layers = []
        for l in range(DEPTH):
            mine, theirs = halves[DEPTH * k + l], others[DEPTH * k + l]
            layers.append(jnp.where(c_i == 0, jnp.concatenate([mine, theirs], axis=0),
                                    jnp.concatenate([theirs, mine], axis=0)))
        big_grads[n] = jnp.stack(layers).reshape(a[n].shape)

    summed = SMALL_WHOLE + SMALL_SHARDED + ["c_ctx"]
    local = dict(grads["small"], c_ctx=0.5 * c_ctx_part)
    partial = _pack([local[n] for n in summed] + [loss.reshape((1,))])
    total = sum_blocks(allgather_small(partial, name="gather_small_grads"), 8, name="sum_small_grads")
    pieces = _unpack(total, [local[n].shape for n in summed] + [(1,)])
    small_grads = dict(zip(summed, pieces[:-1]))
    for n in SMALL_SHARDED:
        width = a[n].shape[-1]
        small_grads[n] = lax.dynamic_slice_in_dim(small_grads[n], slot * width, width, axis=-1)
    small_grads["b_ada"] = jnp.stack(grad_b_ada)
    big_grads["w_ada"] = jnp.stack(ada_grads)
    small_names = summed + ["b_ada"]
    loss_all = pieces[-1].reshape(())

    grad_w, delta, new_m, new_v = {}, {}, {}, {}
    two_d = lambda t: t.reshape((-1, t.shape[-1]))
    for n in BIG + ["w_ada"]:
        d, mn, vn = adamw(two_d(a[n]), two_d(big_grads[n]), two_d(a["m_" + n]), two_d(a["v_" + n]), name="adamw_" + n)
        grad_w[n], delta[n], new_m[n], new_v[n] = big_grads[n], d.reshape(a[n].shape), mn.reshape(a[n].shape), vn.reshape(a[n].shape)
    shapes = [a[n].shape for n in small_names]
    d, mn, vn = adamw(_pack([a[n] for n in small_names]), _pack([small_grads[n] for n in small_names]),
                      _pack([a["m_" + n] for n in small_names]), _pack([a["v_" + n] for n in small_names]), name="adamw_small")
    for n, dd, mm, vv in zip(small_names, _unpack(d, shapes), _unpack(mn, shapes), _unpack(vn, shapes)):
        grad_w[n], delta[n], new_m[n], new_v[n] = small_grads[n], dd, mm, vv

    return (loss_all, grads["x"][None], *[grad_w[n] for n in WEIGHTS], *[delta[n] for n in WEIGHTS],
            *[new_m[n] for n in WEIGHTS], *[new_v[n] for n in WEIGHTS])


def kernel(x, c, ctx, c_ctx, norm1_g, norm2_g, w_ada, b_ada, w_in, b_merge, gla_a2, gla_ab, gla_norm_g, na_rpb, gdn_conv, gdn_a_log, gdn_dt_bias, gdn_norm_g, m2_conv, m2_conv_b, m2_a_log, m2_dt_bias, m2_d, m2_norm_g, w_branch, w_out, w_ffn1, w_ffn3, w_ffn2, final_norm_g, loss_target, m_c_ctx, m_norm1_g, m_norm2_g, m_w_ada, m_b_ada, m_w_in, m_b_merge, m_gla_a2, m_gla_ab, m_gla_norm_g, m_na_rpb, m_gdn_conv, m_gdn_a_log, m_gdn_dt_bias, m_gdn_norm_g, m_m2_conv, m_m2_conv_b, m_m2_a_log, m_m2_dt_bias, m_m2_d, m_m2_norm_g, m_w_branch, m_w_out, m_w_ffn1, m_w_ffn3, m_w_ffn2, m_final_norm_g, v_c_ctx, v_norm1_g, v_norm2_g, v_w_ada, v_b_ada, v_w_in, v_b_merge, v_gla_a2, v_gla_ab, v_gla_norm_g, v_na_rpb, v_gdn_conv, v_gdn_a_log, v_gdn_dt_bias, v_gdn_norm_g, v_m2_conv, v_m2_conv_b, v_m2_a_log, v_m2_dt_bias, v_m2_d, v_m2_norm_g, v_w_branch, v_w_out, v_w_ffn1, v_w_ffn3, v_w_ffn2, v_final_norm_g):
    return _step(dict(locals()))
```

```python
import functools
import math

import numpy as np
import jax
import jax.numpy as jnp
from jax import lax
from jax.experimental import pallas as pl
from jax.experimental.pallas import tpu as pltpu

F32 = jnp.float32
BF16 = jnp.bfloat16
HI = lax.Precision.HIGHEST
MESH = pl.DeviceIdType.MESH
ANY = pl.BlockSpec(memory_space=pl.ANY)

VMEM_LIMIT = 56 * 1024 * 1024
LANES = 128
SUBLANES = 8

D_MODEL = 2048
DEPTH = 2
GRID_W = 64
CHUNK = 64
CONV_W = 5
RMS_EPS = 1e-6
NEG_INF = -1e30
ROPE_BASE = 10000.0
BRANCH_W = 512
GLA_H, GLA_DK, GLA_DV, GLA_LR, GLA_TAU = 4, 64, 128, 16, 16.0
NA_H, NA_D, NA_WIN_R, NA_WIN_C = 4, 128, 8, 16
GDN_H, GDN_D = 4, 128
M2_P, M2_H, M2_N, M2_G = 64, 8, 128, 2
D_FF = 5632
IN_COLS = 14912
N_SLOT = 4
ADAM_LR, ADAM_B1, ADAM_B2, ADAM_EPS, ADAM_WD, ADAM_STEP = 0.001, 0.9, 0.999, 1e-08, 0.01, 10


def _cparams(*sem):
    return pltpu.CompilerParams(dimension_semantics=sem if sem else None, vmem_limit_bytes=VMEM_LIMIT)


def _tile(n, target, mult):
    if n <= target:
        return n
    best = None
    for t in range(mult, target + 1, mult):
        if n % t == 0:
            best = t
    assert best is not None, (n, target, mult)
    return best


def _nt(a, b):
    return lax.dot_general(a.astype(BF16), b.astype(BF16), (((1,), (1,)), ((), ())), preferred_element_type=F32)


def _tn(a, b):
    return lax.dot_general(a.astype(BF16), b.astype(BF16), (((0,), (0,)), ((), ())), preferred_element_type=F32)


def _nn(a, b):
    return jnp.dot(a.astype(BF16), b.astype(BF16), preferred_element_type=F32)


def _dot3(a, b, dims):
    a_hi, b_hi = a.astype(BF16), b.astype(BF16)
    a_lo = (a - a_hi.astype(F32)).astype(BF16)
    b_lo = (b - b_hi.astype(F32)).astype(BF16)
    dot = lambda u, v: lax.dot_general(u, v, dims, preferred_element_type=F32)
    return dot(a_hi, b_hi) + (dot(a_hi, b_lo) + dot(a_lo, b_hi))


_NN, _NT, _TN = ((((1,), (0,)), ((), ())), (((1,), (1,)), ((), ())), (((0,), (0,)), ((), ())))


@jax.custom_vjp
def _nn_hi(a, b):
    return _dot3(a, b, _NN)


_nn_hi.defvjp(lambda a, b: (_dot3(a, b, _NN), (a, b)),
              lambda res, g: (_dot3(g, res[1], _NT), _dot3(res[0], g, _TN)))


def _w_spec(layout, prefix, r_idx, c_idx, br, bc, slot_dim):
    none = (None,) * len(prefix)
    if layout == "plain":
        return pl.BlockSpec(none + (br, bc), lambda i, j, k: prefix + (r_idx(i, j, k), c_idx(i, j, k)))
    if layout == "col":
        per = slot_dim // bc
        return pl.BlockSpec((None,) + none + (br, bc),
                            lambda i, j, k: (c_idx(i, j, k) // per,) + prefix + (r_idx(i, j, k), c_idx(i, j, k) % per))
    per = slot_dim // br
    return pl.BlockSpec((None,) + none + (br, bc),
                        lambda i, j, k: (r_idx(i, j, k) // per,) + prefix + (r_idx(i, j, k) % per, c_idx(i, j, k)))


def _mm(a, b, *, name, ta=False, tb=False, b_layout="plain", b_prefix=(), out_layout="plain", out_dtype=F32):
    m, kdim = (a.shape[1], a.shape[0]) if ta else a.shape
    rows, cols = b.shape[-2:]
    if b_layout == "col":
        cols *= N_SLOT
    elif b_layout == "row":
        rows *= N_SLOT
    n = rows if tb else cols
    assert (cols if tb else rows) == kdim, (a.shape, b.shape, ta, tb)
    n_unit = n // N_SLOT if (out_layout == "col" or (b_layout == ("row" if tb else "col"))) else n
    k_unit = kdim // N_SLOT if b_layout == ("col" if tb else "row") else kdim
    odd_n = n_unit % 1408 == 0 and n_unit % 512 != 0
    odd_k = k_unit % 1408 == 0 and k_unit % 512 != 0
    if ta:
        tm = _tile(m, 1024, LANES)
        tn = _tile(n_unit, 1408 if odd_n else 1024, LANES)
        tk = _tile(k_unit, 1056, 2 * SUBLANES)
    elif tb:
        tm = _tile(m, 768 if odd_k else 704, 2 * SUBLANES)
        tn = _tile(n_unit, 1408 if odd_n else 2048, LANES)
        tk = _tile(k_unit, 1408 if odd_k else 2048, LANES)
    else:
        tm = _tile(m, 768, 2 * SUBLANES)
        tn = _tile(n_unit, 1408 if odd_n else 512, LANES)
        tk = _tile(k_unit, 1408 if odd_k else 2048, LANES)
    nk = kdim // tk
    a_spec = (pl.BlockSpec((tk, tm), lambda i, j, k: (k, i)) if ta else pl.BlockSpec((tm, tk), lambda i, j, k: (i, k)))
    slot_dim = b.shape[-1] if b_layout == "col" else b.shape[-2]
    if tb:
        b_spec = _w_spec(b_layout, tuple(b_prefix), lambda i, j, k: j, lambda i, j, k: k, tn, tk, slot_dim)
    else:
        b_spec = _w_spec(b_layout, tuple(b_prefix), lambda i, j, k: k, lambda i, j, k: j, tk, tn, slot_dim)
    if out_layout == "col":
        per = (n // N_SLOT) // tn
        out_shape = jax.ShapeDtypeStruct((N_SLOT, m, n // N_SLOT), out_dtype)
        out_spec = pl.BlockSpec((None, tm, tn), lambda i, j, k: (j // per, i, j % per))
    else:
        out_shape = jax.ShapeDtypeStruct((m, n), out_dtype)
        out_spec = pl.BlockSpec((tm, tn), lambda i, j, k: (i, j))
    dims = (((0 if ta else 1,), (1 if tb else 0,)), ((), ()))

    def product(a_ref, b_ref):
        return lax.dot_general(a_ref[...].astype(BF16), b_ref[...].astype(BF16), dims, preferred_element_type=F32)

    def body_once(a_ref, b_ref, o_ref):
        o_ref[...] = product(a_ref, b_ref).astype(o_ref.dtype)

    def body(a_ref, b_ref, o_ref, acc_ref):
        k = pl.program_id(2)

        @pl.when(k == 0)
        def _():
            acc_ref[...] = jnp.zeros_like(acc_ref)

        acc_ref[...] += product(a_ref, b_ref)

        @pl.when(k == nk - 1)
        def _():
            o_ref[...] = acc_ref[...].astype(o_ref.dtype)

    return pl.pallas_call(
        body_once if nk == 1 else body, name=name, grid=(m // tm, n // tn, nk), in_specs=[a_spec, b_spec],
        out_specs=out_spec, out_shape=out_shape, scratch_shapes=[] if nk == 1 else [pltpu.VMEM((tm, tn), F32)],
        compiler_params=_cparams("parallel", "parallel", "arbitrary"),
    )(a, b)


def linear(a, w, grad_slot, *, name, layout="plain", prefix=(), out_dtype=F32):
    @jax.custom_vjp
    def f(a, w, grad_slot):
        return _mm(a, w, name=name + "_fwd", b_layout=layout, b_prefix=prefix, out_dtype=out_dtype)

    def fwd(a, w, grad_slot):
        return f(a, w, grad_slot), (a, w)

    def bwd(res, g):
        a, w = res
        da = _mm(g, w, name=name + "_dgrad", tb=True, b_layout=layout, b_prefix=prefix, out_dtype=a.dtype)
        dw = _mm(a, g, name=name + "_wgrad", ta=True, out_layout="col" if layout == "col" else "plain")
        if layout == "row":
            dw = dw.reshape((N_SLOT, dw.shape[0] // N_SLOT, dw.shape[1]))
        return da, None, dw

    f.defvjp(fwd, bwd)
    return f(a, w, grad_slot)


def _rowwise_specs(rows, consts, params, tile, seg_tile):
    def row_spec(r):
        return pl.BlockSpec((tile, r.shape[1]), lambda i: (i, 0))

    def par_spec(p):
        if p.shape[0] == 2:
            return pl.BlockSpec((None,) + p.shape[1:], lambda i: (jnp.where(i >= seg_tile, 1, 0), 0, 0))
        return pl.BlockSpec((None,) + p.shape[1:], lambda i: (0, 0, 0))

    return [row_spec(r) for r in rows], [row_spec(r) for r in consts], [par_spec(p) for p in params]


def rowwise(f, rows, consts, params, *, out_widths, tile, n_lat, name, out_dtype=F32):
    rows, consts, params = tuple(rows), tuple(consts), tuple(params)
    n_rows = rows[0].shape[0]
    tile = math.gcd(math.gcd(n_rows, n_lat), tile)
    assert tile % SUBLANES == 0
    seg_tile = n_lat // tile
    grid = (n_rows // tile,)
    nr, nc, npar = len(rows), len(consts), len(params)
    r_specs, c_specs, p_specs = _rowwise_specs(rows, consts, params, tile, seg_tile)
    out_shape = tuple(jax.ShapeDtypeStruct((n_rows, w), out_dtype) for w in out_widths)
    out_specs = tuple(pl.BlockSpec((tile, w), lambda i: (i, 0)) for w in out_widths)
    n_out = len(out_widths)

    def fwd_call(rows, consts, params):
        def body(*refs):
            ins = [r[...].astype(F32) for r in refs[:nr + nc + npar]]
            outs = f(*ins)
            for o_ref, o in zip(refs[nr + nc + npar:], outs):
                o_ref[...] = o.astype(o_ref.dtype)

        return pl.pallas_call(body, name=name + "_fwd", grid=grid, in_specs=r_specs + c_specs + p_specs,
                              out_specs=out_specs, out_shape=out_shape,
                              compiler_params=_cparams("parallel"))(*rows, *consts, *params)

    def bwd_call(rows, consts, params, gouts):
        def body(*refs):
            i = pl.program_id(0)
            ins = [r[...].astype(F32) for r in refs[:nr + nc + npar]]
            gs = tuple(r[...].astype(F32) for r in refs[nr + nc + npar:nr + nc + npar + n_out])
            d_refs = refs[nr + nc + npar + n_out:]
            cvals = ins[nr:nr + nc]

            def g(*diff):
                return tuple(f(*diff[:nr], *cvals, *diff[nr:]))

            _, vjp = jax.vjp(g, *ins[:nr], *ins[nr + nc:])
            grads = vjp(gs)
            for d_ref, gr in zip(d_refs[:nr], grads[:nr]):
                d_ref[...] = gr.astype(d_ref.dtype)
            for p, d_ref, gr in zip(params, d_refs[nr:], grads[nr:]):
                first = (i == 0) | (i == seg_tile) if p.shape[0] == 2 else (i == 0)

                @pl.when(first)
                def _():
                    d_ref[...] = jnp.zeros_like(d_ref)

                d_ref[...] += gr

        d_shape = tuple(jax.ShapeDtypeStruct(r.shape, r.dtype) for r in rows) + tuple(
            jax.ShapeDtypeStruct(p.shape, F32) for p in params)
        g_specs = [pl.BlockSpec((tile, w), lambda i: (i, 0)) for w in out_widths]
        return pl.pallas_call(body, name=name + "_bwd", grid=grid,
                              in_specs=r_specs + c_specs + p_specs + g_specs,
                              out_specs=tuple(r_specs + p_specs), out_shape=d_shape,
                              compiler_params=_cparams("arbitrary"))(*rows, *consts, *params, *gouts)

    @jax.custom_vjp
    def op(rows, consts, params):
        return fwd_call(rows, consts, params)

    def op_fwd(rows, consts, params):
        return op(rows, consts, params), (rows, consts, params)

    def op_bwd(res, gouts):
        rows, consts, params = res
        d = bwd_call(rows, consts, params, tuple(gouts))
        return tuple(d[:nr]), tuple(None for _ in consts), tuple(d[nr:])

    op.defvjp(op_fwd, op_bwd)
    return op(rows, consts, params)


def _rms(x, width=None):
    w = x.shape[-1] if width is None else width
    return x * lax.rsqrt(jnp.sum(x * x, axis=-1, keepdims=True) * (1.0 / w) + RMS_EPS)


def _silu(x):
    return x * jax.nn.sigmoid(x)


def _f_modnorm(x, g, sc, sh):
    return ((_rms(x) * g) * (1.0 + sc) + sh,)


def _f_silu(x):
    return (_silu(x),)


def _f_gla_prep(lr, a2, ab):
    z = _nn(lr, a2) + ab
    return ((jnp.minimum(z, 0.0) - jnp.log(1.0 + jnp.exp(-jnp.abs(z)))) * (1.0 / GLA_TAU),)


def _f_headnorm_gate(o, g, ng):
    outs = []
    for h in range(BRANCH_W // LANES):
        lo = h * LANES
        oh = o[:, lo:lo + LANES] + o[:, BRANCH_W + lo:BRANCH_W + lo + LANES]
        outs.append(_rms(oh) * ng * _silu(g[:, lo:lo + LANES]))
    return (jnp.concatenate(outs, axis=-1),)


def _f_gdn_prep(x, alog, dtb):
    half = x.shape[1] // 2
    beta = jax.nn.sigmoid(x[:, :half])
    la = -jnp.exp(alog) * jax.nn.softplus(x[:, half:] + dtb)
    return beta, la


def _f_m2_prep(x, alog, dtb):
    dt = jax.nn.softplus(x + dtb)
    return dt, -jnp.exp(alog) * dt


def _f_m2_fin(o, z, xs, dskip, ng):
    w = z.shape[1]
    y = (o[:, :w] + o[:, w:] + dskip * xs) * _silu(z)
    return (_rms(y, BRANCH_W) * ng,)


def _f_merge(gate, z0, z1, z2, z3, bm):
    acc = None
    for i, z in enumerate((z0, z1, z2, z3)):
        lo = i * D_MODEL
        t = jax.nn.sigmoid(gate[:, lo:lo + D_MODEL] + bm[:, lo:lo + D_MODEL]) * z
        acc = t if acc is None else acc + t
    return (acc,)


def _f_resid(x, y, g):
    return (x + g * y,)


def _f_swiglu(u1, u3):
    return (_silu(u1) * u3,)


def _f_loss(x, tgt, g):
    e = _rms(x) * g - tgt
    per_row = 0.5 * jnp.sum(e * e, axis=-1, keepdims=True) * (1.0 / D_MODEL)
    return (jnp.broadcast_to(per_row * (1.0 / LANES), (x.shape[0], LANES)),)


_HALO = 8


def _conv_segments(n_lat, n_ctx):
    segs = [(0, _HALO, n_lat), (n_lat, n_lat + 3 * _HALO, n_ctx)]
    return segs, n_lat + n_ctx + 4 * _HALO


def _conv_stage(buf, src, n_lat, n_ctx):
    zeros = jnp.zeros((_HALO, LANES), F32)
    buf[0:_HALO, :] = zeros
    buf[_HALO:_HALO + n_lat, :] = src[0:n_lat, :]
    buf[n_lat + _HALO:n_lat + 2 * _HALO, :] = zeros
    buf[n_lat + 2 * _HALO:n_lat + 3 * _HALO, :] = zeros
    buf[n_lat + 3 * _HALO:n_lat + 3 * _HALO + n_ctx, :] = src[n_lat:n_lat + n_ctx, :]
    buf[n_lat + n_ctx + 3 * _HALO:n_lat + n_ctx + 4 * _HALO, :] = zeros


def conv_silu(x, w, b, *, n_lat, name):
    n_rows, n_ch = x.shape
    n_ctx = n_rows - n_lat
    segs, n_buf = _conv_segments(n_lat, n_ctx)
    grid = (n_ch // LANES,)
    col = lambda r: pl.BlockSpec((r, LANES), lambda j: (0, j))
    half = CONV_W // 2

    def tiles():
        for row0, off, length in segs:
            tr = _tile(length, 256, SUBLANES)
            for t0 in range(0, length, tr):
                yield row0 + t0, off + t0, tr

    def pre_act(buf, w_ref, b_ref, off, tr):
        acc = jnp.broadcast_to(b_ref[...], (tr, LANES))
        for j in range(CONV_W):
            acc = acc + w_ref[j:j + 1, :] * buf[off + j - half:off + j - half + tr, :]
        return acc

    def fwd_call(x, w, b):
        def body(x_ref, w_ref, b_ref, o_ref, buf):
            _conv_stage(buf, x_ref, n_lat, n_ctx)
            for row, off, tr in tiles():
                o_ref[row:row + tr, :] = _silu(pre_act(buf, w_ref, b_ref, off, tr))

        return pl.pallas_call(body, name=name + "_fwd", grid=grid, in_specs=[col(n_rows), col(CONV_W), col(1)],
                              out_specs=col(n_rows), out_shape=jax.ShapeDtypeStruct(x.shape, F32),
                              scratch_shapes=[pltpu.VMEM((n_buf, LANES), F32)],
                              compiler_params=_cparams("parallel"))(x, w, b)

    def bwd_call(x, w, b, g):
        def body(x_ref, w_ref, b_ref, g_ref, dx_ref, dw_ref, db_ref, xbuf, dbuf):
            _conv_stage(xbuf, x_ref, n_lat, n_ctx)
            _conv_stage(dbuf, g_ref, n_lat, n_ctx)
            dw = [jnp.zeros((1, LANES), F32) for _ in range(CONV_W)]
            db = jnp.zeros((1, LANES), F32)
            for row, off, tr in tiles():
                pre = pre_act(xbuf, w_ref, b_ref, off, tr)
                s = jax.nn.sigmoid(pre)
                dpre = g_ref[row:row + tr, :] * (s * (1.0 + pre * (1.0 - s)))
                dbuf[off:off + tr, :] = dpre
                db = db + jnp.sum(dpre, axis=0, keepdims=True)
                for j in range(CONV_W):
                    dw[j] = dw[j] + jnp.sum(dpre * xbuf[off + j - half:off + j - half + tr, :], axis=0, keepdims=True)
            for row, off, tr in tiles():
                acc = jnp.zeros((tr, LANES), F32)
                for j in range(CONV_W):
                    acc = acc + w_ref[j:j + 1, :] * dbuf[off - j + half:off - j + half + tr, :]
                dx_ref[row:row + tr, :] = acc
            for j in range(CONV_W):
                dw_ref[j:j + 1, :] = dw[j]
            db_ref[...] = db

        return pl.pallas_call(
            body, name=name + "_bwd", grid=grid, in_specs=[col(n_rows), col(CONV_W), col(1), col(n_rows)],
            out_specs=(col(n_rows), col(CONV_W), col(1)),
            out_shape=(jax.ShapeDtypeStruct(x.shape, F32), jax.ShapeDtypeStruct(w.shape, F32),
                       jax.ShapeDtypeStruct(b.shape, F32)),
            scratch_shapes=[pltpu.VMEM((n_buf, LANES), F32), pltpu.VMEM((n_buf, LANES), F32)],
            compiler_params=_cparams("parallel"))(x, w, b, g)

    @jax.custom_vjp
    def op(x, w, b):
        return fwd_call(x, w, b)

    op.defvjp(lambda x, w, b: (op(x, w, b), (x, w, b)), lambda res, g: bwd_call(*res, g))
    return op(x, w, b)


def chunk_scan(step, shared, shared_lanes, perdir, perdir_lanes, consts, *, heads, state_shape, out_w, n_lat, name,
               keep_shape=None):
    assert keep_shape is None or heads == 1
    shared, perdir, consts = tuple(shared), tuple(perdir), tuple(consts)
    n_rows = shared[0].shape[0]
    nl, ncx = n_lat // CHUNK, (n_rows - n_lat) // CHUNK
    n_chunks = nl + ncx
    ow_all = heads * out_w
    ns, npd, ncst = len(shared), len(perdir), len(consts)

    def cidx(d, n):
        m = n - ncx
        return jnp.where(n < ncx, nl + jnp.where(d == 0, n, ncx - 1 - n), jnp.where(d == 0, m, nl - 1 - m))

    def specs(order):
        sh = [pl.BlockSpec((CHUNK, a.shape[1]), lambda d, n: (cidx(d, order(n)), 0)) for a in shared]
        pd = [pl.BlockSpec((CHUNK, a.shape[1] // 2), lambda d, n: (cidx(d, order(n)), d)) for a in perdir]
        cs = [pl.BlockSpec((CHUNK, a.shape[1]), lambda d, n: (cidx(d, order(n)), 0)) for a in consts]
        o = pl.BlockSpec((CHUNK, ow_all), lambda d, n: (cidx(d, order(n)), d))
        st = pl.BlockSpec((None, None, heads) + state_shape, lambda d, n: (d, order(n), 0) + (0,) * len(state_shape))
        kp = [] if keep_shape is None else [pl.BlockSpec((None, None) + keep_shape,
                                                        lambda d, n: (d, order(n)) + (0,) * len(keep_shape))]
        return sh, pd, cs, o, st, kp

    def mask(d):
        r = lax.broadcasted_iota(jnp.int32, (CHUNK, CHUNK), 0)
        c = lax.broadcasted_iota(jnp.int32, (CHUNK, CHUNK), 1)
        lower = jnp.where(r >= c, 1.0, 0.0).astype(F32)
        upper = jnp.where(r <= c, 1.0, 0.0).astype(F32)
        return jnp.where(d == 0, lower, upper)

    def head_slices(h):
        out = []
        for lanes in tuple(shared_lanes) + tuple(perdir_lanes):
            out.append([slice(off + (h // hpg) * w, off + (h // hpg) * w + w) for off, w, hpg in lanes])
        return out

    def load(refs, h):
        return tuple(tuple(ref[:, s] for s in sl) for ref, sl in zip(refs, head_slices(h)))

    state_sds = jax.ShapeDtypeStruct((2, n_chunks, heads) + state_shape, F32)

    def fwd_call(shared, perdir, consts):
        sh, pd, cs, o_spec, st_spec, kp_spec = specs(lambda n: n)

        def body(*refs):
            in_refs = refs[:ns + npd]
            c_refs = refs[ns + npd:ns + npd + ncst]
            o_ref, ss_ref = refs[ns + npd + ncst:ns + npd + ncst + 2]
            s_scr = refs[-1]
            d, n = pl.program_id(0), pl.program_id(1)

            @pl.when(n == 0)
            def _():
                s_scr[...] = jnp.zeros_like(s_scr)

            m = mask(d)
            cv = tuple(c[...] for c in c_refs)
            ins = [load(in_refs, h) for h in range(heads)]
            s0 = [s_scr[h] for h in range(heads)]
            res = [step(ins[h], cv, s0[h], m, None) for h in range(heads)]
            for h in range(heads):
                o_ref[:, h * out_w:(h + 1) * out_w] = res[h][0]
                ss_ref[h] = s0[h]
                s_scr[h] = res[h][1]
            if keep_shape is not None:
                refs[-2][...] = res[0][2]

        keep_sds = [] if keep_shape is None else [jax.ShapeDtypeStruct((2, n_chunks) + keep_shape, F32)]
        return pl.pallas_call(
            body, name=name + "_fwd", grid=(2, n_chunks), in_specs=sh + pd + cs, out_specs=tuple([o_spec, st_spec] + kp_spec),
            out_shape=tuple([jax.ShapeDtypeStruct((n_rows, 2 * ow_all), F32), state_sds] + keep_sds),
            scratch_shapes=[pltpu.VMEM((heads,) + state_shape, F32)],
            compiler_params=_cparams("arbitrary", "arbitrary"))(*shared, *perdir, *consts)

    def bwd_call(shared, perdir, consts, starts, kept, g):
        sh, pd, cs, o_spec, st_spec, kp_spec = specs(lambda n: n_chunks - 1 - n)
        dsh = [pl.BlockSpec((CHUNK, a.shape[1]), lambda d, n: (cidx(d, n_chunks - 1 - n), d)) for a in shared]
        nk = len(kept)

        def body(*refs):
            in_refs = refs[:ns + npd]
            c_refs = refs[ns + npd:ns + npd + ncst]
            ss_ref, g_ref = refs[ns + npd + ncst:ns + npd + ncst + 2]
            kept_val = refs[ns + npd + ncst + 2][...] if nk else None
            d_refs = refs[ns + npd + ncst + 2 + nk:ns + npd + ncst + 2 + nk + ns + npd]
            ds_scr = refs[-1]
            d, n = pl.program_id(0), pl.program_id(1)

            @pl.when(n == 0)
            def _():
                ds_scr[...] = jnp.zeros_like(ds_scr)

            m = mask(d)
            cv = tuple(c[...] for c in c_refs)
            ins = [load(in_refs, h) for h in range(heads)]
            cots = [(g_ref[:, h * out_w:(h + 1) * out_w], ds_scr[h]) for h in range(heads)]
            starts = [ss_ref[h] for h in range(heads)]
            grads = []
            for h in range(heads):
                _, vjp = jax.vjp(lambda i_, s_: step(i_, cv, s_, m, kept_val)[:2], ins[h], starts[h])
                grads.append(vjp(cots[h]))
            for d_ref in d_refs:
                d_ref[...] = jnp.zeros_like(d_ref)
            for h in range(heads):
                g_ins, g_s = grads[h]
                for d_ref, sl, gr in zip(d_refs, head_slices(h), g_ins):
                    for s, gv in zip(sl, gr):
                        d_ref[:, s] += gv
                ds_scr[h] = g_s

        d_shape = tuple(jax.ShapeDtypeStruct((n_rows, 2 * a.shape[1]), F32) for a in shared) + tuple(
            jax.ShapeDtypeStruct(a.shape, F32) for a in perdir)
        return pl.pallas_call(
            body, name=name + "_bwd", grid=(2, n_chunks), in_specs=sh + pd + cs + [st_spec, o_spec] + kp_spec,
            out_specs=tuple(dsh + pd), out_shape=d_shape,
            scratch_shapes=[pltpu.VMEM((heads,) + state_shape, F32)],
            compiler_params=_cparams("arbitrary", "arbitrary"))(*shared, *perdir, *consts, starts, g, *kept)

    @jax.custom_vjp
    def op(shared, perdir, consts):
        return fwd_call(shared, perdir, consts)[0]

    def op_fwd(shared, perdir, consts):
        o, starts, *kept = fwd_call(shared, perdir, consts)
        return o, (shared, perdir, consts, starts, tuple(kept))

    def op_bwd(res, g):
        shared, perdir, consts, starts, kept = res
        d = bwd_call(shared, perdir, consts, starts, kept, g)
        d_sh = tuple(a[:, :a.shape[1] // 2] + a[:, a.shape[1] // 2:] for a in d[:ns])
        return d_sh, tuple(d[ns:]), tuple(None for _ in consts)

    op.defvjp(op_fwd, op_bwd)
    return op(shared, perdir, consts)


@jax.custom_vjp
def _swap_halves(x):
    return pltpu.roll(x, LANES // 2, 1)


_swap_halves.defvjp(lambda x: (_swap_halves(x), None), lambda _, g: (_swap_halves(g),))


def _gla_step(ins, consts, st, m, kept):
    (q_all, k_all), (v_all,), (la_all,) = ins
    cos, sin = consts
    b_all = _nn_hi(m, la_all)
    bl_all = jnp.sum(la_all, axis=0, keepdims=True)
    outs, st_new = [], []
    for h in range(GLA_H):
        blk = lambda x: x[:, h * LANES:(h + 1) * LANES]
        q, k, v, b, bl = blk(q_all), blk(k_all), blk(v_all), blk(b_all), blk(bl_all)
        st_h = st[h * GLA_DV:(h + 1) * GLA_DV]
        q = (q * cos + _swap_halves(q) * sin) * (GLA_DK ** -0.5)
        k = k * cos + _swap_halves(k) * sin
        qi = q * jnp.exp(b)
        ki = k * jnp.exp(-b)
        outs.append(_nt(qi, st_h) + _nn(_nt(qi, ki) * m, v))
        st_new.append(st_h * jnp.exp(bl) + _tn(v, k * jnp.exp(bl - b)))
    return jnp.concatenate(outs, axis=1), jnp.concatenate(st_new, axis=0), None


def _l2n(x):
    return x * lax.rsqrt(jnp.sum(x * x, axis=-1, keepdims=True) + RMS_EPS)


def _tri_inv_fwd(nmat):
    r = lax.broadcasted_iota(jnp.int32, nmat.shape, 0)
    c = lax.broadcasted_iota(jnp.int32, nmat.shape, 1)
    inv = jnp.where(r == c, 1.0, 0.0).astype(F32) - nmat
    p = nmat
    for _ in range(5):
        p = _nn_hi(p, p)
        inv = inv + _nn_hi(inv, p)
    return inv


@jax.custom_vjp
def _unit_tri_inv(nmat):
    return _tri_inv_fwd(nmat)


def _unit_tri_inv_bwd(inv, g):
    return (-_dot3(_dot3(inv, g, _TN), inv, _NT),)


_unit_tri_inv.defvjp(lambda nmat: (lambda inv: (inv, inv))(_tri_inv_fwd(nmat)), _unit_tri_inv_bwd)


@jax.custom_vjp
def _kept_tri_inv(nmat, inv):
    return inv


_kept_tri_inv.defvjp(lambda nmat, inv: (inv, inv), lambda inv, g: _unit_tri_inv_bwd(inv, g) + (jnp.zeros_like(inv),))


def _lane_col(x, h):
    lane = lax.broadcasted_iota(jnp.int32, x.shape, 1)
    return jnp.sum(jnp.where(lane == h, x, 0.0), axis=1, keepdims=True)


def _masked_exp(diff, mask):
    return jnp.where(mask > 0, jnp.exp(jnp.where(mask > 0, diff, 0.0)), 0.0)


def _gdn_step(ins, consts, s, m, kept):
    (q, k, v), (beta,), (la,) = ins
    n = GDN_H * CHUNK
    hs = range(GDN_H)
    blk = lambda x, h: x[:, h * GDN_D:(h + 1) * GDN_D]
    rows = lambda x, h: x[h * CHUNK:(h + 1) * CHUNK]
    qh = [_l2n(blk(q, h)) * (GDN_D ** -0.5) for h in hs]
    kh = [_l2n(blk(k, h)) for h in hs]
    k_st = jnp.concatenate(kh, axis=0)
    q_st = jnp.concatenate(qh, axis=0)
    v_st = jnp.concatenate([blk(v, h) for h in hs], axis=0)
    beta_st = jnp.concatenate([_lane_col(beta, h) for h in hs], axis=0)
    la_cols = [_lane_col(la, h) for h in hs]
    la_st = jnp.concatenate([jnp.broadcast_to(c, (CHUNK, GDN_D)) for c in la_cols], axis=0)
    r = lax.broadcasted_iota(jnp.int32, (n, n), 0)
    c = lax.broadcasted_iota(jnp.int32, (n, n), 1)
    e = jnp.where(lax.broadcasted_iota(jnp.int32, (n, CHUNK), 0) % CHUNK == lax.broadcasted_iota(jnp.int32, (n, CHUNK), 1),
                  1.0, 0.0).astype(F32)
    m_bd = jnp.where(r // CHUNK == c // CHUNK, _nt(_nn(e, m), e), 0.0)
    eye = jnp.where(r == c, 1.0, 0.0).astype(F32)
    b_st = _nn_hi(m_bd, la_st)
    b_t = b_st.T
    diff = jnp.concatenate([b_st, b_st], axis=1) - jnp.concatenate([b_t, b_t], axis=0)
    incl = _masked_exp(diff, m_bd)
    strict = _masked_exp(diff, m_bd - eye)
    nmat = beta_st * _nt(k_st, k_st) * strict
    inv = _unit_tri_inv(nmat) if kept is None else _kept_tri_inv(nmat, kept)
    wu = _nn_hi(inv, jnp.concatenate([k_st * (beta_st * jnp.exp(b_st)), v_st * beta_st], axis=-1))
    w, u0 = wu[:, :GDN_D], wu[:, GDN_D:]
    us, s_new, qs = [], [], []
    for h in hs:
        s_h = s[h * GDN_D:(h + 1) * GDN_D]
        bl = jnp.sum(jnp.broadcast_to(la_cols[h], (CHUNK, GDN_D)), axis=0, keepdims=True)
        u_h = rows(u0, h) - _nn(rows(w, h), s_h)
        s_new.append(jnp.exp(bl) * s_h + _tn(kh[h] * jnp.exp(bl - rows(b_st, h)), u_h))
        us.append(u_h)
        qs.append(_nn(qh[h], s_h))
    o_st = jnp.exp(b_st) * jnp.concatenate(qs, axis=0) + _nn(_nt(q_st, k_st) * incl, jnp.concatenate(us, axis=0))
    return jnp.concatenate([rows(o_st, h) for h in hs], axis=1), jnp.concatenate(s_new, axis=0), inv


def _ssd_step(ins, consts, s, m, kept):
    (xs,), (bc,), (dt,), (la,) = ins
    hpg = M2_H // M2_G
    b_all = _nn_hi(m, la)
    bl_all = jnp.sum(la, axis=0, keepdims=True)
    b_t = b_all.T
    row_id = lax.broadcasted_iota(jnp.int32, b_t.shape, 0)
    bm = [bc[:, g * M2_N:(g + 1) * M2_N] for g in range(M2_G)]
    cm = [bc[:, (M2_G + g) * M2_N:(M2_G + g + 1) * M2_N] for g in range(M2_G)]
    scores = [_nt(cm[g], bm[g]) for g in range(M2_G)]
    outs, s_new = [], []
    for h in range(M2_H):
        g = h // hpg
        s_h = s[h * M2_N:(h + 1) * M2_N]
        b_col = _lane_col(b_all, h)
        bl = _lane_col(bl_all, h)
        b_row = jnp.sum(jnp.where(row_id == h, b_t, 0.0), axis=0, keepdims=True)
        xv = xs[:, h * LANES:(h + 1) * LANES] * _lane_col(dt, h)
        outs.append(jnp.exp(b_col) * _nn(cm[g], s_h) + _nn(scores[g] * _masked_exp(b_col - b_row, m), xv))
        s_new.append(jnp.exp(bl) * s_h + _tn(bm[g] * jnp.exp(bl - b_col), xv))
    return jnp.concatenate(outs, axis=1), jnp.concatenate(s_new, axis=0), None


def _na_tile(q, kw, vw, kc, vc, bias):
    qs = q * (NA_D ** -0.5)
    s1 = _nt(qs, kw) + bias
    s2 = _nt(qs, kc)
    mx = lax.stop_gradient(jnp.maximum(jnp.max(s1, axis=-1, keepdims=True), jnp.max(s2, axis=-1, keepdims=True)))
    p1 = jnp.exp(s1 - mx)
    p2 = jnp.exp(s2 - mx)
    den = jnp.sum(p1, axis=-1, keepdims=True) + jnp.sum(p2, axis=-1, keepdims=True)
    return (_nn(p1, vw) + _nn(p2, vc)) / den


def _ctx_tile(q, k, v):
    s = _nt(q * (NA_D ** -0.5), k)
    p = jnp.exp(s - lax.stop_gradient(jnp.max(s, axis=-1, keepdims=True)))
    return _nn(p, v) / jnp.sum(p, axis=-1, keepdims=True)


def natten(q, k, v, bias, *, n_lat, name):
    n_rows = q.shape[0]
    n_ctx = n_rows - n_lat
    g_rows = n_lat // GRID_W
    win = NA_WIN_R * GRID_W
    ctx_blk = n_lat // n_ctx

    def start(n):
        return jnp.clip(n - NA_WIN_R // 2, 0, g_rows - NA_WIN_R)

    def case(n):
        return n - start(n)

    q_spec = pl.BlockSpec((GRID_W, LANES), lambda h, n: (n, h))
    lat_spec = pl.BlockSpec((n_lat, LANES), lambda h, n: (0, h))
    ctx_in = pl.BlockSpec((n_ctx, LANES), lambda h, n: (ctx_blk, h))
    ctx_out = pl.BlockSpec((n_ctx, LANES), lambda h, n: (0, h))
    bias_spec = pl.BlockSpec((None, None, GRID_W, win), lambda h, n: (h, case(n), 0, 0))
    lat_sds = jax.ShapeDtypeStruct((n_lat, BRANCH_W), F32)
    ctx_sds = jax.ShapeDtypeStruct((n_ctx, BRANCH_W), F32)

    def lat_fwd(q, k, v, bias):
        def body(q_ref, k_ref, v_ref, kc_ref, vc_ref, b_ref, o_ref):
            r0 = pl.multiple_of(start(pl.program_id(1)) * GRID_W, GRID_W)
            o_ref[...] = _na_tile(q_ref[...], k_ref[pl.ds(r0, win), :], v_ref[pl.ds(r0, win), :],
                                  kc_ref[...], vc_ref[...], b_ref[...])

        return pl.pallas_call(body, name=name + "_lat_fwd", grid=(NA_H, g_rows),
                              in_specs=[q_spec, lat_spec, lat_spec, ctx_in, ctx_in, bias_spec], out_specs=q_spec,
                              out_shape=lat_sds, compiler_params=_cparams("parallel", "arbitrary"))(q, k, v, k, v, bias)

    def lat_bwd(q, k, v, bias, g):
        def body(q_ref, k_ref, v_ref, kc_ref, vc_ref, b_ref, g_ref, dq_ref, dk_ref, dv_ref, dkc_ref, dvc_ref, db_ref):
            n = pl.program_id(1)
            r0 = pl.multiple_of(start(n) * GRID_W, GRID_W)

            @pl.when(n == 0)
            def _():
                for r in (dk_ref, dv_ref, dkc_ref, dvc_ref):
                    r[...] = jnp.zeros_like(r)

            @pl.when((n == 0) | (case(n) != case(jnp.maximum(n - 1, 0))))
            def _():
                db_ref[...] = jnp.zeros_like(db_ref)

            _, vjp = jax.vjp(_na_tile, q_ref[...], k_ref[pl.ds(r0, win), :], v_ref[pl.ds(r0, win), :],
                             kc_ref[...], vc_ref[...], b_ref[...])
            dq, dkw, dvw, dkc, dvc, db = vjp(g_ref[...])
            dq_ref[...] = dq
            dk_ref[pl.ds(r0, win), :] += dkw
            dv_ref[pl.ds(r0, win), :] += dvw
            dkc_ref[...] += dkc
            dvc_ref[...] += dvc
            db_ref[...] += db

        return pl.pallas_call(
            body, name=name + "_lat_bwd", grid=(NA_H, g_rows),
            in_specs=[q_spec, lat_spec, lat_spec, ctx_in, ctx_in, bias_spec, q_spec],
            out_specs=(q_spec, lat_spec, lat_spec, ctx_out, ctx_out, bias_spec),
            out_shape=(lat_sds, lat_sds, lat_sds, ctx_sds, ctx_sds, jax.ShapeDtypeStruct(bias.shape, F32)),
            compiler_params=_cparams("parallel", "arbitrary"))(q, k, v, k, v, bias, g)

    c_in = pl.BlockSpec((n_ctx, LANES), lambda h: (ctx_blk, h))
    c_out = pl.BlockSpec((n_ctx, LANES), lambda h: (0, h))

    def ctx_fwd(q, k, v):
        def body(q_ref, k_ref, v_ref, o_ref):
            o_ref[...] = _ctx_tile(q_ref[...], k_ref[...], v_ref[...])

        return pl.pallas_call(body, name=name + "_ctx_fwd", grid=(NA_H,), in_specs=[c_in, c_in, c_in], out_specs=c_out,
                              out_shape=ctx_sds, compiler_params=_cparams("parallel"))(q, k, v)

    def ctx_bwd(q, k, v, g):
        def body(q_ref, k_ref, v_ref, g_ref, dq_ref, dk_ref, dv_ref):
            _, vjp = jax.vjp(_ctx_tile, q_ref[...], k_ref[...], v_ref[...])
            dq_ref[...], dk_ref[...], dv_ref[...] = vjp(g_ref[...])

        return pl.pallas_call(body, name=name + "_ctx_bwd", grid=(NA_H,), in_specs=[c_in, c_in, c_in, c_out],
                              out_specs=(c_out, c_out, c_out), out_shape=(ctx_sds, ctx_sds, ctx_sds),
                              compiler_params=_cparams("parallel"))(q, k, v, g)

    @jax.custom_vjp
    def op(q, k, v, bias):
        return jnp.concatenate([lat_fwd(q, k, v, bias), ctx_fwd(q, k, v)], axis=0)

    def op_bwd(res, g):
        q, k, v, bias = res
        dq, dk, dv, dkc, dvc, db = lat_bwd(q, k, v, bias, g[:n_lat])
        dqc, dkc2, dvc2 = ctx_bwd(q, k, v, g[n_lat:])
        return (jnp.concatenate([dq, dqc], axis=0), jnp.concatenate([dk, dkc + dkc2], axis=0),
                jnp.concatenate([dv, dvc + dvc2], axis=0), db)

    op.defvjp(lambda q, k, v, bias: (op(q, k, v, bias), (q, k, v, bias)), op_bwd)
    return op(q, k, v, bias)


def _runs(src):
    src = np.asarray(src)
    out, i = [], 0
    while i < len(src):
        j = i + 1
        if src[i] < 0:
            while j < len(src) and src[j] < 0:
                j += 1
            out.append((-1, j - i))
        else:
            while j < len(src) and src[j] == src[j - 1] + 1:
                j += 1
            out.append((int(src[i]), j - i))
        i = j
    return out


def _take_cols(w, src):
    pieces = [jnp.zeros(w.shape[:-1] + (ln,), w.dtype) if s < 0 else w[..., s:s + ln] for s, ln in _runs(src)]
    return pieces[0] if len(pieces) == 1 else jnp.concatenate(pieces, axis=-1)


def _take_cols_slabs(slabs, src):
    width = slabs[0].shape[-1]
    pieces = []
    for s, ln in _runs(src):
        if s < 0:
            pieces.append(jnp.zeros(slabs[0].shape[:-1] + (ln,), slabs[0].dtype))
        while s >= 0 and ln > 0:
            off = s % width
            take = min(ln, width - off)
            pieces.append(slabs[s // width][..., off:off + take])
            s, ln = s + take, ln - take
    return pieces[0] if len(pieces) == 1 else jnp.concatenate(pieces, axis=-1)


def _untake_cols(parts, n_cols, n_slabs=1):
    found = []
    for arr, src in parts:
        pos = 0
        for s, ln in _runs(src):
            if s >= 0:
                found.append((s, ln, arr, pos))
            pos += ln
    found.sort(key=lambda t: t[0])
    width = n_cols // n_slabs
    slabs, at = [[] for _ in range(n_slabs)], 0
    for s, ln, arr, pos in found:
        assert s == at, (s, at)
        at += ln
        while ln > 0:
            take = min(ln, width - s % width)
            slabs[s // width].append(arr[..., pos:pos + take])
            s, pos, ln = s + take, pos + take, ln - take
    assert at == n_cols, (at, n_cols)
    return [jnp.concatenate(p, axis=-1) for p in slabs]


def _pad_heads(base, heads, real, width):
    return np.concatenate([np.concatenate([base + h * real + np.arange(real), -np.ones(width - real, np.int64)])
                           for h in range(heads)])


def _rope_heads(base, heads):
    z = -np.ones(32, np.int64)
    return np.concatenate([np.concatenate([base + h * 64 + np.arange(32), z, base + h * 64 + 32 + np.arange(32), z])
                           for h in range(heads)])


def _lane_block(base, n):
    return np.concatenate([base + np.arange(n), -np.ones(LANES - n, np.int64)])


def _in_groups():
    g0, n0, d0, m0, t0 = 0, 1568, 3104, 5168, 6720
    rng = lambda a, n: a + np.arange(n)
    return [
        ("gla_qk", np.concatenate([_rope_heads(g0, GLA_H), _rope_heads(g0 + 256, GLA_H)])),
        ("gla_v", rng(g0 + 512, 512)),
        ("gla_g", rng(g0 + 1024, 512)),
        ("gla_lr", _lane_block(g0 + 1536, 2 * GLA_LR)),
        ("na_q", rng(n0, 512)), ("na_k", rng(n0 + 512, 512)), ("na_v", rng(n0 + 1024, 512)),
        ("gdn_qkv", rng(d0, 1536)),
        ("gdn_z", rng(d0 + 1536, 512)),
        ("gdn_sm", np.concatenate([_lane_block(d0 + 2048 + 4 * i, GDN_H) for i in range(4)])),
        ("m2_z", _pad_heads(m0, M2_H, M2_P, LANES)),
        ("m2_xs", _pad_heads(m0 + 512, M2_H, M2_P, LANES)),
        ("m2_bc", rng(m0 + 1024, 512)),
        ("m2_dt", np.concatenate([_lane_block(m0 + 1536, M2_H), _lane_block(m0 + 1536 + M2_H, M2_H)])),
        ("gate", rng(t0, 4 * D_MODEL)),
    ]


_M2_PAD = _pad_heads(0, M2_H, M2_P, LANES)
_GLA_PAD = _rope_heads(0, GLA_H)


def _row3(v):
    return v.reshape((1, 1, -1))


def _dir_rows(p, n):
    return _row3(jnp.concatenate([_take_cols(p[d][None], _lane_block(0, n)) for d in range(2)], axis=-1))


def _rope_tables(n_lat, n_ctx):
    n_freq = GLA_DK // 4
    freqs = ROPE_BASE ** (-jnp.arange(n_freq, dtype=F32) / n_freq)
    t = jnp.arange(n_lat)
    row = (t // GRID_W).astype(F32)
    colv = (t % GRID_W).astype(F32)
    ang = jnp.concatenate([row[:, None] * freqs, colv[:, None] * freqs], axis=-1)
    c, s = jnp.cos(ang), jnp.sin(ang)
    one, zero = jnp.ones_like(c), jnp.zeros_like(c)
    cos_t = jnp.concatenate([c, one, c, one], axis=-1)
    sin_t = jnp.concatenate([-s, zero, s, zero], axis=-1)
    return (jnp.concatenate([cos_t, jnp.ones((n_ctx, LANES), F32)], axis=0),
            jnp.concatenate([sin_t, jnp.zeros((n_ctx, LANES), F32)], axis=0))


def _na_bias(rpb):
    case = np.arange(NA_WIN_R)
    r = np.arange(NA_WIN_R)
    dr = r[None, :] - case[:, None] + NA_WIN_R - 1
    ci = np.arange(GRID_W)
    dc = np.clip(ci[None, :] - ci[:, None], 1 - NA_WIN_C, NA_WIN_C - 1) + NA_WIN_C - 1
    c0 = np.clip(ci - NA_WIN_C // 2, 0, GRID_W - NA_WIN_C)
    ok = (ci[None, :] >= c0[:, None]) & (ci[None, :] < c0[:, None] + NA_WIN_C)
    pick_r = np.zeros((NA_WIN_R, NA_WIN_R, 2 * NA_WIN_R - 1), np.float32)
    pick_r[case[:, None], r[None, :], dr] = 1.0
    pick_c = np.zeros((2 * NA_WIN_C - 1, GRID_W, GRID_W), np.float32)
    pick_c[dc, ci[:, None], ci[None, :]] = 1.0
    rows = jnp.einsum("hdk,crd->hcrk", rpb, pick_r, precision=HI)
    tbl = jnp.einsum("hcrk,kij->hcirj", rows, pick_c, precision=HI)
    tbl = jnp.where(ok[None, None, :, None, :], tbl, NEG_INF)
    return tbl.reshape((NA_H, NA_WIN_R, GRID_W, NA_WIN_R * GRID_W))


def _layer(l, xs, mod, small, slots, gath, win, tables, *, n_lat):
    rw = functools.partial(rowwise, n_lat=n_lat)
    rwb = functools.partial(rowwise, n_lat=n_lat, out_dtype=BF16)
    nm = lambda s: "l%d_%s" % (l, s)
    sl = slots[l]
    sh1, sc1, g1, sh2, sc2, g2 = [mod[:, i * D_MODEL:(i + 1) * D_MODEL].reshape((2, 1, D_MODEL)) for i in range(6)]
    (h,) = rwb(_f_modnorm, [xs], [], [_row3(small["norm1_g"][l]), sc1, sh1], out_widths=[D_MODEL], tile=256,
              name=nm("norm1"))
    p = {g: linear(h, win[l][g], sl["in_" + g], name=nm("in_" + g), out_dtype=BF16 if g == "gate" else F32)
         for g, _ in _in_groups()}

    a2 = small["gla_a2"][l]
    a2p = jnp.concatenate([
        jnp.concatenate([_take_cols(a2[0], _GLA_PAD), jnp.zeros((GLA_LR, 512), F32)], axis=1),
        jnp.concatenate([jnp.zeros((GLA_LR, 512), F32), _take_cols(a2[1], _GLA_PAD)], axis=1),
        jnp.zeros((LANES - 2 * GLA_LR, 1024), F32)], axis=0)[None]
    abp = _row3(jnp.concatenate([_take_cols(small["gla_ab"][l][d][None], _GLA_PAD) for d in range(2)], axis=-1))
    (la,) = rw(_f_gla_prep, [p["gla_lr"]], [], [a2p, abp], out_widths=[1024], tile=256, name=nm("gla_prep"))
    whole = lambda off, width: (off, width, 1)
    o = chunk_scan(_gla_step, [p["gla_qk"], p["gla_v"]], [[whole(0, BRANCH_W), whole(BRANCH_W, BRANCH_W)], [whole(0, BRANCH_W)]],
                   [la], [[whole(0, BRANCH_W)]], tables, heads=1, state_shape=(GLA_H * GLA_DV, LANES), out_w=BRANCH_W,
                   n_lat=n_lat, name=nm("gla_scan"))
    (ya,) = rwb(_f_headnorm_gate, [o, p["gla_g"]], [], [_row3(small["gla_norm_g"][l])], out_widths=[BRANCH_W], tile=256,
               name=nm("gla_fin"))

    yb = natten(p["na_q"], p["na_k"], p["na_v"], _na_bias(small["na_rpb"][l]), n_lat=n_lat, name=nm("na"))

    cq = conv_silu(p["gdn_qkv"], small["gdn_conv"][l], jnp.zeros((1, 3 * BRANCH_W), F32), n_lat=n_lat, name=nm("gdn_conv"))
    beta, la = rw(_f_gdn_prep, [p["gdn_sm"]], [], [_dir_rows(small["gdn_a_log"][l], GDN_H), _dir_rows(small["gdn_dt_bias"][l], GDN_H)],
                  out_widths=[256, 256], tile=256, name=nm("gdn_prep"))
    o = chunk_scan(_gdn_step, [cq], [[whole(0, 512), whole(512, 512), whole(1024, 512)]], [beta, la],
                   [[whole(0, LANES)], [whole(0, LANES)]], [], heads=1, state_shape=(GDN_H * GDN_D, GDN_D),
                   out_w=BRANCH_W, n_lat=n_lat, name=nm("gdn_scan"), keep_shape=(GDN_H * CHUNK, GDN_H * CHUNK))
    (yc,) = rwb(_f_headnorm_gate, [o, p["gdn_z"]], [], [_row3(small["gdn_norm_g"][l])], out_widths=[BRANCH_W], tile=256,
               name=nm("gdn_fin"))

    cw, cb = small["m2_conv"][l], small["m2_conv_b"][l][None]
    cxs = conv_silu(p["m2_xs"], _take_cols(cw[:, :512], _M2_PAD), _take_cols(cb[:, :512], _M2_PAD), n_lat=n_lat,
                    name=nm("m2_conv_x"))
    cbc = conv_silu(p["m2_bc"], cw[:, 512:], cb[:, 512:], n_lat=n_lat, name=nm("m2_conv_bc"))
    dt, la = rw(_f_m2_prep, [p["m2_dt"]], [], [_dir_rows(small["m2_a_log"][l], M2_H), _dir_rows(small["m2_dt_bias"][l], M2_H)],
                out_widths=[256, 256], tile=256, name=nm("m2_prep"))
    o = chunk_scan(_ssd_step, [cxs, cbc], [[whole(0, 2 * BRANCH_W)], [whole(0, BRANCH_W)]], [dt, la],
                   [[whole(0, LANES)], [whole(0, LANES)]], [], heads=1, state_shape=(M2_H * M2_N, LANES),
                   out_w=2 * BRANCH_W, n_lat=n_lat, name=nm("m2_scan"))
    dskip = _row3(jnp.repeat(small["m2_d"][l], LANES))
    (yd,) = rwb(_f_m2_fin, [o, p["m2_z"], cxs], [], [dskip, _row3(_take_cols(small["m2_norm_g"][l][None], _M2_PAD))],
               out_widths=[2 * BRANCH_W], tile=128, name=nm("m2_fin"))

    wb = gath["w_branch"]
    zs = [linear(y, wb, sl["w_branch%d" % i], name=nm("branch%d" % i), layout="col", prefix=(l, i), out_dtype=BF16)
          for i, y in enumerate((ya, yb, yc))]
    wb3 = wb[:, l, 3].reshape((N_SLOT, M2_H, M2_P, BRANCH_W))
    wb3 = jnp.pad(wb3, ((0, 0), (0, 0), (0, LANES - M2_P), (0, 0))).reshape((N_SLOT, 2 * BRANCH_W, BRANCH_W))
    zs.append(linear(yd, wb3, sl["w_branch3"], name=nm("branch3"), layout="col", out_dtype=BF16))
    (merged,) = rwb(_f_merge, [p["gate"]] + zs, [], [_row3(small["b_merge"][l].reshape(-1))], out_widths=[D_MODEL], tile=64,
                   name=nm("merge"))
    y = linear(merged, gath["w_out"], sl["w_out"], name=nm("out"), layout="row", prefix=(l,))
    (x1,) = rw(_f_resid, [xs, y], [], [g1], out_widths=[D_MODEL], tile=256, name=nm("res1"))

    (h2,) = rwb(_f_modnorm, [x1], [], [_row3(small["norm2_g"][l]), sc2, sh2], out_widths=[D_MODEL], tile=256,
               name=nm("norm2"))
    u1 = linear(h2, gath["w_ffn1"], sl["w_ffn1"], name=nm("ffn1"), layout="col", prefix=(l,), out_dtype=BF16)
    u3 = linear(h2, gath["w_ffn3"], sl["w_ffn3"], name=nm("ffn3"), layout="col", prefix=(l,), out_dtype=BF16)
    (act,) = rwb(_f_swiglu, [u1, u3], [], [], out_widths=[D_FF], tile=128, name=nm("swiglu"))
    f = linear(act, gath["w_ffn2"], sl["w_ffn2"], name=nm("ffn2"), layout="row", prefix=(l,))
    (x2,) = rw(_f_resid, [x1, f], [], [g2], out_widths=[D_MODEL], tile=256, name=nm("res2"))
    return x2


def _slot_shapes():
    s = {"w_out": (N_SLOT, D_MODEL // N_SLOT, D_MODEL),
         "w_ffn1": (N_SLOT, D_MODEL, D_FF // N_SLOT), "w_ffn3": (N_SLOT, D_MODEL, D_FF // N_SLOT),
         "w_ffn2": (N_SLOT, D_FF // N_SLOT, D_MODEL), "w_branch3": (N_SLOT, 2 * BRANCH_W, BRANCH_W)}
    for i in range(3):
        s["w_branch%d" % i] = (N_SLOT, BRANCH_W, BRANCH_W)
    for g, src in _in_groups():
        s["in_" + g] = (D_MODEL, len(src))
    return s


ADA_ROWS = 2 * SUBLANES


def ada_shard(c_all, c_ctx, w_ada, slots):
    cc = jnp.concatenate([c_all, c_ctx[None], jnp.zeros((ADA_ROWS - c_all.shape[0] - 1, D_MODEL), F32)], axis=0)
    (act,) = rowwise(_f_silu, [cc], [], [], out_widths=[D_MODEL], tile=ADA_ROWS, n_lat=ADA_ROWS, name="ada_silu",
                     out_dtype=BF16)
    return [linear(act, w_ada, slots[l], name="l%d_ada" % l, prefix=(l,)) for l in range(DEPTH)]


def _local_loss(diff, fixed, *, n_lat):
    small = diff["small"]
    n_ctx = fixed["ctx"].shape[0]
    xs = jnp.concatenate([diff["x"], fixed["ctx"]], axis=0)
    tables = _rope_tables(n_lat, n_ctx)
    for l in range(DEPTH):
        xs = _layer(l, xs, diff["mod"][l], small, diff["slots"], fixed["gath"], fixed["win"], tables, n_lat=n_lat)
    (lrow,) = rowwise(_f_loss, [xs[:n_lat]], [fixed["target"]], [_row3(small["final_norm_g"])], out_widths=[LANES],
                      tile=256, n_lat=n_lat, name="loss")
    return jnp.sum(lrow)


def _place():
    x, y, c = lax.axis_index("x"), lax.axis_index("y"), lax.axis_index("c")
    chips = [(1 - x, y), (x, 1 - y), (1 - x, 1 - y)]
    return x, y, c, (x, y, 1 - c), chips


def _remote(src, dst, send_sem, recv_sem, dev):
    return pltpu.make_async_remote_copy(src_ref=src, dst_ref=dst, send_sem=send_sem, recv_sem=recv_sem,
                                        device_id=dev, device_id_type=MESH)


def _dma_sems(*shape):
    return pltpu.SemaphoreType.DMA(shape)


def place_shard(w, slot, *, name):
    depth, k, n = w.shape
    tr = _tile(k, max(2 * SUBLANES, (1 << 19) // n // (2 * SUBLANES) * (2 * SUBLANES)), 2 * SUBLANES)

    def body(s_ref, w_ref, o_ref):
        o_ref[...] = w_ref[...].astype(o_ref.dtype)

    return pl.pallas_call(
        body, name=name,
        grid_spec=pltpu.PrefetchScalarGridSpec(
            num_scalar_prefetch=1, grid=(depth, k // tr),
            in_specs=[pl.BlockSpec((None, tr, n), lambda l, i, s: (l, i, 0))],
            out_specs=pl.BlockSpec((None, None, tr, n), lambda l, i, s: (s[0], l, i, 0))),
        out_shape=jax.ShapeDtypeStruct((N_SLOT, depth, k, n), BF16),
        compiler_params=_cparams("parallel", "parallel"))(slot, w)


def gather_weights(bufs):
    n = len(bufs)

    def body(*refs):
        o = refs[n:2 * n]
        send1, recv1, send2, recv2 = refs[2 * n:]
        x, y, c, sibling, chips = _place()
        g = 2 * x + y
        sent = []
        for k in range(n):
            for j, (cx, cy) in enumerate(chips):
                cp = _remote(o[k].at[g, c], o[k].at[g, c], send1.at[k, j], recv1.at[k, j], (cx, cy, c))
                cp.start()
                sent.append(cp)
        for k in range(n):
            for j, (cx, cy) in enumerate(chips):
                gj = 2 * cx + cy
                _remote(o[k].at[g, c], o[k].at[gj, c], send1.at[k, j], recv1.at[k, j], (cx, cy, c)).wait_recv()
                cp = _remote(o[k].at[gj, c], o[k].at[gj, c], send2.at[k, j], recv2.at[k, j], sibling)
                cp.start()
                sent.append(cp)
        for k in range(n):
            for j, (cx, cy) in enumerate(chips):
                gj = 2 * cx + cy
                _remote(o[k].at[gj, 1 - c], o[k].at[gj, 1 - c], send2.at[k, j], recv2.at[k, j], sibling).wait_recv()
        for cp in sent:
            cp.wait_send()

    return pl.pallas_call(
        body, name="gather_weights", in_specs=[ANY] * n, out_specs=[ANY] * n,
        out_shape=[jax.ShapeDtypeStruct(b.shape, b.dtype) for b in bufs],
        input_output_aliases={k: k for k in range(n)},
        scratch_shapes=[_dma_sems(n, 3), _dma_sems(n, 3), _dma_sems(n, 3), _dma_sems(n, 3)],
    )(*bufs)


def allgather_small(buf, *, name):
    m_per = buf.shape[0]

    def body(x_ref, out_ref, send_sems, recv_sems, local_sem):
        x, y, c, sibling, chips = _place()
        me = (x, y, c)

        def rows(px, py, pc):
            return out_ref.at[pl.ds((4 * px + 2 * py + pc) * m_per, m_per), :]

        def copy(k, block, to, src=None):
            return _remote(rows(*block) if src is None else src, rows(*block), send_sems.at[k], recv_sems.at[k], to)

        mine = pltpu.make_async_copy(x_ref, rows(*me), local_sem)
        mine.start()
        first = [copy(0, me, sibling, src=x_ref)]
        first += [copy(1 + j, me, (*chip, c), src=x_ref) for j, chip in enumerate(chips)]
        for cp in first:
            cp.start()
        passed = [copy(4 + j, (*chip, c), sibling) for j, chip in enumerate(chips)]
        for j, chip in enumerate(chips):
            copy(1 + j, (*chip, c), me).wait_recv()
            passed[j].start()
        copy(0, sibling, me).wait_recv()
        for j, chip in enumerate(chips):
            copy(4 + j, (*chip, 1 - c), me).wait_recv()
        for cp in first + passed:
            cp.wait_send()
        mine.wait()

    return pl.pallas_call(
        body, name=name, out_shape=jax.ShapeDtypeStruct((8 * m_per, LANES), buf.dtype),
        in_specs=[pl.BlockSpec(memory_space=pltpu.VMEM)], out_specs=pl.BlockSpec(memory_space=pltpu.VMEM),
        scratch_shapes=[_dma_sems(7), _dma_sems(7), pltpu.SemaphoreType.DMA],
        compiler_params=pltpu.CompilerParams(vmem_limit_bytes=VMEM_LIMIT),
    )(buf)


def sum_blocks(stacked, n_blocks, *, name):
    m = stacked.shape[0] // n_blocks
    width = stacked.shape[1]
    x3 = stacked.reshape((n_blocks, m, width))
    tr = _tile(m, max(SUBLANES, (1 << 18) // width // SUBLANES * SUBLANES), SUBLANES)

    def body(x_ref, o_ref):
        acc = x_ref[0]
        for s in range(1, n_blocks):
            acc = acc + x_ref[s]
        o_ref[...] = acc

    return pl.pallas_call(body, name=name, grid=(m // tr,), in_specs=[pl.BlockSpec((n_blocks, tr, width), lambda i: (0, i, 0))],
                          out_specs=pl.BlockSpec((tr, width), lambda i: (i, 0)),
                          out_shape=jax.ShapeDtypeStruct((m, width), F32), compiler_params=_cparams("parallel"))(x3)


def reduce_pair(gs):
    n = len(gs)

    def body(*refs):
        g, r = refs[:n], refs[n:2 * n]
        send, recv = refs[2 * n:]
        x, y, c, sibling, _ = _place()
        cps = []
        for i in range(n):
            k2 = gs[i].shape[1] // 2
            cp = _remote(g[i].at[:, pl.ds((1 - c) * k2, k2), :], r[i], send.at[i], recv.at[i], sibling)
            cp.start()
            cps.append(cp)
        for cp in cps:
            cp.wait()

    return pl.pallas_call(
        body, name="reduce_pair", in_specs=[ANY] * n, out_specs=[ANY] * n,
        out_shape=[jax.ShapeDtypeStruct((g.shape[0], g.shape[1] // 2, g.shape[2]), g.dtype) for g in gs],
        scratch_shapes=[_dma_sems(n), _dma_sems(n)],
    )(*gs)


def _row_tile(rows, width, budget, mult):
    return _tile(rows, max(mult, budget // width // mult * mult), mult)


def add_own_half(g, recv, core, *, name):
    n_slot, k2, width = recv.shape
    tr = _row_tile(k2, width, 1 << 19, 2 * SUBLANES)
    nb = k2 // tr

    def body(c_ref, g_ref, r_ref, o_ref):
        o_ref[...] = (g_ref[...] + r_ref[...]).astype(o_ref.dtype)

    spec = pl.BlockSpec((None, tr, width), lambda s, i, c: (s, i, 0))
    return pl.pallas_call(
        body, name=name,
        grid_spec=pltpu.PrefetchScalarGridSpec(
            num_scalar_prefetch=1, grid=(n_slot, nb),
            in_specs=[pl.BlockSpec((None, tr, width), lambda s, i, c: (s, c[0] * nb + i, 0)), spec], out_specs=spec),
        out_shape=jax.ShapeDtypeStruct(recv.shape, BF16), compiler_params=_cparams("parallel", "parallel"))(core, g, recv)


def reduce_chips(qs):
    n = len(qs)

    def body(*refs):
        q, r = refs[:n], refs[n:2 * n]
        send, recv = refs[2 * n:]
        x, y, c, _, chips = _place()
        cps = []
        for i in range(n):
            for j, (cx, cy) in enumerate(chips):
                cp = _remote(q[i].at[2 * cx + cy], r[i].at[j], send.at[i, j], recv.at[i, j], (cx, cy, c))
                cp.start()
                cps.append(cp)
        for cp in cps:
            cp.wait()

    return pl.pallas_call(
        body, name="reduce_chips", in_specs=[ANY] * n, out_specs=[ANY] * n,
        out_shape=[jax.ShapeDtypeStruct((3,) + q.shape[1:], q.dtype) for q in qs],
        scratch_shapes=[_dma_sems(n, 3), _dma_sems(n, 3)],
    )(*qs)


def chip_sum(q, recv, slot, *, name):
    _, k2, width = recv.shape
    tr = _row_tile(k2, width, 1 << 18, 2 * SUBLANES)

    def body(s_ref, q_ref, r_ref, o_ref):
        acc = q_ref[...].astype(F32)
        for j in range(3):
            acc = acc + r_ref[j].astype(F32)
        o_ref[...] = acc

    return pl.pallas_call(
        body, name=name,
        grid_spec=pltpu.PrefetchScalarGridSpec(
            num_scalar_prefetch=1, grid=(k2 // tr,),
            in_specs=[pl.BlockSpec((None, tr, width), lambda i, s: (s[0], i, 0)),
                      pl.BlockSpec((3, tr, width), lambda i, s: (0, i, 0))],
            out_specs=pl.BlockSpec((tr, width), lambda i, s: (i, 0))),
        out_shape=jax.ShapeDtypeStruct((k2, width), F32), compiler_params=_cparams("parallel"))(slot, q, recv)


def swap_pair(rs):
    n = len(rs)

    def body(*refs):
        r, o = refs[:n], refs[n:2 * n]
        send, recv = refs[2 * n:]
        x, y, c, sibling, _ = _place()
        cps = []
        for i in range(n):
            cp = _remote(r[i], o[i], send.at[i], recv.at[i], sibling)
            cp.start()
            cps.append(cp)
        for cp in cps:
            cp.wait()

    return pl.pallas_call(
        body, name="swap_pair", in_specs=[ANY] * n, out_specs=[ANY] * n,
        out_shape=[jax.ShapeDtypeStruct(r.shape, r.dtype) for r in rs],
        scratch_shapes=[_dma_sems(n), _dma_sems(n)],
    )(*rs)


def adamw(w, g, m, v, *, name):
    rows, width = w.shape
    tr = _tile(rows, max(SUBLANES, (1 << 19) // width // SUBLANES * SUBLANES), SUBLANES)

    def body(w_ref, g_ref, m_ref, v_ref, d_ref, mo_ref, vo_ref):
        gv = g_ref[...]
        mn = ADAM_B1 * m_ref[...] + (1.0 - ADAM_B1) * gv
        vn = ADAM_B2 * v_ref[...] + (1.0 - ADAM_B2) * (gv * gv)
        m_hat = mn / (1.0 - ADAM_B1 ** ADAM_STEP)
        v_hat = vn / (1.0 - ADAM_B2 ** ADAM_STEP)
        d_ref[...] = -ADAM_LR * (m_hat / (jnp.sqrt(v_hat) + ADAM_EPS) + ADAM_WD * w_ref[...])
        mo_ref[...] = mn
        vo_ref[...] = vn

    spec = pl.BlockSpec((tr, width), lambda i: (i, 0))
    sds = jax.ShapeDtypeStruct((rows, width), F32)
    return pl.pallas_call(body, name=name, grid=(rows // tr,), in_specs=[spec] * 4, out_specs=(spec,) * 3,
                          out_shape=(sds,) * 3, compiler_params=_cparams("parallel"))(w, g, m, v)


def _pack(arrs):
    flat = jnp.concatenate([a.reshape(-1) for a in arrs])
    pad = (-flat.shape[0]) % (SUBLANES * LANES)
    return jnp.pad(flat, (0, pad)).reshape((-1, LANES))


def _unpack(buf, shapes):
    flat, out, at = buf.reshape(-1), [], 0
    for s in shapes:
        size = int(np.prod(s))
        out.append(flat[at:at + size].reshape(s))
        at += size
    return out


BIG = ["w_in", "w_branch", "w_out", "w_ffn1", "w_ffn3", "w_ffn2"]
SMALL_SHARDED = ["b_merge", "gla_a2", "gla_ab", "gdn_conv", "m2_conv"]
SMALL_WHOLE = ["norm1_g", "norm2_g", "gla_norm_g", "na_rpb", "gdn_a_log", "gdn_dt_bias", "gdn_norm_g",
               "m2_conv_b", "m2_a_log", "m2_dt_bias", "m2_d", "m2_norm_g", "final_norm_g"]
WEIGHTS = ["c_ctx", "norm1_g", "norm2_g", "w_ada", "b_ada", "w_in", "b_merge", "gla_a2", "gla_ab", "gla_norm_g", "na_rpb",
           "gdn_conv", "gdn_a_log", "gdn_dt_bias", "gdn_norm_g", "m2_conv", "m2_conv_b", "m2_a_log", "m2_dt_bias", "m2_d",
           "m2_norm_g", "w_branch", "w_out", "w_ffn1", "w_ffn3", "w_ffn2", "final_norm_g"]


def _step(a):
    n_lat = a["x"].shape[1]
    x_i, y_i, c_i = lax.axis_index("x"), lax.axis_index("y"), lax.axis_index("c")
    slot = 2 * x_i + y_i

    slot_arr = slot.astype(jnp.int32).reshape((1,))
    core = c_i.astype(jnp.int32).reshape((1,))
    placed = [place_shard(a[n].reshape((DEPTH, -1, a[n].shape[-1])), slot_arr, name="place_" + n) for n in BIG]
    gath = dict(zip(BIG, gather_weights(placed)))
    gath["w_branch"] = gath["w_branch"].reshape((N_SLOT, DEPTH, 4, BRANCH_W, BRANCH_W))
    shard_shapes = [a[n].shape for n in SMALL_SHARDED]
    own = _pack([a[n] for n in SMALL_SHARDED])
    everyone = allgather_small(own, name="gather_small").reshape((8,) + own.shape)
    per_slot = [_unpack(everyone[2 * s], shard_shapes) for s in range(N_SLOT)]
    small = {n: jnp.concatenate([per_slot[s][i] for s in range(N_SLOT)], axis=-1) for i, n in enumerate(SMALL_SHARDED)}
    small.update({n: a[n] for n in SMALL_WHOLE})

    me = 4 * x_i + 2 * y_i + c_i
    ada_cols = a["w_ada"].shape[-1]
    c_all = allgather_small(a["c"].reshape((-1, LANES)), name="gather_c").reshape((8, D_MODEL))
    ada_slots = [jnp.zeros(a["w_ada"].shape[1:], F32) for _ in range(DEPTH)]
    mod_shards, ada_vjp = jax.vjp(lambda c_ctx, sl: ada_shard(c_all, c_ctx, a["w_ada"], sl), a["c_ctx"], ada_slots)
    packed = _pack(mod_shards)
    every = allgather_small(packed, name="gather_ada").reshape((8,) + packed.shape)
    by_slot = [_unpack(every[2 * s], [(ADA_ROWS, ada_cols)] * DEPTH) for s in range(N_SLOT)]
    mod = []
    for l in range(DEPTH):
        rows = jnp.concatenate([by_slot[s][l] for s in range(N_SLOT)], axis=-1) + a["b_ada"][l]
        mod.append(jnp.concatenate([lax.dynamic_slice_in_dim(rows, me, 1, axis=0), rows[8:9]], axis=0))

    groups = _in_groups()
    win = []
    for l in range(DEPTH):
        slabs = [gath["w_in"][s, l] for s in range(N_SLOT)]
        win.append({g: _take_cols_slabs(slabs, src) for g, src in groups})
    slots = [{n: jnp.zeros(s, F32) for n, s in _slot_shapes().items()} for _ in range(DEPTH)]
    diff = {"x": a["x"][0], "mod": mod, "small": small, "slots": slots}
    fixed = {"ctx": a["ctx"][0], "target": a["loss_target"][0], "gath": gath, "win": win}
    loss, grads = jax.value_and_grad(lambda d: _local_loss(d, fixed, n_lat=n_lat))(diff)

    dmod = _pack(grads["mod"])
    every = allgather_small(dmod, name="gather_dmod").reshape((8,) + dmod.shape)
    per_dev = [_unpack(every[i], [(2, 6 * D_MODEL)] * DEPTH) for i in range(8)]
    grad_b_ada, cots = [], []
    for l in range(DEPTH):
        lat = jnp.concatenate([per_dev[i][l][0:1] for i in range(8)], axis=0)
        ctx_rows = jnp.concatenate([per_dev[i][l][1].reshape((-1, LANES)) for i in range(8)], axis=0)
        ctx_sum = sum_blocks(ctx_rows, 8, name="l%d_dmod_ctx_sum" % l).reshape((1, 6 * D_MODEL))
        all_rows = jnp.concatenate([lat, ctx_sum, jnp.zeros((ADA_ROWS - 9, 6 * D_MODEL), F32)], axis=0)
        grad_b_ada.append(sum_blocks(all_rows.reshape((-1, LANES)), ADA_ROWS, name="l%d_b_ada_sum" % l).reshape(-1))
        cots.append(lax.dynamic_slice_in_dim(all_rows, slot * ada_cols, ada_cols, axis=1))
    c_ctx_part, ada_grads = ada_vjp(cots)

    parts = []
    for n in BIG:
        for l in range(DEPTH):
            sl = grads["slots"][l]
            if n == "w_in":
                parts.append(jnp.stack(_untake_cols([(sl["in_" + g], src) for g, src in groups], IN_COLS, N_SLOT)))
            elif n == "w_branch":
                b3 = sl["w_branch3"].reshape((N_SLOT, M2_H, LANES, BRANCH_W))[:, :, :M2_P].reshape((N_SLOT, BRANCH_W, BRANCH_W))
                parts.append(jnp.concatenate([sl["w_branch0"], sl["w_branch1"], sl["w_branch2"], b3], axis=1))
            else:
                parts.append(sl[n])
    from_sibling = reduce_pair(parts)
    pair_sums = [add_own_half(g, r, core, name="pair_sum%d" % i) for i, (g, r) in enumerate(zip(parts, from_sibling))]
    from_chips = reduce_chips(pair_sums)
    halves = [chip_sum(q, r, slot_arr, name="chip_sum%d" % i) for i, (q, r) in enumerate(zip(pair_sums, from_chips))]
    others = swap_pair(halves)
    big_grads = {}
    for k, n in enumerate(BIG):
        layers = []
        for l in range(DEPTH):
            mine, theirs = halves[DEPTH * k + l], others[DEPTH * k + l]
            layers.append(jnp.where(c_i == 0, jnp.concatenate([mine, theirs], axis=0),
                                    jnp.concatenate([theirs, mine], axis=0)))
        big_grads[n] = jnp.stack(layers).reshape(a[n].shape)

    summed = SMALL_WHOLE + SMALL_SHARDED + ["c_ctx"]
    local = dict(grads["small"], c_ctx=0.5 * c_ctx_part)
    partial = _pack([local[n] for n in summed] + [loss.reshape((1,))])
    total = sum_blocks(allgather_small(partial, name="gather_small_grads"), 8, name="sum_small_grads")
    pieces = _unpack(total, [local[n].shape for n in summed] + [(1,)])
    small_grads = dict(zip(summed, pieces[:-1]))
    for n in SMALL_SHARDED:
        width = a[n].shape[-1]
        small_grads[n] = lax.dynamic_slice_in_dim(small_grads[n], slot * width, width, axis=-1)
    small_grads["b_ada"] = jnp.stack(grad_b_ada)
    big_grads["w_ada"] = jnp.stack(ada_grads)
    small_names = summed + ["b_ada"]
    loss_all = pieces[-1].reshape(())

    grad_w, delta, new_m, new_v = {}, {}, {}, {}
    two_d = lambda t: t.reshape((-1, t.shape[-1]))
    for n in BIG + ["w_ada"]:
        d, mn, vn = adamw(two_d(a[n]), two_d(big_grads[n]), two_d(a["m_" + n]), two_d(a["v_" + n]), name="adamw_" + n)
        grad_w[n], delta[n], new_m[n], new_v[n] = big_grads[n], d.reshape(a[n].shape), mn.reshape(a[n].shape), vn.reshape(a[n].shape)
    shapes = [a[n].shape for n in small_names]
    d, mn, vn = adamw(_pack([a[n] for n in small_names]), _pack([small_grads[n] for n in small_names]),
                      _pack([a["m_" + n] for n in small_names]), _pack([a["v_" + n] for n in small_names]), name="adamw_small")
    for n, dd, mm, vv in zip(small_names, _unpack(d, shapes), _unpack(mn, shapes), _unpack(vn, shapes)):
        grad_w[n], delta[n], new_m[n], new_v[n] = small_grads[n], dd, mm, vv

    return (loss_all, grads["x"][None], *[grad_w[n] for n in WEIGHTS], *[delta[n] for n in WEIGHTS],
            *[new_m[n] for n in WEIGHTS], *[new_v[n] for n in WEIGHTS])


def kernel(x, c, ctx, c_ctx, norm1_g, norm2_g, w_ada, b_ada, w_in, b_merge, gla_a2, gla_ab, gla_norm_g, na_rpb, gdn_conv, gdn_a_log, gdn_dt_bias, gdn_norm_g, m2_conv, m2_conv_b, m2_a_log, m2_dt_bias, m2_d, m2_norm_g, w_branch, w_out, w_ffn1, w_ffn3, w_ffn2, final_norm_g, loss_target, m_c_ctx, m_norm1_g, m_norm2_g, m_w_ada, m_b_ada, m_w_in, m_b_merge, m_gla_a2, m_gla_ab, m_gla_norm_g, m_na_rpb, m_gdn_conv, m_gdn_a_log, m_gdn_dt_bias, m_gdn_norm_g, m_m2_conv, m_m2_conv_b, m_m2_a_log, m_m2_dt_bias, m_m2_d, m_m2_norm_g, m_w_branch, m_w_out, m_w_ffn1, m_w_ffn3, m_w_ffn2, m_final_norm_g, v_c_ctx, v_norm1_g, v_norm2_g, v_w_ada, v_b_ada, v_w_in, v_b_merge, v_gla_a2, v_gla_ab, v_gla_norm_g, v_na_rpb, v_gdn_conv, v_gdn_a_log, v_gdn_dt_bias, v_gdn_norm_g, v_m2_conv, v_m2_conv_b, v_m2_a_log, v_m2_dt_bias, v_m2_d, v_m2_norm_g, v_w_branch, v_w_out, v_w_ffn1, v_w_ffn3, v_w_ffn2, v_final_norm_g):
    return _step(dict(locals()))
```

```python
import functools
import math

import numpy as np
import jax
import jax.numpy as jnp
from jax import lax
from jax.experimental import pallas as pl
from jax.experimental.pallas import tpu as pltpu

F32 = jnp.float32
BF16 = jnp.bfloat16
HI = lax.Precision.HIGHEST
MESH = pl.DeviceIdType.MESH
ANY = pl.BlockSpec(memory_space=pl.ANY)

VMEM_LIMIT = 56 * 1024 * 1024
LANES = 128
SUBLANES = 8

D_MODEL = 2048
DEPTH = 2
GRID_W = 64
CHUNK = 64
CONV_W = 5
RMS_EPS = 1e-6
NEG_INF = -1e30
ROPE_BASE = 10000.0
BRANCH_W = 512
GLA_H, GLA_DK, GLA_DV, GLA_LR, GLA_TAU = 4, 64, 128, 16, 16.0
NA_H, NA_D, NA_WIN_R, NA_WIN_C = 4, 128, 8, 16
GDN_H, GDN_D = 4, 128
M2_P, M2_H, M2_N, M2_G = 64, 8, 128, 2
D_FF = 5632
IN_COLS = 14912
N_SLOT = 4
ADAM_LR, ADAM_B1, ADAM_B2, ADAM_EPS, ADAM_WD, ADAM_STEP = 0.001, 0.9, 0.999, 1e-08, 0.01, 10


def _cparams(*sem):
    return pltpu.CompilerParams(dimension_semantics=sem if sem else None, vmem_limit_bytes=VMEM_LIMIT)


def _tile(n, target, mult):
    if n <= target:
        return n
    best = None
    for t in range(mult, target + 1, mult):
        if n % t == 0:
            best = t
    assert best is not None, (n, target, mult)
    return best


def _nt(a, b):
    return lax.dot_general(a.astype(BF16), b.astype(BF16), (((1,), (1,)), ((), ())), preferred_element_type=F32)


def _tn(a, b):
    return lax.dot_general(a.astype(BF16), b.astype(BF16), (((0,), (0,)), ((), ())), preferred_element_type=F32)


def _nn(a, b):
    return jnp.dot(a.astype(BF16), b.astype(BF16), preferred_element_type=F32)


def _dot3(a, b, dims):
    a_hi, b_hi = a.astype(BF16), b.astype(BF16)
    a_lo = (a - a_hi.astype(F32)).astype(BF16)
    b_lo = (b - b_hi.astype(F32)).astype(BF16)
    dot = lambda u, v: lax.dot_general(u, v, dims, preferred_element_type=F32)
    return dot(a_hi, b_hi) + (dot(a_hi, b_lo) + dot(a_lo, b_hi))


_NN, _NT, _TN = ((((1,), (0,)), ((), ())), (((1,), (1,)), ((), ())), (((0,), (0,)), ((), ())))


@jax.custom_vjp
def _nn_hi(a, b):
    return _dot3(a, b, _NN)


_nn_hi.defvjp(lambda a, b: (_dot3(a, b, _NN), (a, b)),
              lambda res, g: (_dot3(g, res[1], _NT), _dot3(res[0], g, _TN)))


def _w_spec(layout, prefix, r_idx, c_idx, br, bc, slot_dim):
    none = (None,) * len(prefix)
    if layout == "plain":
        return pl.BlockSpec(none + (br, bc), lambda i, j, k: prefix + (r_idx(i, j, k), c_idx(i, j, k)))
    if layout == "col":
        per = slot_dim // bc
        return pl.BlockSpec((None,) + none + (br, bc),
                            lambda i, j, k: (c_idx(i, j, k) // per,) + prefix + (r_idx(i, j, k), c_idx(i, j, k) % per))
    per = slot_dim // br
    return pl.BlockSpec((None,) + none + (br, bc),
                        lambda i, j, k: (r_idx(i, j, k) // per,) + prefix + (r_idx(i, j, k) % per, c_idx(i, j, k)))


def _mm(a, b, *, name, ta=False, tb=False, b_layout="plain", b_prefix=(), out_layout="plain", out_dtype=F32):
    m, kdim = (a.shape[1], a.shape[0]) if ta else a.shape
    rows, cols = b.shape[-2:]
    if b_layout == "col":
        cols *= N_SLOT
    elif b_layout == "row":
        rows *= N_SLOT
    n = rows if tb else cols
    assert (cols if tb else rows) == kdim, (a.shape, b.shape, ta, tb)
    n_unit = n // N_SLOT if (out_layout == "col" or (b_layout == ("row" if tb else "col"))) else n
    k_unit = kdim // N_SLOT if b_layout == ("col" if tb else "row") else kdim
    odd_n = n_unit % 1408 == 0 and n_unit % 512 != 0
    odd_k = k_unit % 1408 == 0 and k_unit % 512 != 0
    if ta:
        tm = _tile(m, 1024, LANES)
        tn = _tile(n_unit, 1408 if odd_n else 1024, LANES)
        tk = _tile(k_unit, 1056, 2 * SUBLANES)
    elif tb:
        tm = _tile(m, 768 if odd_k else 704, 2 * SUBLANES)
        tn = _tile(n_unit, 1408 if odd_n else 2048, LANES)
        tk = _tile(k_unit, 1408 if odd_k else 2048, LANES)
    else:
        tm = _tile(m, 768, 2 * SUBLANES)
        tn = _tile(n_unit, 1408 if odd_n else 512, LANES)
        tk = _tile(k_unit, 1408 if odd_k else 2048, LANES)
    nk = kdim // tk
    a_spec = (pl.BlockSpec((tk, tm), lambda i, j, k: (k, i)) if ta else pl.BlockSpec((tm, tk), lambda i, j, k: (i, k)))
    slot_dim = b.shape[-1] if b_layout == "col" else b.shape[-2]
    if tb:
        b_spec = _w_spec(b_layout, tuple(b_prefix), lambda i, j, k: j, lambda i, j, k: k, tn, tk, slot_dim)
    else:
        b_spec = _w_spec(b_layout, tuple(b_prefix), lambda i, j, k: k, lambda i, j, k: j, tk, tn, slot_dim)
    if out_layout == "col":
        per = (n // N_SLOT) // tn
        out_shape = jax.ShapeDtypeStruct((N_SLOT, m, n // N_SLOT), out_dtype)
        out_spec = pl.BlockSpec((None, tm, tn), lambda i, j, k: (j // per, i, j % per))
    else:
        out_shape = jax.ShapeDtypeStruct((m, n), out_dtype)
        out_spec = pl.BlockSpec((tm, tn), lambda i, j, k: (i, j))
    dims = (((0 if ta else 1,), (1 if tb else 0,)), ((), ()))

    def product(a_ref, b_ref):
        return lax.dot_general(a_ref[...].astype(BF16), b_ref[...].astype(BF16), dims, preferred_element_type=F32)

    def body_once(a_ref, b_ref, o_ref):
        o_ref[...] = product(a_ref, b_ref).astype(o_ref.dtype)

    def body(a_ref, b_ref, o_ref, acc_ref):
        k = pl.program_id(2)

        @pl.when(k == 0)
        def _():
            acc_ref[...] = jnp.zeros_like(acc_ref)

        acc_ref[...] += product(a_ref, b_ref)

        @pl.when(k == nk - 1)
        def _():
            o_ref[...] = acc_ref[...].astype(o_ref.dtype)

    return pl.pallas_call(
        body_once if nk == 1 else body, name=name, grid=(m // tm, n // tn, nk), in_specs=[a_spec, b_spec],
        out_specs=out_spec, out_shape=out_shape, scratch_shapes=[] if nk == 1 else [pltpu.VMEM((tm, tn), F32)],
        compiler_params=_cparams("parallel", "parallel", "arbitrary"),
    )(a, b)


def linear(a, w, grad_slot, *, name, layout="plain", prefix=(), out_dtype=F32):
    @jax.custom_vjp
    def f(a, w, grad_slot):
        return _mm(a, w, name=name + "_fwd", b_layout=layout, b_prefix=prefix, out_dtype=out_dtype)

    def fwd(a, w, grad_slot):
        return f(a, w, grad_slot), (a, w)

    def bwd(res, g):
        a, w = res
        da = _mm(g, w, name=name + "_dgrad", tb=True, b_layout=layout, b_prefix=prefix, out_dtype=a.dtype)
        dw = _mm(a, g, name=name + "_wgrad", ta=True, out_layout="col" if layout == "col" else "plain")
        if layout == "row":
            dw = dw.reshape((N_SLOT, dw.shape[0] // N_SLOT, dw.shape[1]))
        return da, None, dw

    f.defvjp(fwd, bwd)
    return f(a, w, grad_slot)


def _rowwise_specs(rows, consts, params, tile, seg_tile):
    def row_spec(r):
        return pl.BlockSpec((tile, r.shape[1]), lambda i: (i, 0))

    def par_spec(p):
        if p.shape[0] == 2:
            return pl.BlockSpec((None,) + p.shape[1:], lambda i: (jnp.where(i >= seg_tile, 1, 0), 0, 0))
        return pl.BlockSpec((None,) + p.shape[1:], lambda i: (0, 0, 0))

    return [row_spec(r) for r in rows], [row_spec(r) for r in consts], [par_spec(p) for p in params]


def rowwise(f, rows, consts, params, *, out_widths, tile, n_lat, name, out_dtype=F32):
    rows, consts, params = tuple(rows), tuple(consts), tuple(params)
    n_rows = rows[0].shape[0]
    tile = math.gcd(math.gcd(n_rows, n_lat), tile)
    assert tile % SUBLANES == 0
    seg_tile = n_lat // tile
    grid = (n_rows // tile,)
    nr, nc, npar = len(rows), len(consts), len(params)
    r_specs, c_specs, p_specs = _rowwise_specs(rows, consts, params, tile, seg_tile)
    out_shape = tuple(jax.ShapeDtypeStruct((n_rows, w), out_dtype) for w in out_widths)
    out_specs = tuple(pl.BlockSpec((tile, w), lambda i: (i, 0)) for w in out_widths)
    n_out = len(out_widths)

    def fwd_call(rows, consts, params):
        def body(*refs):
            ins = [r[...].astype(F32) for r in refs[:nr + nc + npar]]
            outs = f(*ins)
            for o_ref, o in zip(refs[nr + nc + npar:], outs):
                o_ref[...] = o.astype(o_ref.dtype)

        return pl.pallas_call(body, name=name + "_fwd", grid=grid, in_specs=r_specs + c_specs + p_specs,
                              out_specs=out_specs, out_shape=out_shape,
                              compiler_params=_cparams("parallel"))(*rows, *consts, *params)

    def bwd_call(rows, consts, params, gouts):
        def body(*refs):
            i = pl.program_id(0)
            ins = [r[...].astype(F32) for r in refs[:nr + nc + npar]]
            gs = tuple(r[...].astype(F32) for r in refs[nr + nc + npar:nr + nc + npar + n_out])
            d_refs = refs[nr + nc + npar + n_out:]
            cvals = ins[nr:nr + nc]

            def g(*diff):
                return tuple(f(*diff[:nr], *cvals, *diff[nr:]))

            _, vjp = jax.vjp(g, *ins[:nr], *ins[nr + nc:])
            grads = vjp(gs)
            for d_ref, gr in zip(d_refs[:nr], grads[:nr]):
                d_ref[...] = gr.astype(d_ref.dtype)
            for p, d_ref, gr in zip(params, d_refs[nr:], grads[nr:]):
                first = (i == 0) | (i == seg_tile) if p.shape[0] == 2 else (i == 0)

                @pl.when(first)
                def _():
                    d_ref[...] = jnp.zeros_like(d_ref)

                d_ref[...] += gr

        d_shape = tuple(jax.ShapeDtypeStruct(r.shape, r.dtype) for r in rows) + tuple(
            jax.ShapeDtypeStruct(p.shape, F32) for p in params)
        g_specs = [pl.BlockSpec((tile, w), lambda i: (i, 0)) for w in out_widths]
        return pl.pallas_call(body, name=name + "_bwd", grid=grid,
                              in_specs=r_specs + c_specs + p_specs + g_specs,
                              out_specs=tuple(r_specs + p_specs), out_shape=d_shape,
                              compiler_params=_cparams("arbitrary"))(*rows, *consts, *params, *gouts)

    @jax.custom_vjp
    def op(rows, consts, params):
        return fwd_call(rows, consts, params)

    def op_fwd(rows, consts, params):
        return op(rows, consts, params), (rows, consts, params)

    def op_bwd(res, gouts):
        rows, consts, params = res
        d = bwd_call(rows, consts, params, tuple(gouts))
        return tuple(d[:nr]), tuple(None for _ in consts), tuple(d[nr:])

    op.defvjp(op_fwd, op_bwd)
    return op(rows, consts, params)


def _rms(x, width=None):
    w = x.shape[-1] if width is None else width
    return x * lax.rsqrt(jnp.sum(x * x, axis=-1, keepdims=True) * (1.0 / w) + RMS_EPS)


def _silu(x):
    return x * jax.nn.sigmoid(x)


def _f_modnorm(x, g, sc, sh):
    return ((_rms(x) * g) * (1.0 + sc) + sh,)


def _f_silu(x):
    return (_silu(x),)


def _f_gla_prep(lr, a2, ab):
    z = _nn(lr, a2) + ab
    return ((jnp.minimum(z, 0.0) - jnp.log(1.0 + jnp.exp(-jnp.abs(z)))) * (1.0 / GLA_TAU),)


def _f_headnorm_gate(o, g, ng):
    outs = []
    for h in range(BRANCH_W // LANES):
        lo = h * LANES
        oh = o[:, lo:lo + LANES] + o[:, BRANCH_W + lo:BRANCH_W + lo + LANES]
        outs.append(_rms(oh) * ng * _silu(g[:, lo:lo + LANES]))
    return (jnp.concatenate(outs, axis=-1),)


def _f_gdn_prep(x, alog, dtb):
    half = x.shape[1] // 2
    beta = jax.nn.sigmoid(x[:, :half])
    la = -jnp.exp(alog) * jax.nn.softplus(x[:, half:] + dtb)
    return beta, la


def _f_m2_prep(x, alog, dtb):
    dt = jax.nn.softplus(x + dtb)
    return dt, -jnp.exp(alog) * dt


def _f_m2_fin(o, z, xs, dskip, ng):
    w = z.shape[1]
    y = (o[:, :w] + o[:, w:] + dskip * xs) * _silu(z)
    return (_rms(y, BRANCH_W) * ng,)


def _f_merge(gate, z0, z1, z2, z3, bm):
    acc = None
    for i, z in enumerate((z0, z1, z2, z3)):
        lo = i * D_MODEL
        t = jax.nn.sigmoid(gate[:, lo:lo + D_MODEL] + bm[:, lo:lo + D_MODEL]) * z
        acc = t if acc is None else acc + t
    return (acc,)


def _f_resid(x, y, g):
    return (x + g * y,)


def _f_swiglu(u1, u3):
    return (_silu(u1) * u3,)


def _f_loss(x, tgt, g):
    e = _rms(x) * g - tgt
    per_row = 0.5 * jnp.sum(e * e, axis=-1, keepdims=True) * (1.0 / D_MODEL)
    return (jnp.broadcast_to(per_row * (1.0 / LANES), (x.shape[0], LANES)),)


_HALO = 8


def _conv_segments(n_lat, n_ctx):
    segs = [(0, _HALO, n_lat), (n_lat, n_lat + 3 * _HALO, n_ctx)]
    return segs, n_lat + n_ctx + 4 * _HALO


def _conv_stage(buf, src, n_lat, n_ctx):
    zeros = jnp.zeros((_HALO, LANES), F32)
    buf[0:_HALO, :] = zeros
    buf[_HALO:_HALO + n_lat, :] = src[0:n_lat, :].astype(F32)
    buf[n_lat + _HALO:n_lat + 2 * _HALO, :] = zeros
    buf[n_lat + 2 * _HALO:n_lat + 3 * _HALO, :] = zeros
    buf[n_lat + 3 * _HALO:n_lat + 3 * _HALO + n_ctx, :] = src[n_lat:n_lat + n_ctx, :].astype(F32)
    buf[n_lat + n_ctx + 3 * _HALO:n_lat + n_ctx + 4 * _HALO, :] = zeros


def conv_silu(x, w, b, *, n_lat, name):
    n_rows, n_ch = x.shape
    n_ctx = n_rows - n_lat
    segs, n_buf = _conv_segments(n_lat, n_ctx)
    grid = (n_ch // LANES,)
    col = lambda r: pl.BlockSpec((r, LANES), lambda j: (0, j))
    half = CONV_W // 2

    def tiles():
        for row0, off, length in segs:
            tr = _tile(length, 256, SUBLANES)
            for t0 in range(0, length, tr):
                yield row0 + t0, off + t0, tr

    def pre_act(buf, w_ref, b_ref, off, tr):
        acc = jnp.broadcast_to(b_ref[...], (tr, LANES))
        for j in range(CONV_W):
            acc = acc + w_ref[j:j + 1, :] * buf[off + j - half:off + j - half + tr, :]
        return acc

    def fwd_call(x, w, b):
        def body(x_ref, w_ref, b_ref, o_ref, buf):
            _conv_stage(buf, x_ref, n_lat, n_ctx)
            for row, off, tr in tiles():
                o_ref[row:row + tr, :] = _silu(pre_act(buf, w_ref, b_ref, off, tr))

        return pl.pallas_call(body, name=name + "_fwd", grid=grid, in_specs=[col(n_rows), col(CONV_W), col(1)],
                              out_specs=col(n_rows), out_shape=jax.ShapeDtypeStruct(x.shape, F32),
                              scratch_shapes=[pltpu.VMEM((n_buf, LANES), F32)],
                              compiler_params=_cparams("parallel"))(x, w, b)

    def bwd_call(x, w, b, g):
        def body(x_ref, w_ref, b_ref, g_ref, dx_ref, dw_ref, db_ref, xbuf, dbuf):
            _conv_stage(xbuf, x_ref, n_lat, n_ctx)
            _conv_stage(dbuf, g_ref, n_lat, n_ctx)
            dw = [jnp.zeros((1, LANES), F32) for _ in range(CONV_W)]
            db = jnp.zeros((1, LANES), F32)
            for row, off, tr in tiles():
                pre = pre_act(xbuf, w_ref, b_ref, off, tr)
                s = jax.nn.sigmoid(pre)
                dpre = g_ref[row:row + tr, :] * (s * (1.0 + pre * (1.0 - s)))
                dbuf[off:off + tr, :] = dpre
                db = db + jnp.sum(dpre, axis=0, keepdims=True)
                for j in range(CONV_W):
                    dw[j] = dw[j] + jnp.sum(dpre * xbuf[off + j - half:off + j - half + tr, :], axis=0, keepdims=True)
            for row, off, tr in tiles():
                acc = jnp.zeros((tr, LANES), F32)
                for j in range(CONV_W):
                    acc = acc + w_ref[j:j + 1, :] * dbuf[off - j + half:off - j + half + tr, :]
                dx_ref[row:row + tr, :] = acc.astype(dx_ref.dtype)
            for j in range(CONV_W):
                dw_ref[j:j + 1, :] = dw[j]
            db_ref[...] = db

        return pl.pallas_call(
            body, name=name + "_bwd", grid=grid, in_specs=[col(n_rows), col(CONV_W), col(1), col(n_rows)],
            out_specs=(col(n_rows), col(CONV_W), col(1)),
            out_shape=(jax.ShapeDtypeStruct(x.shape, x.dtype), jax.ShapeDtypeStruct(w.shape, F32),
                       jax.ShapeDtypeStruct(b.shape, F32)),
            scratch_shapes=[pltpu.VMEM((n_buf, LANES), F32), pltpu.VMEM((n_buf, LANES), F32)],
            compiler_params=_cparams("parallel"))(x, w, b, g)

    @jax.custom_vjp
    def op(x, w, b):
        return fwd_call(x, w, b)

    op.defvjp(lambda x, w, b: (op(x, w, b), (x, w, b)), lambda res, g: bwd_call(*res, g))
    return op(x, w, b)


def chunk_scan(step, shared, shared_lanes, perdir, perdir_lanes, consts, *, heads, state_shape, out_w, n_lat, name,
               keep_shape=None):
    assert keep_shape is None or heads == 1
    shared, perdir, consts = tuple(shared), tuple(perdir), tuple(consts)
    n_rows = shared[0].shape[0]
    nl, ncx = n_lat // CHUNK, (n_rows - n_lat) // CHUNK
    n_chunks = nl + ncx
    ow_all = heads * out_w
    ns, npd, ncst = len(shared), len(perdir), len(consts)

    def cidx(d, n):
        m = n - ncx
        return jnp.where(n < ncx, nl + jnp.where(d == 0, n, ncx - 1 - n), jnp.where(d == 0, m, nl - 1 - m))

    def specs(order):
        sh = [pl.BlockSpec((CHUNK, a.shape[1]), lambda d, n: (cidx(d, order(n)), 0)) for a in shared]
        pd = [pl.BlockSpec((CHUNK, a.shape[1] // 2), lambda d, n: (cidx(d, order(n)), d)) for a in perdir]
        cs = [pl.BlockSpec((CHUNK, a.shape[1]), lambda d, n: (cidx(d, order(n)), 0)) for a in consts]
        o = pl.BlockSpec((CHUNK, ow_all), lambda d, n: (cidx(d, order(n)), d))
        st = pl.BlockSpec((None, None, heads) + state_shape, lambda d, n: (d, order(n), 0) + (0,) * len(state_shape))
        kp = [] if keep_shape is None else [pl.BlockSpec((None, None) + keep_shape,
                                                        lambda d, n: (d, order(n)) + (0,) * len(keep_shape))]
        return sh, pd, cs, o, st, kp

    def mask(d):
        r = lax.broadcasted_iota(jnp.int32, (CHUNK, CHUNK), 0)
        c = lax.broadcasted_iota(jnp.int32, (CHUNK, CHUNK), 1)
        lower = jnp.where(r >= c, 1.0, 0.0).astype(F32)
        upper = jnp.where(r <= c, 1.0, 0.0).astype(F32)
        return jnp.where(d == 0, lower, upper)

    def head_slices(h):
        out = []
        for lanes in tuple(shared_lanes) + tuple(perdir_lanes):
            out.append([slice(off + (h // hpg) * w, off + (h // hpg) * w + w) for off, w, hpg in lanes])
        return out

    def load(refs, h):
        return tuple(tuple(ref[:, s].astype(F32) for s in sl) for ref, sl in zip(refs, head_slices(h)))

    state_sds = jax.ShapeDtypeStruct((2, n_chunks, heads) + state_shape, F32)

    def fwd_call(shared, perdir, consts):
        sh, pd, cs, o_spec, st_spec, kp_spec = specs(lambda n: n)

        def body(*refs):
            in_refs = refs[:ns + npd]
            c_refs = refs[ns + npd:ns + npd + ncst]
            o_ref, ss_ref = refs[ns + npd + ncst:ns + npd + ncst + 2]
            s_scr = refs[-1]
            d, n = pl.program_id(0), pl.program_id(1)

            @pl.when(n == 0)
            def _():
                s_scr[...] = jnp.zeros_like(s_scr)

            m = mask(d)
            cv = tuple(c[...] for c in c_refs)
            ins = [load(in_refs, h) for h in range(heads)]
            s0 = [s_scr[h] for h in range(heads)]
            res = [step(ins[h], cv, s0[h], m, None) for h in range(heads)]
            for h in range(heads):
                o_ref[:, h * out_w:(h + 1) * out_w] = res[h][0]
                ss_ref[h] = s0[h]
                s_scr[h] = res[h][1]
            if keep_shape is not None:
                refs[-2][...] = res[0][2]

        keep_sds = [] if keep_shape is None else [jax.ShapeDtypeStruct((2, n_chunks) + keep_shape, F32)]
        return pl.pallas_call(
            body, name=name + "_fwd", grid=(2, n_chunks), in_specs=sh + pd + cs, out_specs=tuple([o_spec, st_spec] + kp_spec),
            out_shape=tuple([jax.ShapeDtypeStruct((n_rows, 2 * ow_all), F32), state_sds] + keep_sds),
            scratch_shapes=[pltpu.VMEM((heads,) + state_shape, F32)],
            compiler_params=_cparams("arbitrary", "arbitrary"))(*shared, *perdir, *consts)

    def bwd_call(shared, perdir, consts, starts, kept, g):
        sh, pd, cs, o_spec, st_spec, kp_spec = specs(lambda n: n_chunks - 1 - n)
        dsh = [pl.BlockSpec((CHUNK, a.shape[1]), lambda d, n: (cidx(d, n_chunks - 1 - n), d)) for a in shared]
        nk = len(kept)

        def body(*refs):
            in_refs = refs[:ns + npd]
            c_refs = refs[ns + npd:ns + npd + ncst]
            ss_ref, g_ref = refs[ns + npd + ncst:ns + npd + ncst + 2]
            kept_val = refs[ns + npd + ncst + 2][...] if nk else None
            d_refs = refs[ns + npd + ncst + 2 + nk:ns + npd + ncst + 2 + nk + ns + npd]
            ds_scr = refs[-1]
            d, n = pl.program_id(0), pl.program_id(1)

            @pl.when(n == 0)
            def _():
                ds_scr[...] = jnp.zeros_like(ds_scr)

            m = mask(d)
            cv = tuple(c[...] for c in c_refs)
            ins = [load(in_refs, h) for h in range(heads)]
            cots = [(g_ref[:, h * out_w:(h + 1) * out_w], ds_scr[h]) for h in range(heads)]
            starts = [ss_ref[h] for h in range(heads)]
            grads = []
            for h in range(heads):
                _, vjp = jax.vjp(lambda i_, s_: step(i_, cv, s_, m, kept_val)[:2], ins[h], starts[h])
                grads.append(vjp(cots[h]))
            for d_ref in d_refs:
                d_ref[...] = jnp.zeros_like(d_ref)
            for h in range(heads):
                g_ins, g_s = grads[h]
                for d_ref, sl, gr in zip(d_refs, head_slices(h), g_ins):
                    for s, gv in zip(sl, gr):
                        d_ref[:, s] += gv
                ds_scr[h] = g_s

        d_shape = tuple(jax.ShapeDtypeStruct((n_rows, 2 * a.shape[1]), F32) for a in shared) + tuple(
            jax.ShapeDtypeStruct(a.shape, F32) for a in perdir)
        return pl.pallas_call(
            body, name=name + "_bwd", grid=(2, n_chunks), in_specs=sh + pd + cs + [st_spec, o_spec] + kp_spec,
            out_specs=tuple(dsh + pd), out_shape=d_shape,
            scratch_shapes=[pltpu.VMEM((heads,) + state_shape, F32)],
            compiler_params=_cparams("arbitrary", "arbitrary"))(*shared, *perdir, *consts, starts, g, *kept)

    @jax.custom_vjp
    def op(shared, perdir, consts):
        return fwd_call(shared, perdir, consts)[0]

    def op_fwd(shared, perdir, consts):
        o, starts, *kept = fwd_call(shared, perdir, consts)
        return o, (shared, perdir, consts, starts, tuple(kept))

    def op_bwd(res, g):
        shared, perdir, consts, starts, kept = res
        d = bwd_call(shared, perdir, consts, starts, kept, g)
        d_sh = tuple((a[:, :a.shape[1] // 2] + a[:, a.shape[1] // 2:]).astype(s.dtype) for a, s in zip(d[:ns], shared))
        return d_sh, tuple(d[ns:]), tuple(None for _ in consts)

    op.defvjp(op_fwd, op_bwd)
    return op(shared, perdir, consts)


@jax.custom_vjp
def _swap_halves(x):
    return pltpu.roll(x, LANES // 2, 1)


_swap_halves.defvjp(lambda x: (_swap_halves(x), None), lambda _, g: (_swap_halves(g),))


def _gla_step(ins, consts, st, m, kept):
    (q_all, k_all), (v_all,), (la_all,) = ins
    cos, sin = consts
    b_all = _nn_hi(m, la_all)
    bl_all = jnp.sum(la_all, axis=0, keepdims=True)
    outs, st_new = [], []
    for h in range(GLA_H):
        blk = lambda x: x[:, h * LANES:(h + 1) * LANES]
        q, k, v, b, bl = blk(q_all), blk(k_all), blk(v_all), blk(b_all), blk(bl_all)
        st_h = st[h * GLA_DV:(h + 1) * GLA_DV]
        q = (q * cos + _swap_halves(q) * sin) * (GLA_DK ** -0.5)
        k = k * cos + _swap_halves(k) * sin
        qi = q * jnp.exp(b)
        ki = k * jnp.exp(-b)
        outs.append(_nt(qi, st_h) + _nn(_nt(qi, ki) * m, v))
        st_new.append(st_h * jnp.exp(bl) + _tn(v, k * jnp.exp(bl - b)))
    return jnp.concatenate(outs, axis=1), jnp.concatenate(st_new, axis=0), None


def _l2n(x):
    return x * lax.rsqrt(jnp.sum(x * x, axis=-1, keepdims=True) + RMS_EPS)


def _tri_inv_fwd(nmat):
    r = lax.broadcasted_iota(jnp.int32, nmat.shape, 0)
    c = lax.broadcasted_iota(jnp.int32, nmat.shape, 1)
    inv = jnp.where(r == c, 1.0, 0.0).astype(F32) - nmat
    p = nmat
    for _ in range(5):
        p = _nn_hi(p, p)
        inv = inv + _nn_hi(inv, p)
    return inv


@jax.custom_vjp
def _unit_tri_inv(nmat):
    return _tri_inv_fwd(nmat)


def _unit_tri_inv_bwd(inv, g):
    return (-_dot3(_dot3(inv, g, _TN), inv, _NT),)


_unit_tri_inv.defvjp(lambda nmat: (lambda inv: (inv, inv))(_tri_inv_fwd(nmat)), _unit_tri_inv_bwd)


@jax.custom_vjp
def _kept_tri_inv(nmat, inv):
    return inv


_kept_tri_inv.defvjp(lambda nmat, inv: (inv, inv), lambda inv, g: _unit_tri_inv_bwd(inv, g) + (jnp.zeros_like(inv),))


def _lane_col(x, h):
    lane = lax.broadcasted_iota(jnp.int32, x.shape, 1)
    return jnp.sum(jnp.where(lane == h, x, 0.0), axis=1, keepdims=True)


def _masked_exp(diff, mask):
    return jnp.where(mask > 0, jnp.exp(jnp.where(mask > 0, diff, 0.0)), 0.0)


def _gdn_step(ins, consts, s, m, kept):
    (q, k, v), (beta,), (la,) = ins
    n = GDN_H * CHUNK
    hs = range(GDN_H)
    blk = lambda x, h: x[:, h * GDN_D:(h + 1) * GDN_D]
    rows = lambda x, h: x[h * CHUNK:(h + 1) * CHUNK]
    qh = [_l2n(blk(q, h)) * (GDN_D ** -0.5) for h in hs]
    kh = [_l2n(blk(k, h)) for h in hs]
    k_st = jnp.concatenate(kh, axis=0)
    q_st = jnp.concatenate(qh, axis=0)
    v_st = jnp.concatenate([blk(v, h) for h in hs], axis=0)
    beta_st = jnp.concatenate([_lane_col(beta, h) for h in hs], axis=0)
    la_cols = [_lane_col(la, h) for h in hs]
    la_st = jnp.concatenate([jnp.broadcast_to(c, (CHUNK, GDN_D)) for c in la_cols], axis=0)
    r = lax.broadcasted_iota(jnp.int32, (n, n), 0)
    c = lax.broadcasted_iota(jnp.int32, (n, n), 1)
    e = jnp.where(lax.broadcasted_iota(jnp.int32, (n, CHUNK), 0) % CHUNK == lax.broadcasted_iota(jnp.int32, (n, CHUNK), 1),
                  1.0, 0.0).astype(F32)
    m_bd = jnp.where(r // CHUNK == c // CHUNK, _nt(_nn(e, m), e), 0.0)
    eye = jnp.where(r == c, 1.0, 0.0).astype(F32)
    b_st = _nn_hi(m_bd, la_st)
    b_t = b_st.T
    diff = jnp.concatenate([b_st, b_st], axis=1) - jnp.concatenate([b_t, b_t], axis=0)
    incl = _masked_exp(diff, m_bd)
    strict = _masked_exp(diff, m_bd - eye)
    nmat = beta_st * _nt(k_st, k_st) * strict
    inv = _unit_tri_inv(nmat) if kept is None else _kept_tri_inv(nmat, kept)
    wu = _nn_hi(inv, jnp.concatenate([k_st * (beta_st * jnp.exp(b_st)), v_st * beta_st], axis=-1))
    w, u0 = wu[:, :GDN_D], wu[:, GDN_D:]
    us, s_new, qs = [], [], []
    for h in hs:
        s_h = s[h * GDN_D:(h + 1) * GDN_D]
        bl = jnp.sum(jnp.broadcast_to(la_cols[h], (CHUNK, GDN_D)), axis=0, keepdims=True)
        u_h = rows(u0, h) - _nn(rows(w, h), s_h)
        s_new.append(jnp.exp(bl) * s_h + _tn(kh[h] * jnp.exp(bl - rows(b_st, h)), u_h))
        us.append(u_h)
        qs.append(_nn(qh[h], s_h))
    o_st = jnp.exp(b_st) * jnp.concatenate(qs, axis=0) + _nn(_nt(q_st, k_st) * incl, jnp.concatenate(us, axis=0))
    return jnp.concatenate([rows(o_st, h) for h in hs], axis=1), jnp.concatenate(s_new, axis=0), inv


def _ssd_step(ins, consts, s, m, kept):
    (xs,), (bc,), (dt,), (la,) = ins
    hpg = M2_H // M2_G
    b_all = _nn_hi(m, la)
    bl_all = jnp.sum(la, axis=0, keepdims=True)
    b_t = b_all.T
    row_id = lax.broadcasted_iota(jnp.int32, b_t.shape, 0)
    bm = [bc[:, g * M2_N:(g + 1) * M2_N] for g in range(M2_G)]
    cm = [bc[:, (M2_G + g) * M2_N:(M2_G + g + 1) * M2_N] for g in range(M2_G)]
    scores = [_nt(cm[g], bm[g]) for g in range(M2_G)]
    outs, s_new = [], []
    for h in range(M2_H):
        g = h // hpg
        s_h = s[h * M2_N:(h + 1) * M2_N]
        b_col = _lane_col(b_all, h)
        bl = _lane_col(bl_all, h)
        b_row = jnp.sum(jnp.where(row_id == h, b_t, 0.0), axis=0, keepdims=True)
        xv = xs[:, h * LANES:(h + 1) * LANES] * _lane_col(dt, h)
        outs.append(jnp.exp(b_col) * _nn(cm[g], s_h) + _nn(scores[g] * _masked_exp(b_col - b_row, m), xv))
        s_new.append(jnp.exp(bl) * s_h + _tn(bm[g] * jnp.exp(bl - b_col), xv))
    return jnp.concatenate(outs, axis=1), jnp.concatenate(s_new, axis=0), None


def _na_tile(q, kw, vw, kc, vc, bias):
    qs = q * (NA_D ** -0.5)
    s1 = _nt(qs, kw) + bias
    s2 = _nt(qs, kc)
    mx = lax.stop_gradient(jnp.maximum(jnp.max(s1, axis=-1, keepdims=True), jnp.max(s2, axis=-1, keepdims=True)))
    p1 = jnp.exp(s1 - mx)
    p2 = jnp.exp(s2 - mx)
    den = jnp.sum(p1, axis=-1, keepdims=True) + jnp.sum(p2, axis=-1, keepdims=True)
    return (_nn(p1, vw) + _nn(p2, vc)) / den


def _ctx_tile(q, k, v):
    s = _nt(q * (NA_D ** -0.5), k)
    p = jnp.exp(s - lax.stop_gradient(jnp.max(s, axis=-1, keepdims=True)))
    return _nn(p, v) / jnp.sum(p, axis=-1, keepdims=True)


def natten(q, k, v, bias, *, n_lat, name):
    n_rows = q.shape[0]
    n_ctx = n_rows - n_lat
    g_rows = n_lat // GRID_W
    win = NA_WIN_R * GRID_W
    ctx_blk = n_lat // n_ctx

    def start(n):
        return jnp.clip(n - NA_WIN_R // 2, 0, g_rows - NA_WIN_R)

    def case(n):
        return n - start(n)

    q_spec = pl.BlockSpec((GRID_W, LANES), lambda h, n: (n, h))
    lat_spec = pl.BlockSpec((n_lat, LANES), lambda h, n: (0, h))
    ctx_in = pl.BlockSpec((n_ctx, LANES), lambda h, n: (ctx_blk, h))
    ctx_out = pl.BlockSpec((n_ctx, LANES), lambda h, n: (0, h))
    bias_spec = pl.BlockSpec((None, None, GRID_W, win), lambda h, n: (h, case(n), 0, 0))
    lat_sds = jax.ShapeDtypeStruct((n_lat, BRANCH_W), F32)
    ctx_sds = jax.ShapeDtypeStruct((n_ctx, BRANCH_W), F32)
    f32 = lambda t: t.astype(F32)

    def lat_fwd(q, k, v, bias):
        def body(q_ref, k_ref, v_ref, kc_ref, vc_ref, b_ref, o_ref):
            r0 = pl.multiple_of(start(pl.program_id(1)) * GRID_W, GRID_W)
            o_ref[...] = _na_tile(f32(q_ref[...]), f32(k_ref[pl.ds(r0, win), :]), f32(v_ref[pl.ds(r0, win), :]),
                                  f32(kc_ref[...]), f32(vc_ref[...]), b_ref[...])

        return pl.pallas_call(body, name=name + "_lat_fwd", grid=(NA_H, g_rows),
                              in_specs=[q_spec, lat_spec, lat_spec, ctx_in, ctx_in, bias_spec], out_specs=q_spec,
                              out_shape=lat_sds, compiler_params=_cparams("parallel", "arbitrary"))(q, k, v, k, v, bias)

    def lat_bwd(q, k, v, bias, g):
        def body(q_ref, k_ref, v_ref, kc_ref, vc_ref, b_ref, g_ref, dq_ref, dk_ref, dv_ref, dkc_ref, dvc_ref, db_ref):
            n = pl.program_id(1)
            r0 = pl.multiple_of(start(n) * GRID_W, GRID_W)

            @pl.when(n == 0)
            def _():
                for r in (dk_ref, dv_ref, dkc_ref, dvc_ref):
                    r[...] = jnp.zeros_like(r)

            @pl.when((n == 0) | (case(n) != case(jnp.maximum(n - 1, 0))))
            def _():
                db_ref[...] = jnp.zeros_like(db_ref)

            _, vjp = jax.vjp(_na_tile, f32(q_ref[...]), f32(k_ref[pl.ds(r0, win), :]), f32(v_ref[pl.ds(r0, win), :]),
                             f32(kc_ref[...]), f32(vc_ref[...]), b_ref[...])
            dq, dkw, dvw, dkc, dvc, db = vjp(g_ref[...])
            dq_ref[...] = dq
            dk_ref[pl.ds(r0, win), :] += dkw
            dv_ref[pl.ds(r0, win), :] += dvw
            dkc_ref[...] += dkc
            dvc_ref[...] += dvc
            db_ref[...] += db

        return pl.pallas_call(
            body, name=name + "_lat_bwd", grid=(NA_H, g_rows),
            in_specs=[q_spec, lat_spec, lat_spec, ctx_in, ctx_in, bias_spec, q_spec],
            out_specs=(q_spec, lat_spec, lat_spec, ctx_out, ctx_out, bias_spec),
            out_shape=(lat_sds, lat_sds, lat_sds, ctx_sds, ctx_sds, jax.ShapeDtypeStruct(bias.shape, F32)),
            compiler_params=_cparams("parallel", "arbitrary"))(q, k, v, k, v, bias, g)

    c_in = pl.BlockSpec((n_ctx, LANES), lambda h: (ctx_blk, h))
    c_out = pl.BlockSpec((n_ctx, LANES), lambda h: (0, h))

    def ctx_fwd(q, k, v):
        def body(q_ref, k_ref, v_ref, o_ref):
            o_ref[...] = _ctx_tile(f32(q_ref[...]), f32(k_ref[...]), f32(v_ref[...]))

        return pl.pallas_call(body, name=name + "_ctx_fwd", grid=(NA_H,), in_specs=[c_in, c_in, c_in], out_specs=c_out,
                              out_shape=ctx_sds, compiler_params=_cparams("parallel"))(q, k, v)

    def ctx_bwd(q, k, v, g):
        def body(q_ref, k_ref, v_ref, g_ref, dq_ref, dk_ref, dv_ref):
            _, vjp = jax.vjp(_ctx_tile, f32(q_ref[...]), f32(k_ref[...]), f32(v_ref[...]))
            dq_ref[...], dk_ref[...], dv_ref[...] = vjp(g_ref[...])

        return pl.pallas_call(body, name=name + "_ctx_bwd", grid=(NA_H,), in_specs=[c_in, c_in, c_in, c_out],
                              out_specs=(c_out, c_out, c_out), out_shape=(ctx_sds, ctx_sds, ctx_sds),
                              compiler_params=_cparams("parallel"))(q, k, v, g)

    @jax.custom_vjp
    def op(q, k, v, bias):
        return jnp.concatenate([lat_fwd(q, k, v, bias), ctx_fwd(q, k, v)], axis=0)

    def op_bwd(res, g):
        q, k, v, bias = res
        dq, dk, dv, dkc, dvc, db = lat_bwd(q, k, v, bias, g[:n_lat])
        dqc, dkc2, dvc2 = ctx_bwd(q, k, v, g[n_lat:])
        return (jnp.concatenate([dq, dqc], axis=0).astype(q.dtype), jnp.concatenate([dk, dkc + dkc2], axis=0).astype(k.dtype),
                jnp.concatenate([dv, dvc + dvc2], axis=0).astype(v.dtype), db)

    op.defvjp(lambda q, k, v, bias: (op(q, k, v, bias), (q, k, v, bias)), op_bwd)
    return op(q, k, v, bias)


def _runs(src):
    src = np.asarray(src)
    out, i = [], 0
    while i < len(src):
        j = i + 1
        if src[i] < 0:
            while j < len(src) and src[j] < 0:
                j += 1
            out.append((-1, j - i))
        else:
            while j < len(src) and src[j] == src[j - 1] + 1:
                j += 1
            out.append((int(src[i]), j - i))
        i = j
    return out


def _take_cols(w, src):
    pieces = [jnp.zeros(w.shape[:-1] + (ln,), w.dtype) if s < 0 else w[..., s:s + ln] for s, ln in _runs(src)]
    return pieces[0] if len(pieces) == 1 else jnp.concatenate(pieces, axis=-1)


def _take_cols_slabs(slabs, src):
    width = slabs[0].shape[-1]
    pieces = []
    for s, ln in _runs(src):
        if s < 0:
            pieces.append(jnp.zeros(slabs[0].shape[:-1] + (ln,), slabs[0].dtype))
        while s >= 0 and ln > 0:
            off = s % width
            take = min(ln, width - off)
            pieces.append(slabs[s // width][..., off:off + take])
            s, ln = s + take, ln - take
    return pieces[0] if len(pieces) == 1 else jnp.concatenate(pieces, axis=-1)


def _untake_cols(parts, n_cols, n_slabs=1):
    found = []
    for arr, src in parts:
        pos = 0
        for s, ln in _runs(src):
            if s >= 0:
                found.append((s, ln, arr, pos))
            pos += ln
    found.sort(key=lambda t: t[0])
    width = n_cols // n_slabs
    slabs, at = [[] for _ in range(n_slabs)], 0
    for s, ln, arr, pos in found:
        assert s == at, (s, at)
        at += ln
        while ln > 0:
            take = min(ln, width - s % width)
            slabs[s // width].append(arr[..., pos:pos + take])
            s, pos, ln = s + take, pos + take, ln - take
    assert at == n_cols, (at, n_cols)
    return [jnp.concatenate(p, axis=-1) for p in slabs]


def _pad_heads(base, heads, real, width):
    return np.concatenate([np.concatenate([base + h * real + np.arange(real), -np.ones(width - real, np.int64)])
                           for h in range(heads)])


def _rope_heads(base, heads):
    z = -np.ones(32, np.int64)
    return np.concatenate([np.concatenate([base + h * 64 + np.arange(32), z, base + h * 64 + 32 + np.arange(32), z])
                           for h in range(heads)])


def _lane_block(base, n):
    return np.concatenate([base + np.arange(n), -np.ones(LANES - n, np.int64)])


def _in_groups():
    g0, n0, d0, m0, t0 = 0, 1568, 3104, 5168, 6720
    rng = lambda a, n: a + np.arange(n)
    return [
        ("gla_qk", np.concatenate([_rope_heads(g0, GLA_H), _rope_heads(g0 + 256, GLA_H)])),
        ("gla_v", rng(g0 + 512, 512)),
        ("gla_g", rng(g0 + 1024, 512)),
        ("gla_lr", _lane_block(g0 + 1536, 2 * GLA_LR)),
        ("na_q", rng(n0, 512)), ("na_k", rng(n0 + 512, 512)), ("na_v", rng(n0 + 1024, 512)),
        ("gdn_qkv", rng(d0, 1536)),
        ("gdn_z", rng(d0 + 1536, 512)),
        ("gdn_sm", np.concatenate([_lane_block(d0 + 2048 + 4 * i, GDN_H) for i in range(4)])),
        ("m2_z", _pad_heads(m0, M2_H, M2_P, LANES)),
        ("m2_xs", _pad_heads(m0 + 512, M2_H, M2_P, LANES)),
        ("m2_bc", rng(m0 + 1024, 512)),
        ("m2_dt", np.concatenate([_lane_block(m0 + 1536, M2_H), _lane_block(m0 + 1536 + M2_H, M2_H)])),
        ("gate", rng(t0, 4 * D_MODEL)),
    ]


_M2_PAD = _pad_heads(0, M2_H, M2_P, LANES)
_GLA_PAD = _rope_heads(0, GLA_H)


def _row3(v):
    return v.reshape((1, 1, -1))


def _dir_rows(p, n):
    return _row3(jnp.concatenate([_take_cols(p[d][None], _lane_block(0, n)) for d in range(2)], axis=-1))


def _rope_tables(n_lat, n_ctx):
    n_freq = GLA_DK // 4
    freqs = ROPE_BASE ** (-jnp.arange(n_freq, dtype=F32) / n_freq)
    t = jnp.arange(n_lat)
    row = (t // GRID_W).astype(F32)
    colv = (t % GRID_W).astype(F32)
    ang = jnp.concatenate([row[:, None] * freqs, colv[:, None] * freqs], axis=-1)
    c, s = jnp.cos(ang), jnp.sin(ang)
    one, zero = jnp.ones_like(c), jnp.zeros_like(c)
    cos_t = jnp.concatenate([c, one, c, one], axis=-1)
    sin_t = jnp.concatenate([-s, zero, s, zero], axis=-1)
    return (jnp.concatenate([cos_t, jnp.ones((n_ctx, LANES), F32)], axis=0),
            jnp.concatenate([sin_t, jnp.zeros((n_ctx, LANES), F32)], axis=0))


def _na_bias(rpb):
    case = np.arange(NA_WIN_R)
    r = np.arange(NA_WIN_R)
    dr = r[None, :] - case[:, None] + NA_WIN_R - 1
    ci = np.arange(GRID_W)
    dc = np.clip(ci[None, :] - ci[:, None], 1 - NA_WIN_C, NA_WIN_C - 1) + NA_WIN_C - 1
    c0 = np.clip(ci - NA_WIN_C // 2, 0, GRID_W - NA_WIN_C)
    ok = (ci[None, :] >= c0[:, None]) & (ci[None, :] < c0[:, None] + NA_WIN_C)
    pick_r = np.zeros((NA_WIN_R, NA_WIN_R, 2 * NA_WIN_R - 1), np.float32)
    pick_r[case[:, None], r[None, :], dr] = 1.0
    pick_c = np.zeros((2 * NA_WIN_C - 1, GRID_W, GRID_W), np.float32)
    pick_c[dc, ci[:, None], ci[None, :]] = 1.0
    rows = jnp.einsum("hdk,crd->hcrk", rpb, pick_r, precision=HI)
    tbl = jnp.einsum("hcrk,kij->hcirj", rows, pick_c, precision=HI)
    tbl = jnp.where(ok[None, None, :, None, :], tbl, NEG_INF)
    return tbl.reshape((NA_H, NA_WIN_R, GRID_W, NA_WIN_R * GRID_W))


def _layer(l, xs, mod, small, slots, gath, win, tables, *, n_lat):
    rw = functools.partial(rowwise, n_lat=n_lat)
    rwb = functools.partial(rowwise, n_lat=n_lat, out_dtype=BF16)
    nm = lambda s: "l%d_%s" % (l, s)
    sl = slots[l]
    sh1, sc1, g1, sh2, sc2, g2 = [mod[:, i * D_MODEL:(i + 1) * D_MODEL].reshape((2, 1, D_MODEL)) for i in range(6)]
    (h,) = rwb(_f_modnorm, [xs], [], [_row3(small["norm1_g"][l]), sc1, sh1], out_widths=[D_MODEL], tile=256,
              name=nm("norm1"))
    logits = ("gla_lr", "gdn_sm", "m2_dt")
    p = {g: linear(h, win[l][g], sl["in_" + g], name=nm("in_" + g), out_dtype=F32 if g in logits else BF16)
         for g, _ in _in_groups()}

    a2 = small["gla_a2"][l]
    a2p = jnp.concatenate([
        jnp.concatenate([_take_cols(a2[0], _GLA_PAD), jnp.zeros((GLA_LR, 512), F32)], axis=1),
        jnp.concatenate([jnp.zeros((GLA_LR, 512), F32), _take_cols(a2[1], _GLA_PAD)], axis=1),
        jnp.zeros((LANES - 2 * GLA_LR, 1024), F32)], axis=0)[None]
    abp = _row3(jnp.concatenate([_take_cols(small["gla_ab"][l][d][None], _GLA_PAD) for d in range(2)], axis=-1))
    (la,) = rw(_f_gla_prep, [p["gla_lr"]], [], [a2p, abp], out_widths=[1024], tile=256, name=nm("gla_prep"))
    whole = lambda off, width: (off, width, 1)
    o = chunk_scan(_gla_step, [p["gla_qk"], p["gla_v"]], [[whole(0, BRANCH_W), whole(BRANCH_W, BRANCH_W)], [whole(0, BRANCH_W)]],
                   [la], [[whole(0, BRANCH_W)]], tables, heads=1, state_shape=(GLA_H * GLA_DV, LANES), out_w=BRANCH_W,
                   n_lat=n_lat, name=nm("gla_scan"))
    (ya,) = rwb(_f_headnorm_gate, [o, p["gla_g"]], [], [_row3(small["gla_norm_g"][l])], out_widths=[BRANCH_W], tile=256,
               name=nm("gla_fin"))

    yb = natten(p["na_q"], p["na_k"], p["na_v"], _na_bias(small["na_rpb"][l]), n_lat=n_lat, name=nm("na"))

    cq = conv_silu(p["gdn_qkv"], small["gdn_conv"][l], jnp.zeros((1, 3 * BRANCH_W), F32), n_lat=n_lat, name=nm("gdn_conv"))
    beta, la = rw(_f_gdn_prep, [p["gdn_sm"]], [], [_dir_rows(small["gdn_a_log"][l], GDN_H), _dir_rows(small["gdn_dt_bias"][l], GDN_H)],
                  out_widths=[256, 256], tile=256, name=nm("gdn_prep"))
    o = chunk_scan(_gdn_step, [cq], [[whole(0, 512), whole(512, 512), whole(1024, 512)]], [beta, la],
                   [[whole(0, LANES)], [whole(0, LANES)]], [], heads=1, state_shape=(GDN_H * GDN_D, GDN_D),
                   out_w=BRANCH_W, n_lat=n_lat, name=nm("gdn_scan"), keep_shape=(GDN_H * CHUNK, GDN_H * CHUNK))
    (yc,) = rwb(_f_headnorm_gate, [o, p["gdn_z"]], [], [_row3(small["gdn_norm_g"][l])], out_widths=[BRANCH_W], tile=256,
               name=nm("gdn_fin"))

    cw, cb = small["m2_conv"][l], small["m2_conv_b"][l][None]
    cxs = conv_silu(p["m2_xs"], _take_cols(cw[:, :512], _M2_PAD), _take_cols(cb[:, :512], _M2_PAD), n_lat=n_lat,
                    name=nm("m2_conv_x"))
    cbc = conv_silu(p["m2_bc"], cw[:, 512:], cb[:, 512:], n_lat=n_lat, name=nm("m2_conv_bc"))
    dt, la = rw(_f_m2_prep, [p["m2_dt"]], [], [_dir_rows(small["m2_a_log"][l], M2_H), _dir_rows(small["m2_dt_bias"][l], M2_H)],
                out_widths=[256, 256], tile=256, name=nm("m2_prep"))
    o = chunk_scan(_ssd_step, [cxs, cbc], [[whole(0, 2 * BRANCH_W)], [whole(0, BRANCH_W)]], [dt, la],
                   [[whole(0, LANES)], [whole(0, LANES)]], [], heads=1, state_shape=(M2_H * M2_N, LANES),
                   out_w=2 * BRANCH_W, n_lat=n_lat, name=nm("m2_scan"))
    dskip = _row3(jnp.repeat(small["m2_d"][l], LANES))
    (yd,) = rwb(_f_m2_fin, [o, p["m2_z"], cxs], [], [dskip, _row3(_take_cols(small["m2_norm_g"][l][None], _M2_PAD))],
               out_widths=[2 * BRANCH_W], tile=128, name=nm("m2_fin"))

    wb = gath["w_branch"]
    zs = [linear(y, wb, sl["w_branch%d" % i], name=nm("branch%d" % i), layout="col", prefix=(l, i), out_dtype=BF16)
          for i, y in enumerate((ya, yb, yc))]
    wb3 = wb[:, l, 3].reshape((N_SLOT, M2_H, M2_P, BRANCH_W))
    wb3 = jnp.pad(wb3, ((0, 0), (0, 0), (0, LANES - M2_P), (0, 0))).reshape((N_SLOT, 2 * BRANCH_W, BRANCH_W))
    zs.append(linear(yd, wb3, sl["w_branch3"], name=nm("branch3"), layout="col", out_dtype=BF16))
    (merged,) = rwb(_f_merge, [p["gate"]] + zs, [], [_row3(small["b_merge"][l].reshape(-1))], out_widths=[D_MODEL], tile=64,
                   name=nm("merge"))
    y = linear(merged, gath["w_out"], sl["w_out"], name=nm("out"), layout="row", prefix=(l,))
    (x1,) = rw(_f_resid, [xs, y], [], [g1], out_widths=[D_MODEL], tile=256, name=nm("res1"))

    (h2,) = rwb(_f_modnorm, [x1], [], [_row3(small["norm2_g"][l]), sc2, sh2], out_widths=[D_MODEL], tile=256,
               name=nm("norm2"))
    u1 = linear(h2, gath["w_ffn1"], sl["w_ffn1"], name=nm("ffn1"), layout="col", prefix=(l,), out_dtype=BF16)
    u3 = linear(h2, gath["w_ffn3"], sl["w_ffn3"], name=nm("ffn3"), layout="col", prefix=(l,), out_dtype=BF16)
    (act,) = rwb(_f_swiglu, [u1, u3], [], [], out_widths=[D_FF], tile=128, name=nm("swiglu"))
    f = linear(act, gath["w_ffn2"], sl["w_ffn2"], name=nm("ffn2"), layout="row", prefix=(l,))
    (x2,) = rw(_f_resid, [x1, f], [], [g2], out_widths=[D_MODEL], tile=256, name=nm("res2"))
    return x2


def _slot_shapes():
    s = {"w_out": (N_SLOT, D_MODEL // N_SLOT, D_MODEL),
         "w_ffn1": (N_SLOT, D_MODEL, D_FF // N_SLOT), "w_ffn3": (N_SLOT, D_MODEL, D_FF // N_SLOT),
         "w_ffn2": (N_SLOT, D_FF // N_SLOT, D_MODEL), "w_branch3": (N_SLOT, 2 * BRANCH_W, BRANCH_W)}
    for i in range(3):
        s["w_branch%d" % i] = (N_SLOT, BRANCH_W, BRANCH_W)
    for g, src in _in_groups():
        s["in_" + g] = (D_MODEL, len(src))
    return s


ADA_ROWS = 2 * SUBLANES


def ada_shard(c_all, c_ctx, w_ada, slots):
    cc = jnp.concatenate([c_all, c_ctx[None], jnp.zeros((ADA_ROWS - c_all.shape[0] - 1, D_MODEL), F32)], axis=0)
    (act,) = rowwise(_f_silu, [cc], [], [], out_widths=[D_MODEL], tile=ADA_ROWS, n_lat=ADA_ROWS, name="ada_silu",
                     out_dtype=BF16)
    return [linear(act, w_ada, slots[l], name="l%d_ada" % l, prefix=(l,)) for l in range(DEPTH)]


def _local_loss(diff, fixed, *, n_lat):
    small = diff["small"]
    n_ctx = fixed["ctx"].shape[0]
    xs = jnp.concatenate([diff["x"], fixed["ctx"]], axis=0)
    tables = _rope_tables(n_lat, n_ctx)
    for l in range(DEPTH):
        xs = _layer(l, xs, diff["mod"][l], small, diff["slots"], fixed["gath"], fixed["win"], tables, n_lat=n_lat)
    (lrow,) = rowwise(_f_loss, [xs[:n_lat]], [fixed["target"]], [_row3(small["final_norm_g"])], out_widths=[LANES],
                      tile=256, n_lat=n_lat, name="loss")
    return jnp.sum(lrow)


def _place():
    x, y, c = lax.axis_index("x"), lax.axis_index("y"), lax.axis_index("c")
    chips = [(1 - x, y), (x, 1 - y), (1 - x, 1 - y)]
    return x, y, c, (x, y, 1 - c), chips


def _remote(src, dst, send_sem, recv_sem, dev):
    return pltpu.make_async_remote_copy(src_ref=src, dst_ref=dst, send_sem=send_sem, recv_sem=recv_sem,
                                        device_id=dev, device_id_type=MESH)


def _dma_sems(*shape):
    return pltpu.SemaphoreType.DMA(shape)


def place_shard(w, slot, *, name):
    depth, k, n = w.shape
    tr = _tile(k, max(2 * SUBLANES, (1 << 19) // n // (2 * SUBLANES) * (2 * SUBLANES)), 2 * SUBLANES)

    def body(s_ref, w_ref, o_ref):
        o_ref[...] = w_ref[...].astype(o_ref.dtype)

    return pl.pallas_call(
        body, name=name,
        grid_spec=pltpu.PrefetchScalarGridSpec(
            num_scalar_prefetch=1, grid=(depth, k // tr),
            in_specs=[pl.BlockSpec((None, tr, n), lambda l, i, s: (l, i, 0))],
            out_specs=pl.BlockSpec((None, None, tr, n), lambda l, i, s: (s[0], l, i, 0))),
        out_shape=jax.ShapeDtypeStruct((N_SLOT, depth, k, n), BF16),
        compiler_params=_cparams("parallel", "parallel"))(slot, w)


def gather_weights(bufs):
    n = len(bufs)

    def body(*refs):
        o = refs[n:2 * n]
        send1, recv1, send2, recv2 = refs[2 * n:]
        x, y, c, sibling, chips = _place()
        g = 2 * x + y
        sent = []
        for k in range(n):
            for j, (cx, cy) in enumerate(chips):
                cp = _remote(o[k].at[g, c], o[k].at[g, c], send1.at[k, j], recv1.at[k, j], (cx, cy, c))
                cp.start()
                sent.append(cp)
        for k in range(n):
            for j, (cx, cy) in enumerate(chips):
                gj = 2 * cx + cy
                _remote(o[k].at[g, c], o[k].at[gj, c], send1.at[k, j], recv1.at[k, j], (cx, cy, c)).wait_recv()
                cp = _remote(o[k].at[gj, c], o[k].at[gj, c], send2.at[k, j], recv2.at[k, j], sibling)
                cp.start()
                sent.append(cp)
        for k in range(n):
            for j, (cx, cy) in enumerate(chips):
                gj = 2 * cx + cy
                _remote(o[k].at[gj, 1 - c], o[k].at[gj, 1 - c], send2.at[k, j], recv2.at[k, j], sibling).wait_recv()
        for cp in sent:
            cp.wait_send()

    return pl.pallas_call(
        body, name="gather_weights", in_specs=[ANY] * n, out_specs=[ANY] * n,
        out_shape=[jax.ShapeDtypeStruct(b.shape, b.dtype) for b in bufs],
        input_output_aliases={k: k for k in range(n)},
        scratch_shapes=[_dma_sems(n, 3), _dma_sems(n, 3), _dma_sems(n, 3), _dma_sems(n, 3)],
    )(*bufs)


def allgather_small(buf, *, name):
    m_per = buf.shape[0]

    def body(x_ref, out_ref, send_sems, recv_sems, local_sem):
        x, y, c, sibling, chips = _place()
        me = (x, y, c)

        def rows(px, py, pc):
            return out_ref.at[pl.ds((4 * px + 2 * py + pc) * m_per, m_per), :]

        def copy(k, block, to, src=None):
            return _remote(rows(*block) if src is None else src, rows(*block), send_sems.at[k], recv_sems.at[k], to)

        mine = pltpu.make_async_copy(x_ref, rows(*me), local_sem)
        mine.start()
        first = [copy(0, me, sibling, src=x_ref)]
        first += [copy(1 + j, me, (*chip, c), src=x_ref) for j, chip in enumerate(chips)]
        for cp in first:
            cp.start()
        passed = [copy(4 + j, (*chip, c), sibling) for j, chip in enumerate(chips)]
        for j, chip in enumerate(chips):
            copy(1 + j, (*chip, c), me).wait_recv()
            passed[j].start()
        copy(0, sibling, me).wait_recv()
        for j, chip in enumerate(chips):
            copy(4 + j, (*chip, 1 - c), me).wait_recv()
        for cp in first + passed:
            cp.wait_send()
        mine.wait()

    return pl.pallas_call(
        body, name=name, out_shape=jax.ShapeDtypeStruct((8 * m_per, LANES), buf.dtype),
        in_specs=[pl.BlockSpec(memory_space=pltpu.VMEM)], out_specs=pl.BlockSpec(memory_space=pltpu.VMEM),
        scratch_shapes=[_dma_sems(7), _dma_sems(7), pltpu.SemaphoreType.DMA],
        compiler_params=pltpu.CompilerParams(vmem_limit_bytes=VMEM_LIMIT),
    )(buf)


def sum_blocks(stacked, n_blocks, *, name):
    m = stacked.shape[0] // n_blocks
    width = stacked.shape[1]
    x3 = stacked.reshape((n_blocks, m, width))
    tr = _tile(m, max(SUBLANES, (1 << 18) // width // SUBLANES * SUBLANES), SUBLANES)

    def body(x_ref, o_ref):
        acc = x_ref[0]
        for s in range(1, n_blocks):
            acc = acc + x_ref[s]
        o_ref[...] = acc

    return pl.pallas_call(body, name=name, grid=(m // tr,), in_specs=[pl.BlockSpec((n_blocks, tr, width), lambda i: (0, i, 0))],
                          out_specs=pl.BlockSpec((tr, width), lambda i: (i, 0)),
                          out_shape=jax.ShapeDtypeStruct((m, width), F32), compiler_params=_cparams("parallel"))(x3)


def reduce_pair(gs):
    n = len(gs)

    def body(*refs):
        g, r = refs[:n], refs[n:2 * n]
        send, recv = refs[2 * n:]
        x, y, c, sibling, _ = _place()
        cps = []
        for i in range(n):
            k2 = gs[i].shape[1] // 2
            cp = _remote(g[i].at[:, pl.ds((1 - c) * k2, k2), :], r[i], send.at[i], recv.at[i], sibling)
            cp.start()
            cps.append(cp)
        for cp in cps:
            cp.wait()

    return pl.pallas_call(
        body, name="reduce_pair", in_specs=[ANY] * n, out_specs=[ANY] * n,
        out_shape=[jax.ShapeDtypeStruct((g.shape[0], g.shape[1] // 2, g.shape[2]), g.dtype) for g in gs],
        scratch_shapes=[_dma_sems(n), _dma_sems(n)],
    )(*gs)


def _row_tile(rows, width, budget, mult):
    return _tile(rows, max(mult, budget // width // mult * mult), mult)


def add_own_half(g, recv, core, *, name):
    n_slot, k2, width = recv.shape
    tr = _row_tile(k2, width, 1 << 19, 2 * SUBLANES)
    nb = k2 // tr

    def body(c_ref, g_ref, r_ref, o_ref):
        o_ref[...] = (g_ref[...] + r_ref[...]).astype(o_ref.dtype)

    spec = pl.BlockSpec((None, tr, width), lambda s, i, c: (s, i, 0))
    return pl.pallas_call(
        body, name=name,
        grid_spec=pltpu.PrefetchScalarGridSpec(
            num_scalar_prefetch=1, grid=(n_slot, nb),
            in_specs=[pl.BlockSpec((None, tr, width), lambda s, i, c: (s, c[0] * nb + i, 0)), spec], out_specs=spec),
        out_shape=jax.ShapeDtypeStruct(recv.shape, BF16), compiler_params=_cparams("parallel", "parallel"))(core, g, recv)


def reduce_chips(qs):
    n = len(qs)

    def body(*refs):
        q, r = refs[:n], refs[n:2 * n]
        send, recv = refs[2 * n:]
        x, y, c, _, chips = _place()
        cps = []
        for i in range(n):
            for j, (cx, cy) in enumerate(chips):
                cp = _remote(q[i].at[2 * cx + cy], r[i].at[j], send.at[i, j], recv.at[i, j], (cx, cy, c))
                cp.start()
                cps.append(cp)
        for cp in cps:
            cp.wait()

    return pl.pallas_call(
        body, name="reduce_chips", in_specs=[ANY] * n, out_specs=[ANY] * n,
        out_shape=[jax.ShapeDtypeStruct((3,) + q.shape[1:], q.dtype) for q in qs],
        scratch_shapes=[_dma_sems(n, 3), _dma_sems(n, 3)],
    )(*qs)


def chip_sum(q, recv, place, grad, layer, *, name):
    _, k2, width = recv.shape
    tr = _row_tile(k2, width, 1 << 18, 2 * SUBLANES)
    nb = k2 // tr

    def body(p_ref, q_ref, r_ref, grad_ref, o_ref):
        acc = q_ref[...].astype(F32)
        for j in range(3):
            acc = acc + r_ref[j].astype(F32)
        o_ref[...] = acc

    return pl.pallas_call(
        body, name=name,
        grid_spec=pltpu.PrefetchScalarGridSpec(
            num_scalar_prefetch=1, grid=(nb,),
            in_specs=[pl.BlockSpec((None, tr, width), lambda i, p: (p[0], i, 0)),
                      pl.BlockSpec((3, tr, width), lambda i, p: (0, i, 0)), ANY],
            out_specs=pl.BlockSpec((None, tr, width), lambda i, p: (layer, p[1] * nb + i, 0))),
        out_shape=jax.ShapeDtypeStruct(grad.shape, F32), input_output_aliases={3: 0},
        compiler_params=_cparams("parallel"))(place, q, recv, grad)


def swap_pair(grads):
    n = len(grads)

    def body(*refs):
        o = refs[n:2 * n]
        send, recv = refs[2 * n:]
        x, y, c, sibling, _ = _place()
        cps = []
        for k in range(n):
            k2 = grads[k].shape[1] // 2
            for l in range(DEPTH):
                half = o[k].at[l, pl.ds(c * k2, k2), :]
                cp = _remote(half, half, send.at[k, l], recv.at[k, l], sibling)
                cp.start()
                cps.append(cp)
        for cp in cps:
            cp.wait()

    return pl.pallas_call(
        body, name="swap_pair", in_specs=[ANY] * n, out_specs=[ANY] * n,
        out_shape=[jax.ShapeDtypeStruct(g.shape, g.dtype) for g in grads],
        input_output_aliases={k: k for k in range(n)},
        scratch_shapes=[_dma_sems(n, DEPTH), _dma_sems(n, DEPTH)],
    )(*grads)


def adamw(w, g, m, v, *, name):
    rows, width = w.shape
    tr = _tile(rows, max(SUBLANES, (1 << 19) // width // SUBLANES * SUBLANES), SUBLANES)

    def body(w_ref, g_ref, m_ref, v_ref, d_ref, mo_ref, vo_ref):
        gv = g_ref[...]
        mn = ADAM_B1 * m_ref[...] + (1.0 - ADAM_B1) * gv
        vn = ADAM_B2 * v_ref[...] + (1.0 - ADAM_B2) * (gv * gv)
        m_hat = mn / (1.0 - ADAM_B1 ** ADAM_STEP)
        v_hat = vn / (1.0 - ADAM_B2 ** ADAM_STEP)
        d_ref[...] = -ADAM_LR * (m_hat / (jnp.sqrt(v_hat) + ADAM_EPS) + ADAM_WD * w_ref[...])
        mo_ref[...] = mn
        vo_ref[...] = vn

    spec = pl.BlockSpec((tr, width), lambda i: (i, 0))
    sds = jax.ShapeDtypeStruct((rows, width), F32)
    return pl.pallas_call(body, name=name, grid=(rows // tr,), in_specs=[spec] * 4, out_specs=(spec,) * 3,
                          out_shape=(sds,) * 3, compiler_params=_cparams("parallel"))(w, g, m, v)


def _pack(arrs):
    flat = jnp.concatenate([a.reshape(-1) for a in arrs])
    pad = (-flat.shape[0]) % (SUBLANES * LANES)
    return jnp.pad(flat, (0, pad)).reshape((-1, LANES))


def _unpack(buf, shapes):
    flat, out, at = buf.reshape(-1), [], 0
    for s in shapes:
        size = int(np.prod(s))
        out.append(flat[at:at + size].reshape(s))
        at += size
    return out


BIG = ["w_in", "w_branch", "w_out", "w_ffn1", "w_ffn3", "w_ffn2"]
SMALL_SHARDED = ["b_merge", "gla_a2", "gla_ab", "gdn_conv", "m2_conv"]
SMALL_WHOLE = ["norm1_g", "norm2_g", "gla_norm_g", "na_rpb", "gdn_a_log", "gdn_dt_bias", "gdn_norm_g",
               "m2_conv_b", "m2_a_log", "m2_dt_bias", "m2_d", "m2_norm_g", "final_norm_g"]
WEIGHTS = ["c_ctx", "norm1_g", "norm2_g", "w_ada", "b_ada", "w_in", "b_merge", "gla_a2", "gla_ab", "gla_norm_g", "na_rpb",
           "gdn_conv", "gdn_a_log", "gdn_dt_bias", "gdn_norm_g", "m2_conv", "m2_conv_b", "m2_a_log", "m2_dt_bias", "m2_d",
           "m2_norm_g", "w_branch", "w_out", "w_ffn1", "w_ffn3", "w_ffn2", "final_norm_g"]


def _step(a):
    n_lat = a["x"].shape[1]
    x_i, y_i, c_i = lax.axis_index("x"), lax.axis_index("y"), lax.axis_index("c")
    slot = 2 * x_i + y_i

    slot_arr = slot.astype(jnp.int32).reshape((1,))
    core = c_i.astype(jnp.int32).reshape((1,))
    placed = [place_shard(a[n].reshape((DEPTH, -1, a[n].shape[-1])), slot_arr, name="place_" + n) for n in BIG]
    gath = dict(zip(BIG, gather_weights(placed)))
    gath["w_branch"] = gath["w_branch"].reshape((N_SLOT, DEPTH, 4, BRANCH_W, BRANCH_W))
    shard_shapes = [a[n].shape for n in SMALL_SHARDED]
    own = _pack([a[n] for n in SMALL_SHARDED])
    everyone = allgather_small(own, name="gather_small").reshape((8,) + own.shape)
    per_slot = [_unpack(everyone[2 * s], shard_shapes) for s in range(N_SLOT)]
    small = {n: jnp.concatenate([per_slot[s][i] for s in range(N_SLOT)], axis=-1) for i, n in enumerate(SMALL_SHARDED)}
    small.update({n: a[n] for n in SMALL_WHOLE})

    me = 4 * x_i + 2 * y_i + c_i
    ada_cols = a["w_ada"].shape[-1]
    c_all = allgather_small(a["c"].reshape((-1, LANES)), name="gather_c").reshape((8, D_MODEL))
    ada_slots = [jnp.zeros(a["w_ada"].shape[1:], F32) for _ in range(DEPTH)]
    mod_shards, ada_vjp = jax.vjp(lambda c_ctx, sl: ada_shard(c_all, c_ctx, a["w_ada"], sl), a["c_ctx"], ada_slots)
    packed = _pack(mod_shards)
    every = allgather_small(packed, name="gather_ada").reshape((8,) + packed.shape)
    by_slot = [_unpack(every[2 * s], [(ADA_ROWS, ada_cols)] * DEPTH) for s in range(N_SLOT)]
    mod = []
    for l in range(DEPTH):
        rows = jnp.concatenate([by_slot[s][l] for s in range(N_SLOT)], axis=-1) + a["b_ada"][l]
        mod.append(jnp.concatenate([lax.dynamic_slice_in_dim(rows, me, 1, axis=0), rows[8:9]], axis=0))

    groups = _in_groups()
    win = []
    for l in range(DEPTH):
        slabs = [gath["w_in"][s, l] for s in range(N_SLOT)]
        win.append({g: _take_cols_slabs(slabs, src) for g, src in groups})
    slots = [{n: jnp.zeros(s, F32) for n, s in _slot_shapes().items()} for _ in range(DEPTH)]
    diff = {"x": a["x"][0], "mod": mod, "small": small, "slots": slots}
    fixed = {"ctx": a["ctx"][0], "target": a["loss_target"][0], "gath": gath, "win": win}
    loss, grads = jax.value_and_grad(lambda d: _local_loss(d, fixed, n_lat=n_lat))(diff)

    dmod = _pack(grads["mod"])
    every = allgather_small(dmod, name="gather_dmod").reshape((8,) + dmod.shape)
    per_dev = [_unpack(every[i], [(2, 6 * D_MODEL)] * DEPTH) for i in range(8)]
    grad_b_ada, cots = [], []
    for l in range(DEPTH):
        lat = jnp.concatenate([per_dev[i][l][0:1] for i in range(8)], axis=0)
        ctx_rows = jnp.concatenate([per_dev[i][l][1].reshape((-1, LANES)) for i in range(8)], axis=0)
        ctx_sum = sum_blocks(ctx_rows, 8, name="l%d_dmod_ctx_sum" % l).reshape((1, 6 * D_MODEL))
        all_rows = jnp.concatenate([lat, ctx_sum, jnp.zeros((ADA_ROWS - 9, 6 * D_MODEL), F32)], axis=0)
        grad_b_ada.append(sum_blocks(all_rows.reshape((-1, LANES)), ADA_ROWS, name="l%d_b_ada_sum" % l).reshape(-1))
        cots.append(lax.dynamic_slice_in_dim(all_rows, slot * ada_cols, ada_cols, axis=1))
    c_ctx_part, ada_grads = ada_vjp(cots)

    parts = []
    for n in BIG:
        for l in range(DEPTH):
            sl = grads["slots"][l]
            if n == "w_in":
                parts.append(jnp.stack(_untake_cols([(sl["in_" + g], src) for g, src in groups], IN_COLS, N_SLOT)))
            elif n == "w_branch":
                b3 = sl["w_branch3"].reshape((N_SLOT, M2_H, LANES, BRANCH_W))[:, :, :M2_P].reshape((N_SLOT, BRANCH_W, BRANCH_W))
                parts.append(jnp.concatenate([sl["w_branch0"], sl["w_branch1"], sl["w_branch2"], b3], axis=1))
            else:
                parts.append(sl[n])
    from_sibling = reduce_pair(parts)
    pair_sums = [add_own_half(g, r, core, name="pair_sum%d" % i) for i, (g, r) in enumerate(zip(parts, from_sibling))]
    from_chips = reduce_chips(pair_sums)
    place = jnp.stack([slot, c_i]).astype(jnp.int32)
    reduced = []
    for k, n in enumerate(BIG):
        grad = lax.empty((DEPTH,) + parts[DEPTH * k].shape[1:], F32)
        for l in range(DEPTH):
            i = DEPTH * k + l
            grad = chip_sum(pair_sums[i], from_chips[i], place, grad, l, name="chip_sum%d" % i)
        reduced.append(grad)
    big_grads = {n: g.reshape(a[n].shape) for n, g in zip(BIG, swap_pair(reduced))}

    summed = SMALL_WHOLE + SMALL_SHARDED + ["c_ctx"]
    local = dict(grads["small"], c_ctx=0.5 * c_ctx_part)
    partial = _pack([local[n] for n in summed] + [loss.reshape((1,))])
    total = sum_blocks(allgather_small(partial, name="gather_small_grads"), 8, name="sum_small_grads")
    pieces = _unpack(total, [local[n].shape for n in summed] + [(1,)])
    small_grads = dict(zip(summed, pieces[:-1]))
    for n in SMALL_SHARDED:
        width = a[n].shape[-1]
        small_grads[n] = lax.dynamic_slice_in_dim(small_grads[n], slot * width, width, axis=-1)
    small_grads["b_ada"] = jnp.stack(grad_b_ada)
    big_grads["w_ada"] = jnp.stack(ada_grads)
    small_names = summed + ["b_ada"]
    loss_all = pieces[-1].reshape(())

    grad_w, delta, new_m, new_v = {}, {}, {}, {}
    two_d = lambda t: t.reshape((-1, t.shape[-1]))
    for n in BIG + ["w_ada"]:
        d, mn, vn = adamw(two_d(a[n]), two_d(big_grads[n]), two_d(a["m_" + n]), two_d(a["v_" + n]), name="adamw_" + n)
        grad_w[n], delta[n], new_m[n], new_v[n] = big_grads[n], d.reshape(a[n].shape), mn.reshape(a[n].shape), vn.reshape(a[n].shape)
    shapes = [a[n].shape for n in small_names]
    d, mn, vn = adamw(_pack([a[n] for n in small_names]), _pack([small_grads[n] for n in small_names]),
                      _pack([a["m_" + n] for n in small_names]), _pack([a["v_" + n] for n in small_names]), name="adamw_small")
    for n, dd, mm, vv in zip(small_names, _unpack(d, shapes), _unpack(mn, shapes), _unpack(vn, shapes)):
        grad_w[n], delta[n], new_m[n], new_v[n] = small_grads[n], dd, mm, vv

    return (loss_all, grads["x"][None], *[grad_w[n] for n in WEIGHTS], *[delta[n] for n in WEIGHTS],
            *[new_m[n] for n in WEIGHTS], *[new_v[n] for n in WEIGHTS])


def kernel(x, c, ctx, c_ctx, norm1_g, norm2_g, w_ada, b_ada, w_in, b_merge, gla_a2, gla_ab, gla_norm_g, na_rpb, gdn_conv, gdn_a_log, gdn_dt_bias, gdn_norm_g, m2_conv, m2_conv_b, m2_a_log, m2_dt_bias, m2_d, m2_norm_g, w_branch, w_out, w_ffn1, w_ffn3, w_ffn2, final_norm_g, loss_target, m_c_ctx, m_norm1_g, m_norm2_g, m_w_ada, m_b_ada, m_w_in, m_b_merge, m_gla_a2, m_gla_ab, m_gla_norm_g, m_na_rpb, m_gdn_conv, m_gdn_a_log, m_gdn_dt_bias, m_gdn_norm_g, m_m2_conv, m_m2_conv_b, m_m2_a_log, m_m2_dt_bias, m_m2_d, m_m2_norm_g, m_w_branch, m_w_out, m_w_ffn1, m_w_ffn3, m_w_ffn2, m_final_norm_g, v_c_ctx, v_norm1_g, v_norm2_g, v_w_ada, v_b_ada, v_w_in, v_b_merge, v_gla_a2, v_gla_ab, v_gla_norm_g, v_na_rpb, v_gdn_conv, v_gdn_a_log, v_gdn_dt_bias, v_gdn_norm_g, v_m2_conv, v_m2_conv_b, v_m2_a_log, v_m2_dt_bias, v_m2_d, v_m2_norm_g, v_w_branch, v_w_out, v_w_ffn1, v_w_ffn3, v_w_ffn2, v_final_norm_g):
    return _step(dict(locals()))
```

```python
import functools
import math

import numpy as np
import jax
import jax.numpy as jnp
from jax import lax
from jax.experimental import pallas as pl
from jax.experimental.pallas import tpu as pltpu

F32 = jnp.float32
BF16 = jnp.bfloat16
HI = lax.Precision.HIGHEST
MESH = pl.DeviceIdType.MESH
ANY = pl.BlockSpec(memory_space=pl.ANY)

VMEM_LIMIT = 56 * 1024 * 1024
LANES = 128
SUBLANES = 8

D_MODEL = 2048
DEPTH = 2
GRID_W = 64
CHUNK = 64
CONV_W = 5
RMS_EPS = 1e-6
NEG_INF = -1e30
ROPE_BASE = 10000.0
BRANCH_W = 512
GLA_H, GLA_DK, GLA_DV, GLA_LR, GLA_TAU = 4, 64, 128, 16, 16.0
NA_H, NA_D, NA_WIN_R, NA_WIN_C = 4, 128, 8, 16
GDN_H, GDN_D = 4, 128
M2_P, M2_H, M2_N, M2_G = 64, 8, 128, 2
D_FF = 5632
IN_COLS = 14912
N_SLOT = 4
ADAM_LR, ADAM_B1, ADAM_B2, ADAM_EPS, ADAM_WD, ADAM_STEP = 0.001, 0.9, 0.999, 1e-08, 0.01, 10


def _cparams(*sem):
    return pltpu.CompilerParams(dimension_semantics=sem if sem else None, vmem_limit_bytes=VMEM_LIMIT)


def _tile(n, target, mult):
    if n <= target:
        return n
    best = None
    for t in range(mult, target + 1, mult):
        if n % t == 0:
            best = t
    assert best is not None, (n, target, mult)
    return best


def _nt(a, b):
    return lax.dot_general(a.astype(BF16), b.astype(BF16), (((1,), (1,)), ((), ())), preferred_element_type=F32)


def _tn(a, b):
    return lax.dot_general(a.astype(BF16), b.astype(BF16), (((0,), (0,)), ((), ())), preferred_element_type=F32)


def _nn(a, b):
    return jnp.dot(a.astype(BF16), b.astype(BF16), preferred_element_type=F32)


def _dot3(a, b, dims):
    a_hi, b_hi = a.astype(BF16), b.astype(BF16)
    a_lo = (a - a_hi.astype(F32)).astype(BF16)
    b_lo = (b - b_hi.astype(F32)).astype(BF16)
    dot = lambda u, v: lax.dot_general(u, v, dims, preferred_element_type=F32)
    return dot(a_hi, b_hi) + (dot(a_hi, b_lo) + dot(a_lo, b_hi))


_NN, _NT, _TN = ((((1,), (0,)), ((), ())), (((1,), (1,)), ((), ())), (((0,), (0,)), ((), ())))


@jax.custom_vjp
def _nn_hi(a, b):
    return _dot3(a, b, _NN)


_nn_hi.defvjp(lambda a, b: (_dot3(a, b, _NN), (a, b)),
              lambda res, g: (_dot3(g, res[1], _NT), _dot3(res[0], g, _TN)))


def _w_spec(layout, prefix, r_idx, c_idx, br, bc, slot_dim):
    none = (None,) * len(prefix)
    if layout == "plain":
        return pl.BlockSpec(none + (br, bc), lambda i, j, k: prefix + (r_idx(i, j, k), c_idx(i, j, k)))
    if layout == "col":
        per = slot_dim // bc
        return pl.BlockSpec((None,) + none + (br, bc),
                            lambda i, j, k: (c_idx(i, j, k) // per,) + prefix + (r_idx(i, j, k), c_idx(i, j, k) % per))
    per = slot_dim // br
    return pl.BlockSpec((None,) + none + (br, bc),
                        lambda i, j, k: (r_idx(i, j, k) // per,) + prefix + (r_idx(i, j, k) % per, c_idx(i, j, k)))


def _mm(a, b, *, name, ta=False, tb=False, b_layout="plain", b_prefix=(), out_layout="plain", out_dtype=F32):
    m, kdim = (a.shape[1], a.shape[0]) if ta else a.shape
    rows, cols = b.shape[-2:]
    if b_layout == "col":
        cols *= N_SLOT
    elif b_layout == "row":
        rows *= N_SLOT
    n = rows if tb else cols
    assert (cols if tb else rows) == kdim, (a.shape, b.shape, ta, tb)
    n_unit = n // N_SLOT if (out_layout == "col" or (b_layout == ("row" if tb else "col"))) else n
    k_unit = kdim // N_SLOT if b_layout == ("col" if tb else "row") else kdim
    odd_n = n_unit % 1408 == 0 and n_unit % 512 != 0
    odd_k = k_unit % 1408 == 0 and k_unit % 512 != 0
    if ta:
        tm = _tile(m, 1024, LANES)
        tn = _tile(n_unit, 1408 if odd_n else 1024, LANES)
        tk = _tile(k_unit, 1056, 2 * SUBLANES)
    elif tb:
        tm = _tile(m, 768 if odd_k else 704, 2 * SUBLANES)
        tn = _tile(n_unit, 1408 if odd_n else 2048, LANES)
        tk = _tile(k_unit, 1408 if odd_k else 2048, LANES)
    else:
        tm = _tile(m, 768, 2 * SUBLANES)
        tn = _tile(n_unit, 1408 if odd_n else 512, LANES)
        tk = _tile(k_unit, 1408 if odd_k else 2048, LANES)
    nk = kdim // tk
    a_spec = (pl.BlockSpec((tk, tm), lambda i, j, k: (k, i)) if ta else pl.BlockSpec((tm, tk), lambda i, j, k: (i, k)))
    slot_dim = b.shape[-1] if b_layout == "col" else b.shape[-2]
    if tb:
        b_spec = _w_spec(b_layout, tuple(b_prefix), lambda i, j, k: j, lambda i, j, k: k, tn, tk, slot_dim)
    else:
        b_spec = _w_spec(b_layout, tuple(b_prefix), lambda i, j, k: k, lambda i, j, k: j, tk, tn, slot_dim)
    if out_layout == "col":
        per = (n // N_SLOT) // tn
        out_shape = jax.ShapeDtypeStruct((N_SLOT, m, n // N_SLOT), out_dtype)
        out_spec = pl.BlockSpec((None, tm, tn), lambda i, j, k: (j // per, i, j % per))
    else:
        out_shape = jax.ShapeDtypeStruct((m, n), out_dtype)
        out_spec = pl.BlockSpec((tm, tn), lambda i, j, k: (i, j))
    dims = (((0 if ta else 1,), (1 if tb else 0,)), ((), ()))

    def product(a_ref, b_ref):
        return lax.dot_general(a_ref[...].astype(BF16), b_ref[...].astype(BF16), dims, preferred_element_type=F32)

    def body_once(a_ref, b_ref, o_ref):
        o_ref[...] = product(a_ref, b_ref).astype(o_ref.dtype)

    def body(a_ref, b_ref, o_ref, acc_ref):
        k = pl.program_id(2)

        @pl.when(k == 0)
        def _():
            acc_ref[...] = jnp.zeros_like(acc_ref)

        acc_ref[...] += product(a_ref, b_ref)

        @pl.when(k == nk - 1)
        def _():
            o_ref[...] = acc_ref[...].astype(o_ref.dtype)

    return pl.pallas_call(
        body_once if nk == 1 else body, name=name, grid=(m // tm, n // tn, nk), in_specs=[a_spec, b_spec],
        out_specs=out_spec, out_shape=out_shape, scratch_shapes=[] if nk == 1 else [pltpu.VMEM((tm, tn), F32)],
        compiler_params=_cparams("parallel", "parallel", "arbitrary"),
    )(a, b)


def linear(a, w, grad_slot, *, name, layout="plain", prefix=(), out_dtype=F32):
    @jax.custom_vjp
    def f(a, w, grad_slot):
        return _mm(a, w, name=name + "_fwd", b_layout=layout, b_prefix=prefix, out_dtype=out_dtype)

    def fwd(a, w, grad_slot):
        return f(a, w, grad_slot), (a, w)

    def bwd(res, g):
        a, w = res
        da = _mm(g, w, name=name + "_dgrad", tb=True, b_layout=layout, b_prefix=prefix, out_dtype=a.dtype)
        dw = _mm(a, g, name=name + "_wgrad", ta=True, out_layout="col" if layout == "col" else "plain")
        if layout == "row":
            dw = dw.reshape((N_SLOT, dw.shape[0] // N_SLOT, dw.shape[1]))
        return da, None, dw

    f.defvjp(fwd, bwd)
    return f(a, w, grad_slot)


def _rowwise_specs(rows, consts, params, tile, seg_tile):
    def row_spec(r):
        return pl.BlockSpec((tile, r.shape[1]), lambda i: (i, 0))

    def par_spec(p):
        if p.shape[0] == 2:
            return pl.BlockSpec((None,) + p.shape[1:], lambda i: (jnp.where(i >= seg_tile, 1, 0), 0, 0))
        return pl.BlockSpec((None,) + p.shape[1:], lambda i: (0, 0, 0))

    return [row_spec(r) for r in rows], [row_spec(r) for r in consts], [par_spec(p) for p in params]


def rowwise(f, rows, consts, params, *, out_widths, tile, n_lat, name, out_dtype=F32):
    rows, consts, params = tuple(rows), tuple(consts), tuple(params)
    n_rows = rows[0].shape[0]
    tile = math.gcd(math.gcd(n_rows, n_lat), tile)
    assert tile % SUBLANES == 0
    seg_tile = n_lat // tile
    grid = (n_rows // tile,)
    nr, nc, npar = len(rows), len(consts), len(params)
    r_specs, c_specs, p_specs = _rowwise_specs(rows, consts, params, tile, seg_tile)
    out_dtypes = out_dtype if isinstance(out_dtype, (tuple, list)) else (out_dtype,) * len(out_widths)
    out_shape = tuple(jax.ShapeDtypeStruct((n_rows, w), dt) for w, dt in zip(out_widths, out_dtypes))
    out_specs = tuple(pl.BlockSpec((tile, w), lambda i: (i, 0)) for w in out_widths)
    n_out = len(out_widths)

    def fwd_call(rows, consts, params):
        def body(*refs):
            ins = [r[...].astype(F32) for r in refs[:nr + nc + npar]]
            outs = f(*ins)
            for o_ref, o in zip(refs[nr + nc + npar:], outs):
                o_ref[...] = o.astype(o_ref.dtype)

        return pl.pallas_call(body, name=name + "_fwd", grid=grid, in_specs=r_specs + c_specs + p_specs,
                              out_specs=out_specs, out_shape=out_shape,
                              compiler_params=_cparams("parallel"))(*rows, *consts, *params)

    def bwd_call(rows, consts, params, gouts):
        def body(*refs):
            i = pl.program_id(0)
            ins = [r[...].astype(F32) for r in refs[:nr + nc + npar]]
            gs = tuple(r[...].astype(F32) for r in refs[nr + nc + npar:nr + nc + npar + n_out])
            d_refs = refs[nr + nc + npar + n_out:]
            cvals = ins[nr:nr + nc]

            def g(*diff):
                return tuple(f(*diff[:nr], *cvals, *diff[nr:]))

            _, vjp = jax.vjp(g, *ins[:nr], *ins[nr + nc:])
            grads = vjp(gs)
            for d_ref, gr in zip(d_refs[:nr], grads[:nr]):
                d_ref[...] = gr.astype(d_ref.dtype)
            for p, d_ref, gr in zip(params, d_refs[nr:], grads[nr:]):
                first = (i == 0) | (i == seg_tile) if p.shape[0] == 2 else (i == 0)

                @pl.when(first)
                def _():
                    d_ref[...] = jnp.zeros_like(d_ref)

                d_ref[...] += gr

        d_shape = tuple(jax.ShapeDtypeStruct(r.shape, r.dtype) for r in rows) + tuple(
            jax.ShapeDtypeStruct(p.shape, F32) for p in params)
        g_specs = [pl.BlockSpec((tile, w), lambda i: (i, 0)) for w in out_widths]
        return pl.pallas_call(body, name=name + "_bwd", grid=grid,
                              in_specs=r_specs + c_specs + p_specs + g_specs,
                              out_specs=tuple(r_specs + p_specs), out_shape=d_shape,
                              compiler_params=_cparams("arbitrary"))(*rows, *consts, *params, *gouts)

    @jax.custom_vjp
    def op(rows, consts, params):
        return fwd_call(rows, consts, params)

    def op_fwd(rows, consts, params):
        return op(rows, consts, params), (rows, consts, params)

    def op_bwd(res, gouts):
        rows, consts, params = res
        d = bwd_call(rows, consts, params, tuple(gouts))
        return tuple(d[:nr]), tuple(None for _ in consts), tuple(d[nr:])

    op.defvjp(op_fwd, op_bwd)
    return op(rows, consts, params)


def _rms(x, width=None):
    w = x.shape[-1] if width is None else width
    return x * lax.rsqrt(jnp.sum(x * x, axis=-1, keepdims=True) * (1.0 / w) + RMS_EPS)


def _silu(x):
    return x * jax.nn.sigmoid(x)


def _f_modnorm(x, g, sc, sh):
    return ((_rms(x) * g) * (1.0 + sc) + sh,)


def _f_silu(x):
    return (_silu(x),)


def _f_gla_prep(lr, a2, ab):
    z = _nn(lr, a2) + ab
    return ((jnp.minimum(z, 0.0) - jnp.log(1.0 + jnp.exp(-jnp.abs(z)))) * (1.0 / GLA_TAU),)


def _f_headnorm_gate(o, g, ng):
    outs = []
    for h in range(BRANCH_W // LANES):
        lo = h * LANES
        oh = o[:, lo:lo + LANES] + o[:, BRANCH_W + lo:BRANCH_W + lo + LANES]
        outs.append(_rms(oh) * ng * _silu(g[:, lo:lo + LANES]))
    return (jnp.concatenate(outs, axis=-1),)


def _f_gdn_prep(x, alog, dtb):
    half = x.shape[1] // 2
    beta = jax.nn.sigmoid(x[:, :half])
    la = -jnp.exp(alog) * jax.nn.softplus(x[:, half:] + dtb)
    return beta, la


def _f_m2_prep(x, alog, dtb):
    dt = jax.nn.softplus(x + dtb)
    return dt, -jnp.exp(alog) * dt


def _f_m2_fin(o, z, xs, dskip, ng):
    w = z.shape[1]
    y = (o[:, :w] + o[:, w:] + dskip * xs) * _silu(z)
    return (_rms(y, BRANCH_W) * ng,)


def _f_merge(gate, z0, z1, z2, z3, bm):
    acc = None
    for i, z in enumerate((z0, z1, z2, z3)):
        lo = i * D_MODEL
        t = jax.nn.sigmoid(gate[:, lo:lo + D_MODEL] + bm[:, lo:lo + D_MODEL]) * z
        acc = t if acc is None else acc + t
    return (acc,)


def _f_resid(x, y, g):
    return (x + g * y,)


def _f_resid_modnorm(x, y, g, ng, sc, sh):
    x1 = x + g * y
    return (x1,) + _f_modnorm(x1, ng, sc, sh)


def _f_swiglu(u1, u3):
    return (_silu(u1) * u3,)


def _f_loss(x, tgt, g):
    e = _rms(x) * g - tgt
    per_row = 0.5 * jnp.sum(e * e, axis=-1, keepdims=True) * (1.0 / D_MODEL)
    return (jnp.broadcast_to(per_row * (1.0 / LANES), (x.shape[0], LANES)),)


_HALO = 8


def _conv_segments(n_lat, n_ctx):
    segs = [(0, _HALO, n_lat), (n_lat, n_lat + 3 * _HALO, n_ctx)]
    return segs, n_lat + n_ctx + 4 * _HALO


def _conv_stage(buf, src, n_lat, n_ctx):
    zeros = jnp.zeros((_HALO, LANES), F32)
    buf[0:_HALO, :] = zeros
    buf[_HALO:_HALO + n_lat, :] = src[0:n_lat, :].astype(F32)
    buf[n_lat + _HALO:n_lat + 2 * _HALO, :] = zeros
    buf[n_lat + 2 * _HALO:n_lat + 3 * _HALO, :] = zeros
    buf[n_lat + 3 * _HALO:n_lat + 3 * _HALO + n_ctx, :] = src[n_lat:n_lat + n_ctx, :].astype(F32)
    buf[n_lat + n_ctx + 3 * _HALO:n_lat + n_ctx + 4 * _HALO, :] = zeros


def conv_silu(x, w, b, *, n_lat, name):
    n_rows, n_ch = x.shape
    n_ctx = n_rows - n_lat
    segs, n_buf = _conv_segments(n_lat, n_ctx)
    grid = (n_ch // LANES,)
    col = lambda r: pl.BlockSpec((r, LANES), lambda j: (0, j))
    half = CONV_W // 2

    def tiles():
        for row0, off, length in segs:
            tr = _tile(length, 256, SUBLANES)
            for t0 in range(0, length, tr):
                yield row0 + t0, off + t0, tr

    def pre_act(buf, w_ref, b_ref, off, tr):
        acc = jnp.broadcast_to(b_ref[...], (tr, LANES))
        for j in range(CONV_W):
            acc = acc + w_ref[j:j + 1, :] * buf[off + j - half:off + j - half + tr, :]
        return acc

    def fwd_call(x, w, b):
        def body(x_ref, w_ref, b_ref, o_ref, buf):
            _conv_stage(buf, x_ref, n_lat, n_ctx)
            for row, off, tr in tiles():
                o_ref[row:row + tr, :] = _silu(pre_act(buf, w_ref, b_ref, off, tr))

        return pl.pallas_call(body, name=name + "_fwd", grid=grid, in_specs=[col(n_rows), col(CONV_W), col(1)],
                              out_specs=col(n_rows), out_shape=jax.ShapeDtypeStruct(x.shape, F32),
                              scratch_shapes=[pltpu.VMEM((n_buf, LANES), F32)],
                              compiler_params=_cparams("parallel"))(x, w, b)

    def bwd_call(x, w, b, g):
        def body(x_ref, w_ref, b_ref, g_ref, dx_ref, dw_ref, db_ref, xbuf, dbuf):
            _conv_stage(xbuf, x_ref, n_lat, n_ctx)
            _conv_stage(dbuf, g_ref, n_lat, n_ctx)
            dw = [jnp.zeros((1, LANES), F32) for _ in range(CONV_W)]
            db = jnp.zeros((1, LANES), F32)
            for row, off, tr in tiles():
                pre = pre_act(xbuf, w_ref, b_ref, off, tr)
                s = jax.nn.sigmoid(pre)
                dpre = g_ref[row:row + tr, :] * (s * (1.0 + pre * (1.0 - s)))
                dbuf[off:off + tr, :] = dpre
                db = db + jnp.sum(dpre, axis=0, keepdims=True)
                for j in range(CONV_W):
                    dw[j] = dw[j] + jnp.sum(dpre * xbuf[off + j - half:off + j - half + tr, :], axis=0, keepdims=True)
            for row, off, tr in tiles():
                acc = jnp.zeros((tr, LANES), F32)
                for j in range(CONV_W):
                    acc = acc + w_ref[j:j + 1, :] * dbuf[off - j + half:off - j + half + tr, :]
                dx_ref[row:row + tr, :] = acc.astype(dx_ref.dtype)
            for j in range(CONV_W):
                dw_ref[j:j + 1, :] = dw[j]
            db_ref[...] = db

        return pl.pallas_call(
            body, name=name + "_bwd", grid=grid, in_specs=[col(n_rows), col(CONV_W), col(1), col(n_rows)],
            out_specs=(col(n_rows), col(CONV_W), col(1)),
            out_shape=(jax.ShapeDtypeStruct(x.shape, x.dtype), jax.ShapeDtypeStruct(w.shape, F32),
                       jax.ShapeDtypeStruct(b.shape, F32)),
            scratch_shapes=[pltpu.VMEM((n_buf, LANES), F32), pltpu.VMEM((n_buf, LANES), F32)],
            compiler_params=_cparams("parallel"))(x, w, b, g)

    @jax.custom_vjp
    def op(x, w, b):
        return fwd_call(x, w, b)

    op.defvjp(lambda x, w, b: (op(x, w, b), (x, w, b)), lambda res, g: bwd_call(*res, g))
    return op(x, w, b)


def chunk_scan(step, shared, shared_lanes, perdir, perdir_lanes, consts, *, heads, state_shape, out_w, n_lat, name,
               keep_shape=None):
    assert keep_shape is None or heads == 1
    shared, perdir, consts = tuple(shared), tuple(perdir), tuple(consts)
    n_rows = shared[0].shape[0]
    nl, ncx = n_lat // CHUNK, (n_rows - n_lat) // CHUNK
    n_chunks = nl + ncx
    ow_all = heads * out_w
    ns, npd, ncst = len(shared), len(perdir), len(consts)

    def cidx(d, n):
        m = n - ncx
        return jnp.where(n < ncx, nl + jnp.where(d == 0, n, ncx - 1 - n), jnp.where(d == 0, m, nl - 1 - m))

    def specs(order):
        sh = [pl.BlockSpec((CHUNK, a.shape[1]), lambda d, n: (cidx(d, order(n)), 0)) for a in shared]
        pd = [pl.BlockSpec((CHUNK, a.shape[1] // 2), lambda d, n: (cidx(d, order(n)), d)) for a in perdir]
        cs = [pl.BlockSpec((CHUNK, a.shape[1]), lambda d, n: (cidx(d, order(n)), 0)) for a in consts]
        o = pl.BlockSpec((CHUNK, ow_all), lambda d, n: (cidx(d, order(n)), d))
        st = pl.BlockSpec((None, None, heads) + state_shape, lambda d, n: (d, order(n), 0) + (0,) * len(state_shape))
        kp = [] if keep_shape is None else [pl.BlockSpec((None, None) + keep_shape,
                                                        lambda d, n: (d, order(n)) + (0,) * len(keep_shape))]
        return sh, pd, cs, o, st, kp

    def mask(d):
        r = lax.broadcasted_iota(jnp.int32, (CHUNK, CHUNK), 0)
        c = lax.broadcasted_iota(jnp.int32, (CHUNK, CHUNK), 1)
        lower = jnp.where(r >= c, 1.0, 0.0).astype(F32)
        upper = jnp.where(r <= c, 1.0, 0.0).astype(F32)
        return jnp.where(d == 0, lower, upper)

    def head_slices(h):
        out = []
        for lanes in tuple(shared_lanes) + tuple(perdir_lanes):
            out.append([slice(off + (h // hpg) * w, off + (h // hpg) * w + w) for off, w, hpg in lanes])
        return out

    def load(refs, h):
        return tuple(tuple(ref[:, s].astype(F32) for s in sl) for ref, sl in zip(refs, head_slices(h)))

    state_sds = jax.ShapeDtypeStruct((2, n_chunks, heads) + state_shape, F32)

    def fwd_call(shared, perdir, consts):
        sh, pd, cs, o_spec, st_spec, kp_spec = specs(lambda n: n)

        def body(*refs):
            in_refs = refs[:ns + npd]
            c_refs = refs[ns + npd:ns + npd + ncst]
            o_ref, ss_ref = refs[ns + npd + ncst:ns + npd + ncst + 2]
            s_scr = refs[-1]
            d, n = pl.program_id(0), pl.program_id(1)

            @pl.when(n == 0)
            def _():
                s_scr[...] = jnp.zeros_like(s_scr)

            m = mask(d)
            cv = tuple(c[...] for c in c_refs)
            ins = [load(in_refs, h) for h in range(heads)]
            s0 = [s_scr[h] for h in range(heads)]
            res = [step(ins[h], cv, s0[h], m, None) for h in range(heads)]
            for h in range(heads):
                o_ref[:, h * out_w:(h + 1) * out_w] = res[h][0]
                ss_ref[h] = s0[h]
                s_scr[h] = res[h][1]
            if keep_shape is not None:
                refs[-2][...] = res[0][2]

        keep_sds = [] if keep_shape is None else [jax.ShapeDtypeStruct((2, n_chunks) + keep_shape, F32)]
        return pl.pallas_call(
            body, name=name + "_fwd", grid=(2, n_chunks), in_specs=sh + pd + cs, out_specs=tuple([o_spec, st_spec] + kp_spec),
            out_shape=tuple([jax.ShapeDtypeStruct((n_rows, 2 * ow_all), F32), state_sds] + keep_sds),
            scratch_shapes=[pltpu.VMEM((heads,) + state_shape, F32)],
            compiler_params=_cparams("arbitrary", "arbitrary"))(*shared, *perdir, *consts)

    def bwd_call(shared, perdir, consts, starts, kept, g):
        sh, pd, cs, o_spec, st_spec, kp_spec = specs(lambda n: n_chunks - 1 - n)
        dsh = [pl.BlockSpec((CHUNK, a.shape[1]), lambda d, n: (cidx(d, n_chunks - 1 - n), d)) for a in shared]
        nk = len(kept)

        def body(*refs):
            in_refs = refs[:ns + npd]
            c_refs = refs[ns + npd:ns + npd + ncst]
            ss_ref, g_ref = refs[ns + npd + ncst:ns + npd + ncst + 2]
            kept_val = refs[ns + npd + ncst + 2][...] if nk else None
            d_refs = refs[ns + npd + ncst + 2 + nk:ns + npd + ncst + 2 + nk + ns + npd]
            ds_scr = refs[-1]
            d, n = pl.program_id(0), pl.program_id(1)

            @pl.when(n == 0)
            def _():
                ds_scr[...] = jnp.zeros_like(ds_scr)

            m = mask(d)
            cv = tuple(c[...] for c in c_refs)
            ins = [load(in_refs, h) for h in range(heads)]
            cots = [(g_ref[:, h * out_w:(h + 1) * out_w], ds_scr[h]) for h in range(heads)]
            starts = [ss_ref[h] for h in range(heads)]
            grads = []
            for h in range(heads):
                _, vjp = jax.vjp(lambda i_, s_: step(i_, cv, s_, m, kept_val)[:2], ins[h], starts[h])
                grads.append(vjp(cots[h]))
            for d_ref in d_refs:
                d_ref[...] = jnp.zeros_like(d_ref)
            for h in range(heads):
                g_ins, g_s = grads[h]
                for d_ref, sl, gr in zip(d_refs, head_slices(h), g_ins):
                    for s, gv in zip(sl, gr):
                        d_ref[:, s] += gv
                ds_scr[h] = g_s

        d_shape = tuple(jax.ShapeDtypeStruct((n_rows, 2 * a.shape[1]), F32) for a in shared) + tuple(
            jax.ShapeDtypeStruct(a.shape, F32) for a in perdir)
        return pl.pallas_call(
            body, name=name + "_bwd", grid=(2, n_chunks), in_specs=sh + pd + cs + [st_spec, o_spec] + kp_spec,
            out_specs=tuple(dsh + pd), out_shape=d_shape,
            scratch_shapes=[pltpu.VMEM((heads,) + state_shape, F32)],
            compiler_params=_cparams("arbitrary", "arbitrary"))(*shared, *perdir, *consts, starts, g, *kept)

    @jax.custom_vjp
    def op(shared, perdir, consts):
        return fwd_call(shared, perdir, consts)[0]

    def op_fwd(shared, perdir, consts):
        o, starts, *kept = fwd_call(shared, perdir, consts)
        return o, (shared, perdir, consts, starts, tuple(kept))

    def op_bwd(res, g):
        shared, perdir, consts, starts, kept = res
        d = bwd_call(shared, perdir, consts, starts, kept, g)
        d_sh = tuple((a[:, :a.shape[1] // 2] + a[:, a.shape[1] // 2:]).astype(s.dtype) for a, s in zip(d[:ns], shared))
        return d_sh, tuple(d[ns:]), tuple(None for _ in consts)

    op.defvjp(op_fwd, op_bwd)
    return op(shared, perdir, consts)


@jax.custom_vjp
def _swap_halves(x):
    return pltpu.roll(x, LANES // 2, 1)


_swap_halves.defvjp(lambda x: (_swap_halves(x), None), lambda _, g: (_swap_halves(g),))


def _gla_step(ins, consts, st, m, kept):
    (q_all, k_all), (v_all,), (la_all,) = ins
    cos, sin = consts
    b_all = _nn_hi(m, la_all)
    bl_all = jnp.sum(la_all, axis=0, keepdims=True)
    outs, st_new = [], []
    for h in range(GLA_H):
        blk = lambda x: x[:, h * LANES:(h + 1) * LANES]
        q, k, v, b, bl = blk(q_all), blk(k_all), blk(v_all), blk(b_all), blk(bl_all)
        st_h = st[h * GLA_DV:(h + 1) * GLA_DV]
        q = (q * cos + _swap_halves(q) * sin) * (GLA_DK ** -0.5)
        k = k * cos + _swap_halves(k) * sin
        qi = q * jnp.exp(b)
        ki = k * jnp.exp(-b)
        outs.append(_nt(qi, st_h) + _nn(_nt(qi, ki) * m, v))
        st_new.append(st_h * jnp.exp(bl) + _tn(v, k * jnp.exp(bl - b)))
    return jnp.concatenate(outs, axis=1), jnp.concatenate(st_new, axis=0), None


def _l2n(x):
    return x * lax.rsqrt(jnp.sum(x * x, axis=-1, keepdims=True) + RMS_EPS)


def _tri_inv_fwd(nmat):
    r = lax.broadcasted_iota(jnp.int32, nmat.shape, 0)
    c = lax.broadcasted_iota(jnp.int32, nmat.shape, 1)
    inv = jnp.where(r == c, 1.0, 0.0).astype(F32) - nmat
    p = nmat
    for _ in range(5):
        p = _nn_hi(p, p)
        inv = inv + _nn_hi(inv, p)
    return inv


@jax.custom_vjp
def _unit_tri_inv(nmat):
    return _tri_inv_fwd(nmat)


def _unit_tri_inv_bwd(inv, g):
    return (-_dot3(_dot3(inv, g, _TN), inv, _NT),)


_unit_tri_inv.defvjp(lambda nmat: (lambda inv: (inv, inv))(_tri_inv_fwd(nmat)), _unit_tri_inv_bwd)


@jax.custom_vjp
def _kept_tri_inv(nmat, inv):
    return inv


_kept_tri_inv.defvjp(lambda nmat, inv: (inv, inv), lambda inv, g: _unit_tri_inv_bwd(inv, g) + (jnp.zeros_like(inv),))


def _lane_col(x, h):
    lane = lax.broadcasted_iota(jnp.int32, x.shape, 1)
    return jnp.sum(jnp.where(lane == h, x, 0.0), axis=1, keepdims=True)


def _masked_exp(diff, mask):
    return jnp.where(mask > 0, jnp.exp(jnp.where(mask > 0, diff, 0.0)), 0.0)


def _gdn_step(ins, consts, s, m, kept):
    (q, k, v), (beta,), (la,) = ins
    n = GDN_H * CHUNK
    hs = range(GDN_H)
    blk = lambda x, h: x[:, h * GDN_D:(h + 1) * GDN_D]
    rows = lambda x, h: x[h * CHUNK:(h + 1) * CHUNK]
    qh = [_l2n(blk(q, h)) * (GDN_D ** -0.5) for h in hs]
    kh = [_l2n(blk(k, h)) for h in hs]
    k_st = jnp.concatenate(kh, axis=0)
    q_st = jnp.concatenate(qh, axis=0)
    v_st = jnp.concatenate([blk(v, h) for h in hs], axis=0)
    beta_st = jnp.concatenate([_lane_col(beta, h) for h in hs], axis=0)
    la_cols = [_lane_col(la, h) for h in hs]
    la_st = jnp.concatenate([jnp.broadcast_to(c, (CHUNK, GDN_D)) for c in la_cols], axis=0)
    r = lax.broadcasted_iota(jnp.int32, (n, n), 0)
    c = lax.broadcasted_iota(jnp.int32, (n, n), 1)
    e = jnp.where(lax.broadcasted_iota(jnp.int32, (n, CHUNK), 0) % CHUNK == lax.broadcasted_iota(jnp.int32, (n, CHUNK), 1),
                  1.0, 0.0).astype(F32)
    m_bd = jnp.where(r // CHUNK == c // CHUNK, _nt(_nn(e, m), e), 0.0)
    eye = jnp.where(r == c, 1.0, 0.0).astype(F32)
    b_st = _nn_hi(m_bd, la_st)
    b_t = b_st.T
    diff = jnp.concatenate([b_st, b_st], axis=1) - jnp.concatenate([b_t, b_t], axis=0)
    incl = _masked_exp(diff, m_bd)
    strict = _masked_exp(diff, m_bd - eye)
    nmat = beta_st * _nt(k_st, k_st) * strict
    inv = _unit_tri_inv(nmat) if kept is None else _kept_tri_inv(nmat, kept)
    wu = _nn_hi(inv, jnp.concatenate([k_st * (beta_st * jnp.exp(b_st)), v_st * beta_st], axis=-1))
    w, u0 = wu[:, :GDN_D], wu[:, GDN_D:]
    us, s_new, qs = [], [], []
    for h in hs:
        s_h = s[h * GDN_D:(h + 1) * GDN_D]
        bl = jnp.sum(jnp.broadcast_to(la_cols[h], (CHUNK, GDN_D)), axis=0, keepdims=True)
        u_h = rows(u0, h) - _nn(rows(w, h), s_h)
        s_new.append(jnp.exp(bl) * s_h + _tn(kh[h] * jnp.exp(bl - rows(b_st, h)), u_h))
        us.append(u_h)
        qs.append(_nn(qh[h], s_h))
    o_st = jnp.exp(b_st) * jnp.concatenate(qs, axis=0) + _nn(_nt(q_st, k_st) * incl, jnp.concatenate(us, axis=0))
    return jnp.concatenate([rows(o_st, h) for h in hs], axis=1), jnp.concatenate(s_new, axis=0), inv


def _ssd_step(ins, consts, s, m, kept):
    (xs,), (bc,), (dt,), (la,) = ins
    hpg = M2_H // M2_G
    b_all = _nn_hi(m, la)
    bl_all = jnp.sum(la, axis=0, keepdims=True)
    b_t = b_all.T
    row_id = lax.broadcasted_iota(jnp.int32, b_t.shape, 0)
    bm = [bc[:, g * M2_N:(g + 1) * M2_N] for g in range(M2_G)]
    cm = [bc[:, (M2_G + g) * M2_N:(M2_G + g + 1) * M2_N] for g in range(M2_G)]
    scores = [_nt(cm[g], bm[g]) for g in range(M2_G)]
    outs, s_new = [], []
    for h in range(M2_H):
        g = h // hpg
        s_h = s[h * M2_N:(h + 1) * M2_N]
        b_col = _lane_col(b_all, h)
        bl = _lane_col(bl_all, h)
        b_row = jnp.sum(jnp.where(row_id == h, b_t, 0.0), axis=0, keepdims=True)
        xv = xs[:, h * LANES:(h + 1) * LANES] * _lane_col(dt, h)
        outs.append(jnp.exp(b_col) * _nn(cm[g], s_h) + _nn(scores[g] * _masked_exp(b_col - b_row, m), xv))
        s_new.append(jnp.exp(bl) * s_h + _tn(bm[g] * jnp.exp(bl - b_col), xv))
    return jnp.concatenate(outs, axis=1), jnp.concatenate(s_new, axis=0), None


def _na_tile(q, kw, vw, kc, vc, bias):
    qs = q * (NA_D ** -0.5)
    s1 = _nt(qs, kw) + bias
    s2 = _nt(qs, kc)
    mx = lax.stop_gradient(jnp.maximum(jnp.max(s1, axis=-1, keepdims=True), jnp.max(s2, axis=-1, keepdims=True)))
    p1 = jnp.exp(s1 - mx)
    p2 = jnp.exp(s2 - mx)
    den = jnp.sum(p1, axis=-1, keepdims=True) + jnp.sum(p2, axis=-1, keepdims=True)
    return (_nn(p1, vw) + _nn(p2, vc)) / den


def _ctx_tile(q, k, v):
    s = _nt(q * (NA_D ** -0.5), k)
    p = jnp.exp(s - lax.stop_gradient(jnp.max(s, axis=-1, keepdims=True)))
    return _nn(p, v) / jnp.sum(p, axis=-1, keepdims=True)


def natten(q, k, v, bias, *, n_lat, name):
    n_rows = q.shape[0]
    n_ctx = n_rows - n_lat
    g_rows = n_lat // GRID_W
    win = NA_WIN_R * GRID_W
    ctx_blk = n_lat // n_ctx

    def start(n):
        return jnp.clip(n - NA_WIN_R // 2, 0, g_rows - NA_WIN_R)

    def case(n):
        return n - start(n)

    q_spec = pl.BlockSpec((GRID_W, LANES), lambda h, n: (n, h))
    lat_spec = pl.BlockSpec((n_lat, LANES), lambda h, n: (0, h))
    ctx_in = pl.BlockSpec((n_ctx, LANES), lambda h, n: (ctx_blk, h))
    ctx_out = pl.BlockSpec((n_ctx, LANES), lambda h, n: (0, h))
    bias_spec = pl.BlockSpec((None, None, GRID_W, win), lambda h, n: (h, case(n), 0, 0))
    lat_sds = jax.ShapeDtypeStruct((n_lat, BRANCH_W), F32)
    ctx_sds = jax.ShapeDtypeStruct((n_ctx, BRANCH_W), F32)
    f32 = lambda t: t.astype(F32)

    def lat_fwd(q, k, v, bias):
        def body(q_ref, k_ref, v_ref, kc_ref, vc_ref, b_ref, o_ref):
            r0 = pl.multiple_of(start(pl.program_id(1)) * GRID_W, GRID_W)
            o_ref[...] = _na_tile(f32(q_ref[...]), f32(k_ref[pl.ds(r0, win), :]), f32(v_ref[pl.ds(r0, win), :]),
                                  f32(kc_ref[...]), f32(vc_ref[...]), b_ref[...])

        return pl.pallas_call(body, name=name + "_lat_fwd", grid=(NA_H, g_rows),
                              in_specs=[q_spec, lat_spec, lat_spec, ctx_in, ctx_in, bias_spec], out_specs=q_spec,
                              out_shape=lat_sds, compiler_params=_cparams("parallel", "arbitrary"))(q, k, v, k, v, bias)

    def lat_bwd(q, k, v, bias, g):
        def body(q_ref, k_ref, v_ref, kc_ref, vc_ref, b_ref, g_ref, dq_ref, dk_ref, dv_ref, dkc_ref, dvc_ref, db_ref):
            n = pl.program_id(1)
            r0 = pl.multiple_of(start(n) * GRID_W, GRID_W)

            @pl.when(n == 0)
            def _():
                for r in (dk_ref, dv_ref, dkc_ref, dvc_ref):
                    r[...] = jnp.zeros_like(r)

            @pl.when((n == 0) | (case(n) != case(jnp.maximum(n - 1, 0))))
            def _():
                db_ref[...] = jnp.zeros_like(db_ref)

            _, vjp = jax.vjp(_na_tile, f32(q_ref[...]), f32(k_ref[pl.ds(r0, win), :]), f32(v_ref[pl.ds(r0, win), :]),
                             f32(kc_ref[...]), f32(vc_ref[...]), b_ref[...])
            dq, dkw, dvw, dkc, dvc, db = vjp(g_ref[...])
            dq_ref[...] = dq
            dk_ref[pl.ds(r0, win), :] += dkw
            dv_ref[pl.ds(r0, win), :] += dvw
            dkc_ref[...] += dkc
            dvc_ref[...] += dvc
            db_ref[...] += db

        return pl.pallas_call(
            body, name=name + "_lat_bwd", grid=(NA_H, g_rows),
            in_specs=[q_spec, lat_spec, lat_spec, ctx_in, ctx_in, bias_spec, q_spec],
            out_specs=(q_spec, lat_spec, lat_spec, ctx_out, ctx_out, bias_spec),
            out_shape=(lat_sds, lat_sds, lat_sds, ctx_sds, ctx_sds, jax.ShapeDtypeStruct(bias.shape, F32)),
            compiler_params=_cparams("parallel", "arbitrary"))(q, k, v, k, v, bias, g)

    c_in = pl.BlockSpec((n_ctx, LANES), lambda h: (ctx_blk, h))
    c_out = pl.BlockSpec((n_ctx, LANES), lambda h: (0, h))

    def ctx_fwd(q, k, v):
        def body(q_ref, k_ref, v_ref, o_ref):
            o_ref[...] = _ctx_tile(f32(q_ref[...]), f32(k_ref[...]), f32(v_ref[...]))

        return pl.pallas_call(body, name=name + "_ctx_fwd", grid=(NA_H,), in_specs=[c_in, c_in, c_in], out_specs=c_out,
                              out_shape=ctx_sds, compiler_params=_cparams("parallel"))(q, k, v)

    def ctx_bwd(q, k, v, g):
        def body(q_ref, k_ref, v_ref, g_ref, dq_ref, dk_ref, dv_ref):
            _, vjp = jax.vjp(_ctx_tile, f32(q_ref[...]), f32(k_ref[...]), f32(v_ref[...]))
            dq_ref[...], dk_ref[...], dv_ref[...] = vjp(g_ref[...])

        return pl.pallas_call(body, name=name + "_ctx_bwd", grid=(NA_H,), in_specs=[c_in, c_in, c_in, c_out],
                              out_specs=(c_out, c_out, c_out), out_shape=(ctx_sds, ctx_sds, ctx_sds),
                              compiler_params=_cparams("parallel"))(q, k, v, g)

    @jax.custom_vjp
    def op(q, k, v, bias):
        return jnp.concatenate([lat_fwd(q, k, v, bias), ctx_fwd(q, k, v)], axis=0)

    def op_bwd(res, g):
        q, k, v, bias = res
        dq, dk, dv, dkc, dvc, db = lat_bwd(q, k, v, bias, g[:n_lat])
        dqc, dkc2, dvc2 = ctx_bwd(q, k, v, g[n_lat:])
        return (jnp.concatenate([dq, dqc], axis=0).astype(q.dtype), jnp.concatenate([dk, dkc + dkc2], axis=0).astype(k.dtype),
                jnp.concatenate([dv, dvc + dvc2], axis=0).astype(v.dtype), db)

    op.defvjp(lambda q, k, v, bias: (op(q, k, v, bias), (q, k, v, bias)), op_bwd)
    return op(q, k, v, bias)


def _runs(src):
    src = np.asarray(src)
    out, i = [], 0
    while i < len(src):
        j = i + 1
        if src[i] < 0:
            while j < len(src) and src[j] < 0:
                j += 1
            out.append((-1, j - i))
        else:
            while j < len(src) and src[j] == src[j - 1] + 1:
                j += 1
            out.append((int(src[i]), j - i))
        i = j
    return out


def _take_cols(w, src):
    pieces = [jnp.zeros(w.shape[:-1] + (ln,), w.dtype) if s < 0 else w[..., s:s + ln] for s, ln in _runs(src)]
    return pieces[0] if len(pieces) == 1 else jnp.concatenate(pieces, axis=-1)


def _take_cols_slabs(slabs, src):
    width = slabs[0].shape[-1]
    pieces = []
    for s, ln in _runs(src):
        if s < 0:
            pieces.append(jnp.zeros(slabs[0].shape[:-1] + (ln,), slabs[0].dtype))
        while s >= 0 and ln > 0:
            off = s % width
            take = min(ln, width - off)
            pieces.append(slabs[s // width][..., off:off + take])
            s, ln = s + take, ln - take
    return pieces[0] if len(pieces) == 1 else jnp.concatenate(pieces, axis=-1)


def _untake_cols(parts, n_cols, n_slabs=1):
    found = []
    for arr, src in parts:
        pos = 0
        for s, ln in _runs(src):
            if s >= 0:
                found.append((s, ln, arr, pos))
            pos += ln
    found.sort(key=lambda t: t[0])
    width = n_cols // n_slabs
    slabs, at = [[] for _ in range(n_slabs)], 0
    for s, ln, arr, pos in found:
        assert s == at, (s, at)
        at += ln
        while ln > 0:
            take = min(ln, width - s % width)
            slabs[s // width].append(arr[..., pos:pos + take])
            s, pos, ln = s + take, pos + take, ln - take
    assert at == n_cols, (at, n_cols)
    return [jnp.concatenate(p, axis=-1) for p in slabs]


def _pad_heads(base, heads, real, width):
    return np.concatenate([np.concatenate([base + h * real + np.arange(real), -np.ones(width - real, np.int64)])
                           for h in range(heads)])


def _rope_heads(base, heads):
    z = -np.ones(32, np.int64)
    return np.concatenate([np.concatenate([base + h * 64 + np.arange(32), z, base + h * 64 + 32 + np.arange(32), z])
                           for h in range(heads)])


def _lane_block(base, n):
    return np.concatenate([base + np.arange(n), -np.ones(LANES - n, np.int64)])


def _in_groups():
    g0, n0, d0, m0, t0 = 0, 1568, 3104, 5168, 6720
    rng = lambda a, n: a + np.arange(n)
    return [
        ("gla_qk", np.concatenate([_rope_heads(g0, GLA_H), _rope_heads(g0 + 256, GLA_H)])),
        ("gla_v", rng(g0 + 512, 512)),
        ("gla_g", rng(g0 + 1024, 512)),
        ("gla_lr", _lane_block(g0 + 1536, 2 * GLA_LR)),
        ("na_q", rng(n0, 512)), ("na_k", rng(n0 + 512, 512)), ("na_v", rng(n0 + 1024, 512)),
        ("gdn_qkv", rng(d0, 1536)),
        ("gdn_z", rng(d0 + 1536, 512)),
        ("gdn_sm", np.concatenate([_lane_block(d0 + 2048 + 4 * i, GDN_H) for i in range(4)])),
        ("m2_z", _pad_heads(m0, M2_H, M2_P, LANES)),
        ("m2_xs", _pad_heads(m0 + 512, M2_H, M2_P, LANES)),
        ("m2_bc", rng(m0 + 1024, 512)),
        ("m2_dt", np.concatenate([_lane_block(m0 + 1536, M2_H), _lane_block(m0 + 1536 + M2_H, M2_H)])),
        ("gate", rng(t0, 4 * D_MODEL)),
    ]


_M2_PAD = _pad_heads(0, M2_H, M2_P, LANES)
_GLA_PAD = _rope_heads(0, GLA_H)


def _row3(v):
    return v.reshape((1, 1, -1))


def _dir_rows(p, n):
    return _row3(jnp.concatenate([_take_cols(p[d][None], _lane_block(0, n)) for d in range(2)], axis=-1))


def _rope_tables(n_lat, n_ctx):
    n_freq = GLA_DK // 4
    freqs = ROPE_BASE ** (-jnp.arange(n_freq, dtype=F32) / n_freq)
    t = jnp.arange(n_lat)
    row = (t // GRID_W).astype(F32)
    colv = (t % GRID_W).astype(F32)
    ang = jnp.concatenate([row[:, None] * freqs, colv[:, None] * freqs], axis=-1)
    c, s = jnp.cos(ang), jnp.sin(ang)
    one, zero = jnp.ones_like(c), jnp.zeros_like(c)
    cos_t = jnp.concatenate([c, one, c, one], axis=-1)
    sin_t = jnp.concatenate([-s, zero, s, zero], axis=-1)
    return (jnp.concatenate([cos_t, jnp.ones((n_ctx, LANES), F32)], axis=0),
            jnp.concatenate([sin_t, jnp.zeros((n_ctx, LANES), F32)], axis=0))


def _na_bias(rpb):
    case = np.arange(NA_WIN_R)
    r = np.arange(NA_WIN_R)
    dr = r[None, :] - case[:, None] + NA_WIN_R - 1
    ci = np.arange(GRID_W)
    dc = np.clip(ci[None, :] - ci[:, None], 1 - NA_WIN_C, NA_WIN_C - 1) + NA_WIN_C - 1
    c0 = np.clip(ci - NA_WIN_C // 2, 0, GRID_W - NA_WIN_C)
    ok = (ci[None, :] >= c0[:, None]) & (ci[None, :] < c0[:, None] + NA_WIN_C)
    pick_r = np.zeros((NA_WIN_R, NA_WIN_R, 2 * NA_WIN_R - 1), np.float32)
    pick_r[case[:, None], r[None, :], dr] = 1.0
    pick_c = np.zeros((2 * NA_WIN_C - 1, GRID_W, GRID_W), np.float32)
    pick_c[dc, ci[:, None], ci[None, :]] = 1.0
    rows = jnp.einsum("hdk,crd->hcrk", rpb, pick_r, precision=HI)
    tbl = jnp.einsum("hcrk,kij->hcirj", rows, pick_c, precision=HI)
    tbl = jnp.where(ok[None, None, :, None, :], tbl, NEG_INF)
    return tbl.reshape((NA_H, NA_WIN_R, GRID_W, NA_WIN_R * GRID_W))


def _layer(l, xs, mod, small, slots, gath, win, tables, *, n_lat):
    rw = functools.partial(rowwise, n_lat=n_lat)
    rwb = functools.partial(rowwise, n_lat=n_lat, out_dtype=BF16)
    nm = lambda s: "l%d_%s" % (l, s)
    sl = slots[l]
    sh1, sc1, g1, sh2, sc2, g2 = [mod[:, i * D_MODEL:(i + 1) * D_MODEL].reshape((2, 1, D_MODEL)) for i in range(6)]
    (h,) = rwb(_f_modnorm, [xs], [], [_row3(small["norm1_g"][l]), sc1, sh1], out_widths=[D_MODEL], tile=256,
              name=nm("norm1"))
    logits = ("gla_lr", "gdn_sm", "m2_dt")
    p = {g: linear(h, win[l][g], sl["in_" + g], name=nm("in_" + g), out_dtype=F32 if g in logits else BF16)
         for g, _ in _in_groups()}

    a2 = small["gla_a2"][l]
    a2p = jnp.concatenate([
        jnp.concatenate([_take_cols(a2[0], _GLA_PAD), jnp.zeros((GLA_LR, 512), F32)], axis=1),
        jnp.concatenate([jnp.zeros((GLA_LR, 512), F32), _take_cols(a2[1], _GLA_PAD)], axis=1),
        jnp.zeros((LANES - 2 * GLA_LR, 1024), F32)], axis=0)[None]
    abp = _row3(jnp.concatenate([_take_cols(small["gla_ab"][l][d][None], _GLA_PAD) for d in range(2)], axis=-1))
    (la,) = rw(_f_gla_prep, [p["gla_lr"]], [], [a2p, abp], out_widths=[1024], tile=256, name=nm("gla_prep"))
    whole = lambda off, width: (off, width, 1)
    o = chunk_scan(_gla_step, [p["gla_qk"], p["gla_v"]], [[whole(0, BRANCH_W), whole(BRANCH_W, BRANCH_W)], [whole(0, BRANCH_W)]],
                   [la], [[whole(0, BRANCH_W)]], tables, heads=1, state_shape=(GLA_H * GLA_DV, LANES), out_w=BRANCH_W,
                   n_lat=n_lat, name=nm("gla_scan"))
    (ya,) = rwb(_f_headnorm_gate, [o, p["gla_g"]], [], [_row3(small["gla_norm_g"][l])], out_widths=[BRANCH_W], tile=256,
               name=nm("gla_fin"))

    yb = natten(p["na_q"], p["na_k"], p["na_v"], _na_bias(small["na_rpb"][l]), n_lat=n_lat, name=nm("na"))

    cq = conv_silu(p["gdn_qkv"], small["gdn_conv"][l], jnp.zeros((1, 3 * BRANCH_W), F32), n_lat=n_lat, name=nm("gdn_conv"))
    beta, la = rw(_f_gdn_prep, [p["gdn_sm"]], [], [_dir_rows(small["gdn_a_log"][l], GDN_H), _dir_rows(small["gdn_dt_bias"][l], GDN_H)],
                  out_widths=[256, 256], tile=256, name=nm("gdn_prep"))
    o = chunk_scan(_gdn_step, [cq], [[whole(0, 512), whole(512, 512), whole(1024, 512)]], [beta, la],
                   [[whole(0, LANES)], [whole(0, LANES)]], [], heads=1, state_shape=(GDN_H * GDN_D, GDN_D),
                   out_w=BRANCH_W, n_lat=n_lat, name=nm("gdn_scan"), keep_shape=(GDN_H * CHUNK, GDN_H * CHUNK))
    (yc,) = rwb(_f_headnorm_gate, [o, p["gdn_z"]], [], [_row3(small["gdn_norm_g"][l])], out_widths=[BRANCH_W], tile=256,
               name=nm("gdn_fin"))

    cw, cb = small["m2_conv"][l], small["m2_conv_b"][l][None]
    cxs = conv_silu(p["m2_xs"], _take_cols(cw[:, :512], _M2_PAD), _take_cols(cb[:, :512], _M2_PAD), n_lat=n_lat,
                    name=nm("m2_conv_x"))
    cbc = conv_silu(p["m2_bc"], cw[:, 512:], cb[:, 512:], n_lat=n_lat, name=nm("m2_conv_bc"))
    dt, la = rw(_f_m2_prep, [p["m2_dt"]], [], [_dir_rows(small["m2_a_log"][l], M2_H), _dir_rows(small["m2_dt_bias"][l], M2_H)],
                out_widths=[256, 256], tile=256, name=nm("m2_prep"))
    o = chunk_scan(_ssd_step, [cxs, cbc], [[whole(0, 2 * BRANCH_W)], [whole(0, BRANCH_W)]], [dt, la],
                   [[whole(0, LANES)], [whole(0, LANES)]], [], heads=1, state_shape=(M2_H * M2_N, LANES),
                   out_w=2 * BRANCH_W, n_lat=n_lat, name=nm("m2_scan"))
    dskip = _row3(jnp.repeat(small["m2_d"][l], LANES))
    (yd,) = rwb(_f_m2_fin, [o, p["m2_z"], cxs], [], [dskip, _row3(_take_cols(small["m2_norm_g"][l][None], _M2_PAD))],
               out_widths=[2 * BRANCH_W], tile=128, name=nm("m2_fin"))

    wb = gath["w_branch"]
    zs = [linear(y, wb, sl["w_branch%d" % i], name=nm("branch%d" % i), layout="col", prefix=(l, i), out_dtype=BF16)
          for i, y in enumerate((ya, yb, yc))]
    wb3 = wb[:, l, 3].reshape((N_SLOT, M2_H, M2_P, BRANCH_W))
    wb3 = jnp.pad(wb3, ((0, 0), (0, 0), (0, LANES - M2_P), (0, 0))).reshape((N_SLOT, 2 * BRANCH_W, BRANCH_W))
    zs.append(linear(yd, wb3, sl["w_branch3"], name=nm("branch3"), layout="col", out_dtype=BF16))
    (merged,) = rwb(_f_merge, [p["gate"]] + zs, [], [_row3(small["b_merge"][l].reshape(-1))], out_widths=[D_MODEL], tile=64,
                   name=nm("merge"))
    y = linear(merged, gath["w_out"], sl["w_out"], name=nm("out"), layout="row", prefix=(l,))
    x1, h2 = rowwise(_f_resid_modnorm, [xs, y], [], [g1, _row3(small["norm2_g"][l]), sc2, sh2], n_lat=n_lat,
                     out_widths=[D_MODEL, D_MODEL], out_dtype=(F32, BF16), tile=256, name=nm("res1_norm2"))
    u1 = linear(h2, gath["w_ffn1"], sl["w_ffn1"], name=nm("ffn1"), layout="col", prefix=(l,), out_dtype=BF16)
    u3 = linear(h2, gath["w_ffn3"], sl["w_ffn3"], name=nm("ffn3"), layout="col", prefix=(l,), out_dtype=BF16)
    (act,) = rwb(_f_swiglu, [u1, u3], [], [], out_widths=[D_FF], tile=128, name=nm("swiglu"))
    f = linear(act, gath["w_ffn2"], sl["w_ffn2"], name=nm("ffn2"), layout="row", prefix=(l,))
    (x2,) = rw(_f_resid, [x1, f], [], [g2], out_widths=[D_MODEL], tile=256, name=nm("res2"))
    return x2


def _slot_shapes():
    s = {"w_out": (N_SLOT, D_MODEL // N_SLOT, D_MODEL),
         "w_ffn1": (N_SLOT, D_MODEL, D_FF // N_SLOT), "w_ffn3": (N_SLOT, D_MODEL, D_FF // N_SLOT),
         "w_ffn2": (N_SLOT, D_FF // N_SLOT, D_MODEL), "w_branch3": (N_SLOT, 2 * BRANCH_W, BRANCH_W)}
    for i in range(3):
        s["w_branch%d" % i] = (N_SLOT, BRANCH_W, BRANCH_W)
    for g, src in _in_groups():
        s["in_" + g] = (D_MODEL, len(src))
    return s


ADA_ROWS = 2 * SUBLANES


def ada_shard(c_all, c_ctx, w_ada, slots):
    cc = jnp.concatenate([c_all, c_ctx[None], jnp.zeros((ADA_ROWS - c_all.shape[0] - 1, D_MODEL), F32)], axis=0)
    (act,) = rowwise(_f_silu, [cc], [], [], out_widths=[D_MODEL], tile=ADA_ROWS, n_lat=ADA_ROWS, name="ada_silu",
                     out_dtype=BF16)
    return [linear(act, w_ada, slots[l], name="l%d_ada" % l, prefix=(l,)) for l in range(DEPTH)]


def _local_loss(diff, fixed, *, n_lat):
    small = diff["small"]
    n_ctx = fixed["ctx"].shape[0]
    xs = jnp.concatenate([diff["x"], fixed["ctx"]], axis=0)
    tables = _rope_tables(n_lat, n_ctx)
    for l in range(DEPTH):
        xs = _layer(l, xs, diff["mod"][l], small, diff["slots"], fixed["gath"], fixed["win"], tables, n_lat=n_lat)
    (lrow,) = rowwise(_f_loss, [xs[:n_lat]], [fixed["target"]], [_row3(small["final_norm_g"])], out_widths=[LANES],
                      tile=256, n_lat=n_lat, name="loss")
    return jnp.sum(lrow)


def _place():
    x, y, c = lax.axis_index("x"), lax.axis_index("y"), lax.axis_index("c")
    chips = [(1 - x, y), (x, 1 - y), (1 - x, 1 - y)]
    return x, y, c, (x, y, 1 - c), chips


def _remote(src, dst, send_sem, recv_sem, dev):
    return pltpu.make_async_remote_copy(src_ref=src, dst_ref=dst, send_sem=send_sem, recv_sem=recv_sem,
                                        device_id=dev, device_id_type=MESH)


def _dma_sems(*shape):
    return pltpu.SemaphoreType.DMA(shape)


def place_shard(w, slot, *, name):
    depth, k, n = w.shape
    tr = _tile(k, max(2 * SUBLANES, (1 << 19) // n // (2 * SUBLANES) * (2 * SUBLANES)), 2 * SUBLANES)

    def body(s_ref, w_ref, o_ref):
        o_ref[...] = w_ref[...].astype(o_ref.dtype)

    return pl.pallas_call(
        body, name=name,
        grid_spec=pltpu.PrefetchScalarGridSpec(
            num_scalar_prefetch=1, grid=(depth, k // tr),
            in_specs=[pl.BlockSpec((None, tr, n), lambda l, i, s: (l, i, 0))],
            out_specs=pl.BlockSpec((None, None, tr, n), lambda l, i, s: (s[0], l, i, 0))),
        out_shape=jax.ShapeDtypeStruct((N_SLOT, depth, k, n), BF16),
        compiler_params=_cparams("parallel", "parallel"))(slot, w)


def gather_weights(bufs):
    n = len(bufs)

    def body(*refs):
        o = refs[n:2 * n]
        send1, recv1, send2, recv2 = refs[2 * n:]
        x, y, c, sibling, chips = _place()
        g = 2 * x + y
        sent = []
        for k in range(n):
            for j, (cx, cy) in enumerate(chips):
                cp = _remote(o[k].at[g, c], o[k].at[g, c], send1.at[k, j], recv1.at[k, j], (cx, cy, c))
                cp.start()
                sent.append(cp)
        for k in range(n):
            for j, (cx, cy) in enumerate(chips):
                gj = 2 * cx + cy
                _remote(o[k].at[g, c], o[k].at[gj, c], send1.at[k, j], recv1.at[k, j], (cx, cy, c)).wait_recv()
                cp = _remote(o[k].at[gj, c], o[k].at[gj, c], send2.at[k, j], recv2.at[k, j], sibling)
                cp.start()
                sent.append(cp)
        for k in range(n):
            for j, (cx, cy) in enumerate(chips):
                gj = 2 * cx + cy
                _remote(o[k].at[gj, 1 - c], o[k].at[gj, 1 - c], send2.at[k, j], recv2.at[k, j], sibling).wait_recv()
        for cp in sent:
            cp.wait_send()

    return pl.pallas_call(
        body, name="gather_weights", in_specs=[ANY] * n, out_specs=[ANY] * n,
        out_shape=[jax.ShapeDtypeStruct(b.shape, b.dtype) for b in bufs],
        input_output_aliases={k: k for k in range(n)},
        scratch_shapes=[_dma_sems(n, 3), _dma_sems(n, 3), _dma_sems(n, 3), _dma_sems(n, 3)],
    )(*bufs)


def allgather_small(buf, *, name):
    m_per = buf.shape[0]

    def body(x_ref, out_ref, send_sems, recv_sems, local_sem):
        x, y, c, sibling, chips = _place()
        me = (x, y, c)

        def rows(px, py, pc):
            return out_ref.at[pl.ds((4 * px + 2 * py + pc) * m_per, m_per), :]

        def copy(k, block, to, src=None):
            return _remote(rows(*block) if src is None else src, rows(*block), send_sems.at[k], recv_sems.at[k], to)

        mine = pltpu.make_async_copy(x_ref, rows(*me), local_sem)
        mine.start()
        first = [copy(0, me, sibling, src=x_ref)]
        first += [copy(1 + j, me, (*chip, c), src=x_ref) for j, chip in enumerate(chips)]
        for cp in first:
            cp.start()
        passed = [copy(4 + j, (*chip, c), sibling) for j, chip in enumerate(chips)]
        for j, chip in enumerate(chips):
            copy(1 + j, (*chip, c), me).wait_recv()
            passed[j].start()
        copy(0, sibling, me).wait_recv()
        for j, chip in enumerate(chips):
            copy(4 + j, (*chip, 1 - c), me).wait_recv()
        for cp in first + passed:
            cp.wait_send()
        mine.wait()

    return pl.pallas_call(
        body, name=name, out_shape=jax.ShapeDtypeStruct((8 * m_per, LANES), buf.dtype),
        in_specs=[pl.BlockSpec(memory_space=pltpu.VMEM)], out_specs=pl.BlockSpec(memory_space=pltpu.VMEM),
        scratch_shapes=[_dma_sems(7), _dma_sems(7), pltpu.SemaphoreType.DMA],
        compiler_params=pltpu.CompilerParams(vmem_limit_bytes=VMEM_LIMIT),
    )(buf)


def sum_blocks(stacked, n_blocks, *, name):
    m = stacked.shape[0] // n_blocks
    width = stacked.shape[1]
    x3 = stacked.reshape((n_blocks, m, width))
    tr = _tile(m, max(SUBLANES, (1 << 18) // width // SUBLANES * SUBLANES), SUBLANES)

    def body(x_ref, o_ref):
        acc = x_ref[0]
        for s in range(1, n_blocks):
            acc = acc + x_ref[s]
        o_ref[...] = acc

    return pl.pallas_call(body, name=name, grid=(m // tr,), in_specs=[pl.BlockSpec((n_blocks, tr, width), lambda i: (0, i, 0))],
                          out_specs=pl.BlockSpec((tr, width), lambda i: (i, 0)),
                          out_shape=jax.ShapeDtypeStruct((m, width), F32), compiler_params=_cparams("parallel"))(x3)


def reduce_pair(gs):
    n = len(gs)

    def body(*refs):
        g, r = refs[:n], refs[n:2 * n]
        send, recv = refs[2 * n:]
        x, y, c, sibling, _ = _place()
        cps = []
        for i in range(n):
            k2 = gs[i].shape[1] // 2
            cp = _remote(g[i].at[:, pl.ds((1 - c) * k2, k2), :], r[i], send.at[i], recv.at[i], sibling)
            cp.start()
            cps.append(cp)
        for cp in cps:
            cp.wait()

    return pl.pallas_call(
        body, name="reduce_pair", in_specs=[ANY] * n, out_specs=[ANY] * n,
        out_shape=[jax.ShapeDtypeStruct((g.shape[0], g.shape[1] // 2, g.shape[2]), g.dtype) for g in gs],
        scratch_shapes=[_dma_sems(n), _dma_sems(n)],
    )(*gs)


def _row_tile(rows, width, budget, mult):
    return _tile(rows, max(mult, budget // width // mult * mult), mult)


def add_own_half(g, recv, core, *, name):
    n_slot, k2, width = recv.shape
    tr = _row_tile(k2, width, 1 << 19, 2 * SUBLANES)
    nb = k2 // tr

    def body(c_ref, g_ref, r_ref, o_ref):
        o_ref[...] = (g_ref[...] + r_ref[...]).astype(o_ref.dtype)

    spec = pl.BlockSpec((None, tr, width), lambda s, i, c: (s, i, 0))
    return pl.pallas_call(
        body, name=name,
        grid_spec=pltpu.PrefetchScalarGridSpec(
            num_scalar_prefetch=1, grid=(n_slot, nb),
            in_specs=[pl.BlockSpec((None, tr, width), lambda s, i, c: (s, c[0] * nb + i, 0)), spec], out_specs=spec),
        out_shape=jax.ShapeDtypeStruct(recv.shape, BF16), compiler_params=_cparams("parallel", "parallel"))(core, g, recv)


def reduce_chips(qs):
    n = len(qs)

    def body(*refs):
        q, r = refs[:n], refs[n:2 * n]
        send, recv = refs[2 * n:]
        x, y, c, _, chips = _place()
        cps = []
        for i in range(n):
            for j, (cx, cy) in enumerate(chips):
                cp = _remote(q[i].at[2 * cx + cy], r[i].at[j], send.at[i, j], recv.at[i, j], (cx, cy, c))
                cp.start()
                cps.append(cp)
        for cp in cps:
            cp.wait()

    return pl.pallas_call(
        body, name="reduce_chips", in_specs=[ANY] * n, out_specs=[ANY] * n,
        out_shape=[jax.ShapeDtypeStruct((3,) + q.shape[1:], q.dtype) for q in qs],
        scratch_shapes=[_dma_sems(n, 3), _dma_sems(n, 3)],
    )(*qs)


def chip_sum(q, recv, place, grad, layer, *, name):
    _, k2, width = recv.shape
    tr = _row_tile(k2, width, 1 << 18, 2 * SUBLANES)
    nb = k2 // tr

    def body(p_ref, q_ref, r_ref, grad_ref, o_ref):
        acc = q_ref[...].astype(F32)
        for j in range(3):
            acc = acc + r_ref[j].astype(F32)
        o_ref[...] = acc

    return pl.pallas_call(
        body, name=name,
        grid_spec=pltpu.PrefetchScalarGridSpec(
            num_scalar_prefetch=1, grid=(nb,),
            in_specs=[pl.BlockSpec((None, tr, width), lambda i, p: (p[0], i, 0)),
                      pl.BlockSpec((3, tr, width), lambda i, p: (0, i, 0)), ANY],
            out_specs=pl.BlockSpec((None, tr, width), lambda i, p: (layer, p[1] * nb + i, 0))),
        out_shape=jax.ShapeDtypeStruct(grad.shape, F32), input_output_aliases={3: 0},
        compiler_params=_cparams("parallel"))(place, q, recv, grad)


def swap_pair(grads):
    n = len(grads)

    def body(*refs):
        o = refs[n:2 * n]
        send, recv = refs[2 * n:]
        x, y, c, sibling, _ = _place()
        cps = []
        for k in range(n):
            k2 = grads[k].shape[1] // 2
            for l in range(DEPTH):
                half = o[k].at[l, pl.ds(c * k2, k2), :]
                cp = _remote(half, half, send.at[k, l], recv.at[k, l], sibling)
                cp.start()
                cps.append(cp)
        for cp in cps:
            cp.wait()

    return pl.pallas_call(
        body, name="swap_pair", in_specs=[ANY] * n, out_specs=[ANY] * n,
        out_shape=[jax.ShapeDtypeStruct(g.shape, g.dtype) for g in grads],
        input_output_aliases={k: k for k in range(n)},
        scratch_shapes=[_dma_sems(n, DEPTH), _dma_sems(n, DEPTH)],
    )(*grads)


def adamw(w, g, m, v, *, name):
    rows, width = w.shape
    tr = _tile(rows, max(SUBLANES, (1 << 19) // width // SUBLANES * SUBLANES), SUBLANES)

    def body(w_ref, g_ref, m_ref, v_ref, d_ref, mo_ref, vo_ref):
        gv = g_ref[...]
        mn = ADAM_B1 * m_ref[...] + (1.0 - ADAM_B1) * gv
        vn = ADAM_B2 * v_ref[...] + (1.0 - ADAM_B2) * (gv * gv)
        m_hat = mn / (1.0 - ADAM_B1 ** ADAM_STEP)
        v_hat = vn / (1.0 - ADAM_B2 ** ADAM_STEP)
        d_ref[...] = -ADAM_LR * (m_hat / (jnp.sqrt(v_hat) + ADAM_EPS) + ADAM_WD * w_ref[...])
        mo_ref[...] = mn
        vo_ref[...] = vn

    spec = pl.BlockSpec((tr, width), lambda i: (i, 0))
    sds = jax.ShapeDtypeStruct((rows, width), F32)
    return pl.pallas_call(body, name=name, grid=(rows // tr,), in_specs=[spec] * 4, out_specs=(spec,) * 3,
                          out_shape=(sds,) * 3, compiler_params=_cparams("parallel"))(w, g, m, v)


def _pack(arrs):
    flat = jnp.concatenate([a.reshape(-1) for a in arrs])
    pad = (-flat.shape[0]) % (SUBLANES * LANES)
    return jnp.pad(flat, (0, pad)).reshape((-1, LANES))


def _unpack(buf, shapes):
    flat, out, at = buf.reshape(-1), [], 0
    for s in shapes:
        size = int(np.prod(s))
        out.append(flat[at:at + size].reshape(s))
        at += size
    return out


BIG = ["w_in", "w_branch", "w_out", "w_ffn1", "w_ffn3", "w_ffn2"]
SMALL_SHARDED = ["b_merge", "gla_a2", "gla_ab", "gdn_conv", "m2_conv"]
SMALL_WHOLE = ["norm1_g", "norm2_g", "gla_norm_g", "na_rpb", "gdn_a_log", "gdn_dt_bias", "gdn_norm_g",
               "m2_conv_b", "m2_a_log", "m2_dt_bias", "m2_d", "m2_norm_g", "final_norm_g"]
WEIGHTS = ["c_ctx", "norm1_g", "norm2_g", "w_ada", "b_ada", "w_in", "b_merge", "gla_a2", "gla_ab", "gla_norm_g", "na_rpb",
           "gdn_conv", "gdn_a_log", "gdn_dt_bias", "gdn_norm_g", "m2_conv", "m2_conv_b", "m2_a_log", "m2_dt_bias", "m2_d",
           "m2_norm_g", "w_branch", "w_out", "w_ffn1", "w_ffn3", "w_ffn2", "final_norm_g"]


def _step(a):
    n_lat = a["x"].shape[1]
    x_i, y_i, c_i = lax.axis_index("x"), lax.axis_index("y"), lax.axis_index("c")
    slot = 2 * x_i + y_i

    slot_arr = slot.astype(jnp.int32).reshape((1,))
    core = c_i.astype(jnp.int32).reshape((1,))
    placed = [place_shard(a[n].reshape((DEPTH, -1, a[n].shape[-1])), slot_arr, name="place_" + n) for n in BIG]
    gath = dict(zip(BIG, gather_weights(placed)))
    gath["w_branch"] = gath["w_branch"].reshape((N_SLOT, DEPTH, 4, BRANCH_W, BRANCH_W))
    shard_shapes = [a[n].shape for n in SMALL_SHARDED]
    own = _pack([a[n] for n in SMALL_SHARDED])
    everyone = allgather_small(own, name="gather_small").reshape((8,) + own.shape)
    per_slot = [_unpack(everyone[2 * s], shard_shapes) for s in range(N_SLOT)]
    small = {n: jnp.concatenate([per_slot[s][i] for s in range(N_SLOT)], axis=-1) for i, n in enumerate(SMALL_SHARDED)}
    small.update({n: a[n] for n in SMALL_WHOLE})

    me = 4 * x_i + 2 * y_i + c_i
    ada_cols = a["w_ada"].shape[-1]
    c_all = allgather_small(a["c"].reshape((-1, LANES)), name="gather_c").reshape((8, D_MODEL))
    ada_slots = [jnp.zeros(a["w_ada"].shape[1:], F32) for _ in range(DEPTH)]
    mod_shards, ada_vjp = jax.vjp(lambda c_ctx, sl: ada_shard(c_all, c_ctx, a["w_ada"], sl), a["c_ctx"], ada_slots)
    packed = _pack(mod_shards)
    every = allgather_small(packed, name="gather_ada").reshape((8,) + packed.shape)
    by_slot = [_unpack(every[2 * s], [(ADA_ROWS, ada_cols)] * DEPTH) for s in range(N_SLOT)]
    mod = []
    for l in range(DEPTH):
        rows = jnp.concatenate([by_slot[s][l] for s in range(N_SLOT)], axis=-1) + a["b_ada"][l]
        mod.append(jnp.concatenate([lax.dynamic_slice_in_dim(rows, me, 1, axis=0), rows[8:9]], axis=0))

    groups = _in_groups()
    win = []
    for l in range(DEPTH):
        slabs = [gath["w_in"][s, l] for s in range(N_SLOT)]
        win.append({g: _take_cols_slabs(slabs, src) for g, src in groups})
    slots = [{n: jnp.zeros(s, F32) for n, s in _slot_shapes().items()} for _ in range(DEPTH)]
    diff = {"x": a["x"][0], "mod": mod, "small": small, "slots": slots}
    fixed = {"ctx": a["ctx"][0], "target": a["loss_target"][0], "gath": gath, "win": win}
    loss, grads = jax.value_and_grad(lambda d: _local_loss(d, fixed, n_lat=n_lat))(diff)

    dmod = _pack(grads["mod"])
    every = allgather_small(dmod, name="gather_dmod").reshape((8,) + dmod.shape)
    per_dev = [_unpack(every[i], [(2, 6 * D_MODEL)] * DEPTH) for i in range(8)]
    grad_b_ada, cots = [], []
    for l in range(DEPTH):
        lat = jnp.concatenate([per_dev[i][l][0:1] for i in range(8)], axis=0)
        ctx_rows = jnp.concatenate([per_dev[i][l][1].reshape((-1, LANES)) for i in range(8)], axis=0)
        ctx_sum = sum_blocks(ctx_rows, 8, name="l%d_dmod_ctx_sum" % l).reshape((1, 6 * D_MODEL))
        all_rows = jnp.concatenate([lat, ctx_sum, jnp.zeros((ADA_ROWS - 9, 6 * D_MODEL), F32)], axis=0)
        grad_b_ada.append(sum_blocks(all_rows.reshape((-1, LANES)), ADA_ROWS, name="l%d_b_ada_sum" % l).reshape(-1))
        cots.append(lax.dynamic_slice_in_dim(all_rows, slot * ada_cols, ada_cols, axis=1))
    c_ctx_part, ada_grads = ada_vjp(cots)

    parts = []
    for n in BIG:
        for l in range(DEPTH):
            sl = grads["slots"][l]
            if n == "w_in":
                parts.append(jnp.stack(_untake_cols([(sl["in_" + g], src) for g, src in groups], IN_COLS, N_SLOT)))
            elif n == "w_branch":
                b3 = sl["w_branch3"].reshape((N_SLOT, M2_H, LANES, BRANCH_W))[:, :, :M2_P].reshape((N_SLOT, BRANCH_W, BRANCH_W))
                parts.append(jnp.concatenate([sl["w_branch0"], sl["w_branch1"], sl["w_branch2"], b3], axis=1))
            else:
                parts.append(sl[n])
    from_sibling = reduce_pair(parts)
    pair_sums = [add_own_half(g, r, core, name="pair_sum%d" % i) for i, (g, r) in enumerate(zip(parts, from_sibling))]
    from_chips = reduce_chips(pair_sums)
    place = jnp.stack([slot, c_i]).astype(jnp.int32)
    reduced = []
    for k, n in enumerate(BIG):
        grad = jnp.zeros((DEPTH,) + parts[DEPTH * k].shape[1:], F32)
        for l in range(DEPTH):
            i = DEPTH * k + l
            grad = chip_sum(pair_sums[i], from_chips[i], place, grad, l, name="chip_sum%d" % i)
        reduced.append(grad)
    big_grads = {n: g.reshape(a[n].shape) for n, g in zip(BIG, swap_pair(reduced))}

    summed = SMALL_WHOLE + SMALL_SHARDED + ["c_ctx"]
    local = dict(grads["small"], c_ctx=0.5 * c_ctx_part)
    partial = _pack([local[n] for n in summed] + [loss.reshape((1,))])
    total = sum_blocks(allgather_small(partial, name="gather_small_grads"), 8, name="sum_small_grads")
    pieces = _unpack(total, [local[n].shape for n in summed] + [(1,)])
    small_grads = dict(zip(summed, pieces[:-1]))
    for n in SMALL_SHARDED:
        width = a[n].shape[-1]
        small_grads[n] = lax.dynamic_slice_in_dim(small_grads[n], slot * width, width, axis=-1)
    small_grads["b_ada"] = jnp.stack(grad_b_ada)
    big_grads["w_ada"] = jnp.stack(ada_grads)
    small_names = summed + ["b_ada"]
    loss_all = pieces[-1].reshape(())

    grad_w, delta, new_m, new_v = {}, {}, {}, {}
    two_d = lambda t: t.reshape((-1, t.shape[-1]))
    for n in BIG + ["w_ada"]:
        d, mn, vn = adamw(two_d(a[n]), two_d(big_grads[n]), two_d(a["m_" + n]), two_d(a["v_" + n]), name="adamw_" + n)
        grad_w[n], delta[n], new_m[n], new_v[n] = big_grads[n], d.reshape(a[n].shape), mn.reshape(a[n].shape), vn.reshape(a[n].shape)
    shapes = [a[n].shape for n in small_names]
    d, mn, vn = adamw(_pack([a[n] for n in small_names]), _pack([small_grads[n] for n in small_names]),
                      _pack([a["m_" + n] for n in small_names]), _pack([a["v_" + n] for n in small_names]), name="adamw_small")
    for n, dd, mm, vv in zip(small_names, _unpack(d, shapes), _unpack(mn, shapes), _unpack(vn, shapes)):
        grad_w[n], delta[n], new_m[n], new_v[n] = small_grads[n], dd, mm, vv

    return (loss_all, grads["x"][None], *[grad_w[n] for n in WEIGHTS], *[delta[n] for n in WEIGHTS],
            *[new_m[n] for n in WEIGHTS], *[new_v[n] for n in WEIGHTS])


def kernel(x, c, ctx, c_ctx, norm1_g, norm2_g, w_ada, b_ada, w_in, b_merge, gla_a2, gla_ab, gla_norm_g, na_rpb, gdn_conv, gdn_a_log, gdn_dt_bias, gdn_norm_g, m2_conv, m2_conv_b, m2_a_log, m2_dt_bias, m2_d, m2_norm_g, w_branch, w_out, w_ffn1, w_ffn3, w_ffn2, final_norm_g, loss_target, m_c_ctx, m_norm1_g, m_norm2_g, m_w_ada, m_b_ada, m_w_in, m_b_merge, m_gla_a2, m_gla_ab, m_gla_norm_g, m_na_rpb, m_gdn_conv, m_gdn_a_log, m_gdn_dt_bias, m_gdn_norm_g, m_m2_conv, m_m2_conv_b, m_m2_a_log, m_m2_dt_bias, m_m2_d, m_m2_norm_g, m_w_branch, m_w_out, m_w_ffn1, m_w_ffn3, m_w_ffn2, m_final_norm_g, v_c_ctx, v_norm1_g, v_norm2_g, v_w_ada, v_b_ada, v_w_in, v_b_merge, v_gla_a2, v_gla_ab, v_gla_norm_g, v_na_rpb, v_gdn_conv, v_gdn_a_log, v_gdn_dt_bias, v_gdn_norm_g, v_m2_conv, v_m2_conv_b, v_m2_a_log, v_m2_dt_bias, v_m2_d, v_m2_norm_g, v_w_branch, v_w_out, v_w_ffn1, v_w_ffn3, v_w_ffn2, v_final_norm_g):
    return _step(dict(locals()))
```

```python
import functools
import math

import numpy as np
import jax
import jax.numpy as jnp
from jax import lax
from jax.experimental import pallas as pl
from jax.experimental.pallas import tpu as pltpu

F32 = jnp.float32
BF16 = jnp.bfloat16
HI = lax.Precision.HIGHEST
MESH = pl.DeviceIdType.MESH
ANY = pl.BlockSpec(memory_space=pl.ANY)

VMEM_LIMIT = 56 * 1024 * 1024
LANES = 128
SUBLANES = 8

D_MODEL = 2048
DEPTH = 2
GRID_W = 64
CHUNK = 64
CONV_W = 5
RMS_EPS = 1e-6
NEG_INF = -1e30
ROPE_BASE = 10000.0
BRANCH_W = 512
GLA_H, GLA_DK, GLA_DV, GLA_LR, GLA_TAU = 4, 64, 128, 16, 16.0
NA_H, NA_D, NA_WIN_R, NA_WIN_C = 4, 128, 8, 16
GDN_H, GDN_D = 4, 128
M2_P, M2_H, M2_N, M2_G = 64, 8, 128, 2
D_FF = 5632
IN_COLS = 14912
N_SLOT = 4
ADAM_LR, ADAM_B1, ADAM_B2, ADAM_EPS, ADAM_WD, ADAM_STEP = 0.001, 0.9, 0.999, 1e-08, 0.01, 10


def _cparams(*sem):
    return pltpu.CompilerParams(dimension_semantics=sem if sem else None, vmem_limit_bytes=VMEM_LIMIT)


def _tile(n, target, mult):
    if n <= target:
        return n
    best = None
    for t in range(mult, target + 1, mult):
        if n % t == 0:
            best = t
    assert best is not None, (n, target, mult)
    return best


def _nt(a, b):
    return lax.dot_general(a.astype(BF16), b.astype(BF16), (((1,), (1,)), ((), ())), preferred_element_type=F32)


def _tn(a, b):
    return lax.dot_general(a.astype(BF16), b.astype(BF16), (((0,), (0,)), ((), ())), preferred_element_type=F32)


def _nn(a, b):
    return jnp.dot(a.astype(BF16), b.astype(BF16), preferred_element_type=F32)


def _dot3(a, b, dims):
    a_hi, b_hi = a.astype(BF16), b.astype(BF16)
    a_lo = (a - a_hi.astype(F32)).astype(BF16)
    b_lo = (b - b_hi.astype(F32)).astype(BF16)
    dot = lambda u, v: lax.dot_general(u, v, dims, preferred_element_type=F32)
    return dot(a_hi, b_hi) + (dot(a_hi, b_lo) + dot(a_lo, b_hi))


_NN, _NT, _TN = ((((1,), (0,)), ((), ())), (((1,), (1,)), ((), ())), (((0,), (0,)), ((), ())))


@jax.custom_vjp
def _nn_hi(a, b):
    return _dot3(a, b, _NN)


_nn_hi.defvjp(lambda a, b: (_dot3(a, b, _NN), (a, b)),
              lambda res, g: (_dot3(g, res[1], _NT), _dot3(res[0], g, _TN)))


def _w_spec(layout, prefix, r_idx, c_idx, br, bc, slot_dim):
    none = (None,) * len(prefix)
    if layout == "plain":
        return pl.BlockSpec(none + (br, bc), lambda i, j, k: prefix + (r_idx(i, j, k), c_idx(i, j, k)))
    if layout == "col":
        per = slot_dim // bc
        return pl.BlockSpec((None,) + none + (br, bc),
                            lambda i, j, k: (c_idx(i, j, k) // per,) + prefix + (r_idx(i, j, k), c_idx(i, j, k) % per))
    per = slot_dim // br
    return pl.BlockSpec((None,) + none + (br, bc),
                        lambda i, j, k: (r_idx(i, j, k) // per,) + prefix + (r_idx(i, j, k) % per, c_idx(i, j, k)))


def _mm(a, b, *, name, ta=False, tb=False, b_layout="plain", b_prefix=(), out_layout="plain", out_dtype=F32):
    m, kdim = (a.shape[1], a.shape[0]) if ta else a.shape
    rows, cols = b.shape[-2:]
    if b_layout == "col":
        cols *= N_SLOT
    elif b_layout == "row":
        rows *= N_SLOT
    n = rows if tb else cols
    assert (cols if tb else rows) == kdim, (a.shape, b.shape, ta, tb)
    n_unit = n // N_SLOT if (out_layout == "col" or (b_layout == ("row" if tb else "col"))) else n
    k_unit = kdim // N_SLOT if b_layout == ("col" if tb else "row") else kdim
    odd_n = n_unit % 1408 == 0 and n_unit % 512 != 0
    odd_k = k_unit % 1408 == 0 and k_unit % 512 != 0
    if ta:
        tm = _tile(m, 1024, LANES)
        tn = _tile(n_unit, 1408 if odd_n else 1024, LANES)
        tk = _tile(k_unit, 1056, 2 * SUBLANES)
    elif tb:
        tm = _tile(m, 768 if odd_k else 704, 2 * SUBLANES)
        tn = _tile(n_unit, 1408 if odd_n else 2048, LANES)
        tk = _tile(k_unit, 1408 if odd_k else 2048, LANES)
    else:
        tm = _tile(m, 768, 2 * SUBLANES)
        tn = _tile(n_unit, 1408 if odd_n else 512, LANES)
        tk = _tile(k_unit, 1408 if odd_k else 2048, LANES)
    nk = kdim // tk
    a_spec = (pl.BlockSpec((tk, tm), lambda i, j, k: (k, i)) if ta else pl.BlockSpec((tm, tk), lambda i, j, k: (i, k)))
    slot_dim = b.shape[-1] if b_layout == "col" else b.shape[-2]
    if tb:
        b_spec = _w_spec(b_layout, tuple(b_prefix), lambda i, j, k: j, lambda i, j, k: k, tn, tk, slot_dim)
    else:
        b_spec = _w_spec(b_layout, tuple(b_prefix), lambda i, j, k: k, lambda i, j, k: j, tk, tn, slot_dim)
    if out_layout == "col":
        per = (n // N_SLOT) // tn
        out_shape = jax.ShapeDtypeStruct((N_SLOT, m, n // N_SLOT), out_dtype)
        out_spec = pl.BlockSpec((None, tm, tn), lambda i, j, k: (j // per, i, j % per))
    else:
        out_shape = jax.ShapeDtypeStruct((m, n), out_dtype)
        out_spec = pl.BlockSpec((tm, tn), lambda i, j, k: (i, j))
    dims = (((0 if ta else 1,), (1 if tb else 0,)), ((), ()))

    def product(a_ref, b_ref):
        return lax.dot_general(a_ref[...].astype(BF16), b_ref[...].astype(BF16), dims, preferred_element_type=F32)

    def body_once(a_ref, b_ref, o_ref):
        o_ref[...] = product(a_ref, b_ref).astype(o_ref.dtype)

    def body(a_ref, b_ref, o_ref, acc_ref):
        k = pl.program_id(2)

        @pl.when(k == 0)
        def _():
            acc_ref[...] = jnp.zeros_like(acc_ref)

        acc_ref[...] += product(a_ref, b_ref)

        @pl.when(k == nk - 1)
        def _():
            o_ref[...] = acc_ref[...].astype(o_ref.dtype)

    return pl.pallas_call(
        body_once if nk == 1 else body, name=name, grid=(m // tm, n // tn, nk), in_specs=[a_spec, b_spec],
        out_specs=out_spec, out_shape=out_shape, scratch_shapes=[] if nk == 1 else [pltpu.VMEM((tm, tn), F32)],
        compiler_params=_cparams("parallel", "parallel", "arbitrary"),
    )(a, b)


def linear(a, w, grad_slot, *, name, layout="plain", prefix=(), out_dtype=F32):
    @jax.custom_vjp
    def f(a, w, grad_slot):
        return _mm(a, w, name=name + "_fwd", b_layout=layout, b_prefix=prefix, out_dtype=out_dtype)

    def fwd(a, w, grad_slot):
        return f(a, w, grad_slot), (a, w)

    def bwd(res, g):
        a, w = res
        da = _mm(g, w, name=name + "_dgrad", tb=True, b_layout=layout, b_prefix=prefix, out_dtype=a.dtype)
        dw = _mm(a, g, name=name + "_wgrad", ta=True, out_layout="col" if layout == "col" else "plain")
        if layout == "row":
            dw = dw.reshape((N_SLOT, dw.shape[0] // N_SLOT, dw.shape[1]))
        return da, None, dw

    f.defvjp(fwd, bwd)
    return f(a, w, grad_slot)


def _rowwise_specs(rows, consts, params, tile, seg_tile):
    def row_spec(r):
        return pl.BlockSpec((tile, r.shape[1]), lambda i: (i, 0))

    def par_spec(p):
        if p.shape[0] == 2:
            return pl.BlockSpec((None,) + p.shape[1:], lambda i: (jnp.where(i >= seg_tile, 1, 0), 0, 0))
        return pl.BlockSpec((None,) + p.shape[1:], lambda i: (0, 0, 0))

    return [row_spec(r) for r in rows], [row_spec(r) for r in consts], [par_spec(p) for p in params]


def rowwise(f, rows, consts, params, *, out_widths, tile, n_lat, name, out_dtype=F32):
    rows, consts, params = tuple(rows), tuple(consts), tuple(params)
    n_rows = rows[0].shape[0]
    tile = math.gcd(math.gcd(n_rows, n_lat), tile)
    assert tile % SUBLANES == 0
    seg_tile = n_lat // tile
    grid = (n_rows // tile,)
    nr, nc, npar = len(rows), len(consts), len(params)
    r_specs, c_specs, p_specs = _rowwise_specs(rows, consts, params, tile, seg_tile)
    out_dtypes = out_dtype if isinstance(out_dtype, (tuple, list)) else (out_dtype,) * len(out_widths)
    out_shape = tuple(jax.ShapeDtypeStruct((n_rows, w), dt) for w, dt in zip(out_widths, out_dtypes))
    out_specs = tuple(pl.BlockSpec((tile, w), lambda i: (i, 0)) for w in out_widths)
    n_out = len(out_widths)

    def fwd_call(rows, consts, params):
        def body(*refs):
            ins = [r[...].astype(F32) for r in refs[:nr + nc + npar]]
            outs = f(*ins)
            for o_ref, o in zip(refs[nr + nc + npar:], outs):
                o_ref[...] = o.astype(o_ref.dtype)

        return pl.pallas_call(body, name=name + "_fwd", grid=grid, in_specs=r_specs + c_specs + p_specs,
                              out_specs=out_specs, out_shape=out_shape,
                              compiler_params=_cparams("parallel"))(*rows, *consts, *params)

    def bwd_call(rows, consts, params, gouts):
        def body(*refs):
            i = pl.program_id(0)
            ins = [r[...].astype(F32) for r in refs[:nr + nc + npar]]
            gs = tuple(r[...].astype(F32) for r in refs[nr + nc + npar:nr + nc + npar + n_out])
            d_refs = refs[nr + nc + npar + n_out:]
            cvals = ins[nr:nr + nc]

            def g(*diff):
                return tuple(f(*diff[:nr], *cvals, *diff[nr:]))

            _, vjp = jax.vjp(g, *ins[:nr], *ins[nr + nc:])
            grads = vjp(gs)
            for d_ref, gr in zip(d_refs[:nr], grads[:nr]):
                d_ref[...] = gr.astype(d_ref.dtype)
            for p, d_ref, gr in zip(params, d_refs[nr:], grads[nr:]):
                first = (i == 0) | (i == seg_tile) if p.shape[0] == 2 else (i == 0)

                @pl.when(first)
                def _():
                    d_ref[...] = jnp.zeros_like(d_ref)

                d_ref[...] += gr

        d_shape = tuple(jax.ShapeDtypeStruct(r.shape, r.dtype) for r in rows) + tuple(
            jax.ShapeDtypeStruct(p.shape, F32) for p in params)
        g_specs = [pl.BlockSpec((tile, w), lambda i: (i, 0)) for w in out_widths]
        return pl.pallas_call(body, name=name + "_bwd", grid=grid,
                              in_specs=r_specs + c_specs + p_specs + g_specs,
                              out_specs=tuple(r_specs + p_specs), out_shape=d_shape,
                              compiler_params=_cparams("arbitrary"))(*rows, *consts, *params, *gouts)

    @jax.custom_vjp
    def op(rows, consts, params):
        return fwd_call(rows, consts, params)

    def op_fwd(rows, consts, params):
        return op(rows, consts, params), (rows, consts, params)

    def op_bwd(res, gouts):
        rows, consts, params = res
        d = bwd_call(rows, consts, params, tuple(gouts))
        return tuple(d[:nr]), tuple(None for _ in consts), tuple(d[nr:])

    op.defvjp(op_fwd, op_bwd)
    return op(rows, consts, params)


def _rms(x, width=None):
    w = x.shape[-1] if width is None else width
    return x * lax.rsqrt(jnp.sum(x * x, axis=-1, keepdims=True) * (1.0 / w) + RMS_EPS)


def _silu(x):
    return x * jax.nn.sigmoid(x)


def _f_modnorm(x, g, sc, sh):
    return ((_rms(x) * g) * (1.0 + sc) + sh,)


def _f_silu(x):
    return (_silu(x),)


def _f_gla_prep(lr, a2, ab):
    z = _nn(lr, a2) + ab
    return ((jnp.minimum(z, 0.0) - jnp.log(1.0 + jnp.exp(-jnp.abs(z)))) * (1.0 / GLA_TAU),)


def _f_headnorm_gate(o, g, ng):
    outs = []
    for h in range(BRANCH_W // LANES):
        lo = h * LANES
        oh = o[:, lo:lo + LANES] + o[:, BRANCH_W + lo:BRANCH_W + lo + LANES]
        outs.append(_rms(oh) * ng * _silu(g[:, lo:lo + LANES]))
    return (jnp.concatenate(outs, axis=-1),)


def _f_gdn_prep(x, alog, dtb):
    half = x.shape[1] // 2
    beta = jax.nn.sigmoid(x[:, :half])
    la = -jnp.exp(alog) * jax.nn.softplus(x[:, half:] + dtb)
    return beta, la


def _f_m2_prep(x, alog, dtb):
    dt = jax.nn.softplus(x + dtb)
    return dt, -jnp.exp(alog) * dt


def _f_m2_fin(o, z, xs, dskip, ng):
    w = z.shape[1]
    y = (o[:, :w] + o[:, w:] + dskip * xs) * _silu(z)
    return (_rms(y, BRANCH_W) * ng,)


def _f_merge(gate, z0, z1, z2, z3, bm):
    acc = None
    for i, z in enumerate((z0, z1, z2, z3)):
        lo = i * D_MODEL
        t = jax.nn.sigmoid(gate[:, lo:lo + D_MODEL] + bm[:, lo:lo + D_MODEL]) * z
        acc = t if acc is None else acc + t
    return (acc,)


def _f_resid(x, y, g):
    return (x + g * y,)


def _f_resid_modnorm(x, y, g, ng, sc, sh):
    x1 = x + g * y
    return (x1,) + _f_modnorm(x1, ng, sc, sh)


def _f_swiglu(u1, u3):
    return (_silu(u1) * u3,)


def _f_loss(x, tgt, g):
    e = _rms(x) * g - tgt
    per_row = 0.5 * jnp.sum(e * e, axis=-1, keepdims=True) * (1.0 / D_MODEL)
    return (jnp.broadcast_to(per_row * (1.0 / LANES), (x.shape[0], LANES)),)


_HALO = 8


def _conv_segments(n_lat, n_ctx):
    segs = [(0, _HALO, n_lat), (n_lat, n_lat + 3 * _HALO, n_ctx)]
    return segs, n_lat + n_ctx + 4 * _HALO


def _conv_stage(buf, src, n_lat, n_ctx):
    zeros = jnp.zeros((_HALO, LANES), F32)
    buf[0:_HALO, :] = zeros
    buf[_HALO:_HALO + n_lat, :] = src[0:n_lat, :].astype(F32)
    buf[n_lat + _HALO:n_lat + 2 * _HALO, :] = zeros
    buf[n_lat + 2 * _HALO:n_lat + 3 * _HALO, :] = zeros
    buf[n_lat + 3 * _HALO:n_lat + 3 * _HALO + n_ctx, :] = src[n_lat:n_lat + n_ctx, :].astype(F32)
    buf[n_lat + n_ctx + 3 * _HALO:n_lat + n_ctx + 4 * _HALO, :] = zeros


def conv_silu(x, w, b, *, n_lat, name):
    n_rows, n_ch = x.shape
    n_ctx = n_rows - n_lat
    segs, n_buf = _conv_segments(n_lat, n_ctx)
    grid = (n_ch // LANES,)
    col = lambda r: pl.BlockSpec((r, LANES), lambda j: (0, j))
    half = CONV_W // 2

    def tiles():
        for row0, off, length in segs:
            tr = _tile(length, 256, SUBLANES)
            for t0 in range(0, length, tr):
                yield row0 + t0, off + t0, tr

    def pre_act(buf, w_ref, b_ref, off, tr):
        acc = jnp.broadcast_to(b_ref[...], (tr, LANES))
        for j in range(CONV_W):
            acc = acc + w_ref[j:j + 1, :] * buf[off + j - half:off + j - half + tr, :]
        return acc

    def fwd_call(x, w, b):
        def body(x_ref, w_ref, b_ref, o_ref, buf):
            _conv_stage(buf, x_ref, n_lat, n_ctx)
            for row, off, tr in tiles():
                o_ref[row:row + tr, :] = _silu(pre_act(buf, w_ref, b_ref, off, tr))

        return pl.pallas_call(body, name=name + "_fwd", grid=grid, in_specs=[col(n_rows), col(CONV_W), col(1)],
                              out_specs=col(n_rows), out_shape=jax.ShapeDtypeStruct(x.shape, F32),
                              scratch_shapes=[pltpu.VMEM((n_buf, LANES), F32)],
                              compiler_params=_cparams("parallel"))(x, w, b)

    def bwd_call(x, w, b, g):
        def body(x_ref, w_ref, b_ref, g_ref, dx_ref, dw_ref, db_ref, xbuf, dbuf):
            _conv_stage(xbuf, x_ref, n_lat, n_ctx)
            _conv_stage(dbuf, g_ref, n_lat, n_ctx)
            dw = [jnp.zeros((1, LANES), F32) for _ in range(CONV_W)]
            db = jnp.zeros((1, LANES), F32)
            for row, off, tr in tiles():
                pre = pre_act(xbuf, w_ref, b_ref, off, tr)
                s = jax.nn.sigmoid(pre)
                dpre = g_ref[row:row + tr, :] * (s * (1.0 + pre * (1.0 - s)))
                dbuf[off:off + tr, :] = dpre
                db = db + jnp.sum(dpre, axis=0, keepdims=True)
                for j in range(CONV_W):
                    dw[j] = dw[j] + jnp.sum(dpre * xbuf[off + j - half:off + j - half + tr, :], axis=0, keepdims=True)
            for row, off, tr in tiles():
                acc = jnp.zeros((tr, LANES), F32)
                for j in range(CONV_W):
                    acc = acc + w_ref[j:j + 1, :] * dbuf[off - j + half:off - j + half + tr, :]
                dx_ref[row:row + tr, :] = acc.astype(dx_ref.dtype)
            for j in range(CONV_W):
                dw_ref[j:j + 1, :] = dw[j]
            db_ref[...] = db

        return pl.pallas_call(
            body, name=name + "_bwd", grid=grid, in_specs=[col(n_rows), col(CONV_W), col(1), col(n_rows)],
            out_specs=(col(n_rows), col(CONV_W), col(1)),
            out_shape=(jax.ShapeDtypeStruct(x.shape, x.dtype), jax.ShapeDtypeStruct(w.shape, F32),
                       jax.ShapeDtypeStruct(b.shape, F32)),
            scratch_shapes=[pltpu.VMEM((n_buf, LANES), F32), pltpu.VMEM((n_buf, LANES), F32)],
            compiler_params=_cparams("parallel"))(x, w, b, g)

    @jax.custom_vjp
    def op(x, w, b):
        return fwd_call(x, w, b)

    op.defvjp(lambda x, w, b: (op(x, w, b), (x, w, b)), lambda res, g: bwd_call(*res, g))
    return op(x, w, b)


def chunk_scan(step, shared, shared_lanes, perdir, perdir_lanes, consts, *, heads, state_shape, out_w, n_lat, name,
               keep_shape=None):
    assert keep_shape is None or heads == 1
    shared, perdir, consts = tuple(shared), tuple(perdir), tuple(consts)
    n_rows = shared[0].shape[0]
    nl, ncx = n_lat // CHUNK, (n_rows - n_lat) // CHUNK
    n_chunks = nl + ncx
    ow_all = heads * out_w
    ns, npd, ncst = len(shared), len(perdir), len(consts)

    def cidx(d, n):
        m = n - ncx
        return jnp.where(n < ncx, nl + jnp.where(d == 0, n, ncx - 1 - n), jnp.where(d == 0, m, nl - 1 - m))

    def specs(order):
        sh = [pl.BlockSpec((CHUNK, a.shape[1]), lambda d, n: (cidx(d, order(n)), 0)) for a in shared]
        pd = [pl.BlockSpec((CHUNK, a.shape[1] // 2), lambda d, n: (cidx(d, order(n)), d)) for a in perdir]
        cs = [pl.BlockSpec((CHUNK, a.shape[1]), lambda d, n: (cidx(d, order(n)), 0)) for a in consts]
        o = pl.BlockSpec((CHUNK, ow_all), lambda d, n: (cidx(d, order(n)), d))
        st = pl.BlockSpec((None, None, heads) + state_shape, lambda d, n: (d, order(n), 0) + (0,) * len(state_shape))
        kp = [] if keep_shape is None else [pl.BlockSpec((None, None) + keep_shape,
                                                        lambda d, n: (d, order(n)) + (0,) * len(keep_shape))]
        return sh, pd, cs, o, st, kp

    def mask(d):
        r = lax.broadcasted_iota(jnp.int32, (CHUNK, CHUNK), 0)
        c = lax.broadcasted_iota(jnp.int32, (CHUNK, CHUNK), 1)
        lower = jnp.where(r >= c, 1.0, 0.0).astype(F32)
        upper = jnp.where(r <= c, 1.0, 0.0).astype(F32)
        return jnp.where(d == 0, lower, upper)

    def head_slices(h):
        out = []
        for lanes in tuple(shared_lanes) + tuple(perdir_lanes):
            out.append([slice(off + (h // hpg) * w, off + (h // hpg) * w + w) for off, w, hpg in lanes])
        return out

    def load(refs, h):
        return tuple(tuple(ref[:, s].astype(F32) for s in sl) for ref, sl in zip(refs, head_slices(h)))

    state_sds = jax.ShapeDtypeStruct((2, n_chunks, heads) + state_shape, F32)

    def fwd_call(shared, perdir, consts):
        sh, pd, cs, o_spec, st_spec, kp_spec = specs(lambda n: n)

        def body(*refs):
            in_refs = refs[:ns + npd]
            c_refs = refs[ns + npd:ns + npd + ncst]
            o_ref, ss_ref = refs[ns + npd + ncst:ns + npd + ncst + 2]
            s_scr = refs[-1]
            d, n = pl.program_id(0), pl.program_id(1)

            @pl.when(n == 0)
            def _():
                s_scr[...] = jnp.zeros_like(s_scr)

            m = mask(d)
            cv = tuple(c[...] for c in c_refs)
            ins = [load(in_refs, h) for h in range(heads)]
            s0 = [s_scr[h] for h in range(heads)]
            res = [step(ins[h], cv, s0[h], m, None) for h in range(heads)]
            for h in range(heads):
                o_ref[:, h * out_w:(h + 1) * out_w] = res[h][0]
                ss_ref[h] = s0[h]
                s_scr[h] = res[h][1]
            if keep_shape is not None:
                refs[-2][...] = res[0][2]

        keep_sds = [] if keep_shape is None else [jax.ShapeDtypeStruct((2, n_chunks) + keep_shape, F32)]
        return pl.pallas_call(
            body, name=name + "_fwd", grid=(2, n_chunks), in_specs=sh + pd + cs, out_specs=tuple([o_spec, st_spec] + kp_spec),
            out_shape=tuple([jax.ShapeDtypeStruct((n_rows, 2 * ow_all), F32), state_sds] + keep_sds),
            scratch_shapes=[pltpu.VMEM((heads,) + state_shape, F32)],
            compiler_params=_cparams("arbitrary", "arbitrary"))(*shared, *perdir, *consts)

    def bwd_call(shared, perdir, consts, starts, kept, g):
        sh, pd, cs, o_spec, st_spec, kp_spec = specs(lambda n: n_chunks - 1 - n)
        dsh = [pl.BlockSpec((CHUNK, a.shape[1]), lambda d, n: (cidx(d, n_chunks - 1 - n), d)) for a in shared]
        nk = len(kept)

        def body(*refs):
            in_refs = refs[:ns + npd]
            c_refs = refs[ns + npd:ns + npd + ncst]
            ss_ref, g_ref = refs[ns + npd + ncst:ns + npd + ncst + 2]
            kept_val = refs[ns + npd + ncst + 2][...] if nk else None
            d_refs = refs[ns + npd + ncst + 2 + nk:ns + npd + ncst + 2 + nk + ns + npd]
            ds_scr = refs[-1]
            d, n = pl.program_id(0), pl.program_id(1)

            @pl.when(n == 0)
            def _():
                ds_scr[...] = jnp.zeros_like(ds_scr)

            m = mask(d)
            cv = tuple(c[...] for c in c_refs)
            ins = [load(in_refs, h) for h in range(heads)]
            cots = [(g_ref[:, h * out_w:(h + 1) * out_w], ds_scr[h]) for h in range(heads)]
            starts = [ss_ref[h] for h in range(heads)]
            grads = []
            for h in range(heads):
                _, vjp = jax.vjp(lambda i_, s_: step(i_, cv, s_, m, kept_val)[:2], ins[h], starts[h])
                grads.append(vjp(cots[h]))
            for d_ref in d_refs:
                d_ref[...] = jnp.zeros_like(d_ref)
            for h in range(heads):
                g_ins, g_s = grads[h]
                for d_ref, sl, gr in zip(d_refs, head_slices(h), g_ins):
                    for s, gv in zip(sl, gr):
                        d_ref[:, s] += gv
                ds_scr[h] = g_s

        d_shape = tuple(jax.ShapeDtypeStruct((n_rows, 2 * a.shape[1]), F32) for a in shared) + tuple(
            jax.ShapeDtypeStruct(a.shape, F32) for a in perdir)
        return pl.pallas_call(
            body, name=name + "_bwd", grid=(2, n_chunks), in_specs=sh + pd + cs + [st_spec, o_spec] + kp_spec,
            out_specs=tuple(dsh + pd), out_shape=d_shape,
            scratch_shapes=[pltpu.VMEM((heads,) + state_shape, F32)],
            compiler_params=_cparams("arbitrary", "arbitrary"))(*shared, *perdir, *consts, starts, g, *kept)

    @jax.custom_vjp
    def op(shared, perdir, consts):
        return fwd_call(shared, perdir, consts)[0]

    def op_fwd(shared, perdir, consts):
        o, starts, *kept = fwd_call(shared, perdir, consts)
        return o, (shared, perdir, consts, starts, tuple(kept))

    def op_bwd(res, g):
        shared, perdir, consts, starts, kept = res
        d = bwd_call(shared, perdir, consts, starts, kept, g)
        d_sh = tuple((a[:, :a.shape[1] // 2] + a[:, a.shape[1] // 2:]).astype(s.dtype) for a, s in zip(d[:ns], shared))
        return d_sh, tuple(d[ns:]), tuple(None for _ in consts)

    op.defvjp(op_fwd, op_bwd)
    return op(shared, perdir, consts)


@jax.custom_vjp
def _swap_halves(x):
    return pltpu.roll(x, LANES // 2, 1)


_swap_halves.defvjp(lambda x: (_swap_halves(x), None), lambda _, g: (_swap_halves(g),))


def _gla_step(ins, consts, st, m, kept):
    (q_all, k_all), (v_all,), (la_all,) = ins
    cos, sin = consts
    b_all = _nn_hi(m, la_all)
    bl_all = jnp.sum(la_all, axis=0, keepdims=True)
    outs, st_new = [], []
    for h in range(GLA_H):
        blk = lambda x: x[:, h * LANES:(h + 1) * LANES]
        q, k, v, b, bl = blk(q_all), blk(k_all), blk(v_all), blk(b_all), blk(bl_all)
        st_h = st[h * GLA_DV:(h + 1) * GLA_DV]
        q = (q * cos + _swap_halves(q) * sin) * (GLA_DK ** -0.5)
        k = k * cos + _swap_halves(k) * sin
        qi = q * jnp.exp(b)
        ki = k * jnp.exp(-b)
        outs.append(_nt(qi, st_h) + _nn(_nt(qi, ki) * m, v))
        st_new.append(st_h * jnp.exp(bl) + _tn(v, k * jnp.exp(bl - b)))
    return jnp.concatenate(outs, axis=1), jnp.concatenate(st_new, axis=0), None


def _l2n(x):
    return x * lax.rsqrt(jnp.sum(x * x, axis=-1, keepdims=True) + RMS_EPS)


def _tri_inv_fwd(nmat):
    r = lax.broadcasted_iota(jnp.int32, nmat.shape, 0)
    c = lax.broadcasted_iota(jnp.int32, nmat.shape, 1)
    inv = jnp.where(r == c, 1.0, 0.0).astype(F32) - nmat
    p = nmat
    for _ in range(5):
        p = _nn_hi(p, p)
        inv = inv + _nn_hi(inv, p)
    return inv


@jax.custom_vjp
def _unit_tri_inv(nmat):
    return _tri_inv_fwd(nmat)


def _unit_tri_inv_bwd(inv, g):
    return (-_dot3(_dot3(inv, g, _TN), inv, _NT),)


_unit_tri_inv.defvjp(lambda nmat: (lambda inv: (inv, inv))(_tri_inv_fwd(nmat)), _unit_tri_inv_bwd)


@jax.custom_vjp
def _kept_tri_inv(nmat, inv):
    return inv


_kept_tri_inv.defvjp(lambda nmat, inv: (inv, inv), lambda inv, g: _unit_tri_inv_bwd(inv, g) + (jnp.zeros_like(inv),))


def _lane_col(x, h):
    lane = lax.broadcasted_iota(jnp.int32, x.shape, 1)
    return jnp.sum(jnp.where(lane == h, x, 0.0), axis=1, keepdims=True)


def _masked_exp(diff, mask):
    return jnp.where(mask > 0, jnp.exp(jnp.where(mask > 0, diff, 0.0)), 0.0)


def _gdn_step(ins, consts, s, m, kept):
    (q, k, v), (beta,), (la,) = ins
    n = GDN_H * CHUNK
    hs = range(GDN_H)
    blk = lambda x, h: x[:, h * GDN_D:(h + 1) * GDN_D]
    rows = lambda x, h: x[h * CHUNK:(h + 1) * CHUNK]
    qh = [_l2n(blk(q, h)) * (GDN_D ** -0.5) for h in hs]
    kh = [_l2n(blk(k, h)) for h in hs]
    k_st = jnp.concatenate(kh, axis=0)
    q_st = jnp.concatenate(qh, axis=0)
    v_st = jnp.concatenate([blk(v, h) for h in hs], axis=0)
    beta_st = jnp.concatenate([_lane_col(beta, h) for h in hs], axis=0)
    la_cols = [_lane_col(la, h) for h in hs]
    la_st = jnp.concatenate([jnp.broadcast_to(c, (CHUNK, GDN_D)) for c in la_cols], axis=0)
    r = lax.broadcasted_iota(jnp.int32, (n, n), 0)
    c = lax.broadcasted_iota(jnp.int32, (n, n), 1)
    e = jnp.where(lax.broadcasted_iota(jnp.int32, (n, CHUNK), 0) % CHUNK == lax.broadcasted_iota(jnp.int32, (n, CHUNK), 1),
                  1.0, 0.0).astype(F32)
    m_bd = jnp.where(r // CHUNK == c // CHUNK, _nt(_nn(e, m), e), 0.0)
    eye = jnp.where(r == c, 1.0, 0.0).astype(F32)
    b_st = _nn_hi(m_bd, la_st)
    b_t = b_st.T
    diff = jnp.concatenate([b_st, b_st], axis=1) - jnp.concatenate([b_t, b_t], axis=0)
    strict = _masked_exp(diff, m_bd - eye)
    incl = strict + eye
    nmat = beta_st * _nt(k_st, k_st) * strict
    inv = _unit_tri_inv(nmat) if kept is None else _kept_tri_inv(nmat, kept)
    wu = _nn_hi(inv, jnp.concatenate([k_st * (beta_st * jnp.exp(b_st)), v_st * beta_st], axis=-1))
    w, u0 = wu[:, :GDN_D], wu[:, GDN_D:]
    us, s_new, qs = [], [], []
    for h in hs:
        s_h = s[h * GDN_D:(h + 1) * GDN_D]
        bl = jnp.sum(jnp.broadcast_to(la_cols[h], (CHUNK, GDN_D)), axis=0, keepdims=True)
        u_h = rows(u0, h) - _nn(rows(w, h), s_h)
        s_new.append(jnp.exp(bl) * s_h + _tn(kh[h] * jnp.exp(bl - rows(b_st, h)), u_h))
        us.append(u_h)
        qs.append(_nn(qh[h], s_h))
    o_st = jnp.exp(b_st) * jnp.concatenate(qs, axis=0) + _nn(_nt(q_st, k_st) * incl, jnp.concatenate(us, axis=0))
    return jnp.concatenate([rows(o_st, h) for h in hs], axis=1), jnp.concatenate(s_new, axis=0), inv


def _ssd_step(ins, consts, s, m, kept):
    (xs,), (bc,), (dt,), (la,) = ins
    hpg = M2_H // M2_G
    b_all = _nn_hi(m, la)
    bl_all = jnp.sum(la, axis=0, keepdims=True)
    b_t = b_all.T
    row_id = lax.broadcasted_iota(jnp.int32, b_t.shape, 0)
    bm = [bc[:, g * M2_N:(g + 1) * M2_N] for g in range(M2_G)]
    cm = [bc[:, (M2_G + g) * M2_N:(M2_G + g + 1) * M2_N] for g in range(M2_G)]
    scores = [_nt(cm[g], bm[g]) for g in range(M2_G)]
    outs, s_new = [], []
    for h in range(M2_H):
        g = h // hpg
        s_h = s[h * M2_N:(h + 1) * M2_N]
        b_col = _lane_col(b_all, h)
        bl = _lane_col(bl_all, h)
        b_row = jnp.sum(jnp.where(row_id == h, b_t, 0.0), axis=0, keepdims=True)
        xv = xs[:, h * LANES:(h + 1) * LANES] * _lane_col(dt, h)
        outs.append(jnp.exp(b_col) * _nn(cm[g], s_h) + _nn(scores[g] * _masked_exp(b_col - b_row, m), xv))
        s_new.append(jnp.exp(bl) * s_h + _tn(bm[g] * jnp.exp(bl - b_col), xv))
    return jnp.concatenate(outs, axis=1), jnp.concatenate(s_new, axis=0), None


def _na_tile(q, kw, vw, kc, vc, bias):
    qs = q * (NA_D ** -0.5)
    s1 = _nt(qs, kw) + bias
    s2 = _nt(qs, kc)
    mx = lax.stop_gradient(jnp.maximum(jnp.max(s1, axis=-1, keepdims=True), jnp.max(s2, axis=-1, keepdims=True)))
    p1 = jnp.exp(s1 - mx)
    p2 = jnp.exp(s2 - mx)
    den = jnp.sum(p1, axis=-1, keepdims=True) + jnp.sum(p2, axis=-1, keepdims=True)
    return (_nn(p1, vw) + _nn(p2, vc)) / den


def _ctx_tile(q, k, v):
    s = _nt(q * (NA_D ** -0.5), k)
    p = jnp.exp(s - lax.stop_gradient(jnp.max(s, axis=-1, keepdims=True)))
    return _nn(p, v) / jnp.sum(p, axis=-1, keepdims=True)


def natten(q, k, v, bias, *, n_lat, name):
    n_rows = q.shape[0]
    n_ctx = n_rows - n_lat
    g_rows = n_lat // GRID_W
    win = NA_WIN_R * GRID_W
    ctx_blk = n_lat // n_ctx

    def start(n):
        return jnp.clip(n - NA_WIN_R // 2, 0, g_rows - NA_WIN_R)

    def case(n):
        return n - start(n)

    q_spec = pl.BlockSpec((GRID_W, LANES), lambda h, n: (n, h))
    lat_spec = pl.BlockSpec((n_lat, LANES), lambda h, n: (0, h))
    ctx_in = pl.BlockSpec((n_ctx, LANES), lambda h, n: (ctx_blk, h))
    ctx_out = pl.BlockSpec((n_ctx, LANES), lambda h, n: (0, h))
    bias_spec = pl.BlockSpec((None, None, GRID_W, win), lambda h, n: (h, case(n), 0, 0))
    lat_sds = jax.ShapeDtypeStruct((n_lat, BRANCH_W), F32)
    ctx_sds = jax.ShapeDtypeStruct((n_ctx, BRANCH_W), F32)
    f32 = lambda t: t.astype(F32)

    def lat_fwd(q, k, v, bias):
        def body(q_ref, k_ref, v_ref, kc_ref, vc_ref, b_ref, o_ref):
            r0 = pl.multiple_of(start(pl.program_id(1)) * GRID_W, GRID_W)
            o_ref[...] = _na_tile(f32(q_ref[...]), f32(k_ref[pl.ds(r0, win), :]), f32(v_ref[pl.ds(r0, win), :]),
                                  f32(kc_ref[...]), f32(vc_ref[...]), b_ref[...])

        return pl.pallas_call(body, name=name + "_lat_fwd", grid=(NA_H, g_rows),
                              in_specs=[q_spec, lat_spec, lat_spec, ctx_in, ctx_in, bias_spec], out_specs=q_spec,
                              out_shape=lat_sds, compiler_params=_cparams("parallel", "arbitrary"))(q, k, v, k, v, bias)

    def lat_bwd(q, k, v, bias, g):
        def body(q_ref, k_ref, v_ref, kc_ref, vc_ref, b_ref, g_ref, dq_ref, dk_ref, dv_ref, dkc_ref, dvc_ref, db_ref):
            n = pl.program_id(1)
            r0 = pl.multiple_of(start(n) * GRID_W, GRID_W)

            @pl.when(n == 0)
            def _():
                for r in (dk_ref, dv_ref, dkc_ref, dvc_ref):
                    r[...] = jnp.zeros_like(r)

            @pl.when((n == 0) | (case(n) != case(jnp.maximum(n - 1, 0))))
            def _():
                db_ref[...] = jnp.zeros_like(db_ref)

            _, vjp = jax.vjp(_na_tile, f32(q_ref[...]), f32(k_ref[pl.ds(r0, win), :]), f32(v_ref[pl.ds(r0, win), :]),
                             f32(kc_ref[...]), f32(vc_ref[...]), b_ref[...])
            dq, dkw, dvw, dkc, dvc, db = vjp(g_ref[...])
            dq_ref[...] = dq
            dk_ref[pl.ds(r0, win), :] += dkw
            dv_ref[pl.ds(r0, win), :] += dvw
            dkc_ref[...] += dkc
            dvc_ref[...] += dvc
            db_ref[...] += db

        return pl.pallas_call(
            body, name=name + "_lat_bwd", grid=(NA_H, g_rows),
            in_specs=[q_spec, lat_spec, lat_spec, ctx_in, ctx_in, bias_spec, q_spec],
            out_specs=(q_spec, lat_spec, lat_spec, ctx_out, ctx_out, bias_spec),
            out_shape=(lat_sds, lat_sds, lat_sds, ctx_sds, ctx_sds, jax.ShapeDtypeStruct(bias.shape, F32)),
            compiler_params=_cparams("parallel", "arbitrary"))(q, k, v, k, v, bias, g)

    c_in = pl.BlockSpec((n_ctx, LANES), lambda h: (ctx_blk, h))
    c_out = pl.BlockSpec((n_ctx, LANES), lambda h: (0, h))

    def ctx_fwd(q, k, v):
        def body(q_ref, k_ref, v_ref, o_ref):
            o_ref[...] = _ctx_tile(f32(q_ref[...]), f32(k_ref[...]), f32(v_ref[...]))

        return pl.pallas_call(body, name=name + "_ctx_fwd", grid=(NA_H,), in_specs=[c_in, c_in, c_in], out_specs=c_out,
                              out_shape=ctx_sds, compiler_params=_cparams("parallel"))(q, k, v)

    def ctx_bwd(q, k, v, g):
        def body(q_ref, k_ref, v_ref, g_ref, dq_ref, dk_ref, dv_ref):
            _, vjp = jax.vjp(_ctx_tile, f32(q_ref[...]), f32(k_ref[...]), f32(v_ref[...]))
            dq_ref[...], dk_ref[...], dv_ref[...] = vjp(g_ref[...])

        return pl.pallas_call(body, name=name + "_ctx_bwd", grid=(NA_H,), in_specs=[c_in, c_in, c_in, c_out],
                              out_specs=(c_out, c_out, c_out), out_shape=(ctx_sds, ctx_sds, ctx_sds),
                              compiler_params=_cparams("parallel"))(q, k, v, g)

    @jax.custom_vjp
    def op(q, k, v, bias):
        return jnp.concatenate([lat_fwd(q, k, v, bias), ctx_fwd(q, k, v)], axis=0)

    def op_bwd(res, g):
        q, k, v, bias = res
        dq, dk, dv, dkc, dvc, db = lat_bwd(q, k, v, bias, g[:n_lat])
        dqc, dkc2, dvc2 = ctx_bwd(q, k, v, g[n_lat:])
        return (jnp.concatenate([dq, dqc], axis=0).astype(q.dtype), jnp.concatenate([dk, dkc + dkc2], axis=0).astype(k.dtype),
                jnp.concatenate([dv, dvc + dvc2], axis=0).astype(v.dtype), db)

    op.defvjp(lambda q, k, v, bias: (op(q, k, v, bias), (q, k, v, bias)), op_bwd)
    return op(q, k, v, bias)


def _runs(src):
    src = np.asarray(src)
    out, i = [], 0
    while i < len(src):
        j = i + 1
        if src[i] < 0:
            while j < len(src) and src[j] < 0:
                j += 1
            out.append((-1, j - i))
        else:
            while j < len(src) and src[j] == src[j - 1] + 1:
                j += 1
            out.append((int(src[i]), j - i))
        i = j
    return out


def _take_cols(w, src):
    pieces = [jnp.zeros(w.shape[:-1] + (ln,), w.dtype) if s < 0 else w[..., s:s + ln] for s, ln in _runs(src)]
    return pieces[0] if len(pieces) == 1 else jnp.concatenate(pieces, axis=-1)


def _take_cols_slabs(slabs, src):
    width = slabs[0].shape[-1]
    pieces = []
    for s, ln in _runs(src):
        if s < 0:
            pieces.append(jnp.zeros(slabs[0].shape[:-1] + (ln,), slabs[0].dtype))
        while s >= 0 and ln > 0:
            off = s % width
            take = min(ln, width - off)
            pieces.append(slabs[s // width][..., off:off + take])
            s, ln = s + take, ln - take
    return pieces[0] if len(pieces) == 1 else jnp.concatenate(pieces, axis=-1)


def _untake_cols(parts, n_cols, n_slabs=1):
    found = []
    for arr, src in parts:
        pos = 0
        for s, ln in _runs(src):
            if s >= 0:
                found.append((s, ln, arr, pos))
            pos += ln
    found.sort(key=lambda t: t[0])
    width = n_cols // n_slabs
    slabs, at = [[] for _ in range(n_slabs)], 0
    for s, ln, arr, pos in found:
        assert s == at, (s, at)
        at += ln
        while ln > 0:
            take = min(ln, width - s % width)
            slabs[s // width].append(arr[..., pos:pos + take])
            s, pos, ln = s + take, pos + take, ln - take
    assert at == n_cols, (at, n_cols)
    return [jnp.concatenate(p, axis=-1) for p in slabs]


def _pad_heads(base, heads, real, width):
    return np.concatenate([np.concatenate([base + h * real + np.arange(real), -np.ones(width - real, np.int64)])
                           for h in range(heads)])


def _rope_heads(base, heads):
    z = -np.ones(32, np.int64)
    return np.concatenate([np.concatenate([base + h * 64 + np.arange(32), z, base + h * 64 + 32 + np.arange(32), z])
                           for h in range(heads)])


def _lane_block(base, n):
    return np.concatenate([base + np.arange(n), -np.ones(LANES - n, np.int64)])


def _in_groups():
    g0, n0, d0, m0, t0 = 0, 1568, 3104, 5168, 6720
    rng = lambda a, n: a + np.arange(n)
    return [
        ("gla_qk", np.concatenate([_rope_heads(g0, GLA_H), _rope_heads(g0 + 256, GLA_H)])),
        ("gla_v", rng(g0 + 512, 512)),
        ("gla_g", rng(g0 + 1024, 512)),
        ("gla_lr", _lane_block(g0 + 1536, 2 * GLA_LR)),
        ("na_q", rng(n0, 512)), ("na_k", rng(n0 + 512, 512)), ("na_v", rng(n0 + 1024, 512)),
        ("gdn_qkv", rng(d0, 1536)),
        ("gdn_z", rng(d0 + 1536, 512)),
        ("gdn_sm", np.concatenate([_lane_block(d0 + 2048 + 4 * i, GDN_H) for i in range(4)])),
        ("m2_z", _pad_heads(m0, M2_H, M2_P, LANES)),
        ("m2_xs", _pad_heads(m0 + 512, M2_H, M2_P, LANES)),
        ("m2_bc", rng(m0 + 1024, 512)),
        ("m2_dt", np.concatenate([_lane_block(m0 + 1536, M2_H), _lane_block(m0 + 1536 + M2_H, M2_H)])),
        ("gate", rng(t0, 4 * D_MODEL)),
    ]


_M2_PAD = _pad_heads(0, M2_H, M2_P, LANES)
_GLA_PAD = _rope_heads(0, GLA_H)


def _row3(v):
    return v.reshape((1, 1, -1))


def _dir_rows(p, n):
    return _row3(jnp.concatenate([_take_cols(p[d][None], _lane_block(0, n)) for d in range(2)], axis=-1))


def _rope_tables(n_lat, n_ctx):
    n_freq = GLA_DK // 4
    freqs = ROPE_BASE ** (-jnp.arange(n_freq, dtype=F32) / n_freq)
    t = jnp.arange(n_lat)
    row = (t // GRID_W).astype(F32)
    colv = (t % GRID_W).astype(F32)
    ang = jnp.concatenate([row[:, None] * freqs, colv[:, None] * freqs], axis=-1)
    c, s = jnp.cos(ang), jnp.sin(ang)
    one, zero = jnp.ones_like(c), jnp.zeros_like(c)
    cos_t = jnp.concatenate([c, one, c, one], axis=-1)
    sin_t = jnp.concatenate([-s, zero, s, zero], axis=-1)
    return (jnp.concatenate([cos_t, jnp.ones((n_ctx, LANES), F32)], axis=0),
            jnp.concatenate([sin_t, jnp.zeros((n_ctx, LANES), F32)], axis=0))


def _na_bias(rpb):
    case = np.arange(NA_WIN_R)
    r = np.arange(NA_WIN_R)
    dr = r[None, :] - case[:, None] + NA_WIN_R - 1
    ci = np.arange(GRID_W)
    dc = np.clip(ci[None, :] - ci[:, None], 1 - NA_WIN_C, NA_WIN_C - 1) + NA_WIN_C - 1
    c0 = np.clip(ci - NA_WIN_C // 2, 0, GRID_W - NA_WIN_C)
    ok = (ci[None, :] >= c0[:, None]) & (ci[None, :] < c0[:, None] + NA_WIN_C)
    pick_r = np.zeros((NA_WIN_R, NA_WIN_R, 2 * NA_WIN_R - 1), np.float32)
    pick_r[case[:, None], r[None, :], dr] = 1.0
    pick_c = np.zeros((2 * NA_WIN_C - 1, GRID_W, GRID_W), np.float32)
    pick_c[dc, ci[:, None], ci[None, :]] = 1.0
    rows = jnp.einsum("hdk,crd->hcrk", rpb, pick_r, precision=HI)
    tbl = jnp.einsum("hcrk,kij->hcirj", rows, pick_c, precision=HI)
    tbl = jnp.where(ok[None, None, :, None, :], tbl, NEG_INF)
    return tbl.reshape((NA_H, NA_WIN_R, GRID_W, NA_WIN_R * GRID_W))


def _layer(l, xs, h, mod, mod_next, small, slots, gath, win, tables, *, n_lat):
    rw = functools.partial(rowwise, n_lat=n_lat)
    rwb = functools.partial(rowwise, n_lat=n_lat, out_dtype=BF16)
    nm = lambda s: "l%d_%s" % (l, s)
    sl = slots[l]
    rows6 = lambda t: [t[:, i * D_MODEL:(i + 1) * D_MODEL].reshape((2, 1, D_MODEL)) for i in range(6)]
    sh1, sc1, g1, sh2, sc2, g2 = rows6(mod)
    if h is None:
        (h,) = rwb(_f_modnorm, [xs], [], [_row3(small["norm1_g"][l]), sc1, sh1], out_widths=[D_MODEL], tile=256,
                  name=nm("norm1"))
    logits = ("gla_lr", "gdn_sm", "m2_dt")
    p = {g: linear(h, win[l][g], sl["in_" + g], name=nm("in_" + g), out_dtype=F32 if g in logits else BF16)
         for g, _ in _in_groups()}

    a2 = small["gla_a2"][l]
    a2p = jnp.concatenate([
        jnp.concatenate([_take_cols(a2[0], _GLA_PAD), jnp.zeros((GLA_LR, 512), F32)], axis=1),
        jnp.concatenate([jnp.zeros((GLA_LR, 512), F32), _take_cols(a2[1], _GLA_PAD)], axis=1),
        jnp.zeros((LANES - 2 * GLA_LR, 1024), F32)], axis=0)[None]
    abp = _row3(jnp.concatenate([_take_cols(small["gla_ab"][l][d][None], _GLA_PAD) for d in range(2)], axis=-1))
    (la,) = rw(_f_gla_prep, [p["gla_lr"]], [], [a2p, abp], out_widths=[1024], tile=256, name=nm("gla_prep"))
    whole = lambda off, width: (off, width, 1)
    o = chunk_scan(_gla_step, [p["gla_qk"], p["gla_v"]], [[whole(0, BRANCH_W), whole(BRANCH_W, BRANCH_W)], [whole(0, BRANCH_W)]],
                   [la], [[whole(0, BRANCH_W)]], tables, heads=1, state_shape=(GLA_H * GLA_DV, LANES), out_w=BRANCH_W,
                   n_lat=n_lat, name=nm("gla_scan"))
    (ya,) = rwb(_f_headnorm_gate, [o, p["gla_g"]], [], [_row3(small["gla_norm_g"][l])], out_widths=[BRANCH_W], tile=256,
               name=nm("gla_fin"))

    yb = natten(p["na_q"], p["na_k"], p["na_v"], _na_bias(small["na_rpb"][l]), n_lat=n_lat, name=nm("na"))

    cq = conv_silu(p["gdn_qkv"], small["gdn_conv"][l], jnp.zeros((1, 3 * BRANCH_W), F32), n_lat=n_lat, name=nm("gdn_conv"))
    beta, la = rw(_f_gdn_prep, [p["gdn_sm"]], [], [_dir_rows(small["gdn_a_log"][l], GDN_H), _dir_rows(small["gdn_dt_bias"][l], GDN_H)],
                  out_widths=[256, 256], tile=256, name=nm("gdn_prep"))
    o = chunk_scan(_gdn_step, [cq], [[whole(0, 512), whole(512, 512), whole(1024, 512)]], [beta, la],
                   [[whole(0, LANES)], [whole(0, LANES)]], [], heads=1, state_shape=(GDN_H * GDN_D, GDN_D),
                   out_w=BRANCH_W, n_lat=n_lat, name=nm("gdn_scan"), keep_shape=(GDN_H * CHUNK, GDN_H * CHUNK))
    (yc,) = rwb(_f_headnorm_gate, [o, p["gdn_z"]], [], [_row3(small["gdn_norm_g"][l])], out_widths=[BRANCH_W], tile=256,
               name=nm("gdn_fin"))

    cw, cb = small["m2_conv"][l], small["m2_conv_b"][l][None]
    cxs = conv_silu(p["m2_xs"], _take_cols(cw[:, :512], _M2_PAD), _take_cols(cb[:, :512], _M2_PAD), n_lat=n_lat,
                    name=nm("m2_conv_x"))
    cbc = conv_silu(p["m2_bc"], cw[:, 512:], cb[:, 512:], n_lat=n_lat, name=nm("m2_conv_bc"))
    dt, la = rw(_f_m2_prep, [p["m2_dt"]], [], [_dir_rows(small["m2_a_log"][l], M2_H), _dir_rows(small["m2_dt_bias"][l], M2_H)],
                out_widths=[256, 256], tile=256, name=nm("m2_prep"))
    o = chunk_scan(_ssd_step, [cxs, cbc], [[whole(0, 2 * BRANCH_W)], [whole(0, BRANCH_W)]], [dt, la],
                   [[whole(0, LANES)], [whole(0, LANES)]], [], heads=1, state_shape=(M2_H * M2_N, LANES),
                   out_w=2 * BRANCH_W, n_lat=n_lat, name=nm("m2_scan"))
    dskip = _row3(jnp.repeat(small["m2_d"][l], LANES))
    (yd,) = rwb(_f_m2_fin, [o, p["m2_z"], cxs], [], [dskip, _row3(_take_cols(small["m2_norm_g"][l][None], _M2_PAD))],
               out_widths=[2 * BRANCH_W], tile=128, name=nm("m2_fin"))

    wb = gath["w_branch"]
    zs = [linear(y, wb, sl["w_branch%d" % i], name=nm("branch%d" % i), layout="col", prefix=(l, i), out_dtype=BF16)
          for i, y in enumerate((ya, yb, yc))]
    wb3 = wb[:, l, 3].reshape((N_SLOT, M2_H, M2_P, BRANCH_W))
    wb3 = jnp.pad(wb3, ((0, 0), (0, 0), (0, LANES - M2_P), (0, 0))).reshape((N_SLOT, 2 * BRANCH_W, BRANCH_W))
    zs.append(linear(yd, wb3, sl["w_branch3"], name=nm("branch3"), layout="col", out_dtype=BF16))
    (merged,) = rwb(_f_merge, [p["gate"]] + zs, [], [_row3(small["b_merge"][l].reshape(-1))], out_widths=[D_MODEL], tile=64,
                   name=nm("merge"))
    y = linear(merged, gath["w_out"], sl["w_out"], name=nm("out"), layout="row", prefix=(l,))
    x1, h2 = rowwise(_f_resid_modnorm, [xs, y], [], [g1, _row3(small["norm2_g"][l]), sc2, sh2], n_lat=n_lat,
                     out_widths=[D_MODEL, D_MODEL], out_dtype=(F32, BF16), tile=256, name=nm("res1_norm2"))
    u1 = linear(h2, gath["w_ffn1"], sl["w_ffn1"], name=nm("ffn1"), layout="col", prefix=(l,), out_dtype=BF16)
    u3 = linear(h2, gath["w_ffn3"], sl["w_ffn3"], name=nm("ffn3"), layout="col", prefix=(l,), out_dtype=BF16)
    (act,) = rwb(_f_swiglu, [u1, u3], [], [], out_widths=[D_FF], tile=128, name=nm("swiglu"))
    f = linear(act, gath["w_ffn2"], sl["w_ffn2"], name=nm("ffn2"), layout="row", prefix=(l,))
    if mod_next is None:
        (x2,) = rw(_f_resid, [x1, f], [], [g2], out_widths=[D_MODEL], tile=256, name=nm("res2"))
        return x2, None
    sh_n, sc_n = rows6(mod_next)[:2]
    return rowwise(_f_resid_modnorm, [x1, f], [], [g2, _row3(small["norm1_g"][l + 1]), sc_n, sh_n], n_lat=n_lat,
                   out_widths=[D_MODEL, D_MODEL], out_dtype=(F32, BF16), tile=256, name=nm("res2_norm1"))


def _slot_shapes():
    s = {"w_out": (N_SLOT, D_MODEL // N_SLOT, D_MODEL),
         "w_ffn1": (N_SLOT, D_MODEL, D_FF // N_SLOT), "w_ffn3": (N_SLOT, D_MODEL, D_FF // N_SLOT),
         "w_ffn2": (N_SLOT, D_FF // N_SLOT, D_MODEL), "w_branch3": (N_SLOT, 2 * BRANCH_W, BRANCH_W)}
    for i in range(3):
        s["w_branch%d" % i] = (N_SLOT, BRANCH_W, BRANCH_W)
    for g, src in _in_groups():
        s["in_" + g] = (D_MODEL, len(src))
    return s


ADA_ROWS = 2 * SUBLANES


def ada_shard(c_all, c_ctx, w_ada, slots):
    cc = jnp.concatenate([c_all, c_ctx[None], jnp.zeros((ADA_ROWS - c_all.shape[0] - 1, D_MODEL), F32)], axis=0)
    (act,) = rowwise(_f_silu, [cc], [], [], out_widths=[D_MODEL], tile=ADA_ROWS, n_lat=ADA_ROWS, name="ada_silu",
                     out_dtype=BF16)
    return [linear(act, w_ada, slots[l], name="l%d_ada" % l, prefix=(l,)) for l in range(DEPTH)]


def _local_loss(diff, fixed, *, n_lat):
    small = diff["small"]
    n_ctx = fixed["ctx"].shape[0]
    xs = jnp.concatenate([diff["x"], fixed["ctx"]], axis=0)
    tables = _rope_tables(n_lat, n_ctx)
    h = None
    for l in range(DEPTH):
        mod_next = diff["mod"][l + 1] if l + 1 < DEPTH else None
        xs, h = _layer(l, xs, h, diff["mod"][l], mod_next, small, diff["slots"], fixed["gath"], fixed["win"], tables,
                       n_lat=n_lat)
    (lrow,) = rowwise(_f_loss, [xs[:n_lat]], [fixed["target"]], [_row3(small["final_norm_g"])], out_widths=[LANES],
                      tile=256, n_lat=n_lat, name="loss")
    return jnp.sum(lrow)


def _place():
    x, y, c = lax.axis_index("x"), lax.axis_index("y"), lax.axis_index("c")
    chips = [(1 - x, y), (x, 1 - y), (1 - x, 1 - y)]
    return x, y, c, (x, y, 1 - c), chips


def _remote(src, dst, send_sem, recv_sem, dev):
    return pltpu.make_async_remote_copy(src_ref=src, dst_ref=dst, send_sem=send_sem, recv_sem=recv_sem,
                                        device_id=dev, device_id_type=MESH)


def _dma_sems(*shape):
    return pltpu.SemaphoreType.DMA(shape)


def place_shard(w, slot, *, name):
    depth, k, n = w.shape
    tr = _tile(k, max(2 * SUBLANES, (1 << 19) // n // (2 * SUBLANES) * (2 * SUBLANES)), 2 * SUBLANES)

    def body(s_ref, w_ref, o_ref):
        o_ref[...] = w_ref[...].astype(o_ref.dtype)

    return pl.pallas_call(
        body, name=name,
        grid_spec=pltpu.PrefetchScalarGridSpec(
            num_scalar_prefetch=1, grid=(depth, k // tr),
            in_specs=[pl.BlockSpec((None, tr, n), lambda l, i, s: (l, i, 0))],
            out_specs=pl.BlockSpec((None, None, tr, n), lambda l, i, s: (s[0], l, i, 0))),
        out_shape=jax.ShapeDtypeStruct((N_SLOT, depth, k, n), BF16),
        compiler_params=_cparams("parallel", "parallel"))(slot, w)


def gather_weights(bufs):
    n = len(bufs)

    def body(*refs):
        o = refs[n:2 * n]
        send1, recv1, send2, recv2 = refs[2 * n:]
        x, y, c, sibling, chips = _place()
        g = 2 * x + y
        sent = []
        for k in range(n):
            for j, (cx, cy) in enumerate(chips):
                cp = _remote(o[k].at[g, c], o[k].at[g, c], send1.at[k, j], recv1.at[k, j], (cx, cy, c))
                cp.start()
                sent.append(cp)
        for k in range(n):
            for j, (cx, cy) in enumerate(chips):
                gj = 2 * cx + cy
                _remote(o[k].at[g, c], o[k].at[gj, c], send1.at[k, j], recv1.at[k, j], (cx, cy, c)).wait_recv()
                cp = _remote(o[k].at[gj, c], o[k].at[gj, c], send2.at[k, j], recv2.at[k, j], sibling)
                cp.start()
                sent.append(cp)
        for k in range(n):
            for j, (cx, cy) in enumerate(chips):
                gj = 2 * cx + cy
                _remote(o[k].at[gj, 1 - c], o[k].at[gj, 1 - c], send2.at[k, j], recv2.at[k, j], sibling).wait_recv()
        for cp in sent:
            cp.wait_send()

    return pl.pallas_call(
        body, name="gather_weights", in_specs=[ANY] * n, out_specs=[ANY] * n,
        out_shape=[jax.ShapeDtypeStruct(b.shape, b.dtype) for b in bufs],
        input_output_aliases={k: k for k in range(n)},
        scratch_shapes=[_dma_sems(n, 3), _dma_sems(n, 3), _dma_sems(n, 3), _dma_sems(n, 3)],
    )(*bufs)


def allgather_small(buf, *, name):
    m_per = buf.shape[0]

    def body(x_ref, out_ref, send_sems, recv_sems, local_sem):
        x, y, c, sibling, chips = _place()
        me = (x, y, c)

        def rows(px, py, pc):
            return out_ref.at[pl.ds((4 * px + 2 * py + pc) * m_per, m_per), :]

        def copy(k, block, to, src=None):
            return _remote(rows(*block) if src is None else src, rows(*block), send_sems.at[k], recv_sems.at[k], to)

        mine = pltpu.make_async_copy(x_ref, rows(*me), local_sem)
        mine.start()
        first = [copy(0, me, sibling, src=x_ref)]
        first += [copy(1 + j, me, (*chip, c), src=x_ref) for j, chip in enumerate(chips)]
        for cp in first:
            cp.start()
        passed = [copy(4 + j, (*chip, c), sibling) for j, chip in enumerate(chips)]
        for j, chip in enumerate(chips):
            copy(1 + j, (*chip, c), me).wait_recv()
            passed[j].start()
        copy(0, sibling, me).wait_recv()
        for j, chip in enumerate(chips):
            copy(4 + j, (*chip, 1 - c), me).wait_recv()
        for cp in first + passed:
            cp.wait_send()
        mine.wait()

    return pl.pallas_call(
        body, name=name, out_shape=jax.ShapeDtypeStruct((8 * m_per, LANES), buf.dtype),
        in_specs=[pl.BlockSpec(memory_space=pltpu.VMEM)], out_specs=pl.BlockSpec(memory_space=pltpu.VMEM),
        scratch_shapes=[_dma_sems(7), _dma_sems(7), pltpu.SemaphoreType.DMA],
        compiler_params=pltpu.CompilerParams(vmem_limit_bytes=VMEM_LIMIT),
    )(buf)


def sum_blocks(stacked, n_blocks, *, name):
    m = stacked.shape[0] // n_blocks
    width = stacked.shape[1]
    x3 = stacked.reshape((n_blocks, m, width))
    tr = _tile(m, max(SUBLANES, (1 << 18) // width // SUBLANES * SUBLANES), SUBLANES)

    def body(x_ref, o_ref):
        acc = x_ref[0]
        for s in range(1, n_blocks):
            acc = acc + x_ref[s]
        o_ref[...] = acc

    return pl.pallas_call(body, name=name, grid=(m // tr,), in_specs=[pl.BlockSpec((n_blocks, tr, width), lambda i: (0, i, 0))],
                          out_specs=pl.BlockSpec((tr, width), lambda i: (i, 0)),
                          out_shape=jax.ShapeDtypeStruct((m, width), F32), compiler_params=_cparams("parallel"))(x3)


def reduce_pair(gs):
    n = len(gs)

    def body(*refs):
        g, r = refs[:n], refs[n:2 * n]
        send, recv = refs[2 * n:]
        x, y, c, sibling, _ = _place()
        cps = []
        for i in range(n):
            k2 = gs[i].shape[1] // 2
            cp = _remote(g[i].at[:, pl.ds((1 - c) * k2, k2), :], r[i], send.at[i], recv.at[i], sibling)
            cp.start()
            cps.append(cp)
        for cp in cps:
            cp.wait()

    return pl.pallas_call(
        body, name="reduce_pair", in_specs=[ANY] * n, out_specs=[ANY] * n,
        out_shape=[jax.ShapeDtypeStruct((g.shape[0], g.shape[1] // 2, g.shape[2]), g.dtype) for g in gs],
        scratch_shapes=[_dma_sems(n), _dma_sems(n)],
    )(*gs)


def _row_tile(rows, width, budget, mult):
    return _tile(rows, max(mult, budget // width // mult * mult), mult)


def add_own_half(g, recv, core, *, name):
    n_slot, k2, width = recv.shape
    tr = _row_tile(k2, width, 1 << 19, 2 * SUBLANES)
    nb = k2 // tr

    def body(c_ref, g_ref, r_ref, o_ref):
        o_ref[...] = (g_ref[...] + r_ref[...]).astype(o_ref.dtype)

    spec = pl.BlockSpec((None, tr, width), lambda s, i, c: (s, i, 0))
    return pl.pallas_call(
        body, name=name,
        grid_spec=pltpu.PrefetchScalarGridSpec(
            num_scalar_prefetch=1, grid=(n_slot, nb),
            in_specs=[pl.BlockSpec((None, tr, width), lambda s, i, c: (s, c[0] * nb + i, 0)), spec], out_specs=spec),
        out_shape=jax.ShapeDtypeStruct(recv.shape, BF16), compiler_params=_cparams("parallel", "parallel"))(core, g, recv)


def reduce_chips(qs):
    n = len(qs)

    def body(*refs):
        q, r = refs[:n], refs[n:2 * n]
        send, recv = refs[2 * n:]
        x, y, c, _, chips = _place()
        cps = []
        for i in range(n):
            for j, (cx, cy) in enumerate(chips):
                cp = _remote(q[i].at[2 * cx + cy], r[i].at[j], send.at[i, j], recv.at[i, j], (cx, cy, c))
                cp.start()
                cps.append(cp)
        for cp in cps:
            cp.wait()

    return pl.pallas_call(
        body, name="reduce_chips", in_specs=[ANY] * n, out_specs=[ANY] * n,
        out_shape=[jax.ShapeDtypeStruct((3,) + q.shape[1:], q.dtype) for q in qs],
        scratch_shapes=[_dma_sems(n, 3), _dma_sems(n, 3)],
    )(*qs)


def chip_sum(q, recv, place, grad, layer, *, name):
    _, k2, width = recv.shape
    tr = _row_tile(k2, width, 1 << 18, 2 * SUBLANES)
    nb = k2 // tr

    def body(p_ref, q_ref, r_ref, grad_ref, o_ref):
        acc = q_ref[...].astype(F32)
        for j in range(3):
            acc = acc + r_ref[j].astype(F32)
        o_ref[...] = acc

    return pl.pallas_call(
        body, name=name,
        grid_spec=pltpu.PrefetchScalarGridSpec(
            num_scalar_prefetch=1, grid=(nb,),
            in_specs=[pl.BlockSpec((None, tr, width), lambda i, p: (p[0], i, 0)),
                      pl.BlockSpec((3, tr, width), lambda i, p: (0, i, 0)), ANY],
            out_specs=pl.BlockSpec((None, tr, width), lambda i, p: (layer, p[1] * nb + i, 0))),
        out_shape=jax.ShapeDtypeStruct(grad.shape, F32), input_output_aliases={3: 0},
        compiler_params=_cparams("parallel"))(place, q, recv, grad)


def swap_pair(grads):
    n = len(grads)

    def body(*refs):
        o = refs[n:2 * n]
        send, recv = refs[2 * n:]
        x, y, c, sibling, _ = _place()
        cps = []
        for k in range(n):
            k2 = grads[k].shape[1] // 2
            for l in range(DEPTH):
                half = o[k].at[l, pl.ds(c * k2, k2), :]
                cp = _remote(half, half, send.at[k, l], recv.at[k, l], sibling)
                cp.start()
                cps.append(cp)
        for cp in cps:
            cp.wait()

    return pl.pallas_call(
        body, name="swap_pair", in_specs=[ANY] * n, out_specs=[ANY] * n,
        out_shape=[jax.ShapeDtypeStruct(g.shape, g.dtype) for g in grads],
        input_output_aliases={k: k for k in range(n)},
        scratch_shapes=[_dma_sems(n, DEPTH), _dma_sems(n, DEPTH)],
    )(*grads)


def adamw(w, g, m, v, *, name):
    rows, width = w.shape
    tr = _tile(rows, max(SUBLANES, (1 << 19) // width // SUBLANES * SUBLANES), SUBLANES)

    def body(w_ref, g_ref, m_ref, v_ref, d_ref, mo_ref, vo_ref):
        gv = g_ref[...]
        mn = ADAM_B1 * m_ref[...] + (1.0 - ADAM_B1) * gv
        vn = ADAM_B2 * v_ref[...] + (1.0 - ADAM_B2) * (gv * gv)
        m_hat = mn / (1.0 - ADAM_B1 ** ADAM_STEP)
        v_hat = vn / (1.0 - ADAM_B2 ** ADAM_STEP)
        d_ref[...] = -ADAM_LR * (m_hat / (jnp.sqrt(v_hat) + ADAM_EPS) + ADAM_WD * w_ref[...])
        mo_ref[...] = mn
        vo_ref[...] = vn

    spec = pl.BlockSpec((tr, width), lambda i: (i, 0))
    sds = jax.ShapeDtypeStruct((rows, width), F32)
    return pl.pallas_call(body, name=name, grid=(rows // tr,), in_specs=[spec] * 4, out_specs=(spec,) * 3,
                          out_shape=(sds,) * 3, compiler_params=_cparams("parallel"))(w, g, m, v)


def _pack(arrs):
    flat = jnp.concatenate([a.reshape(-1) for a in arrs])
    pad = (-flat.shape[0]) % (SUBLANES * LANES)
    return jnp.pad(flat, (0, pad)).reshape((-1, LANES))


def _unpack(buf, shapes):
    flat, out, at = buf.reshape(-1), [], 0
    for s in shapes:
        size = int(np.prod(s))
        out.append(flat[at:at + size].reshape(s))
        at += size
    return out


BIG = ["w_in", "w_branch", "w_out", "w_ffn1", "w_ffn3", "w_ffn2"]
SMALL_SHARDED = ["b_merge", "gla_a2", "gla_ab", "gdn_conv", "m2_conv"]
SMALL_WHOLE = ["norm1_g", "norm2_g", "gla_norm_g", "na_rpb", "gdn_a_log", "gdn_dt_bias", "gdn_norm_g",
               "m2_conv_b", "m2_a_log", "m2_dt_bias", "m2_d", "m2_norm_g", "final_norm_g"]
WEIGHTS = ["c_ctx", "norm1_g", "norm2_g", "w_ada", "b_ada", "w_in", "b_merge", "gla_a2", "gla_ab", "gla_norm_g", "na_rpb",
           "gdn_conv", "gdn_a_log", "gdn_dt_bias", "gdn_norm_g", "m2_conv", "m2_conv_b", "m2_a_log", "m2_dt_bias", "m2_d",
           "m2_norm_g", "w_branch", "w_out", "w_ffn1", "w_ffn3", "w_ffn2", "final_norm_g"]


def _step(a):
    n_lat = a["x"].shape[1]
    x_i, y_i, c_i = lax.axis_index("x"), lax.axis_index("y"), lax.axis_index("c")
    slot = 2 * x_i + y_i

    slot_arr = slot.astype(jnp.int32).reshape((1,))
    core = c_i.astype(jnp.int32).reshape((1,))
    placed = [place_shard(a[n].reshape((DEPTH, -1, a[n].shape[-1])), slot_arr, name="place_" + n) for n in BIG]
    gath = dict(zip(BIG, gather_weights(placed)))
    gath["w_branch"] = gath["w_branch"].reshape((N_SLOT, DEPTH, 4, BRANCH_W, BRANCH_W))
    shard_shapes = [a[n].shape for n in SMALL_SHARDED]
    own = _pack([a[n] for n in SMALL_SHARDED])
    everyone = allgather_small(own, name="gather_small").reshape((8,) + own.shape)
    per_slot = [_unpack(everyone[2 * s], shard_shapes) for s in range(N_SLOT)]
    small = {n: jnp.concatenate([per_slot[s][i] for s in range(N_SLOT)], axis=-1) for i, n in enumerate(SMALL_SHARDED)}
    small.update({n: a[n] for n in SMALL_WHOLE})

    me = 4 * x_i + 2 * y_i + c_i
    ada_cols = a["w_ada"].shape[-1]
    c_all = allgather_small(a["c"].reshape((-1, LANES)), name="gather_c").reshape((8, D_MODEL))
    ada_slots = [jnp.zeros(a["w_ada"].shape[1:], F32) for _ in range(DEPTH)]
    mod_shards, ada_vjp = jax.vjp(lambda c_ctx, sl: ada_shard(c_all, c_ctx, a["w_ada"], sl), a["c_ctx"], ada_slots)
    packed = _pack(mod_shards)
    every = allgather_small(packed, name="gather_ada").reshape((8,) + packed.shape)
    by_slot = [_unpack(every[2 * s], [(ADA_ROWS, ada_cols)] * DEPTH) for s in range(N_SLOT)]
    mod = []
    for l in range(DEPTH):
        rows = jnp.concatenate([by_slot[s][l] for s in range(N_SLOT)], axis=-1) + a["b_ada"][l]
        mod.append(jnp.concatenate([lax.dynamic_slice_in_dim(rows, me, 1, axis=0), rows[8:9]], axis=0))

    groups = _in_groups()
    win = []
    for l in range(DEPTH):
        slabs = [gath["w_in"][s, l] for s in range(N_SLOT)]
        win.append({g: _take_cols_slabs(slabs, src) for g, src in groups})
    slots = [{n: jnp.zeros(s, F32) for n, s in _slot_shapes().items()} for _ in range(DEPTH)]
    diff = {"x": a["x"][0], "mod": mod, "small": small, "slots": slots}
    fixed = {"ctx": a["ctx"][0], "target": a["loss_target"][0], "gath": gath, "win": win}
    loss, grads = jax.value_and_grad(lambda d: _local_loss(d, fixed, n_lat=n_lat))(diff)

    dmod = _pack(grads["mod"])
    every = allgather_small(dmod, name="gather_dmod").reshape((8,) + dmod.shape)
    per_dev = [_unpack(every[i], [(2, 6 * D_MODEL)] * DEPTH) for i in range(8)]
    grad_b_ada, cots = [], []
    for l in range(DEPTH):
        lat = jnp.concatenate([per_dev[i][l][0:1] for i in range(8)], axis=0)
        ctx_rows = jnp.concatenate([per_dev[i][l][1].reshape((-1, LANES)) for i in range(8)], axis=0)
        ctx_sum = sum_blocks(ctx_rows, 8, name="l%d_dmod_ctx_sum" % l).reshape((1, 6 * D_MODEL))
        all_rows = jnp.concatenate([lat, ctx_sum, jnp.zeros((ADA_ROWS - 9, 6 * D_MODEL), F32)], axis=0)
        grad_b_ada.append(sum_blocks(all_rows.reshape((-1, LANES)), ADA_ROWS, name="l%d_b_ada_sum" % l).reshape(-1))
        cots.append(lax.dynamic_slice_in_dim(all_rows, slot * ada_cols, ada_cols, axis=1))
    c_ctx_part, ada_grads = ada_vjp(cots)

    parts = []
    for n in BIG:
        for l in range(DEPTH):
            sl = grads["slots"][l]
            if n == "w_in":
                parts.append(jnp.stack(_untake_cols([(sl["in_" + g], src) for g, src in groups], IN_COLS, N_SLOT)))
            elif n == "w_branch":
                b3 = sl["w_branch3"].reshape((N_SLOT, M2_H, LANES, BRANCH_W))[:, :, :M2_P].reshape((N_SLOT, BRANCH_W, BRANCH_W))
                parts.append(jnp.concatenate([sl["w_branch0"], sl["w_branch1"], sl["w_branch2"], b3], axis=1))
            else:
                parts.append(sl[n])
    from_sibling = reduce_pair(parts)
    pair_sums = [add_own_half(g, r, core, name="pair_sum%d" % i) for i, (g, r) in enumerate(zip(parts, from_sibling))]
    from_chips = reduce_chips(pair_sums)
    place = jnp.stack([slot, c_i]).astype(jnp.int32)
    reduced = []
    for k, n in enumerate(BIG):
        grad = jnp.zeros((DEPTH,) + parts[DEPTH * k].shape[1:], F32)
        for l in range(DEPTH):
            i = DEPTH * k + l
            grad = chip_sum(pair_sums[i], from_chips[i], place, grad, l, name="chip_sum%d" % i)
        reduced.append(grad)
    big_grads = {n: g.reshape(a[n].shape) for n, g in zip(BIG, swap_pair(reduced))}

    summed = SMALL_WHOLE + SMALL_SHARDED + ["c_ctx"]
    local = dict(grads["small"], c_ctx=0.5 * c_ctx_part)
    partial = _pack([local[n] for n in summed] + [loss.reshape((1,))])
    total = sum_blocks(allgather_small(partial, name="gather_small_grads"), 8, name="sum_small_grads")
    pieces = _unpack(total, [local[n].shape for n in summed] + [(1,)])
    small_grads = dict(zip(summed, pieces[:-1]))
    for n in SMALL_SHARDED:
        width = a[n].shape[-1]
        small_grads[n] = lax.dynamic_slice_in_dim(small_grads[n], slot * width, width, axis=-1)
    small_grads["b_ada"] = jnp.stack(grad_b_ada)
    big_grads["w_ada"] = jnp.stack(ada_grads)
    small_names = summed + ["b_ada"]
    loss_all = pieces[-1].reshape(())

    grad_w, delta, new_m, new_v = {}, {}, {}, {}
    two_d = lambda t: t.reshape((-1, t.shape[-1]))
    for n in BIG + ["w_ada"]:
        d, mn, vn = adamw(two_d(a[n]), two_d(big_grads[n]), two_d(a["m_" + n]), two_d(a["v_" + n]), name="adamw_" + n)
        grad_w[n], delta[n], new_m[n], new_v[n] = big_grads[n], d.reshape(a[n].shape), mn.reshape(a[n].shape), vn.reshape(a[n].shape)
    shapes = [a[n].shape for n in small_names]
    d, mn, vn = adamw(_pack([a[n] for n in small_names]), _pack([small_grads[n] for n in small_names]),
                      _pack([a["m_" + n] for n in small_names]), _pack([a["v_" + n] for n in small_names]), name="adamw_small")
    for n, dd, mm, vv in zip(small_names, _unpack(d, shapes), _unpack(mn, shapes), _unpack(vn, shapes)):
        grad_w[n], delta[n], new_m[n], new_v[n] = small_grads[n], dd, mm, vv

    return (loss_all, grads["x"][None], *[grad_w[n] for n in WEIGHTS], *[delta[n] for n in WEIGHTS],
            *[new_m[n] for n in WEIGHTS], *[new_v[n] for n in WEIGHTS])


def kernel(x, c, ctx, c_ctx, norm1_g, norm2_g, w_ada, b_ada, w_in, b_merge, gla_a2, gla_ab, gla_norm_g, na_rpb, gdn_conv, gdn_a_log, gdn_dt_bias, gdn_norm_g, m2_conv, m2_conv_b, m2_a_log, m2_dt_bias, m2_d, m2_norm_g, w_branch, w_out, w_ffn1, w_ffn3, w_ffn2, final_norm_g, loss_target, m_c_ctx, m_norm1_g, m_norm2_g, m_w_ada, m_b_ada, m_w_in, m_b_merge, m_gla_a2, m_gla_ab, m_gla_norm_g, m_na_rpb, m_gdn_conv, m_gdn_a_log, m_gdn_dt_bias, m_gdn_norm_g, m_m2_conv, m_m2_conv_b, m_m2_a_log, m_m2_dt_bias, m_m2_d, m_m2_norm_g, m_w_branch, m_w_out, m_w_ffn1, m_w_ffn3, m_w_ffn2, m_final_norm_g, v_c_ctx, v_norm1_g, v_norm2_g, v_w_ada, v_b_ada, v_w_in, v_b_merge, v_gla_a2, v_gla_ab, v_gla_norm_g, v_na_rpb, v_gdn_conv, v_gdn_a_log, v_gdn_dt_bias, v_gdn_norm_g, v_m2_conv, v_m2_conv_b, v_m2_a_log, v_m2_dt_bias, v_m2_d, v_m2_norm_g, v_w_branch, v_w_out, v_w_ffn1, v_w_ffn3, v_w_ffn2, v_final_norm_g):
    return _step(dict(locals()))
```
